```python
import math
import jax, jax.numpy as jnp
from jax import lax
import numpy as np

D_MODEL = 1024
BATCH = 8
SEQ = 4096
DEPTH = 4

HEAD_DIM = 64
D_MIX = D_MODEL
NSA_WIDTH = 3 * D_MIX // 8
NSA_HEADS = NSA_WIDTH // HEAD_DIM
NSA_KV_HEADS = 2
NSA_KV_WIDTH = NSA_KV_HEADS * HEAD_DIM
CMP_BLOCK = 32
CMP_STRIDE = 16
CMP_HIDDEN = 128
SEL_BLOCK = 64
SEL_TOP = 16
WINDOW = 512
N_BRANCH = 3
MLA_WIDTH = 3 * D_MIX // 8
MLA_V = 64
MLA_HEADS = MLA_WIDTH // MLA_V
MLA_NOPE = 64
MLA_ROPE = 32
Q_LORA = 3 * D_MODEL // 8
KV_LORA = D_MODEL // 8
ROPE_THETA = 10000.0
SSM_WIDTH = D_MIX - NSA_WIDTH - MLA_WIDTH
SSM_GROUP_CH = 16
SSM_GROUPS = SSM_WIDTH // SSM_GROUP_CH
SSM_STATE = 64
DT_MIN = 1e-3
DT_MAX = 1e-1
D_FF = 2816
CONV_WIDTH = 3
Q_BLOCK = 128
SEL_Q_BLOCK = 64
EPS = 1e-6
NEG = -1e30
IN_SIZES = (NSA_WIDTH,) + (NSA_KV_WIDTH,) * 6 + (NSA_HEADS * N_BRANCH, Q_LORA, KV_LORA, MLA_ROPE, SSM_WIDTH)
D_IN = sum(IN_SIZES)

kernel_name = 'hybrid_nsa_mla_s5_convffn'


def rms_norm(x, g):
    xf = x.astype(jnp.float32)
    y = xf * lax.rsqrt(jnp.mean(xf * xf, axis=-1, keepdims=True) + EPS)
    return (y * g.astype(jnp.float32)).astype(x.dtype)


def masked_softmax(s, mask):
    p = jax.nn.softmax(jnp.where(mask, s.astype(jnp.float32), NEG), axis=-1)
    return p * mask


def rope_angles(positions):
    half = MLA_ROPE // 2
    inv_freq = ROPE_THETA ** (-jnp.arange(half, dtype=jnp.float32) / half)
    ang = positions.astype(jnp.float32)[..., None] * inv_freq
    return jnp.cos(ang), jnp.sin(ang)


def apply_rope(x, cos, sin):
    x1, x2 = jnp.split(x.astype(jnp.float32), 2, axis=-1)
    return jnp.concatenate([x1 * cos - x2 * sin, x2 * cos + x1 * sin], axis=-1).astype(x.dtype)


def compress_blocks(blocks, pe, w1, b1, w2, b2):
    h = jax.nn.gelu(jnp.einsum('bnlkd,ldf->bnkf', blocks + pe[:, None, :], w1) + b1)
    return jnp.einsum('bnkf,fd->bnkd', h, w2) + b2


def nsa_attention(q, kc, vc, ks, vs, kw, vw, gate_logits, pe,
                  ck_w1, ck_b1, ck_w2, ck_b2, cv_w1, cv_b1, cv_w2, cv_b2, gate_b):
    B, S, _ = q.shape
    G = NSA_HEADS // NSA_KV_HEADS
    q = q.reshape(B, S, NSA_KV_HEADS, G, HEAD_DIM) * (HEAD_DIM ** -0.5)
    kc, vc, ks, vs, kw, vw = [t.reshape(B, S, NSA_KV_HEADS, HEAD_DIM) for t in (kc, vc, ks, vs, kw, vw)]
    pos = jnp.arange(S)

    nc = (S - CMP_BLOCK) // CMP_STRIDE + 1
    cmp_start = jnp.arange(nc) * CMP_STRIDE
    cidx = cmp_start[:, None] + jnp.arange(CMP_BLOCK)[None, :]
    k_cmp = compress_blocks(kc[:, cidx], pe, ck_w1, ck_b1, ck_w2, ck_b2)
    v_cmp = compress_blocks(vc[:, cidx], pe, cv_w1, cv_b1, cv_w2, cv_b2)
    cmask = (cmp_start + CMP_BLOCK - 1)[None, :] <= pos[:, None]
    p_cmp = masked_softmax(jnp.einsum('btkgd,bnkd->bkgtn', q, k_cmp), cmask)
    o_cmp = jnp.einsum('bkgtn,bnkd->btkgd', p_cmp.astype(v_cmp.dtype), v_cmp)

    ns = S // SEL_BLOCK
    n_top = min(SEL_TOP, ns)
    sel_id = jnp.arange(ns)
    overlap = ((cmp_start[:, None] < (sel_id[None, :] + 1) * SEL_BLOCK)
               & (cmp_start[:, None] + CMP_BLOCK > sel_id[None, :] * SEL_BLOCK)).astype(jnp.float32)
    imp = jnp.einsum('bkgtn,nj->bktj', p_cmp, overlap)
    cur = pos // SEL_BLOCK
    forced = (sel_id[None, :] == 0) | (sel_id[None, :] == cur[:, None]) | (sel_id[None, :] == cur[:, None] - 1)
    future = sel_id[None, :] > cur[:, None]
    imp = jnp.where(forced, jnp.inf, jnp.where(future, -jnp.inf, imp))
    _, top_idx = lax.top_k(imp, n_top)

    ks_blk = ks.reshape(B, ns, SEL_BLOCK, NSA_KV_HEADS, HEAD_DIM).transpose(0, 3, 1, 2, 4)
    vs_blk = vs.reshape(B, ns, SEL_BLOCK, NSA_KV_HEADS, HEAD_DIM).transpose(0, 3, 1, 2, 4)
    nq = S // SEL_Q_BLOCK
    q_chunks = q.reshape(B, nq, SEL_Q_BLOCK, NSA_KV_HEADS, G, HEAD_DIM).transpose(1, 0, 2, 3, 4, 5)
    idx_chunks = top_idx.reshape(B, NSA_KV_HEADS, nq, SEL_Q_BLOCK, n_top).transpose(2, 0, 1, 3, 4)
    pos_chunks = pos.reshape(nq, SEL_Q_BLOCK)
    b_ix = jnp.arange(B)[:, None, None, None]
    h_ix = jnp.arange(NSA_KV_HEADS)[None, :, None, None]

    def sel_chunk(args):
        qc, ic, tc = args
        kg = ks_blk[b_ix, h_ix, ic]
        vg = vs_blk[b_ix, h_ix, ic]
        s = jnp.einsum('bqkgd,bkqnsd->bkgqns', qc, kg)
        key_pos = ic[..., None] * SEL_BLOCK + jnp.arange(SEL_BLOCK)
        m = (key_pos <= tc[None, None, :, None, None])[:, :, None]
        p = masked_softmax(s.reshape(s.shape[:4] + (-1,)), m.reshape(m.shape[:4] + (-1,)))
        return jnp.einsum('bkgqns,bkqnsd->bqkgd', p.reshape(s.shape).astype(vg.dtype), vg)

    o_sel = lax.map(sel_chunk, (q_chunks, idx_chunks, pos_chunks))
    o_sel = o_sel.transpose(1, 0, 2, 3, 4, 5).reshape(B, S, NSA_KV_HEADS, G, HEAD_DIM)

    nqb = S // Q_BLOCK
    span = WINDOW + Q_BLOCK
    pad = ((0, 0), (WINDOW, 0), (0, 0), (0, 0))
    widx = jnp.arange(nqb)[:, None] * Q_BLOCK + jnp.arange(span)[None, :]
    kwb = jnp.pad(kw, pad)[:, widx]
    vwb = jnp.pad(vw, pad)[:, widx]
    qb = q.reshape(B, nqb, Q_BLOCK, NSA_KV_HEADS, G, HEAD_DIM)
    s_win = jnp.einsum('biqkgd,bijkd->bikgqj', qb, kwb)
    q_pos = pos.reshape(nqb, Q_BLOCK)[:, :, None]
    k_pos = (widx - WINDOW)[:, None, :]
    wmask = (k_pos <= q_pos) & (k_pos > q_pos - WINDOW) & (k_pos >= 0)
    p_win = masked_softmax(s_win, wmask[None, :, None, None])
    o_win = jnp.einsum('bikgqj,bijkd->biqkgd', p_win.astype(vwb.dtype), vwb).reshape(B, S, NSA_KV_HEADS, G, HEAD_DIM)

    g = jax.nn.sigmoid((gate_logits + gate_b).astype(jnp.float32)).reshape(B, S, NSA_KV_HEADS, G, N_BRANCH).astype(q.dtype)
    o = g[..., 0:1] * o_cmp + g[..., 1:2] * o_sel + g[..., 2:3] * o_win
    return o.reshape(B, S, NSA_WIDTH)


def mla_attention(c_q, c_kv, k_rope, cos, sin, q_norm, kv_norm, w_uq, w_uk, w_uv):
    B, S, _ = c_q.shape
    H = MLA_HEADS
    q = (rms_norm(c_q, q_norm) @ w_uq).reshape(B, S, H, MLA_NOPE + MLA_ROPE)
    ckv = rms_norm(c_kv, kv_norm)
    k_nope = (ckv @ w_uk).reshape(B, S, H, MLA_NOPE)
    v = (ckv @ w_uv).reshape(B, S, H, MLA_V)
    scale = (MLA_NOPE + MLA_ROPE) ** -0.5
    q_nope = q[..., :MLA_NOPE] * scale
    q_rope = apply_rope(q[..., MLA_NOPE:], cos[:, :, None], sin[:, :, None]) * scale
    k_rope = apply_rope(k_rope, cos, sin)
    nqb = S // Q_BLOCK
    qn_b = q_nope.reshape(B, nqb, Q_BLOCK, H, MLA_NOPE).transpose(1, 0, 2, 3, 4)
    qr_b = q_rope.reshape(B, nqb, Q_BLOCK, H, MLA_ROPE).transpose(1, 0, 2, 3, 4)
    q_pos = jnp.arange(S).reshape(nqb, Q_BLOCK)
    k_pos = jnp.arange(S)

    def block(args):
        qn, qr, qp = args
        s = jnp.einsum('bqhd,bshd->bhqs', qn, k_nope) + jnp.einsum('bqhr,bsr->bhqs', qr, k_rope)
        p = masked_softmax(s, k_pos[None, :] <= qp[:, None])
        return jnp.einsum('bhqs,bshd->bqhd', p.astype(v.dtype), v)

    o = lax.map(block, (qn_b, qr_b, q_pos))
    return o.transpose(1, 0, 2, 3, 4).reshape(B, S, MLA_WIDTH)


def ssm_combine(e1, e2):
    a1, b1 = e1
    a2, b2 = e2
    return a1 * a2, a2 * b1 + b2


def s5_ssm(u, log_dt, a_re, a_im, b_re, b_im, c_re, c_im, d, w_glu, b_glu):
    B, S, _ = u.shape
    f32 = jnp.float32
    uf = u.astype(f32).reshape(B, S, SSM_GROUPS, SSM_GROUP_CH)
    a = lax.complex(a_re.astype(f32), a_im.astype(f32))
    dt = jnp.exp(log_dt.astype(f32))[:, None]
    a_bar = jnp.exp(a * dt)
    b_bar = ((a_bar - 1.0) / a)[..., None] * lax.complex(b_re.astype(f32), b_im.astype(f32))
    bu = jnp.einsum('bsgc,gpc->bsgp', uf.astype(jnp.complex64), b_bar)
    _, h = lax.associative_scan(ssm_combine, (jnp.broadcast_to(a_bar, bu.shape), bu), axis=1)
    c = lax.complex(c_re.astype(f32), c_im.astype(f32))
    y = jnp.einsum('bsgp,gcp->bsgc', h, c).real + d.astype(f32) * uf
    z = jax.nn.gelu(y.reshape(B, S, SSM_WIDTH).astype(u.dtype))
    return z * jax.nn.sigmoid(z @ w_glu + b_glu)


def conv_ffn(h, w_up, conv_w, conv_b, w_down):
    gate, val = jnp.split(h @ w_up, 2, axis=-1)
    S = gate.shape[1]
    gp = jnp.pad(gate, ((0, 0), (CONV_WIDTH - 1, 0), (0, 0)))
    gate = sum(gp[:, k:k + S] * conv_w[k] for k in range(CONV_WIDTH)) + conv_b
    return (jax.nn.silu(gate) * val) @ w_down


def setup_inputs(seed: int = 0) -> dict:
    key = jax.random.key(seed)
    ks = iter(jax.random.split(key, 48))
    L = DEPTH
    f32 = jnp.float32

    def normal(shape, scale):
        return jax.random.normal(next(ks), shape, f32) * scale

    def gain(shape):
        return 1.0 + normal(shape, 0.02)

    x = jax.random.normal(next(ks), (BATCH, SEQ, D_MODEL), f32)
    positions = jax.random.randint(next(ks), (BATCH, 1), 0, 1024, dtype=jnp.int32) + jnp.arange(SEQ, dtype=jnp.int32)[None, :]
    return {
        'x': x,
        'positions': positions,
        'attn_norm': gain((L, D_MODEL)),
        'w_in': normal((L, D_MODEL, D_IN), D_MODEL ** -0.5),
        'nsa_pe': normal((L, CMP_BLOCK, HEAD_DIM), 0.1),
        'nsa_ck_w1': normal((L, CMP_BLOCK, HEAD_DIM, CMP_HIDDEN), (CMP_BLOCK * HEAD_DIM) ** -0.5),
        'nsa_ck_b1': normal((L, CMP_HIDDEN), 0.01),
        'nsa_ck_w2': normal((L, CMP_HIDDEN, HEAD_DIM), CMP_HIDDEN ** -0.5),
        'nsa_ck_b2': normal((L, HEAD_DIM), 0.01),
        'nsa_cv_w1': normal((L, CMP_BLOCK, HEAD_DIM, CMP_HIDDEN), (CMP_BLOCK * HEAD_DIM) ** -0.5),
        'nsa_cv_b1': normal((L, CMP_HIDDEN), 0.01),
        'nsa_cv_w2': normal((L, CMP_HIDDEN, HEAD_DIM), CMP_HIDDEN ** -0.5),
        'nsa_cv_b2': normal((L, HEAD_DIM), 0.01),
        'nsa_gate_b': normal((L, NSA_HEADS * N_BRANCH), 0.1),
        'mla_q_norm': gain((L, Q_LORA)),
        'mla_kv_norm': gain((L, KV_LORA)),
        'mla_w_uq': normal((L, Q_LORA, MLA_HEADS * (MLA_NOPE + MLA_ROPE)), Q_LORA ** -0.5),
        'mla_w_uk': normal((L, KV_LORA, MLA_HEADS * MLA_NOPE), KV_LORA ** -0.5),
        'mla_w_uv': normal((L, KV_LORA, MLA_HEADS * MLA_V), KV_LORA ** -0.5),
        'ssm_log_dt': jax.random.uniform(next(ks), (L, SSM_GROUPS), f32, math.log(DT_MIN), math.log(DT_MAX)),
        'ssm_a_re': -0.5 + normal((L, SSM_GROUPS, SSM_STATE), 0.01),
        'ssm_a_im': math.pi * jnp.arange(SSM_STATE, dtype=f32) + normal((L, SSM_GROUPS, SSM_STATE), 0.01),
        'ssm_b_re': normal((L, SSM_GROUPS, SSM_STATE, SSM_GROUP_CH), (2 * SSM_GROUP_CH) ** -0.5),
        'ssm_b_im': normal((L, SSM_GROUPS, SSM_STATE, SSM_GROUP_CH), (2 * SSM_GROUP_CH) ** -0.5),
        'ssm_c_re': normal((L, SSM_GROUPS, SSM_GROUP_CH, SSM_STATE), SSM_STATE ** -0.5),
        'ssm_c_im': normal((L, SSM_GROUPS, SSM_GROUP_CH, SSM_STATE), SSM_STATE ** -0.5),
        'ssm_d': normal((L, SSM_GROUPS, SSM_GROUP_CH), 0.5),
        'ssm_w_glu': normal((L, SSM_WIDTH, SSM_WIDTH), SSM_WIDTH ** -0.5),
        'ssm_b_glu': normal((L, SSM_WIDTH), 0.01),
        'out_norm_nsa': gain((L, NSA_WIDTH)),
        'out_norm_mla': gain((L, MLA_WIDTH)),
        'out_norm_ssm': gain((L, SSM_WIDTH)),
        'w_out': normal((L, D_MIX, D_MODEL), D_MIX ** -0.5),
        'ffn_norm': gain((L, D_MODEL)),
        'ffn_w_up': normal((L, D_MODEL, 2 * D_FF), D_MODEL ** -0.5),
        'ffn_conv_w': normal((L, CONV_WIDTH, D_FF), CONV_WIDTH ** -0.5),
        'ffn_conv_b': normal((L, D_FF), 0.01),
        'ffn_w_down': normal((L, D_FF, D_MODEL), D_FF ** -0.5),
        'final_norm': gain((D_MODEL,)),
    }


def reference(x, positions, attn_norm, w_in, nsa_pe,
              nsa_ck_w1, nsa_ck_b1, nsa_ck_w2, nsa_ck_b2,
              nsa_cv_w1, nsa_cv_b1, nsa_cv_w2, nsa_cv_b2, nsa_gate_b,
              mla_q_norm, mla_kv_norm, mla_w_uq, mla_w_uk, mla_w_uv,
              ssm_log_dt, ssm_a_re, ssm_a_im, ssm_b_re, ssm_b_im, ssm_c_re, ssm_c_im, ssm_d,
              ssm_w_glu, ssm_b_glu, out_norm_nsa, out_norm_mla, out_norm_ssm, w_out,
              ffn_norm, ffn_w_up, ffn_conv_w, ffn_conv_b, ffn_w_down, final_norm):
    split_points = np.cumsum(IN_SIZES)[:-1]
    cos, sin = rope_angles(positions)
    for l in range(DEPTH):
        h = rms_norm(x, attn_norm[l])
        (q_a, k_c, v_c, k_s, v_s, k_w, v_w, gate_a,
         c_q, c_kv, k_r, u) = jnp.split(h @ w_in[l], split_points, axis=-1)
        o_a = nsa_attention(q_a, k_c, v_c, k_s, v_s, k_w, v_w, gate_a, nsa_pe[l],
                            nsa_ck_w1[l], nsa_ck_b1[l], nsa_ck_w2[l], nsa_ck_b2[l],
                            nsa_cv_w1[l], nsa_cv_b1[l], nsa_cv_w2[l], nsa_cv_b2[l], nsa_gate_b[l])
        o_b = mla_attention(c_q, c_kv, k_r, cos, sin, mla_q_norm[l], mla_kv_norm[l],
                            mla_w_uq[l], mla_w_uk[l], mla_w_uv[l])
        o_c = s5_ssm(u, ssm_log_dt[l], ssm_a_re[l], ssm_a_im[l], ssm_b_re[l], ssm_b_im[l],
                     ssm_c_re[l], ssm_c_im[l], ssm_d[l], ssm_w_glu[l], ssm_b_glu[l])
        mix = jnp.concatenate([rms_norm(o_a, out_norm_nsa[l]),
                               rms_norm(o_b, out_norm_mla[l]),
                               rms_norm(o_c, out_norm_ssm[l])], axis=-1)
        x = x + mix @ w_out[l]
        x = x + conv_ffn(rms_norm(x, ffn_norm[l]), ffn_w_up[l], ffn_conv_w[l], ffn_conv_b[l], ffn_w_down[l])
    return rms_norm(x, final_norm)
```

```python
import functools
import math

import jax
import jax.numpy as jnp
from jax import lax
from jax.experimental import pallas as pl
from jax.experimental.pallas import tpu as pltpu

F32 = jnp.float32
BF16 = jnp.bfloat16

LANES = 128
HEAD_DIM = 64
NSA_HEADS = 6
NSA_KV_HEADS = 2
NSA_GROUP = NSA_HEADS // NSA_KV_HEADS
N_BRANCH = 3
CMP_BLOCK = 32
CMP_STRIDE = 16
CMP_HIDDEN = 128
SEL_BLOCK = 64
SEL_TOP = 16
MAX_SEL_BLOCKS = 64
WINDOW = 512
MLA_HEADS = 6
MLA_NOPE = 64
MLA_ROPE = 32
Q_LORA = 384
KV_LORA = 128
ROPE_THETA = 10000.0
SSM_WIDTH = 256
SSM_GROUPS = 16
SSM_GROUP_CH = 16
SSM_STATE = 64
SSM_LANES = SSM_GROUPS * SSM_STATE
D_FF = 2816
EPS = 1e-6
NEG = -1e30

ROW_TILE = 512
Q_TILE = 128
KV_CHUNK = 512
SSM_CHUNK = 64
FF_CHUNK = 256
VMEM_LIMIT = 56 * 1024 * 1024


def _params(sem, vmem=None):
    return pltpu.CompilerParams(dimension_semantics=sem, vmem_limit_bytes=vmem)


def _rms(x, g):
    return x * lax.rsqrt(jnp.mean(x * x, axis=-1, keepdims=True) + EPS) * g


def _gelu(x):
    c = math.sqrt(2.0 / math.pi)
    return 0.5 * x * (1.0 + jnp.tanh(c * (x + 0.044715 * (x * x * x))))


def _sigmoid(x):
    return 1.0 / (1.0 + jnp.exp(-x))


def _mod_pow2(x, n):
    assert n & (n - 1) == 0
    return jnp.bitwise_and(x, n - 1)


def _div_pow2(x, n):
    assert n & (n - 1) == 0
    return jnp.right_shift(x, n.bit_length() - 1)


def _const_spec(shape):
    nd = len(shape)
    return pl.BlockSpec(shape, lambda *_: (0,) * nd)


def _trig_kernel(pos_ref, invf_ref, sign_ref, c_ref, s_ref):
    ang = pos_ref[...] * invf_ref[...]
    c_ref[...] = jnp.cos(ang)
    s_ref[...] = jnp.sin(ang) * sign_ref[...]


def _trig(pos_rows, invf_row, sign_row):
    t = pos_rows.shape[0]
    spec = pl.BlockSpec((ROW_TILE, LANES), lambda i: (i, 0))
    return pl.pallas_call(
        _trig_kernel,
        grid=(t // ROW_TILE,),
        in_specs=[spec, _const_spec((1, LANES)), _const_spec((1, LANES))],
        out_specs=[spec, spec],
        out_shape=[jax.ShapeDtypeStruct((t, LANES), F32)] * 2,
        compiler_params=_params(("parallel",)),
        name="rope_trig",
    )(pos_rows, invf_row, sign_row)


_IN_SEGS = (
    ("q", NSA_HEADS * LANES, BF16),
    ("kvc", 4 * HEAD_DIM, F32),
    ("ks", NSA_KV_HEADS * LANES, BF16),
    ("vs", NSA_KV_HEADS * LANES, BF16),
    ("kw", NSA_KV_HEADS * LANES, BF16),
    ("vw", NSA_KV_HEADS * LANES, BF16),
    ("gate", LANES, F32),
    ("cq", Q_LORA, F32),
    ("ckv", KV_LORA, F32),
    ("kra", LANES, F32),
    ("krb", LANES, F32),
    ("u", SSM_WIDTH, F32),
)
_IN_COLS = sum(n for _, n, _ in _IN_SEGS)


def _inproj_kernel(x_ref, g_ref, w_ref, *o_refs, seq, tm):
    h = _rms(x_ref[...], g_ref[...]).astype(BF16)
    off = 0
    for (name, n, dt), o_ref in zip(_IN_SEGS, o_refs):
        y = jnp.dot(h, w_ref[:, off:off + n], preferred_element_type=F32)
        if name == "ks":
            s0 = lax.rem(pl.program_id(0) * tm, seq)
            blk = _div_pow2(s0 + lax.broadcasted_iota(jnp.int32, (tm, n), 0), SEL_BLOCK)
            lane = _mod_pow2(lax.broadcasted_iota(jnp.int32, (tm, n), 1), LANES)
            y = y + jnp.where(lane - HEAD_DIM == blk, 1.0, 0.0)
        o_ref[...] = y.astype(dt)
        off += n


def _inproj(x, g, w, seq):
    t, d = x.shape
    tm = ROW_TILE
    return pl.pallas_call(
        functools.partial(_inproj_kernel, seq=seq, tm=tm),
        grid=(t // tm,),
        in_specs=[pl.BlockSpec((tm, d), lambda i: (i, 0)), _const_spec((1, d)),
                  _const_spec((d, _IN_COLS))],
        out_specs=[pl.BlockSpec((tm, n), lambda i: (i, 0)) for _, n, _ in _IN_SEGS],
        out_shape=[jax.ShapeDtypeStruct((t, n), dt) for _, n, dt in _IN_SEGS],
        compiler_params=_params(("parallel",), VMEM_LIMIT),
        name="in_proj",
    )(x, g, w)


def _compress_kernel(a_ref, pe_ref, w1_ref, b1_ref, w2_ref, b2_ref, o_ref):
    half = CMP_STRIDE * HEAD_DIM
    a = a_ref[0, 0]
    top = jnp.dot((a + pe_ref[0, :, 0:half]).astype(BF16), w1_ref[0, 0:half, :],
                  preferred_element_type=F32)
    bot = jnp.dot((a + pe_ref[0, :, half:2 * half]).astype(BF16), w1_ref[0, half:2 * half, :],
                  preferred_element_type=F32)
    n = a.shape[0]
    hid = _gelu(top + pltpu.roll(bot, n - 1, 0) + b1_ref[0])
    o_ref[0, 0] = jnp.dot(hid.astype(BF16), w2_ref[0], preferred_element_type=F32) + b2_ref[0]


def _compress(a, pe, w1, b1, w2, b2):
    b, four, n, width = a.shape
    wspec = lambda shape: pl.BlockSpec((1,) + shape, lambda i, j: (j // NSA_KV_HEADS,) + (0,) * len(shape))
    return pl.pallas_call(
        _compress_kernel,
        grid=(b, four),
        in_specs=[pl.BlockSpec((1, 1, n, width), lambda i, j: (i, j, 0, 0)),
                  wspec((1, width * 2)), wspec((width * 2, CMP_HIDDEN)), wspec((1, CMP_HIDDEN)),
                  wspec((CMP_HIDDEN, LANES)), wspec((1, LANES))],
        out_specs=pl.BlockSpec((1, 1, n, LANES), lambda i, j: (i, j, 0, 0)),
        out_shape=jax.ShapeDtypeStruct((b, four, n, LANES), F32),
        compiler_params=_params(("parallel", "parallel")),
        name="nsa_compress",
    )(a, pe, w1, b1, w2, b2)


def _stack_heads(q_ref, groups):
    return jnp.concatenate([q_ref[:, g * LANES:(g + 1) * LANES] for g in range(groups)], axis=0)


def _cmp_topk_kernel(q_ref, kc_ref, vc_ref, ovl_ref, o_ref, qa_ref, *, tq, n_top):
    q0 = pl.program_id(2) * tq
    q = _stack_heads(q_ref, NSA_GROUP)
    m_rows = NSA_GROUP * tq
    kc = kc_ref[0, 0].astype(BF16)
    vc = vc_ref[0, 0].astype(BF16)
    nc = kc.shape[0]
    s = lax.dot_general(q, kc, (((1,), (1,)), ((), ())), preferred_element_type=F32)
    t_row = q0 + _mod_pow2(lax.broadcasted_iota(jnp.int32, (m_rows, nc), 0), tq)
    n_col = lax.broadcasted_iota(jnp.int32, (m_rows, nc), 1)
    mask = n_col * CMP_STRIDE + (CMP_BLOCK - 1) <= t_row
    sm = jnp.where(mask, s, NEG)
    e = jnp.exp(sm - jnp.max(sm, axis=-1, keepdims=True))
    p = jnp.where(mask, e / jnp.sum(e, axis=-1, keepdims=True), 0.0)
    o = jnp.dot(p.astype(BF16), vc, preferred_element_type=F32)
    for g in range(NSA_GROUP):
        o_ref[:, g * LANES:(g + 1) * LANES] = o[g * tq:(g + 1) * tq].astype(o_ref.dtype)

    psum = p[0:tq] + p[tq:2 * tq] + p[2 * tq:3 * tq]
    hi = psum.astype(BF16)
    lo = (psum - hi.astype(F32)).astype(BF16)
    nt = (((1,), (1,)), ((), ()))
    imp = (lax.dot_general(ovl_ref[...], hi, nt, preferred_element_type=F32)
           + lax.dot_general(ovl_ref[...], lo, nt, preferred_element_type=F32))
    j_idx = lax.broadcasted_iota(jnp.int32, (MAX_SEL_BLOCKS, tq), 0)
    cur = _div_pow2(q0 + lax.broadcasted_iota(jnp.int32, (MAX_SEL_BLOCKS, tq), 1), SEL_BLOCK)
    forced = (j_idx == 0) | (j_idx == cur) | (j_idx == cur - 1)
    val = jnp.where(forced, jnp.inf, jnp.where(j_idx > cur, -jnp.inf, imp))
    rank = jnp.zeros((MAX_SEL_BLOCKS, tq), F32)
    for i in range(MAX_SEL_BLOCKS):
        row = val[i:i + 1, :]
        ge = jnp.where(row >= val, 1.0, 0.0)
        gt = jnp.where(row > val, 1.0, 0.0)
        rank = rank + jnp.where(j_idx > i, ge, gt)
    bias_t = jnp.where(rank < n_top, 0.0, NEG)
    bias = jnp.concatenate([jnp.zeros_like(bias_t), bias_t], axis=0).T
    for g in range(NSA_GROUP):
        sl = slice(g * LANES, (g + 1) * LANES)
        qa_ref[:, sl] = (q_ref[:, sl].astype(F32) + bias).astype(qa_ref.dtype)


def _cmp_topk(q, kvcmp, ovl_t, batch, seq, n_top):
    t = q.shape[0]
    tq = Q_TILE
    nq = seq // tq
    nc = kvcmp.shape[2]
    qspec = pl.BlockSpec((tq, NSA_GROUP * LANES), lambda b, k, i: (b * nq + i, k))
    return pl.pallas_call(
        functools.partial(_cmp_topk_kernel, tq=tq, n_top=n_top),
        grid=(batch, NSA_KV_HEADS, nq),
        in_specs=[qspec,
                  pl.BlockSpec((1, 1, nc, LANES), lambda b, k, i: (b, k, 0, 0)),
                  pl.BlockSpec((1, 1, nc, LANES), lambda b, k, i: (b, NSA_KV_HEADS + k, 0, 0)),
                  _const_spec((MAX_SEL_BLOCKS, nc))],
        out_specs=[qspec, qspec],
        out_shape=[jax.ShapeDtypeStruct((t, NSA_HEADS * LANES), BF16)] * 2,
        compiler_params=_params(("parallel", "parallel", "parallel")),
        name="nsa_cmp_topk",
    )(q, kvcmp, kvcmp, ovl_t)


def _flash_kernel(q_ref, k_ref, v_ref, o_ref, *, groups, tq, ck):
    q0 = pl.program_id(2) * tq
    q = _stack_heads(q_ref, groups)
    m_rows = groups * tq
    row_pos = q0 + _mod_pow2(lax.broadcasted_iota(jnp.int32, (m_rows, ck), 0), tq)
    col_iota = lax.broadcasted_iota(jnp.int32, (m_rows, ck), 1)

    def body(c, carry):
        m_old, l_old, acc = carry
        k0 = pl.multiple_of(c * ck, ck)
        k = k_ref[pl.ds(k0, ck), :]
        v = v_ref[pl.ds(k0, ck), :]
        s = lax.dot_general(q, k, (((1,), (1,)), ((), ())), preferred_element_type=F32)
        s = jnp.where(k0 + col_iota <= row_pos, s, NEG)
        m_new = jnp.maximum(m_old, jnp.max(s, axis=-1, keepdims=True))
        alpha = jnp.exp(m_old - m_new)
        p = jnp.exp(s - m_new)
        l_new = alpha * l_old + jnp.sum(p, axis=-1, keepdims=True)
        acc = alpha * acc + jnp.dot(p.astype(BF16), v, preferred_element_type=F32)
        return m_new, l_new, acc

    n_chunks = (q0 + tq + ck - 1) // ck
    init = (jnp.full((m_rows, 1), NEG, F32), jnp.zeros((m_rows, 1), F32),
            jnp.zeros((m_rows, LANES), F32))
    _, l_fin, acc = lax.fori_loop(0, n_chunks, body, init)
    o = acc / l_fin
    for g in range(groups):
        o_ref[:, g * LANES:(g + 1) * LANES] = o[g * tq:(g + 1) * tq].astype(o_ref.dtype)


def _flash(q, k, v, batch, seq, groups, name):
    t, qcols = q.shape
    kv_heads = k.shape[1] // LANES
    tq = Q_TILE
    ck = min(KV_CHUNK, seq)
    nq = seq // tq
    qspec = pl.BlockSpec((tq, groups * LANES), lambda b, h, i: (b * nq + i, h))
    kvspec = pl.BlockSpec((seq, LANES), lambda b, h, i: (b, h))
    return pl.pallas_call(
        functools.partial(_flash_kernel, groups=groups, tq=tq, ck=ck),
        grid=(batch, kv_heads, nq),
        in_specs=[qspec, kvspec, kvspec],
        out_specs=qspec,
        out_shape=jax.ShapeDtypeStruct((t, qcols), BF16),
        compiler_params=_params(("parallel", "parallel", "parallel")),
        name=name,
    )(q, k, v)


def _window_kernel(q_ref, k_ref, v_ref, o_ref, *, tq, span):
    q0 = pl.program_id(2) * tq
    q = _stack_heads(q_ref, NSA_GROUP)
    m_rows = NSA_GROUP * tq
    start = pl.multiple_of(jnp.maximum(q0 + tq - span, 0), tq)
    k = k_ref[pl.ds(start, span), :]
    v = v_ref[pl.ds(start, span), :]
    s = lax.dot_general(q, k, (((1,), (1,)), ((), ())), preferred_element_type=F32)
    row = q0 + _mod_pow2(lax.broadcasted_iota(jnp.int32, (m_rows, span), 0), tq)
    col = start + lax.broadcasted_iota(jnp.int32, (m_rows, span), 1)
    mask = (col <= row) & (col > row - WINDOW)
    sm = jnp.where(mask, s, NEG)
    e = jnp.exp(sm - jnp.max(sm, axis=-1, keepdims=True))
    p = jnp.where(mask, e / jnp.sum(e, axis=-1, keepdims=True), 0.0)
    o = jnp.dot(p.astype(BF16), v, preferred_element_type=F32)
    for g in range(NSA_GROUP):
        o_ref[:, g * LANES:(g + 1) * LANES] = o[g * tq:(g + 1) * tq].astype(o_ref.dtype)


def _window(q, k, v, batch, seq):
    t = q.shape[0]
    tq = Q_TILE
    span = min(WINDOW + tq, seq)
    nq = seq // tq
    qspec = pl.BlockSpec((tq, NSA_GROUP * LANES), lambda b, h, i: (b * nq + i, h))
    kvspec = pl.BlockSpec((seq, LANES), lambda b, h, i: (b, h))
    return pl.pallas_call(
        functools.partial(_window_kernel, tq=tq, span=span),
        grid=(batch, NSA_KV_HEADS, nq),
        in_specs=[qspec, kvspec, kvspec],
        out_specs=qspec,
        out_shape=jax.ShapeDtypeStruct((t, NSA_HEADS * LANES), BF16),
        compiler_params=_params(("parallel", "parallel", "parallel")),
        name="nsa_window",
    )(q, k, v)


def _mla_proj_kernel(cq_ref, ckv_ref, kra_ref, krb_ref, cm_ref, sm_ref, qg_ref, kg_ref,
                     wa_ref, wb_ref, wk_ref, wv_ref, q_ref, k_ref, v_ref, *, scale):
    qn = _rms(cq_ref[...], qg_ref[...]).astype(BF16)
    cn = _rms(ckv_ref[...], kg_ref[...]).astype(BF16)
    cm = cm_ref[...]
    sm = sm_ref[...]
    k_rot = kra_ref[...] * cm + krb_ref[...] * sm
    for h in range(MLA_HEADS):
        sl = slice(h * LANES, (h + 1) * LANES)
        qa = jnp.dot(qn, wa_ref[:, sl], preferred_element_type=F32)
        qb = jnp.dot(qn, wb_ref[:, sl], preferred_element_type=F32)
        q_ref[:, sl] = ((qa * cm + qb * sm) * scale).astype(q_ref.dtype)
        k_ref[:, sl] = (jnp.dot(cn, wk_ref[:, sl], preferred_element_type=F32) + k_rot).astype(k_ref.dtype)
        v_ref[:, sl] = jnp.dot(cn, wv_ref[:, sl], preferred_element_type=F32).astype(v_ref.dtype)


def _mla_proj(cq, ckv, kra, krb, cmul, smul, qg, kg, wa, wb, wk, wv):
    t = cq.shape[0]
    tm = ROW_TILE
    width = MLA_HEADS * LANES
    rows = lambda n: pl.BlockSpec((tm, n), lambda i: (i, 0))
    scale = (MLA_NOPE + MLA_ROPE) ** -0.5
    return pl.pallas_call(
        functools.partial(_mla_proj_kernel, scale=scale),
        grid=(t // tm,),
        in_specs=[rows(Q_LORA), rows(KV_LORA), rows(LANES), rows(LANES), rows(LANES), rows(LANES),
                  _const_spec((1, Q_LORA)), _const_spec((1, KV_LORA)),
                  _const_spec((Q_LORA, width)), _const_spec((Q_LORA, width)),
                  _const_spec((KV_LORA, width)), _const_spec((KV_LORA, width))],
        out_specs=[rows(width)] * 3,
        out_shape=[jax.ShapeDtypeStruct((t, width), BF16)] * 3,
        compiler_params=_params(("parallel",)),
        name="mla_proj",
    )(cq, ckv, kra, krb, cmul, smul, qg, kg, wa, wb, wk, wv)


def _ssm_kernel(u_ref, bblk_ref, a_ref, ccat_ref, d_ref, wglu_ref, bglu_ref, o_ref,
                h_sc, st_sc, *, tc, nb):
    @pl.when(pl.program_id(0) == 0)
    def _():
        st_sc[...] = jnp.zeros_like(st_sc)

    n = SSM_LANES
    u = u_ref[...].reshape(tc * nb, SSM_WIDTH)
    h_sc[...] = jnp.dot(u.astype(BF16), bblk_ref[...], preferred_element_type=F32)
    ar = jnp.broadcast_to(a_ref[0:1, :], (nb, n))
    ai = jnp.broadcast_to(a_ref[1:2, :], (nb, n))

    def step(t, carry):
        hr, hi = carry
        r0 = pl.multiple_of(t * nb, nb)
        nr = ar * hr - ai * hi + h_sc[pl.ds(r0, nb), 0:n]
        ni = ar * hi + ai * hr + h_sc[pl.ds(r0, nb), n:2 * n]
        h_sc[pl.ds(r0, nb), 0:n] = nr
        h_sc[pl.ds(r0, nb), n:2 * n] = ni
        return nr, ni

    hr, hi = lax.fori_loop(0, tc, step, (st_sc[0], st_sc[1]), unroll=4)
    st_sc[0] = hr
    st_sc[1] = hi
    y = jnp.dot(h_sc[...].astype(BF16), ccat_ref[...], preferred_element_type=F32) + d_ref[...] * u
    z = _gelu(y)
    gate = jnp.dot(z.astype(BF16), wglu_ref[...], preferred_element_type=F32) + bglu_ref[...]
    o_ref[...] = (z * _sigmoid(gate)).reshape(tc, nb, SSM_WIDTH)


def _ssm(u_t, bblk, a_rows, ccat, d_row, wglu, bglu):
    seq, nb, _ = u_t.shape
    tc = SSM_CHUNK
    n = SSM_LANES
    uspec = pl.BlockSpec((tc, nb, SSM_WIDTH), lambda i: (i, 0, 0))
    return pl.pallas_call(
        functools.partial(_ssm_kernel, tc=tc, nb=nb),
        grid=(seq // tc,),
        in_specs=[uspec, _const_spec((SSM_WIDTH, 2 * n)), _const_spec((2, n)),
                  _const_spec((2 * n, SSM_WIDTH)), _const_spec((1, SSM_WIDTH)),
                  _const_spec((SSM_WIDTH, SSM_WIDTH)), _const_spec((1, SSM_WIDTH))],
        out_specs=uspec,
        out_shape=jax.ShapeDtypeStruct(u_t.shape, F32),
        scratch_shapes=[pltpu.VMEM((tc * nb, 2 * n), F32), pltpu.VMEM((2, nb, n), F32)],
        compiler_params=_params(("arbitrary",), VMEM_LIMIT),
        name="s5_scan",
    )(u_t, bblk, a_rows, ccat, d_row, wglu, bglu)


def _outproj_kernel(oc_ref, os_ref, ow_ref, gate_ref, gb_ref, om_ref, oz_ref, x_ref,
                    gn_ref, gm_ref, gz_ref, wn_ref, wm_ref, wz_ref, o_ref):
    g = _sigmoid(gate_ref[...] + gb_ref[...])
    heads = []
    ss = 0.0
    for h in range(NSA_HEADS):
        sl = slice(h * LANES, (h + 1) * LANES)
        c = N_BRANCH * h
        o_h = (g[:, c:c + 1] * oc_ref[:, sl].astype(F32)
               + g[:, c + 1:c + 2] * os_ref[:, sl].astype(F32)
               + g[:, c + 2:c + 3] * ow_ref[:, sl].astype(F32))
        ss = ss + jnp.sum(o_h * o_h, axis=-1, keepdims=True)
        heads.append(o_h)
    inv = lax.rsqrt(ss * (1.0 / (NSA_HEADS * HEAD_DIM)) + EPS)
    acc = x_ref[...]
    for h in range(NSA_HEADS):
        sl = slice(h * LANES, (h + 1) * LANES)
        acc = acc + jnp.dot((heads[h] * inv * gn_ref[:, sl]).astype(BF16), wn_ref[sl, :],
                            preferred_element_type=F32)
    om = om_ref[...].astype(F32)
    inv = lax.rsqrt(jnp.sum(om * om, axis=-1, keepdims=True) * (1.0 / (MLA_HEADS * HEAD_DIM)) + EPS)
    acc = acc + jnp.dot((om * inv * gm_ref[...]).astype(BF16), wm_ref[...], preferred_element_type=F32)
    acc = acc + jnp.dot(_rms(oz_ref[...], gz_ref[...]).astype(BF16), wz_ref[...],
                        preferred_element_type=F32)
    o_ref[...] = acc


def _outproj(oc, osel, ow, gate, gate_b, om, oz, x, gn, gm, gz, wn, wm, wz):
    t, d = x.shape
    tm = ROW_TILE
    wide = NSA_HEADS * LANES
    rows = lambda n: pl.BlockSpec((tm, n), lambda i: (i, 0))
    return pl.pallas_call(
        _outproj_kernel,
        grid=(t // tm,),
        in_specs=[rows(wide), rows(wide), rows(wide), rows(LANES), _const_spec((1, LANES)),
                  rows(wide), rows(SSM_WIDTH), rows(d),
                  _const_spec((1, wide)), _const_spec((1, wide)), _const_spec((1, SSM_WIDTH)),
                  _const_spec((wide, d)), _const_spec((wide, d)), _const_spec((SSM_WIDTH, d))],
        out_specs=rows(d),
        out_shape=jax.ShapeDtypeStruct((t, d), F32),
        compiler_params=_params(("parallel",), VMEM_LIMIT),
        name="out_proj",
    )(oc, osel, ow, gate, gate_b, om, oz, x, gn, gm, gz, wn, wm, wz)


def _ffn_kernel(x_ref, g_ref, wg_ref, wv_ref, cw_ref, cb_ref, wd_ref, o_ref, carry_sc,
                *, tm, tiles_per_seq, cf):
    @pl.when(lax.rem(pl.program_id(0), tiles_per_seq) == 0)
    def _():
        carry_sc[...] = jnp.zeros_like(carry_sc)

    x = x_ref[...]
    h = _rms(x, g_ref[...]).astype(BF16)
    o_ref[...] = x
    row = lax.broadcasted_iota(jnp.int32, (tm, cf), 0)
    for c in range(D_FF // cf):
        sl = slice(c * cf, (c + 1) * cf)
        gate = jnp.dot(h, wg_ref[:, sl], preferred_element_type=F32)
        val = jnp.dot(h, wv_ref[:, sl], preferred_element_type=F32)
        tail = carry_sc[:, sl]
        p1 = tail[7:8, :]
        p2 = tail[6:7, :]
        g1 = jnp.where(row == 0, p1, pltpu.roll(gate, 1, 0))
        g2 = jnp.where(row == 0, p2, jnp.where(row == 1, p1, pltpu.roll(gate, 2, 0)))
        carry_sc[:, sl] = gate[tm - 8:tm, :]
        gc = cw_ref[0:1, sl] * g2 + cw_ref[1:2, sl] * g1 + cw_ref[2:3, sl] * gate + cb_ref[:, sl]
        act = gc * _sigmoid(gc) * val
        o_ref[...] += jnp.dot(act.astype(BF16), wd_ref[sl, :], preferred_element_type=F32)


def _ffn(x, g, wg, wv, cw, cb, wd, seq):
    t, d = x.shape
    tm = ROW_TILE
    rows = pl.BlockSpec((tm, d), lambda i: (i, 0))
    return pl.pallas_call(
        functools.partial(_ffn_kernel, tm=tm, tiles_per_seq=seq // tm, cf=FF_CHUNK),
        grid=(t // tm,),
        in_specs=[rows, _const_spec((1, d)), _const_spec((d, D_FF)), _const_spec((d, D_FF)),
                  _const_spec((8, D_FF)), _const_spec((1, D_FF)), _const_spec((D_FF, d))],
        out_specs=rows,
        out_shape=jax.ShapeDtypeStruct((t, d), F32),
        scratch_shapes=[pltpu.VMEM((8, D_FF), F32)],
        compiler_params=_params(("arbitrary",), VMEM_LIMIT),
        name="conv_ffn",
    )(x, g, wg, wv, cw, cb, wd)


def _final_norm_kernel(x_ref, g_ref, o_ref):
    o_ref[...] = _rms(x_ref[...], g_ref[...])


def _final_norm(x, g):
    t, d = x.shape
    rows = pl.BlockSpec((ROW_TILE, d), lambda i: (i, 0))
    return pl.pallas_call(
        _final_norm_kernel,
        grid=(t // ROW_TILE,),
        in_specs=[rows, _const_spec((1, d))],
        out_specs=rows,
        out_shape=jax.ShapeDtypeStruct((t, d), F32),
        compiler_params=_params(("parallel",)),
        name="final_norm",
    )(x, g)


def _pad_heads(w, heads, width):
    lead = w.shape[:-1]
    w = w.reshape(lead + (heads, width))
    w = jnp.pad(w, [(0, 0)] * len(lead) + [(0, 0), (0, LANES - width)])
    return w.reshape(lead + (heads * LANES,))


def _pad_head_rows(w, heads, width):
    n = w.shape[-1]
    w = w.reshape(heads, width, n)
    w = jnp.pad(w, [(0, 0), (0, LANES - width), (0, 0)])
    return w.reshape(heads * LANES, n)


def _inproj_weight(w):
    sizes = (384, 128, 128, 128, 128, 128, 128, 18, Q_LORA, KV_LORA, MLA_ROPE, SSM_WIDTH)
    offs = [0]
    for n in sizes:
        offs.append(offs[-1] + n)
    (w_q, w_kc, w_vc, w_ks, w_vs, w_kw, w_vw, w_g, w_cq, w_ckv, w_kr, w_u) = [
        w[:, a:b] for a, b in zip(offs[:-1], offs[1:])]
    d = w.shape[0]
    half = MLA_ROPE // 2
    z64 = jnp.zeros((d, HEAD_DIM), w.dtype)
    z32 = jnp.zeros((d, LANES - HEAD_DIM - MLA_ROPE), w.dtype)
    r1, r2 = w_kr[:, :half], w_kr[:, half:]
    cols = [
        _pad_heads(w_q * (HEAD_DIM ** -0.5), NSA_HEADS, HEAD_DIM),
        w_kc, w_vc,
        _pad_heads(w_ks, NSA_KV_HEADS, HEAD_DIM), _pad_heads(w_vs, NSA_KV_HEADS, HEAD_DIM),
        _pad_heads(w_kw, NSA_KV_HEADS, HEAD_DIM), _pad_heads(w_vw, NSA_KV_HEADS, HEAD_DIM),
        jnp.pad(w_g, ((0, 0), (0, LANES - w_g.shape[1]))),
        w_cq, w_ckv,
        jnp.concatenate([z64, r1, r2, z32], axis=1),
        jnp.concatenate([z64, r2, r1, z32], axis=1),
        w_u,
    ]
    return jnp.concatenate(cols, axis=1).astype(BF16)


def _mla_weights(w_uq, w_uk, w_uv):
    half = MLA_ROPE // 2
    w = w_uq.reshape(Q_LORA, MLA_HEADS, MLA_NOPE + MLA_ROPE)
    nope, r1, r2 = w[..., :MLA_NOPE], w[..., MLA_NOPE:MLA_NOPE + half], w[..., MLA_NOPE + half:]
    z32 = jnp.zeros((Q_LORA, MLA_HEADS, LANES - MLA_NOPE - MLA_ROPE), w.dtype)
    wa = jnp.concatenate([nope, r1, r2, z32], axis=-1).reshape(Q_LORA, MLA_HEADS * LANES)
    wb = jnp.concatenate([jnp.zeros_like(nope), r2, r1, z32], axis=-1).reshape(Q_LORA, MLA_HEADS * LANES)
    wk = _pad_heads(w_uk, MLA_HEADS, MLA_NOPE)
    wv = _pad_heads(w_uv, MLA_HEADS, HEAD_DIM)
    return wa.astype(BF16), wb.astype(BF16), wk.astype(BF16), wv.astype(BF16)


def _ssm_weights(log_dt, a_re, a_im, b_re, b_im, c_re, c_im, d):
    dt = jnp.exp(log_dt.astype(F32))[:, None]
    a = lax.complex(a_re.astype(F32), a_im.astype(F32))
    a_bar = jnp.exp(a * dt)
    b_bar = ((a_bar - 1.0) / a)[..., None] * lax.complex(b_re.astype(F32), b_im.astype(F32))
    eye = jnp.eye(SSM_GROUPS, dtype=F32)
    blk_in = lambda m: jnp.einsum("gpc,gh->gchp", m, eye).reshape(SSM_WIDTH, SSM_LANES)
    blk_out = lambda m: jnp.einsum("gcp,gh->gphc", m, eye).reshape(SSM_LANES, SSM_WIDTH)
    bblk = jnp.concatenate([blk_in(jnp.real(b_bar)), blk_in(jnp.imag(b_bar))], axis=1)
    ccat = jnp.concatenate([blk_out(c_re.astype(F32)), -blk_out(c_im.astype(F32))], axis=0)
    a_rows = jnp.stack([jnp.real(a_bar).reshape(SSM_LANES), jnp.imag(a_bar).reshape(SSM_LANES)])
    return bblk.astype(BF16), a_rows, ccat.astype(BF16), d.astype(F32).reshape(1, SSM_WIDTH)


def _overlap_t(nc_pad):
    start = jnp.arange(nc_pad) * CMP_STRIDE
    lo = jnp.arange(MAX_SEL_BLOCKS) * SEL_BLOCK
    hit = (start[None, :] < lo[:, None] + SEL_BLOCK) & (start[None, :] + CMP_BLOCK > lo[:, None])
    return hit.astype(BF16)


def _rope_rows():
    half = MLA_ROPE // 2
    inv_freq = ROPE_THETA ** (-jnp.arange(half, dtype=F32) / half)
    z64 = jnp.zeros((HEAD_DIM,), F32)
    z32 = jnp.zeros((LANES - HEAD_DIM - MLA_ROPE,), F32)
    invf = jnp.concatenate([z64, inv_freq, inv_freq, z32]).reshape(1, LANES)
    sign = jnp.concatenate([z64, -jnp.ones((half,), F32), jnp.ones((half,), F32), z32]).reshape(1, LANES)
    return invf, sign


def kernel(x, positions, attn_norm, w_in, nsa_pe, nsa_ck_w1, nsa_ck_b1, nsa_ck_w2, nsa_ck_b2, nsa_cv_w1, nsa_cv_b1, nsa_cv_w2, nsa_cv_b2, nsa_gate_b, mla_q_norm, mla_kv_norm, mla_w_uq, mla_w_uk, mla_w_uv, ssm_log_dt, ssm_a_re, ssm_a_im, ssm_b_re, ssm_b_im, ssm_c_re, ssm_c_im, ssm_d, ssm_w_glu, ssm_b_glu, out_norm_nsa, out_norm_mla, out_norm_ssm, w_out, ffn_norm, ffn_w_up, ffn_conv_w, ffn_conv_b, ffn_w_down, final_norm):
    batch, seq, d_model = x.shape
    depth = w_in.shape[0]
    t = batch * seq
    n_half = seq // CMP_STRIDE
    n_sel = seq // SEL_BLOCK
    assert seq % ROW_TILE == 0 and n_sel <= MAX_SEL_BLOCKS and n_half % LANES == 0
    n_top = min(SEL_TOP, n_sel)
    row = lambda v: v.astype(F32).reshape(1, -1)

    invf, sign = _rope_rows()
    pos_rows = jnp.broadcast_to(positions.astype(F32).reshape(t, 1), (t, LANES))
    cmul, smul = _trig(pos_rows, invf, sign)
    ovl_t = _overlap_t(n_half)

    xf = x.reshape(t, d_model)
    for l in range(depth):
        (q, kvc, ks, vs, kw, vw, gate, cq, ckv, kra, krb, u) = _inproj(
            xf, row(attn_norm[l]), _inproj_weight(w_in[l]), seq)

        a = kvc.reshape(batch, n_half, CMP_STRIDE, 4, HEAD_DIM).transpose(0, 3, 1, 2, 4)
        a = a.reshape(batch, 4, n_half, CMP_STRIDE * HEAD_DIM)
        flat = lambda w: w.reshape(CMP_BLOCK * HEAD_DIM, CMP_HIDDEN)
        pad2 = lambda w: jnp.pad(w, ((0, 0), (0, LANES - HEAD_DIM)))
        kvcmp = _compress(
            a,
            jnp.broadcast_to(nsa_pe[l].reshape(1, 1, -1), (2, 1, CMP_BLOCK * HEAD_DIM)).astype(F32),
            jnp.stack([flat(nsa_ck_w1[l]), flat(nsa_cv_w1[l])]).astype(BF16),
            jnp.stack([row(nsa_ck_b1[l]), row(nsa_cv_b1[l])]),
            jnp.stack([pad2(nsa_ck_w2[l]), pad2(nsa_cv_w2[l])]).astype(BF16),
            jnp.stack([pad2(row(nsa_ck_b2[l])), pad2(row(nsa_cv_b2[l]))]))
        o_cmp, q_aug = _cmp_topk(q, kvcmp, ovl_t, batch, seq, n_top)
        o_sel = _flash(q_aug, ks, vs, batch, seq, NSA_GROUP, "nsa_selected")
        o_win = _window(q, kw, vw, batch, seq)

        wa, wb, wk, wv = _mla_weights(mla_w_uq[l], mla_w_uk[l], mla_w_uv[l])
        q_m, k_m, v_m = _mla_proj(cq, ckv, kra, krb, cmul, smul, row(mla_q_norm[l]),
                                  row(mla_kv_norm[l]), wa, wb, wk, wv)
        o_mla = _flash(q_m, k_m, v_m, batch, seq, 1, "mla_attention")

        bblk, a_rows, ccat, d_row = _ssm_weights(
            ssm_log_dt[l], ssm_a_re[l], ssm_a_im[l], ssm_b_re[l], ssm_b_im[l],
            ssm_c_re[l], ssm_c_im[l], ssm_d[l])
        u_t = u.reshape(batch, seq, SSM_WIDTH).transpose(1, 0, 2)
        o_ssm = _ssm(u_t, bblk, a_rows, ccat, d_row, ssm_w_glu[l].astype(BF16), row(ssm_b_glu[l]))
        o_ssm = o_ssm.transpose(1, 0, 2).reshape(t, SSM_WIDTH)

        w_o = w_out[l].astype(BF16)
        n_a = NSA_HEADS * HEAD_DIM
        n_b = n_a + MLA_HEADS * HEAD_DIM
        xf = _outproj(
            o_cmp, o_sel, o_win, gate,
            jnp.pad(row(nsa_gate_b[l]), ((0, 0), (0, LANES - NSA_HEADS * N_BRANCH))),
            o_mla, o_ssm, xf,
            _pad_heads(row(out_norm_nsa[l]), NSA_HEADS, HEAD_DIM),
            _pad_heads(row(out_norm_mla[l]), MLA_HEADS, HEAD_DIM),
            row(out_norm_ssm[l]),
            _pad_head_rows(w_o[:n_a], NSA_HEADS, HEAD_DIM),
            _pad_head_rows(w_o[n_a:n_b], MLA_HEADS, HEAD_DIM),
            w_o[n_b:])

        w_up = ffn_w_up[l].astype(BF16)
        xf = _ffn(xf, row(ffn_norm[l]), w_up[:, :D_FF], w_up[:, D_FF:],
                  jnp.pad(ffn_conv_w[l].astype(F32), ((0, 8 - ffn_conv_w.shape[1]), (0, 0))),
                  row(ffn_conv_b[l]), ffn_w_down[l].astype(BF16), seq)

    return _final_norm(xf, row(final_norm)).reshape(batch, seq, d_model)
```

```python
import functools
import math

import jax
import jax.numpy as jnp
from jax import lax
from jax.experimental import pallas as pl
from jax.experimental.pallas import tpu as pltpu

F32 = jnp.float32
BF16 = jnp.bfloat16

LANES = 128
HEAD_DIM = 64
NSA_HEADS = 6
NSA_KV_HEADS = 2
NSA_GROUP = NSA_HEADS // NSA_KV_HEADS
N_BRANCH = 3
CMP_BLOCK = 32
CMP_STRIDE = 16
CMP_HIDDEN = 128
SEL_BLOCK = 64
SEL_TOP = 16
MAX_SEL_BLOCKS = 64
WINDOW = 512
MLA_HEADS = 6
MLA_NOPE = 64
MLA_ROPE = 32
Q_LORA = 384
KV_LORA = 128
ROPE_THETA = 10000.0
SSM_WIDTH = 256
SSM_GROUPS = 16
SSM_GROUP_CH = 16
SSM_STATE = 64
SSM_LANES = SSM_GROUPS * SSM_STATE
D_FF = 2816
EPS = 1e-6
NEG = -1e30

ROW_TILE = 512
Q_TILE = 128
KV_CHUNK = 512
FLASH_ROWS = 768
SSM_CHUNK = 64
FF_CHUNK = 256
VMEM_LIMIT = 56 * 1024 * 1024


def _params(sem, vmem=None):
    return pltpu.CompilerParams(dimension_semantics=sem, vmem_limit_bytes=vmem)


def _rms(x, g):
    return x * lax.rsqrt(jnp.mean(x * x, axis=-1, keepdims=True) + EPS) * g


def _gelu(x):
    c = math.sqrt(2.0 / math.pi)
    return 0.5 * x * (1.0 + jnp.tanh(c * (x + 0.044715 * (x * x * x))))


def _sigmoid(x):
    return 1.0 / (1.0 + jnp.exp(-x))


def _mod_pow2(x, n):
    assert n & (n - 1) == 0
    return jnp.bitwise_and(x, n - 1)


def _div_pow2(x, n):
    assert n & (n - 1) == 0
    return jnp.right_shift(x, n.bit_length() - 1)


def _const_spec(shape):
    nd = len(shape)
    return pl.BlockSpec(shape, lambda *_: (0,) * nd)


def _trig_kernel(pos_ref, invf_ref, sign_ref, c_ref, s_ref):
    ang = pos_ref[...] * invf_ref[...]
    c_ref[...] = jnp.cos(ang)
    s_ref[...] = jnp.sin(ang) * sign_ref[...]


def _trig(pos_rows, invf_row, sign_row):
    t = pos_rows.shape[0]
    spec = pl.BlockSpec((ROW_TILE, LANES), lambda i: (i, 0))
    return pl.pallas_call(
        _trig_kernel,
        grid=(t // ROW_TILE,),
        in_specs=[spec, _const_spec((1, LANES)), _const_spec((1, LANES))],
        out_specs=[spec, spec],
        out_shape=[jax.ShapeDtypeStruct((t, LANES), F32)] * 2,
        compiler_params=_params(("parallel",)),
        name="rope_trig",
    )(pos_rows, invf_row, sign_row)


_IN_SEGS = (
    ("q", NSA_HEADS * LANES, BF16),
    ("kvc", 4 * HEAD_DIM, F32),
    ("ks", NSA_KV_HEADS * LANES, BF16),
    ("vs", NSA_KV_HEADS * LANES, BF16),
    ("kw", NSA_KV_HEADS * LANES, BF16),
    ("vw", NSA_KV_HEADS * LANES, BF16),
    ("gate", LANES, F32),
    ("cq", Q_LORA, F32),
    ("ckv", KV_LORA, F32),
    ("kra", LANES, F32),
    ("krb", LANES, F32),
    ("u", SSM_WIDTH, F32),
)
_IN_COLS = sum(n for _, n, _ in _IN_SEGS)


def _inproj_kernel(x_ref, g_ref, w_ref, *o_refs, seq, tm):
    h = _rms(x_ref[...], g_ref[...]).astype(BF16)
    off = 0
    for (name, n, dt), o_ref in zip(_IN_SEGS, o_refs):
        y = jnp.dot(h, w_ref[:, off:off + n], preferred_element_type=F32)
        if name == "ks":
            s0 = lax.rem(pl.program_id(0) * tm, seq)
            blk = _div_pow2(s0 + lax.broadcasted_iota(jnp.int32, (tm, n), 0), SEL_BLOCK)
            lane = _mod_pow2(lax.broadcasted_iota(jnp.int32, (tm, n), 1), LANES)
            y = y + jnp.where(lane - HEAD_DIM == blk, 1.0, 0.0)
        o_ref[...] = y.astype(dt)
        off += n


def _inproj(x, g, w, seq):
    t, d = x.shape
    tm = ROW_TILE
    return pl.pallas_call(
        functools.partial(_inproj_kernel, seq=seq, tm=tm),
        grid=(t // tm,),
        in_specs=[pl.BlockSpec((tm, d), lambda i: (i, 0)), _const_spec((1, d)),
                  _const_spec((d, _IN_COLS))],
        out_specs=[pl.BlockSpec((tm, n), lambda i: (i, 0)) for _, n, _ in _IN_SEGS],
        out_shape=[jax.ShapeDtypeStruct((t, n), dt) for _, n, dt in _IN_SEGS],
        compiler_params=_params(("parallel",), VMEM_LIMIT),
        name="in_proj",
    )(x, g, w)


def _compress_kernel(a_ref, pe_ref, w1_ref, b1_ref, w2_ref, b2_ref, o_ref):
    half = CMP_STRIDE * HEAD_DIM
    a = a_ref[0, 0]
    top = jnp.dot((a + pe_ref[0, :, 0:half]).astype(BF16), w1_ref[0, 0:half, :],
                  preferred_element_type=F32)
    bot = jnp.dot((a + pe_ref[0, :, half:2 * half]).astype(BF16), w1_ref[0, half:2 * half, :],
                  preferred_element_type=F32)
    n = a.shape[0]
    hid = _gelu(top + pltpu.roll(bot, n - 1, 0) + b1_ref[0])
    o_ref[0, 0] = jnp.dot(hid.astype(BF16), w2_ref[0], preferred_element_type=F32) + b2_ref[0]


def _compress(a, pe, w1, b1, w2, b2):
    b, four, n, width = a.shape
    wspec = lambda shape: pl.BlockSpec((1,) + shape, lambda i, j: (j // NSA_KV_HEADS,) + (0,) * len(shape))
    return pl.pallas_call(
        _compress_kernel,
        grid=(b, four),
        in_specs=[pl.BlockSpec((1, 1, n, width), lambda i, j: (i, j, 0, 0)),
                  wspec((1, width * 2)), wspec((width * 2, CMP_HIDDEN)), wspec((1, CMP_HIDDEN)),
                  wspec((CMP_HIDDEN, LANES)), wspec((1, LANES))],
        out_specs=pl.BlockSpec((1, 1, n, LANES), lambda i, j: (i, j, 0, 0)),
        out_shape=jax.ShapeDtypeStruct((b, four, n, LANES), F32),
        compiler_params=_params(("parallel", "parallel")),
        name="nsa_compress",
    )(a, pe, w1, b1, w2, b2)


def _stack_heads(q_ref, groups):
    return jnp.concatenate([q_ref[:, g * LANES:(g + 1) * LANES] for g in range(groups)], axis=0)


def _cmp_topk_kernel(q_ref, kc_ref, vc_ref, ovl_ref, o_ref, qa_ref, *, tq, n_top):
    q0 = pl.program_id(2) * tq
    q = _stack_heads(q_ref, NSA_GROUP)
    m_rows = NSA_GROUP * tq
    kc = kc_ref[0, 0].astype(BF16)
    vc = vc_ref[0, 0].astype(BF16)
    nc = kc.shape[0]
    s = lax.dot_general(q, kc, (((1,), (1,)), ((), ())), preferred_element_type=F32)
    t_row = q0 + _mod_pow2(lax.broadcasted_iota(jnp.int32, (m_rows, nc), 0), tq)
    n_col = lax.broadcasted_iota(jnp.int32, (m_rows, nc), 1)
    mask = n_col * CMP_STRIDE + (CMP_BLOCK - 1) <= t_row
    sm = jnp.where(mask, s, NEG)
    e = jnp.exp(sm - jnp.max(sm, axis=-1, keepdims=True))
    p = jnp.where(mask, e / jnp.sum(e, axis=-1, keepdims=True), 0.0)
    o = jnp.dot(p.astype(BF16), vc, preferred_element_type=F32)
    for g in range(NSA_GROUP):
        o_ref[:, g * LANES:(g + 1) * LANES] = o[g * tq:(g + 1) * tq].astype(o_ref.dtype)

    psum = p[0:tq] + p[tq:2 * tq] + p[2 * tq:3 * tq]
    hi = psum.astype(BF16)
    lo = (psum - hi.astype(F32)).astype(BF16)
    nt = (((1,), (1,)), ((), ()))
    imp = (lax.dot_general(ovl_ref[...], hi, nt, preferred_element_type=F32)
           + lax.dot_general(ovl_ref[...], lo, nt, preferred_element_type=F32))
    j_idx = lax.broadcasted_iota(jnp.int32, (MAX_SEL_BLOCKS, tq), 0)
    cur = _div_pow2(q0 + lax.broadcasted_iota(jnp.int32, (MAX_SEL_BLOCKS, tq), 1), SEL_BLOCK)
    forced = (j_idx == 0) | (j_idx == cur) | (j_idx == cur - 1)
    val = jnp.where(forced, jnp.inf, jnp.where(j_idx > cur, -jnp.inf, imp))
    rank = jnp.zeros((MAX_SEL_BLOCKS, tq), F32)
    for i in range(MAX_SEL_BLOCKS):
        row = val[i:i + 1, :]
        ge = jnp.where(row >= val, 1.0, 0.0)
        gt = jnp.where(row > val, 1.0, 0.0)
        rank = rank + jnp.where(j_idx > i, ge, gt)
    bias_t = jnp.where(rank < n_top, 0.0, NEG)
    bias = jnp.concatenate([jnp.zeros_like(bias_t), bias_t], axis=0).T
    for g in range(NSA_GROUP):
        sl = slice(g * LANES, (g + 1) * LANES)
        qa_ref[:, sl] = (q_ref[:, sl].astype(F32) + bias).astype(qa_ref.dtype)


def _cmp_topk(q, kvcmp, ovl_t, batch, seq, n_top):
    t = q.shape[0]
    tq = Q_TILE
    nq = seq // tq
    nc = kvcmp.shape[2]
    qspec = pl.BlockSpec((tq, NSA_GROUP * LANES), lambda b, k, i: (b * nq + i, k))
    return pl.pallas_call(
        functools.partial(_cmp_topk_kernel, tq=tq, n_top=n_top),
        grid=(batch, NSA_KV_HEADS, nq),
        in_specs=[qspec,
                  pl.BlockSpec((1, 1, nc, LANES), lambda b, k, i: (b, k, 0, 0)),
                  pl.BlockSpec((1, 1, nc, LANES), lambda b, k, i: (b, NSA_KV_HEADS + k, 0, 0)),
                  _const_spec((MAX_SEL_BLOCKS, nc))],
        out_specs=[qspec, qspec],
        out_shape=[jax.ShapeDtypeStruct((t, NSA_HEADS * LANES), BF16)] * 2,
        compiler_params=_params(("parallel", "parallel", "parallel")),
        name="nsa_cmp_topk",
    )(q, kvcmp, kvcmp, ovl_t)


def _flash_kernel(q_ref, k_ref, v_ref, o_ref, *, groups, tq, ck):
    q0 = pl.program_id(2) * tq
    q = _stack_heads(q_ref, groups)
    m_rows = groups * tq

    def step(c, carry, diagonal):
        m_old, l_old, acc = carry
        k0 = pl.multiple_of(c * ck, ck)
        k = k_ref[pl.ds(k0, ck), :]
        v = v_ref[pl.ds(k0, ck), :]
        s = lax.dot_general(q, k, (((1,), (1,)), ((), ())), preferred_element_type=F32)
        if diagonal:
            row_pos = q0 + _mod_pow2(lax.broadcasted_iota(jnp.int32, (m_rows, ck), 0), tq)
            col_pos = k0 + lax.broadcasted_iota(jnp.int32, (m_rows, ck), 1)
            s = jnp.where(col_pos <= row_pos, s, NEG)
        m_new = jnp.maximum(m_old, jnp.max(s, axis=-1, keepdims=True))
        alpha = jnp.exp(m_old - m_new)
        p = jnp.exp(s - m_new)
        l_new = alpha * l_old + jnp.sum(p, axis=-1, keepdims=True)
        acc = alpha * acc + jnp.dot(p.astype(BF16), v, preferred_element_type=F32)
        return m_new, l_new, acc

    last = (q0 + tq - 1) // ck
    init = (jnp.full((m_rows, 1), NEG, F32), jnp.zeros((m_rows, 1), F32),
            jnp.zeros((m_rows, LANES), F32))
    carry = lax.fori_loop(0, last, lambda c, cr: step(c, cr, False), init)
    _, l_fin, acc = step(last, carry, True)
    o = acc / l_fin
    for g in range(groups):
        o_ref[:, g * LANES:(g + 1) * LANES] = o[g * tq:(g + 1) * tq].astype(o_ref.dtype)


def _flash(q, k, v, batch, seq, groups, name):
    t, qcols = q.shape
    kv_heads = k.shape[1] // LANES
    ck = min(KV_CHUNK, seq)
    tq = min(FLASH_ROWS // groups, ck)
    assert tq & (tq - 1) == 0 and ck % tq == 0 and seq % ck == 0
    nq = seq // tq
    qspec = pl.BlockSpec((tq, groups * LANES), lambda b, h, i: (b * nq + i, h))
    kvspec = pl.BlockSpec((seq, LANES), lambda b, h, i: (b, h))
    return pl.pallas_call(
        functools.partial(_flash_kernel, groups=groups, tq=tq, ck=ck),
        grid=(batch, kv_heads, nq),
        in_specs=[qspec, kvspec, kvspec],
        out_specs=qspec,
        out_shape=jax.ShapeDtypeStruct((t, qcols), BF16),
        compiler_params=_params(("parallel", "parallel", "parallel")),
        name=name,
    )(q, k, v)


def _window_kernel(q_ref, k_ref, v_ref, o_ref, *, tq, span):
    q0 = pl.program_id(2) * tq
    q = _stack_heads(q_ref, NSA_GROUP)
    m_rows = NSA_GROUP * tq
    start = pl.multiple_of(jnp.maximum(q0 + tq - span, 0), tq)
    k = k_ref[pl.ds(start, span), :]
    v = v_ref[pl.ds(start, span), :]
    s = lax.dot_general(q, k, (((1,), (1,)), ((), ())), preferred_element_type=F32)
    row = q0 + _mod_pow2(lax.broadcasted_iota(jnp.int32, (m_rows, span), 0), tq)
    col = start + lax.broadcasted_iota(jnp.int32, (m_rows, span), 1)
    mask = (col <= row) & (col > row - WINDOW)
    sm = jnp.where(mask, s, NEG)
    e = jnp.exp(sm - jnp.max(sm, axis=-1, keepdims=True))
    p = jnp.where(mask, e / jnp.sum(e, axis=-1, keepdims=True), 0.0)
    o = jnp.dot(p.astype(BF16), v, preferred_element_type=F32)
    for g in range(NSA_GROUP):
        o_ref[:, g * LANES:(g + 1) * LANES] = o[g * tq:(g + 1) * tq].astype(o_ref.dtype)


def _window(q, k, v, batch, seq):
    t = q.shape[0]
    tq = Q_TILE
    span = min(WINDOW + tq, seq)
    nq = seq // tq
    qspec = pl.BlockSpec((tq, NSA_GROUP * LANES), lambda b, h, i: (b * nq + i, h))
    kvspec = pl.BlockSpec((seq, LANES), lambda b, h, i: (b, h))
    return pl.pallas_call(
        functools.partial(_window_kernel, tq=tq, span=span),
        grid=(batch, NSA_KV_HEADS, nq),
        in_specs=[qspec, kvspec, kvspec],
        out_specs=qspec,
        out_shape=jax.ShapeDtypeStruct((t, NSA_HEADS * LANES), BF16),
        compiler_params=_params(("parallel", "parallel", "parallel")),
        name="nsa_window",
    )(q, k, v)


def _mla_proj_kernel(cq_ref, ckv_ref, kra_ref, krb_ref, cm_ref, sm_ref, qg_ref, kg_ref,
                     wa_ref, wb_ref, wk_ref, wv_ref, q_ref, k_ref, v_ref, *, scale):
    qn = _rms(cq_ref[...], qg_ref[...]).astype(BF16)
    cn = _rms(ckv_ref[...], kg_ref[...]).astype(BF16)
    cm = cm_ref[...]
    sm = sm_ref[...]
    k_rot = kra_ref[...] * cm + krb_ref[...] * sm
    for h in range(MLA_HEADS):
        sl = slice(h * LANES, (h + 1) * LANES)
        qa = jnp.dot(qn, wa_ref[:, sl], preferred_element_type=F32)
        qb = jnp.dot(qn, wb_ref[:, sl], preferred_element_type=F32)
        q_ref[:, sl] = ((qa * cm + qb * sm) * scale).astype(q_ref.dtype)
        k_ref[:, sl] = (jnp.dot(cn, wk_ref[:, sl], preferred_element_type=F32) + k_rot).astype(k_ref.dtype)
        v_ref[:, sl] = jnp.dot(cn, wv_ref[:, sl], preferred_element_type=F32).astype(v_ref.dtype)


def _mla_proj(cq, ckv, kra, krb, cmul, smul, qg, kg, wa, wb, wk, wv):
    t = cq.shape[0]
    tm = ROW_TILE
    width = MLA_HEADS * LANES
    rows = lambda n: pl.BlockSpec((tm, n), lambda i: (i, 0))
    scale = (MLA_NOPE + MLA_ROPE) ** -0.5
    return pl.pallas_call(
        functools.partial(_mla_proj_kernel, scale=scale),
        grid=(t // tm,),
        in_specs=[rows(Q_LORA), rows(KV_LORA), rows(LANES), rows(LANES), rows(LANES), rows(LANES),
                  _const_spec((1, Q_LORA)), _const_spec((1, KV_LORA)),
                  _const_spec((Q_LORA, width)), _const_spec((Q_LORA, width)),
                  _const_spec((KV_LORA, width)), _const_spec((KV_LORA, width))],
        out_specs=[rows(width)] * 3,
        out_shape=[jax.ShapeDtypeStruct((t, width), BF16)] * 3,
        compiler_params=_params(("parallel",)),
        name="mla_proj",
    )(cq, ckv, kra, krb, cmul, smul, qg, kg, wa, wb, wk, wv)


def _ssm_kernel(u_ref, bblk_ref, a_ref, ccat_ref, d_ref, wglu_ref, bglu_ref, o_ref,
                h_sc, st_sc, *, tc, nb):
    @pl.when(pl.program_id(0) == 0)
    def _():
        st_sc[...] = jnp.zeros_like(st_sc)

    n = SSM_LANES
    u = u_ref[...].reshape(tc * nb, SSM_WIDTH)
    h_sc[...] = jnp.dot(u.astype(BF16), bblk_ref[...], preferred_element_type=F32)
    ar = jnp.broadcast_to(a_ref[0:1, :], (nb, n))
    ai = jnp.broadcast_to(a_ref[1:2, :], (nb, n))

    def step(t, carry):
        hr, hi = carry
        r0 = pl.multiple_of(t * nb, nb)
        nr = ar * hr - ai * hi + h_sc[pl.ds(r0, nb), 0:n]
        ni = ar * hi + ai * hr + h_sc[pl.ds(r0, nb), n:2 * n]
        h_sc[pl.ds(r0, nb), 0:n] = nr
        h_sc[pl.ds(r0, nb), n:2 * n] = ni
        return nr, ni

    hr, hi = lax.fori_loop(0, tc, step, (st_sc[0], st_sc[1]), unroll=4)
    st_sc[0] = hr
    st_sc[1] = hi
    y = jnp.dot(h_sc[...].astype(BF16), ccat_ref[...], preferred_element_type=F32) + d_ref[...] * u
    z = _gelu(y)
    gate = jnp.dot(z.astype(BF16), wglu_ref[...], preferred_element_type=F32) + bglu_ref[...]
    o_ref[...] = (z * _sigmoid(gate)).reshape(tc, nb, SSM_WIDTH)


def _ssm(u_t, bblk, a_rows, ccat, d_row, wglu, bglu):
    seq, nb, _ = u_t.shape
    tc = SSM_CHUNK
    n = SSM_LANES
    uspec = pl.BlockSpec((tc, nb, SSM_WIDTH), lambda i: (i, 0, 0))
    return pl.pallas_call(
        functools.partial(_ssm_kernel, tc=tc, nb=nb),
        grid=(seq // tc,),
        in_specs=[uspec, _const_spec((SSM_WIDTH, 2 * n)), _const_spec((2, n)),
                  _const_spec((2 * n, SSM_WIDTH)), _const_spec((1, SSM_WIDTH)),
                  _const_spec((SSM_WIDTH, SSM_WIDTH)), _const_spec((1, SSM_WIDTH))],
        out_specs=uspec,
        out_shape=jax.ShapeDtypeStruct(u_t.shape, F32),
        scratch_shapes=[pltpu.VMEM((tc * nb, 2 * n), F32), pltpu.VMEM((2, nb, n), F32)],
        compiler_params=_params(("arbitrary",), VMEM_LIMIT),
        name="s5_scan",
    )(u_t, bblk, a_rows, ccat, d_row, wglu, bglu)


def _outproj_kernel(oc_ref, os_ref, ow_ref, gate_ref, gb_ref, om_ref, oz_ref, x_ref,
                    gn_ref, gm_ref, gz_ref, wn_ref, wm_ref, wz_ref, o_ref):
    g = _sigmoid(gate_ref[...] + gb_ref[...])
    heads = []
    ss = 0.0
    for h in range(NSA_HEADS):
        sl = slice(h * LANES, (h + 1) * LANES)
        c = N_BRANCH * h
        o_h = (g[:, c:c + 1] * oc_ref[:, sl].astype(F32)
               + g[:, c + 1:c + 2] * os_ref[:, sl].astype(F32)
               + g[:, c + 2:c + 3] * ow_ref[:, sl].astype(F32))
        ss = ss + jnp.sum(o_h * o_h, axis=-1, keepdims=True)
        heads.append(o_h)
    inv = lax.rsqrt(ss * (1.0 / (NSA_HEADS * HEAD_DIM)) + EPS)
    acc = x_ref[...]
    for h in range(NSA_HEADS):
        sl = slice(h * LANES, (h + 1) * LANES)
        acc = acc + jnp.dot((heads[h] * inv * gn_ref[:, sl]).astype(BF16), wn_ref[sl, :],
                            preferred_element_type=F32)
    om = om_ref[...].astype(F32)
    inv = lax.rsqrt(jnp.sum(om * om, axis=-1, keepdims=True) * (1.0 / (MLA_HEADS * HEAD_DIM)) + EPS)
    acc = acc + jnp.dot((om * inv * gm_ref[...]).astype(BF16), wm_ref[...], preferred_element_type=F32)
    acc = acc + jnp.dot(_rms(oz_ref[...], gz_ref[...]).astype(BF16), wz_ref[...],
                        preferred_element_type=F32)
    o_ref[...] = acc


def _outproj(oc, osel, ow, gate, gate_b, om, oz, x, gn, gm, gz, wn, wm, wz):
    t, d = x.shape
    tm = ROW_TILE
    wide = NSA_HEADS * LANES
    rows = lambda n: pl.BlockSpec((tm, n), lambda i: (i, 0))
    return pl.pallas_call(
        _outproj_kernel,
        grid=(t // tm,),
        in_specs=[rows(wide), rows(wide), rows(wide), rows(LANES), _const_spec((1, LANES)),
                  rows(wide), rows(SSM_WIDTH), rows(d),
                  _const_spec((1, wide)), _const_spec((1, wide)), _const_spec((1, SSM_WIDTH)),
                  _const_spec((wide, d)), _const_spec((wide, d)), _const_spec((SSM_WIDTH, d))],
        out_specs=rows(d),
        out_shape=jax.ShapeDtypeStruct((t, d), F32),
        compiler_params=_params(("parallel",), VMEM_LIMIT),
        name="out_proj",
    )(oc, osel, ow, gate, gate_b, om, oz, x, gn, gm, gz, wn, wm, wz)


def _ffn_kernel(x_ref, g_ref, wg_ref, wv_ref, cw_ref, cb_ref, wd_ref, o_ref, carry_sc,
                *, tm, tiles_per_seq, cf):
    @pl.when(lax.rem(pl.program_id(0), tiles_per_seq) == 0)
    def _():
        carry_sc[...] = jnp.zeros_like(carry_sc)

    x = x_ref[...]
    h = _rms(x, g_ref[...]).astype(BF16)
    o_ref[...] = x
    row = lax.broadcasted_iota(jnp.int32, (tm, cf), 0)
    for c in range(D_FF // cf):
        sl = slice(c * cf, (c + 1) * cf)
        gate = jnp.dot(h, wg_ref[:, sl], preferred_element_type=F32)
        val = jnp.dot(h, wv_ref[:, sl], preferred_element_type=F32)
        tail = carry_sc[:, sl]
        p1 = tail[7:8, :]
        p2 = tail[6:7, :]
        g1 = jnp.where(row == 0, p1, pltpu.roll(gate, 1, 0))
        g2 = jnp.where(row == 0, p2, jnp.where(row == 1, p1, pltpu.roll(gate, 2, 0)))
        carry_sc[:, sl] = gate[tm - 8:tm, :]
        gc = cw_ref[0:1, sl] * g2 + cw_ref[1:2, sl] * g1 + cw_ref[2:3, sl] * gate + cb_ref[:, sl]
        act = gc * _sigmoid(gc) * val
        o_ref[...] += jnp.dot(act.astype(BF16), wd_ref[sl, :], preferred_element_type=F32)


def _ffn(x, g, wg, wv, cw, cb, wd, seq):
    t, d = x.shape
    tm = ROW_TILE
    rows = pl.BlockSpec((tm, d), lambda i: (i, 0))
    return pl.pallas_call(
        functools.partial(_ffn_kernel, tm=tm, tiles_per_seq=seq // tm, cf=FF_CHUNK),
        grid=(t // tm,),
        in_specs=[rows, _const_spec((1, d)), _const_spec((d, D_FF)), _const_spec((d, D_FF)),
                  _const_spec((8, D_FF)), _const_spec((1, D_FF)), _const_spec((D_FF, d))],
        out_specs=rows,
        out_shape=jax.ShapeDtypeStruct((t, d), F32),
        scratch_shapes=[pltpu.VMEM((8, D_FF), F32)],
        compiler_params=_params(("arbitrary",), VMEM_LIMIT),
        name="conv_ffn",
    )(x, g, wg, wv, cw, cb, wd)


def _final_norm_kernel(x_ref, g_ref, o_ref):
    o_ref[...] = _rms(x_ref[...], g_ref[...])


def _final_norm(x, g):
    t, d = x.shape
    rows = pl.BlockSpec((ROW_TILE, d), lambda i: (i, 0))
    return pl.pallas_call(
        _final_norm_kernel,
        grid=(t // ROW_TILE,),
        in_specs=[rows, _const_spec((1, d))],
        out_specs=rows,
        out_shape=jax.ShapeDtypeStruct((t, d), F32),
        compiler_params=_params(("parallel",)),
        name="final_norm",
    )(x, g)


def _pad_heads(w, heads, width):
    lead = w.shape[:-1]
    w = w.reshape(lead + (heads, width))
    w = jnp.pad(w, [(0, 0)] * len(lead) + [(0, 0), (0, LANES - width)])
    return w.reshape(lead + (heads * LANES,))


def _pad_head_rows(w, heads, width):
    n = w.shape[-1]
    w = w.reshape(heads, width, n)
    w = jnp.pad(w, [(0, 0), (0, LANES - width), (0, 0)])
    return w.reshape(heads * LANES, n)


def _inproj_weight(w):
    sizes = (384, 128, 128, 128, 128, 128, 128, 18, Q_LORA, KV_LORA, MLA_ROPE, SSM_WIDTH)
    offs = [0]
    for n in sizes:
        offs.append(offs[-1] + n)
    (w_q, w_kc, w_vc, w_ks, w_vs, w_kw, w_vw, w_g, w_cq, w_ckv, w_kr, w_u) = [
        w[:, a:b] for a, b in zip(offs[:-1], offs[1:])]
    d = w.shape[0]
    half = MLA_ROPE // 2
    z64 = jnp.zeros((d, HEAD_DIM), w.dtype)
    z32 = jnp.zeros((d, LANES - HEAD_DIM - MLA_ROPE), w.dtype)
    r1, r2 = w_kr[:, :half], w_kr[:, half:]
    cols = [
        _pad_heads(w_q * (HEAD_DIM ** -0.5), NSA_HEADS, HEAD_DIM),
        w_kc, w_vc,
        _pad_heads(w_ks, NSA_KV_HEADS, HEAD_DIM), _pad_heads(w_vs, NSA_KV_HEADS, HEAD_DIM),
        _pad_heads(w_kw, NSA_KV_HEADS, HEAD_DIM), _pad_heads(w_vw, NSA_KV_HEADS, HEAD_DIM),
        jnp.pad(w_g, ((0, 0), (0, LANES - w_g.shape[1]))),
        w_cq, w_ckv,
        jnp.concatenate([z64, r1, r2, z32], axis=1),
        jnp.concatenate([z64, r2, r1, z32], axis=1),
        w_u,
    ]
    return jnp.concatenate(cols, axis=1).astype(BF16)


def _mla_weights(w_uq, w_uk, w_uv):
    half = MLA_ROPE // 2
    w = w_uq.reshape(Q_LORA, MLA_HEADS, MLA_NOPE + MLA_ROPE)
    nope, r1, r2 = w[..., :MLA_NOPE], w[..., MLA_NOPE:MLA_NOPE + half], w[..., MLA_NOPE + half:]
    z32 = jnp.zeros((Q_LORA, MLA_HEADS, LANES - MLA_NOPE - MLA_ROPE), w.dtype)
    wa = jnp.concatenate([nope, r1, r2, z32], axis=-1).reshape(Q_LORA, MLA_HEADS * LANES)
    wb = jnp.concatenate([jnp.zeros_like(nope), r2, r1, z32], axis=-1).reshape(Q_LORA, MLA_HEADS * LANES)
    wk = _pad_heads(w_uk, MLA_HEADS, MLA_NOPE)
    wv = _pad_heads(w_uv, MLA_HEADS, HEAD_DIM)
    return wa.astype(BF16), wb.astype(BF16), wk.astype(BF16), wv.astype(BF16)


def _ssm_weights(log_dt, a_re, a_im, b_re, b_im, c_re, c_im, d):
    dt = jnp.exp(log_dt.astype(F32))[:, None]
    ar, ai = a_re.astype(F32), a_im.astype(F32)
    mag = jnp.exp(ar * dt)
    abr, abi = mag * jnp.cos(ai * dt), mag * jnp.sin(ai * dt)
    den = ar * ar + ai * ai
    fr = ((abr - 1.0) * ar + abi * ai) / den
    fi = (abi * ar - (abr - 1.0) * ai) / den
    br, bi = b_re.astype(F32), b_im.astype(F32)
    bbr = fr[..., None] * br - fi[..., None] * bi
    bbi = fr[..., None] * bi + fi[..., None] * br
    eye = jnp.eye(SSM_GROUPS, dtype=F32)
    blk_in = lambda m: jnp.einsum("gpc,gh->gchp", m, eye).reshape(SSM_WIDTH, SSM_LANES)
    blk_out = lambda m: jnp.einsum("gcp,gh->gphc", m, eye).reshape(SSM_LANES, SSM_WIDTH)
    bblk = jnp.concatenate([blk_in(bbr), blk_in(bbi)], axis=1)
    ccat = jnp.concatenate([blk_out(c_re.astype(F32)), -blk_out(c_im.astype(F32))], axis=0)
    a_rows = jnp.stack([abr.reshape(SSM_LANES), abi.reshape(SSM_LANES)])
    return bblk.astype(BF16), a_rows, ccat.astype(BF16), d.astype(F32).reshape(1, SSM_WIDTH)


def _overlap_t(nc_pad):
    start = jnp.arange(nc_pad) * CMP_STRIDE
    lo = jnp.arange(MAX_SEL_BLOCKS) * SEL_BLOCK
    hit = (start[None, :] < lo[:, None] + SEL_BLOCK) & (start[None, :] + CMP_BLOCK > lo[:, None])
    return hit.astype(BF16)


def _rope_rows():
    half = MLA_ROPE // 2
    inv_freq = ROPE_THETA ** (-jnp.arange(half, dtype=F32) / half)
    z64 = jnp.zeros((HEAD_DIM,), F32)
    z32 = jnp.zeros((LANES - HEAD_DIM - MLA_ROPE,), F32)
    invf = jnp.concatenate([z64, inv_freq, inv_freq, z32]).reshape(1, LANES)
    sign = jnp.concatenate([z64, -jnp.ones((half,), F32), jnp.ones((half,), F32), z32]).reshape(1, LANES)
    return invf, sign


def kernel(x, positions, attn_norm, w_in, nsa_pe, nsa_ck_w1, nsa_ck_b1, nsa_ck_w2, nsa_ck_b2, nsa_cv_w1, nsa_cv_b1, nsa_cv_w2, nsa_cv_b2, nsa_gate_b, mla_q_norm, mla_kv_norm, mla_w_uq, mla_w_uk, mla_w_uv, ssm_log_dt, ssm_a_re, ssm_a_im, ssm_b_re, ssm_b_im, ssm_c_re, ssm_c_im, ssm_d, ssm_w_glu, ssm_b_glu, out_norm_nsa, out_norm_mla, out_norm_ssm, w_out, ffn_norm, ffn_w_up, ffn_conv_w, ffn_conv_b, ffn_w_down, final_norm):
    batch, seq, d_model = x.shape
    depth = w_in.shape[0]
    t = batch * seq
    n_half = seq // CMP_STRIDE
    n_sel = seq // SEL_BLOCK
    assert seq % ROW_TILE == 0 and n_sel <= MAX_SEL_BLOCKS and n_half % LANES == 0
    n_top = min(SEL_TOP, n_sel)
    row = lambda v: v.astype(F32).reshape(1, -1)

    invf, sign = _rope_rows()
    pos_rows = jnp.broadcast_to(positions.astype(F32).reshape(t, 1), (t, LANES))
    cmul, smul = _trig(pos_rows, invf, sign)
    ovl_t = _overlap_t(n_half)

    xf = x.reshape(t, d_model)
    for l in range(depth):
        (q, kvc, ks, vs, kw, vw, gate, cq, ckv, kra, krb, u) = _inproj(
            xf, row(attn_norm[l]), _inproj_weight(w_in[l]), seq)

        a = kvc.reshape(batch, n_half, CMP_STRIDE, 4, HEAD_DIM).transpose(0, 3, 1, 2, 4)
        a = a.reshape(batch, 4, n_half, CMP_STRIDE * HEAD_DIM)
        flat = lambda w: w.reshape(CMP_BLOCK * HEAD_DIM, CMP_HIDDEN)
        pad2 = lambda w: jnp.pad(w, ((0, 0), (0, LANES - HEAD_DIM)))
        kvcmp = _compress(
            a,
            jnp.broadcast_to(nsa_pe[l].reshape(1, 1, -1), (2, 1, CMP_BLOCK * HEAD_DIM)).astype(F32),
            jnp.stack([flat(nsa_ck_w1[l]), flat(nsa_cv_w1[l])]).astype(BF16),
            jnp.stack([row(nsa_ck_b1[l]), row(nsa_cv_b1[l])]),
            jnp.stack([pad2(nsa_ck_w2[l]), pad2(nsa_cv_w2[l])]).astype(BF16),
            jnp.stack([pad2(row(nsa_ck_b2[l])), pad2(row(nsa_cv_b2[l]))]))
        o_cmp, q_aug = _cmp_topk(q, kvcmp, ovl_t, batch, seq, n_top)
        o_sel = _flash(q_aug, ks, vs, batch, seq, NSA_GROUP, "nsa_selected")
        o_win = _window(q, kw, vw, batch, seq)

        wa, wb, wk, wv = _mla_weights(mla_w_uq[l], mla_w_uk[l], mla_w_uv[l])
        q_m, k_m, v_m = _mla_proj(cq, ckv, kra, krb, cmul, smul, row(mla_q_norm[l]),
                                  row(mla_kv_norm[l]), wa, wb, wk, wv)
        o_mla = _flash(q_m, k_m, v_m, batch, seq, 1, "mla_attention")

        bblk, a_rows, ccat, d_row = _ssm_weights(
            ssm_log_dt[l], ssm_a_re[l], ssm_a_im[l], ssm_b_re[l], ssm_b_im[l],
            ssm_c_re[l], ssm_c_im[l], ssm_d[l])
        u_t = u.reshape(batch, seq, SSM_WIDTH).transpose(1, 0, 2)
        o_ssm = _ssm(u_t, bblk, a_rows, ccat, d_row, ssm_w_glu[l].astype(BF16), row(ssm_b_glu[l]))
        o_ssm = o_ssm.transpose(1, 0, 2).reshape(t, SSM_WIDTH)

        w_o = w_out[l].astype(BF16)
        n_a = NSA_HEADS * HEAD_DIM
        n_b = n_a + MLA_HEADS * HEAD_DIM
        xf = _outproj(
            o_cmp, o_sel, o_win, gate,
            jnp.pad(row(nsa_gate_b[l]), ((0, 0), (0, LANES - NSA_HEADS * N_BRANCH))),
            o_mla, o_ssm, xf,
            _pad_heads(row(out_norm_nsa[l]), NSA_HEADS, HEAD_DIM),
            _pad_heads(row(out_norm_mla[l]), MLA_HEADS, HEAD_DIM),
            row(out_norm_ssm[l]),
            _pad_head_rows(w_o[:n_a], NSA_HEADS, HEAD_DIM),
            _pad_head_rows(w_o[n_a:n_b], MLA_HEADS, HEAD_DIM),
            w_o[n_b:])

        w_up = ffn_w_up[l].astype(BF16)
        xf = _ffn(xf, row(ffn_norm[l]), w_up[:, :D_FF], w_up[:, D_FF:],
                  jnp.pad(ffn_conv_w[l].astype(F32), ((0, 8 - ffn_conv_w.shape[1]), (0, 0))),
                  row(ffn_conv_b[l]), ffn_w_down[l].astype(BF16), seq)

    return _final_norm(xf, row(final_norm)).reshape(batch, seq, d_model)
```

```python
import functools
import math

import jax
import jax.numpy as jnp
from jax import lax
from jax.experimental import pallas as pl
from jax.experimental.pallas import tpu as pltpu

F32 = jnp.float32
BF16 = jnp.bfloat16

LANES = 128
HEAD_DIM = 64
NSA_HEADS = 6
NSA_KV_HEADS = 2
NSA_GROUP = NSA_HEADS // NSA_KV_HEADS
N_BRANCH = 3
CMP_BLOCK = 32
CMP_STRIDE = 16
CMP_HIDDEN = 128
SEL_BLOCK = 64
SEL_TOP = 16
MAX_SEL_BLOCKS = 64
WINDOW = 512
MLA_HEADS = 6
MLA_NOPE = 64
MLA_ROPE = 32
Q_LORA = 384
KV_LORA = 128
ROPE_THETA = 10000.0
SSM_WIDTH = 256
SSM_GROUPS = 16
SSM_GROUP_CH = 16
SSM_STATE = 64
SSM_LANES = SSM_GROUPS * SSM_STATE
D_FF = 2816
EPS = 1e-6
NEG = -1e30

ROW_TILE = 512
Q_TILE = 128
KV_CHUNK = 512
FLASH_ROWS = 768
FLASH_CHAINS = 3
V_ROWS = 80
SSM_CHUNK = 64
FF_CHUNK = 256
VMEM_LIMIT = 56 * 1024 * 1024


def _params(sem, vmem=None):
    return pltpu.CompilerParams(dimension_semantics=sem, vmem_limit_bytes=vmem)


def _rms(x, g):
    return x * lax.rsqrt(jnp.mean(x * x, axis=-1, keepdims=True) + EPS) * g


def _gelu(x):
    c = math.sqrt(2.0 / math.pi)
    return 0.5 * x * (1.0 + jnp.tanh(c * (x + 0.044715 * (x * x * x))))


def _sigmoid(x):
    return 1.0 / (1.0 + jnp.exp(-x))


def _mod_pow2(x, n):
    assert n & (n - 1) == 0
    return jnp.bitwise_and(x, n - 1)


def _div_pow2(x, n):
    assert n & (n - 1) == 0
    return jnp.right_shift(x, n.bit_length() - 1)


def _ones_lane(shape):
    lane = _mod_pow2(lax.broadcasted_iota(jnp.int32, shape, 1), LANES)
    return jnp.where(lane == HEAD_DIM, 1.0, 0.0)


def _const_spec(shape):
    nd = len(shape)
    return pl.BlockSpec(shape, lambda *_: (0,) * nd)


def _trig_kernel(pos_ref, invf_ref, sign_ref, c_ref, s_ref):
    ang = pos_ref[...] * invf_ref[...]
    c_ref[...] = jnp.cos(ang)
    s_ref[...] = jnp.sin(ang) * sign_ref[...]


def _trig(pos_rows, invf_row, sign_row):
    t = pos_rows.shape[0]
    spec = pl.BlockSpec((ROW_TILE, LANES), lambda i: (i, 0))
    return pl.pallas_call(
        _trig_kernel,
        grid=(t // ROW_TILE,),
        in_specs=[spec, _const_spec((1, LANES)), _const_spec((1, LANES))],
        out_specs=[spec, spec],
        out_shape=[jax.ShapeDtypeStruct((t, LANES), F32)] * 2,
        compiler_params=_params(("parallel",)),
        name="rope_trig",
    )(pos_rows, invf_row, sign_row)


_IN_SEGS = (
    ("q", NSA_HEADS * LANES, BF16),
    ("kvc", 4 * HEAD_DIM, F32),
    ("ks", NSA_KV_HEADS * LANES, BF16),
    ("vs", NSA_KV_HEADS * LANES, BF16),
    ("kw", NSA_KV_HEADS * LANES, BF16),
    ("vw", NSA_KV_HEADS * LANES, BF16),
    ("gate", LANES, F32),
    ("cq", Q_LORA, F32),
    ("ckv", KV_LORA, F32),
    ("kra", LANES, F32),
    ("krb", LANES, F32),
    ("u", SSM_WIDTH, F32),
)
_IN_COLS = sum(n for _, n, _ in _IN_SEGS)


def _inproj_kernel(x_ref, g_ref, w_ref, *o_refs, seq, tm):
    h = _rms(x_ref[...], g_ref[...]).astype(BF16)
    off = 0
    for (name, n, dt), o_ref in zip(_IN_SEGS, o_refs):
        y = jnp.dot(h, w_ref[:, off:off + n], preferred_element_type=F32)
        if name == "ks":
            s0 = lax.rem(pl.program_id(0) * tm, seq)
            blk = _div_pow2(s0 + lax.broadcasted_iota(jnp.int32, (tm, n), 0), SEL_BLOCK)
            lane = _mod_pow2(lax.broadcasted_iota(jnp.int32, (tm, n), 1), LANES)
            y = y + jnp.where(lane - HEAD_DIM == blk, 1.0, 0.0)
        if name == "vs":
            y = y + _ones_lane((tm, n))
        o_ref[...] = y.astype(dt)
        off += n


def _inproj(x, g, w, seq):
    t, d = x.shape
    tm = ROW_TILE
    return pl.pallas_call(
        functools.partial(_inproj_kernel, seq=seq, tm=tm),
        grid=(t // tm,),
        in_specs=[pl.BlockSpec((tm, d), lambda i: (i, 0)), _const_spec((1, d)),
                  _const_spec((d, _IN_COLS))],
        out_specs=[pl.BlockSpec((tm, n), lambda i: (i, 0)) for _, n, _ in _IN_SEGS],
        out_shape=[jax.ShapeDtypeStruct((t, n), dt) for _, n, dt in _IN_SEGS],
        compiler_params=_params(("parallel",), VMEM_LIMIT),
        name="in_proj",
    )(x, g, w)


def _compress_kernel(a_ref, pe_ref, w1_ref, b1_ref, w2_ref, b2_ref, o_ref):
    half = CMP_STRIDE * HEAD_DIM
    a = a_ref[0, 0]
    top = jnp.dot((a + pe_ref[0, :, 0:half]).astype(BF16), w1_ref[0, 0:half, :],
                  preferred_element_type=F32)
    bot = jnp.dot((a + pe_ref[0, :, half:2 * half]).astype(BF16), w1_ref[0, half:2 * half, :],
                  preferred_element_type=F32)
    n = a.shape[0]
    hid = _gelu(top + pltpu.roll(bot, n - 1, 0) + b1_ref[0])
    o_ref[0, 0] = jnp.dot(hid.astype(BF16), w2_ref[0], preferred_element_type=F32) + b2_ref[0]


def _compress(a, pe, w1, b1, w2, b2):
    b, four, n, width = a.shape
    wspec = lambda shape: pl.BlockSpec((1,) + shape, lambda i, j: (j // NSA_KV_HEADS,) + (0,) * len(shape))
    return pl.pallas_call(
        _compress_kernel,
        grid=(b, four),
        in_specs=[pl.BlockSpec((1, 1, n, width), lambda i, j: (i, j, 0, 0)),
                  wspec((1, width * 2)), wspec((width * 2, CMP_HIDDEN)), wspec((1, CMP_HIDDEN)),
                  wspec((CMP_HIDDEN, LANES)), wspec((1, LANES))],
        out_specs=pl.BlockSpec((1, 1, n, LANES), lambda i, j: (i, j, 0, 0)),
        out_shape=jax.ShapeDtypeStruct((b, four, n, LANES), F32),
        compiler_params=_params(("parallel", "parallel")),
        name="nsa_compress",
    )(a, pe, w1, b1, w2, b2)


def _stack_heads(q_ref, groups):
    return jnp.concatenate([q_ref[:, g * LANES:(g + 1) * LANES] for g in range(groups)], axis=0)


def _cmp_topk_kernel(q_ref, kc_ref, vc_ref, ovl_ref, o_ref, qa_ref, *, tq, n_top):
    q0 = pl.program_id(2) * tq
    q = _stack_heads(q_ref, NSA_GROUP)
    m_rows = NSA_GROUP * tq
    kc = kc_ref[0, 0].astype(BF16)
    vc = vc_ref[0, 0].astype(BF16)
    nc = kc.shape[0]
    s = lax.dot_general(q, kc, (((1,), (1,)), ((), ())), preferred_element_type=F32)
    t_row = q0 + _mod_pow2(lax.broadcasted_iota(jnp.int32, (m_rows, nc), 0), tq)
    n_col = lax.broadcasted_iota(jnp.int32, (m_rows, nc), 1)
    mask = n_col * CMP_STRIDE + (CMP_BLOCK - 1) <= t_row
    sm = jnp.where(mask, s, NEG)
    e = jnp.exp(sm - jnp.max(sm, axis=-1, keepdims=True))
    p = jnp.where(mask, e / jnp.sum(e, axis=-1, keepdims=True), 0.0)
    o = jnp.dot(p.astype(BF16), vc, preferred_element_type=F32)
    for g in range(NSA_GROUP):
        o_ref[:, g * LANES:(g + 1) * LANES] = o[g * tq:(g + 1) * tq].astype(o_ref.dtype)

    psum = p[0:tq] + p[tq:2 * tq] + p[2 * tq:3 * tq]
    hi = psum.astype(BF16)
    lo = (psum - hi.astype(F32)).astype(BF16)
    nt = (((1,), (1,)), ((), ()))
    imp = (lax.dot_general(ovl_ref[...], hi, nt, preferred_element_type=F32)
           + lax.dot_general(ovl_ref[...], lo, nt, preferred_element_type=F32))
    j_idx = lax.broadcasted_iota(jnp.int32, (MAX_SEL_BLOCKS, tq), 0)
    cur = _div_pow2(q0 + lax.broadcasted_iota(jnp.int32, (MAX_SEL_BLOCKS, tq), 1), SEL_BLOCK)
    forced = (j_idx == 0) | (j_idx == cur) | (j_idx == cur - 1)
    val = jnp.where(forced, jnp.inf, jnp.where(j_idx > cur, -jnp.inf, imp))
    rank = jnp.zeros((MAX_SEL_BLOCKS, tq), F32)
    for i in range(MAX_SEL_BLOCKS):
        row = val[i:i + 1, :]
        ge = jnp.where(row >= val, 1.0, 0.0)
        gt = jnp.where(row > val, 1.0, 0.0)
        rank = rank + jnp.where(j_idx > i, ge, gt)
    bias_t = jnp.where(rank < n_top, 0.0, NEG)
    bias = jnp.concatenate([jnp.zeros_like(bias_t), bias_t], axis=0).T
    for g in range(NSA_GROUP):
        sl = slice(g * LANES, (g + 1) * LANES)
        qa_ref[:, sl] = (q_ref[:, sl].astype(F32) + bias).astype(qa_ref.dtype)


def _cmp_topk(q, kvcmp, ovl_t, batch, seq, n_top):
    t = q.shape[0]
    tq = Q_TILE
    nq = seq // tq
    nc = kvcmp.shape[2]
    qspec = pl.BlockSpec((tq, NSA_GROUP * LANES), lambda b, k, i: (b * nq + i, k))
    return pl.pallas_call(
        functools.partial(_cmp_topk_kernel, tq=tq, n_top=n_top),
        grid=(batch, NSA_KV_HEADS, nq),
        in_specs=[qspec,
                  pl.BlockSpec((1, 1, nc, LANES), lambda b, k, i: (b, k, 0, 0)),
                  pl.BlockSpec((1, 1, nc, LANES), lambda b, k, i: (b, NSA_KV_HEADS + k, 0, 0)),
                  _const_spec((MAX_SEL_BLOCKS, nc))],
        out_specs=[qspec, qspec],
        out_shape=[jax.ShapeDtypeStruct((t, NSA_HEADS * LANES), BF16)] * 2,
        compiler_params=_params(("parallel", "parallel", "parallel")),
        name="nsa_cmp_topk",
    )(q, kvcmp, kvcmp, ovl_t)


def _flash_kernel(q_ref, k_ref, vt_ref, o_ref, *, chains, groups, tq, ck):
    q0 = pl.program_id(2) * tq
    m_cols = groups * tq
    qs = [jnp.concatenate([q_ref[:, (h * groups + g) * LANES:(h * groups + g + 1) * LANES]
                           for g in range(groups)], axis=0) for h in range(chains)]

    def scores(c):
        k0 = pl.multiple_of(c * ck, ck)
        return tuple(lax.dot_general(k_ref[pl.ds(k0, ck), h * LANES:(h + 1) * LANES], qs[h],
                                     (((1,), (1,)), ((), ())), preferred_element_type=F32)
                     for h in range(chains))

    def absorb(c, s, stats):
        out = []
        for h in range(chains):
            m_old, acc = stats[h]
            m_new = jnp.maximum(m_old, jnp.max(s[h], axis=0, keepdims=True))
            alpha = jnp.exp(m_old - m_new)
            p = jnp.exp(s[h] - m_new).astype(BF16)
            vt = vt_ref[0, h, c, 0:V_ROWS, :]
            out.append((m_new, alpha * acc + jnp.dot(vt, p, preferred_element_type=F32)))
        return tuple(out)

    last = (q0 + tq - 1) // ck
    stats = tuple((jnp.full((1, m_cols), NEG, F32), jnp.zeros((V_ROWS, m_cols), F32))
                  for _ in range(chains))
    stats = lax.fori_loop(0, last, lambda c, st: absorb(c, scores(c), st), stats)
    key_pos = last * ck + lax.broadcasted_iota(jnp.int32, (ck, m_cols), 0)
    q_pos = q0 + _mod_pow2(lax.broadcasted_iota(jnp.int32, (ck, m_cols), 1), tq)
    visible = key_pos <= q_pos
    stats = absorb(last, tuple(jnp.where(visible, s, NEG) for s in scores(last)), stats)
    for h in range(chains):
        acc = stats[h][1]
        o_t = acc[0:HEAD_DIM] / acc[HEAD_DIM:HEAD_DIM + 1]
        o_t = jnp.concatenate([o_t, jnp.zeros_like(o_t)], axis=0)
        for g in range(groups):
            lane0 = (h * groups + g) * LANES
            o_ref[:, lane0:lane0 + LANES] = o_t[:, g * tq:(g + 1) * tq].T.astype(o_ref.dtype)


def _flash(q, k, v, batch, seq, groups, name):
    t, qcols = q.shape
    kv_heads = k.shape[1] // LANES
    ck = min(KV_CHUNK, seq)
    tq = min(FLASH_ROWS // groups, ck)
    assert tq & (tq - 1) == 0 and ck % tq == 0 and seq % ck == 0
    nq = seq // tq
    n_chunks = seq // ck
    v_t = v.reshape(batch, n_chunks, ck, kv_heads, LANES).transpose(0, 3, 1, 4, 2)
    chains = max(c for c in range(1, FLASH_CHAINS + 1) if kv_heads % c == 0)
    qspec = pl.BlockSpec((tq, chains * groups * LANES), lambda b, h, i: (b * nq + i, h))
    return pl.pallas_call(
        functools.partial(_flash_kernel, chains=chains, groups=groups, tq=tq, ck=ck),
        grid=(batch, kv_heads // chains, nq),
        in_specs=[qspec, pl.BlockSpec((seq, chains * LANES), lambda b, h, i: (b, h)),
                  pl.BlockSpec((1, chains, n_chunks, LANES, ck), lambda b, h, i: (b, h, 0, 0, 0))],
        out_specs=qspec,
        out_shape=jax.ShapeDtypeStruct((t, qcols), BF16),
        compiler_params=_params(("parallel", "parallel", "parallel")),
        name=name,
    )(q, k, v_t)


def _window_kernel(q_ref, k_ref, v_ref, o_ref, *, tq, span):
    q0 = pl.program_id(2) * tq
    q = _stack_heads(q_ref, NSA_GROUP)
    m_rows = NSA_GROUP * tq
    start = pl.multiple_of(jnp.maximum(q0 + tq - span, 0), tq)
    k = k_ref[pl.ds(start, span), :]
    v = v_ref[pl.ds(start, span), :]
    s = lax.dot_general(q, k, (((1,), (1,)), ((), ())), preferred_element_type=F32)
    row = q0 + _mod_pow2(lax.broadcasted_iota(jnp.int32, (m_rows, span), 0), tq)
    col = start + lax.broadcasted_iota(jnp.int32, (m_rows, span), 1)
    mask = (col <= row) & (col > row - WINDOW)
    sm = jnp.where(mask, s, NEG)
    e = jnp.exp(sm - jnp.max(sm, axis=-1, keepdims=True))
    p = jnp.where(mask, e / jnp.sum(e, axis=-1, keepdims=True), 0.0)
    o = jnp.dot(p.astype(BF16), v, preferred_element_type=F32)
    for g in range(NSA_GROUP):
        o_ref[:, g * LANES:(g + 1) * LANES] = o[g * tq:(g + 1) * tq].astype(o_ref.dtype)


def _window(q, k, v, batch, seq):
    t = q.shape[0]
    tq = Q_TILE
    span = min(WINDOW + tq, seq)
    nq = seq // tq
    qspec = pl.BlockSpec((tq, NSA_GROUP * LANES), lambda b, h, i: (b * nq + i, h))
    kvspec = pl.BlockSpec((seq, LANES), lambda b, h, i: (b, h))
    return pl.pallas_call(
        functools.partial(_window_kernel, tq=tq, span=span),
        grid=(batch, NSA_KV_HEADS, nq),
        in_specs=[qspec, kvspec, kvspec],
        out_specs=qspec,
        out_shape=jax.ShapeDtypeStruct((t, NSA_HEADS * LANES), BF16),
        compiler_params=_params(("parallel", "parallel", "parallel")),
        name="nsa_window",
    )(q, k, v)


def _mla_proj_kernel(cq_ref, ckv_ref, kra_ref, krb_ref, cm_ref, sm_ref, qg_ref, kg_ref,
                     wa_ref, wb_ref, wk_ref, wv_ref, q_ref, k_ref, v_ref, *, scale):
    qn = _rms(cq_ref[...], qg_ref[...]).astype(BF16)
    cn = _rms(ckv_ref[...], kg_ref[...]).astype(BF16)
    cm = cm_ref[...]
    sm = sm_ref[...]
    k_rot = kra_ref[...] * cm + krb_ref[...] * sm
    for h in range(MLA_HEADS):
        sl = slice(h * LANES, (h + 1) * LANES)
        qa = jnp.dot(qn, wa_ref[:, sl], preferred_element_type=F32)
        qb = jnp.dot(qn, wb_ref[:, sl], preferred_element_type=F32)
        q_ref[:, sl] = ((qa * cm + qb * sm) * scale).astype(q_ref.dtype)
        k_ref[:, sl] = (jnp.dot(cn, wk_ref[:, sl], preferred_element_type=F32) + k_rot).astype(k_ref.dtype)
        v_ref[:, sl] = (jnp.dot(cn, wv_ref[:, sl], preferred_element_type=F32)
                        + _ones_lane((cn.shape[0], LANES))).astype(v_ref.dtype)


def _mla_proj(cq, ckv, kra, krb, cmul, smul, qg, kg, wa, wb, wk, wv):
    t = cq.shape[0]
    tm = ROW_TILE
    width = MLA_HEADS * LANES
    rows = lambda n: pl.BlockSpec((tm, n), lambda i: (i, 0))
    scale = (MLA_NOPE + MLA_ROPE) ** -0.5
    return pl.pallas_call(
        functools.partial(_mla_proj_kernel, scale=scale),
        grid=(t // tm,),
        in_specs=[rows(Q_LORA), rows(KV_LORA), rows(LANES), rows(LANES), rows(LANES), rows(LANES),
                  _const_spec((1, Q_LORA)), _const_spec((1, KV_LORA)),
                  _const_spec((Q_LORA, width)), _const_spec((Q_LORA, width)),
                  _const_spec((KV_LORA, width)), _const_spec((KV_LORA, width))],
        out_specs=[rows(width)] * 3,
        out_shape=[jax.ShapeDtypeStruct((t, width), BF16)] * 3,
        compiler_params=_params(("parallel",)),
        name="mla_proj",
    )(cq, ckv, kra, krb, cmul, smul, qg, kg, wa, wb, wk, wv)


def _ssm_kernel(u_ref, bblk_ref, a_ref, ccat_ref, d_ref, wglu_ref, bglu_ref, o_ref,
                h_sc, st_sc, *, tc, nb):
    @pl.when(pl.program_id(0) == 0)
    def _():
        st_sc[...] = jnp.zeros_like(st_sc)

    n = SSM_LANES
    u = u_ref[...].reshape(tc * nb, SSM_WIDTH)
    h_sc[...] = jnp.dot(u.astype(BF16), bblk_ref[...], preferred_element_type=F32)
    ar = jnp.broadcast_to(a_ref[0:1, :], (nb, n))
    ai = jnp.broadcast_to(a_ref[1:2, :], (nb, n))

    def step(t, carry):
        hr, hi = carry
        r0 = pl.multiple_of(t * nb, nb)
        nr = ar * hr - ai * hi + h_sc[pl.ds(r0, nb), 0:n]
        ni = ar * hi + ai * hr + h_sc[pl.ds(r0, nb), n:2 * n]
        h_sc[pl.ds(r0, nb), 0:n] = nr
        h_sc[pl.ds(r0, nb), n:2 * n] = ni
        return nr, ni

    hr, hi = lax.fori_loop(0, tc, step, (st_sc[0], st_sc[1]), unroll=4)
    st_sc[0] = hr
    st_sc[1] = hi
    y = jnp.dot(h_sc[...].astype(BF16), ccat_ref[...], preferred_element_type=F32) + d_ref[...] * u
    z = _gelu(y)
    gate = jnp.dot(z.astype(BF16), wglu_ref[...], preferred_element_type=F32) + bglu_ref[...]
    o_ref[...] = (z * _sigmoid(gate)).reshape(tc, nb, SSM_WIDTH)


def _ssm(u_t, bblk, a_rows, ccat, d_row, wglu, bglu):
    seq, nb, _ = u_t.shape
    tc = SSM_CHUNK
    n = SSM_LANES
    uspec = pl.BlockSpec((tc, nb, SSM_WIDTH), lambda i: (i, 0, 0))
    return pl.pallas_call(
        functools.partial(_ssm_kernel, tc=tc, nb=nb),
        grid=(seq // tc,),
        in_specs=[uspec, _const_spec((SSM_WIDTH, 2 * n)), _const_spec((2, n)),
                  _const_spec((2 * n, SSM_WIDTH)), _const_spec((1, SSM_WIDTH)),
                  _const_spec((SSM_WIDTH, SSM_WIDTH)), _const_spec((1, SSM_WIDTH))],
        out_specs=uspec,
        out_shape=jax.ShapeDtypeStruct(u_t.shape, F32),
        scratch_shapes=[pltpu.VMEM((tc * nb, 2 * n), F32), pltpu.VMEM((2, nb, n), F32)],
        compiler_params=_params(("arbitrary",), VMEM_LIMIT),
        name="s5_scan",
    )(u_t, bblk, a_rows, ccat, d_row, wglu, bglu)


def _outproj_kernel(oc_ref, os_ref, ow_ref, gate_ref, gb_ref, om_ref, oz_ref, x_ref,
                    gn_ref, gm_ref, gz_ref, wn_ref, wm_ref, wz_ref, o_ref):
    g = _sigmoid(gate_ref[...] + gb_ref[...])
    heads = []
    ss = 0.0
    for h in range(NSA_HEADS):
        sl = slice(h * LANES, (h + 1) * LANES)
        c = N_BRANCH * h
        o_h = (g[:, c:c + 1] * oc_ref[:, sl].astype(F32)
               + g[:, c + 1:c + 2] * os_ref[:, sl].astype(F32)
               + g[:, c + 2:c + 3] * ow_ref[:, sl].astype(F32))
        ss = ss + jnp.sum(o_h * o_h, axis=-1, keepdims=True)
        heads.append(o_h)
    inv = lax.rsqrt(ss * (1.0 / (NSA_HEADS * HEAD_DIM)) + EPS)
    acc = x_ref[...]
    for h in range(NSA_HEADS):
        sl = slice(h * LANES, (h + 1) * LANES)
        acc = acc + jnp.dot((heads[h] * inv * gn_ref[:, sl]).astype(BF16), wn_ref[sl, :],
                            preferred_element_type=F32)
    om = om_ref[...].astype(F32)
    inv = lax.rsqrt(jnp.sum(om * om, axis=-1, keepdims=True) * (1.0 / (MLA_HEADS * HEAD_DIM)) + EPS)
    acc = acc + jnp.dot((om * inv * gm_ref[...]).astype(BF16), wm_ref[...], preferred_element_type=F32)
    acc = acc + jnp.dot(_rms(oz_ref[...], gz_ref[...]).astype(BF16), wz_ref[...],
                        preferred_element_type=F32)
    o_ref[...] = acc


def _outproj(oc, osel, ow, gate, gate_b, om, oz, x, gn, gm, gz, wn, wm, wz):
    t, d = x.shape
    tm = ROW_TILE
    wide = NSA_HEADS * LANES
    rows = lambda n: pl.BlockSpec((tm, n), lambda i: (i, 0))
    return pl.pallas_call(
        _outproj_kernel,
        grid=(t // tm,),
        in_specs=[rows(wide), rows(wide), rows(wide), rows(LANES), _const_spec((1, LANES)),
                  rows(wide), rows(SSM_WIDTH), rows(d),
                  _const_spec((1, wide)), _const_spec((1, wide)), _const_spec((1, SSM_WIDTH)),
                  _const_spec((wide, d)), _const_spec((wide, d)), _const_spec((SSM_WIDTH, d))],
        out_specs=rows(d),
        out_shape=jax.ShapeDtypeStruct((t, d), F32),
        compiler_params=_params(("parallel",), VMEM_LIMIT),
        name="out_proj",
    )(oc, osel, ow, gate, gate_b, om, oz, x, gn, gm, gz, wn, wm, wz)


def _ffn_kernel(x_ref, g_ref, wg_ref, wv_ref, cw_ref, cb_ref, wd_ref, o_ref, carry_sc,
                *, tm, tiles_per_seq, cf):
    @pl.when(lax.rem(pl.program_id(0), tiles_per_seq) == 0)
    def _():
        carry_sc[...] = jnp.zeros_like(carry_sc)

    x = x_ref[...]
    h = _rms(x, g_ref[...]).astype(BF16)
    o_ref[...] = x
    row = lax.broadcasted_iota(jnp.int32, (tm, cf), 0)
    for c in range(D_FF // cf):
        sl = slice(c * cf, (c + 1) * cf)
        gate = jnp.dot(h, wg_ref[:, sl], preferred_element_type=F32)
        val = jnp.dot(h, wv_ref[:, sl], preferred_element_type=F32)
        tail = carry_sc[:, sl]
        p1 = tail[7:8, :]
        p2 = tail[6:7, :]
        g1 = jnp.where(row == 0, p1, pltpu.roll(gate, 1, 0))
        g2 = jnp.where(row == 0, p2, jnp.where(row == 1, p1, pltpu.roll(gate, 2, 0)))
        carry_sc[:, sl] = gate[tm - 8:tm, :]
        gc = cw_ref[0:1, sl] * g2 + cw_ref[1:2, sl] * g1 + cw_ref[2:3, sl] * gate + cb_ref[:, sl]
        act = gc * _sigmoid(gc) * val
        o_ref[...] += jnp.dot(act.astype(BF16), wd_ref[sl, :], preferred_element_type=F32)


def _ffn(x, g, wg, wv, cw, cb, wd, seq):
    t, d = x.shape
    tm = ROW_TILE
    rows = pl.BlockSpec((tm, d), lambda i: (i, 0))
    return pl.pallas_call(
        functools.partial(_ffn_kernel, tm=tm, tiles_per_seq=seq // tm, cf=FF_CHUNK),
        grid=(t // tm,),
        in_specs=[rows, _const_spec((1, d)), _const_spec((d, D_FF)), _const_spec((d, D_FF)),
                  _const_spec((8, D_FF)), _const_spec((1, D_FF)), _const_spec((D_FF, d))],
        out_specs=rows,
        out_shape=jax.ShapeDtypeStruct((t, d), F32),
        scratch_shapes=[pltpu.VMEM((8, D_FF), F32)],
        compiler_params=_params(("arbitrary",), VMEM_LIMIT),
        name="conv_ffn",
    )(x, g, wg, wv, cw, cb, wd)


def _final_norm_kernel(x_ref, g_ref, o_ref):
    o_ref[...] = _rms(x_ref[...], g_ref[...])


def _final_norm(x, g):
    t, d = x.shape
    rows = pl.BlockSpec((ROW_TILE, d), lambda i: (i, 0))
    return pl.pallas_call(
        _final_norm_kernel,
        grid=(t // ROW_TILE,),
        in_specs=[rows, _const_spec((1, d))],
        out_specs=rows,
        out_shape=jax.ShapeDtypeStruct((t, d), F32),
        compiler_params=_params(("parallel",)),
        name="final_norm",
    )(x, g)


def _pad_heads(w, heads, width):
    lead = w.shape[:-1]
    w = w.reshape(lead + (heads, width))
    w = jnp.pad(w, [(0, 0)] * len(lead) + [(0, 0), (0, LANES - width)])
    return w.reshape(lead + (heads * LANES,))


def _pad_head_rows(w, heads, width):
    n = w.shape[-1]
    w = w.reshape(heads, width, n)
    w = jnp.pad(w, [(0, 0), (0, LANES - width), (0, 0)])
    return w.reshape(heads * LANES, n)


def _inproj_weight(w):
    sizes = (384, 128, 128, 128, 128, 128, 128, 18, Q_LORA, KV_LORA, MLA_ROPE, SSM_WIDTH)
    offs = [0]
    for n in sizes:
        offs.append(offs[-1] + n)
    (w_q, w_kc, w_vc, w_ks, w_vs, w_kw, w_vw, w_g, w_cq, w_ckv, w_kr, w_u) = [
        w[:, a:b] for a, b in zip(offs[:-1], offs[1:])]
    d = w.shape[0]
    half = MLA_ROPE // 2
    z64 = jnp.zeros((d, HEAD_DIM), w.dtype)
    z32 = jnp.zeros((d, LANES - HEAD_DIM - MLA_ROPE), w.dtype)
    r1, r2 = w_kr[:, :half], w_kr[:, half:]
    cols = [
        _pad_heads(w_q * (HEAD_DIM ** -0.5), NSA_HEADS, HEAD_DIM),
        w_kc, w_vc,
        _pad_heads(w_ks, NSA_KV_HEADS, HEAD_DIM), _pad_heads(w_vs, NSA_KV_HEADS, HEAD_DIM),
        _pad_heads(w_kw, NSA_KV_HEADS, HEAD_DIM), _pad_heads(w_vw, NSA_KV_HEADS, HEAD_DIM),
        jnp.pad(w_g, ((0, 0), (0, LANES - w_g.shape[1]))),
        w_cq, w_ckv,
        jnp.concatenate([z64, r1, r2, z32], axis=1),
        jnp.concatenate([z64, r2, r1, z32], axis=1),
        w_u,
    ]
    return jnp.concatenate(cols, axis=1).astype(BF16)


def _mla_weights(w_uq, w_uk, w_uv):
    half = MLA_ROPE // 2
    w = w_uq.reshape(Q_LORA, MLA_HEADS, MLA_NOPE + MLA_ROPE)
    nope, r1, r2 = w[..., :MLA_NOPE], w[..., MLA_NOPE:MLA_NOPE + half], w[..., MLA_NOPE + half:]
    z32 = jnp.zeros((Q_LORA, MLA_HEADS, LANES - MLA_NOPE - MLA_ROPE), w.dtype)
    wa = jnp.concatenate([nope, r1, r2, z32], axis=-1).reshape(Q_LORA, MLA_HEADS * LANES)
    wb = jnp.concatenate([jnp.zeros_like(nope), r2, r1, z32], axis=-1).reshape(Q_LORA, MLA_HEADS * LANES)
    wk = _pad_heads(w_uk, MLA_HEADS, MLA_NOPE)
    wv = _pad_heads(w_uv, MLA_HEADS, HEAD_DIM)
    return wa.astype(BF16), wb.astype(BF16), wk.astype(BF16), wv.astype(BF16)


def _ssm_weights(log_dt, a_re, a_im, b_re, b_im, c_re, c_im, d):
    dt = jnp.exp(log_dt.astype(F32))[:, None]
    ar, ai = a_re.astype(F32), a_im.astype(F32)
    mag = jnp.exp(ar * dt)
    abr, abi = mag * jnp.cos(ai * dt), mag * jnp.sin(ai * dt)
    den = ar * ar + ai * ai
    fr = ((abr - 1.0) * ar + abi * ai) / den
    fi = (abi * ar - (abr - 1.0) * ai) / den
    br, bi = b_re.astype(F32), b_im.astype(F32)
    bbr = fr[..., None] * br - fi[..., None] * bi
    bbi = fr[..., None] * bi + fi[..., None] * br
    eye = jnp.eye(SSM_GROUPS, dtype=F32)
    blk_in = lambda m: jnp.einsum("gpc,gh->gchp", m, eye).reshape(SSM_WIDTH, SSM_LANES)
    blk_out = lambda m: jnp.einsum("gcp,gh->gphc", m, eye).reshape(SSM_LANES, SSM_WIDTH)
    bblk = jnp.concatenate([blk_in(bbr), blk_in(bbi)], axis=1)
    ccat = jnp.concatenate([blk_out(c_re.astype(F32)), -blk_out(c_im.astype(F32))], axis=0)
    a_rows = jnp.stack([abr.reshape(SSM_LANES), abi.reshape(SSM_LANES)])
    return bblk.astype(BF16), a_rows, ccat.astype(BF16), d.astype(F32).reshape(1, SSM_WIDTH)


def _overlap_t(nc_pad):
    start = jnp.arange(nc_pad) * CMP_STRIDE
    lo = jnp.arange(MAX_SEL_BLOCKS) * SEL_BLOCK
    hit = (start[None, :] < lo[:, None] + SEL_BLOCK) & (start[None, :] + CMP_BLOCK > lo[:, None])
    return hit.astype(BF16)


def _rope_rows():
    half = MLA_ROPE // 2
    inv_freq = ROPE_THETA ** (-jnp.arange(half, dtype=F32) / half)
    z64 = jnp.zeros((HEAD_DIM,), F32)
    z32 = jnp.zeros((LANES - HEAD_DIM - MLA_ROPE,), F32)
    invf = jnp.concatenate([z64, inv_freq, inv_freq, z32]).reshape(1, LANES)
    sign = jnp.concatenate([z64, -jnp.ones((half,), F32), jnp.ones((half,), F32), z32]).reshape(1, LANES)
    return invf, sign


def kernel(x, positions, attn_norm, w_in, nsa_pe, nsa_ck_w1, nsa_ck_b1, nsa_ck_w2, nsa_ck_b2, nsa_cv_w1, nsa_cv_b1, nsa_cv_w2, nsa_cv_b2, nsa_gate_b, mla_q_norm, mla_kv_norm, mla_w_uq, mla_w_uk, mla_w_uv, ssm_log_dt, ssm_a_re, ssm_a_im, ssm_b_re, ssm_b_im, ssm_c_re, ssm_c_im, ssm_d, ssm_w_glu, ssm_b_glu, out_norm_nsa, out_norm_mla, out_norm_ssm, w_out, ffn_norm, ffn_w_up, ffn_conv_w, ffn_conv_b, ffn_w_down, final_norm):
    batch, seq, d_model = x.shape
    depth = w_in.shape[0]
    t = batch * seq
    n_half = seq // CMP_STRIDE
    n_sel = seq // SEL_BLOCK
    assert seq % ROW_TILE == 0 and n_sel <= MAX_SEL_BLOCKS and n_half % LANES == 0
    n_top = min(SEL_TOP, n_sel)
    row = lambda v: v.astype(F32).reshape(1, -1)

    invf, sign = _rope_rows()
    pos_rows = jnp.broadcast_to(positions.astype(F32).reshape(t, 1), (t, LANES))
    cmul, smul = _trig(pos_rows, invf, sign)
    ovl_t = _overlap_t(n_half)

    xf = x.reshape(t, d_model)
    for l in range(depth):
        (q, kvc, ks, vs, kw, vw, gate, cq, ckv, kra, krb, u) = _inproj(
            xf, row(attn_norm[l]), _inproj_weight(w_in[l]), seq)

        a = kvc.reshape(batch, n_half, CMP_STRIDE, 4, HEAD_DIM).transpose(0, 3, 1, 2, 4)
        a = a.reshape(batch, 4, n_half, CMP_STRIDE * HEAD_DIM)
        flat = lambda w: w.reshape(CMP_BLOCK * HEAD_DIM, CMP_HIDDEN)
        pad2 = lambda w: jnp.pad(w, ((0, 0), (0, LANES - HEAD_DIM)))
        kvcmp = _compress(
            a,
            jnp.broadcast_to(nsa_pe[l].reshape(1, 1, -1), (2, 1, CMP_BLOCK * HEAD_DIM)).astype(F32),
            jnp.stack([flat(nsa_ck_w1[l]), flat(nsa_cv_w1[l])]).astype(BF16),
            jnp.stack([row(nsa_ck_b1[l]), row(nsa_cv_b1[l])]),
            jnp.stack([pad2(nsa_ck_w2[l]), pad2(nsa_cv_w2[l])]).astype(BF16),
            jnp.stack([pad2(row(nsa_ck_b2[l])), pad2(row(nsa_cv_b2[l]))]))
        o_cmp, q_aug = _cmp_topk(q, kvcmp, ovl_t, batch, seq, n_top)
        o_sel = _flash(q_aug, ks, vs, batch, seq, NSA_GROUP, "nsa_selected")
        o_win = _window(q, kw, vw, batch, seq)

        wa, wb, wk, wv = _mla_weights(mla_w_uq[l], mla_w_uk[l], mla_w_uv[l])
        q_m, k_m, v_m = _mla_proj(cq, ckv, kra, krb, cmul, smul, row(mla_q_norm[l]),
                                  row(mla_kv_norm[l]), wa, wb, wk, wv)
        o_mla = _flash(q_m, k_m, v_m, batch, seq, 1, "mla_attention")

        bblk, a_rows, ccat, d_row = _ssm_weights(
            ssm_log_dt[l], ssm_a_re[l], ssm_a_im[l], ssm_b_re[l], ssm_b_im[l],
            ssm_c_re[l], ssm_c_im[l], ssm_d[l])
        u_t = u.reshape(batch, seq, SSM_WIDTH).transpose(1, 0, 2)
        o_ssm = _ssm(u_t, bblk, a_rows, ccat, d_row, ssm_w_glu[l].astype(BF16), row(ssm_b_glu[l]))
        o_ssm = o_ssm.transpose(1, 0, 2).reshape(t, SSM_WIDTH)

        w_o = w_out[l].astype(BF16)
        n_a = NSA_HEADS * HEAD_DIM
        n_b = n_a + MLA_HEADS * HEAD_DIM
        xf = _outproj(
            o_cmp, o_sel, o_win, gate,
            jnp.pad(row(nsa_gate_b[l]), ((0, 0), (0, LANES - NSA_HEADS * N_BRANCH))),
            o_mla, o_ssm, xf,
            _pad_heads(row(out_norm_nsa[l]), NSA_HEADS, HEAD_DIM),
            _pad_heads(row(out_norm_mla[l]), MLA_HEADS, HEAD_DIM),
            row(out_norm_ssm[l]),
            _pad_head_rows(w_o[:n_a], NSA_HEADS, HEAD_DIM),
            _pad_head_rows(w_o[n_a:n_b], MLA_HEADS, HEAD_DIM),
            w_o[n_b:])

        w_up = ffn_w_up[l].astype(BF16)
        xf = _ffn(xf, row(ffn_norm[l]), w_up[:, :D_FF], w_up[:, D_FF:],
                  jnp.pad(ffn_conv_w[l].astype(F32), ((0, 8 - ffn_conv_w.shape[1]), (0, 0))),
                  row(ffn_conv_b[l]), ffn_w_down[l].astype(BF16), seq)

    return _final_norm(xf, row(final_norm)).reshape(batch, seq, d_model)
```

```python
import functools
import math

import jax
import jax.numpy as jnp
from jax import lax
from jax.experimental import pallas as pl
from jax.experimental.pallas import tpu as pltpu

F32 = jnp.float32
BF16 = jnp.bfloat16

LANES = 128
HEAD_DIM = 64
NSA_HEADS = 6
NSA_KV_HEADS = 2
NSA_GROUP = NSA_HEADS // NSA_KV_HEADS
N_BRANCH = 3
CMP_BLOCK = 32
CMP_STRIDE = 16
CMP_HIDDEN = 128
SEL_BLOCK = 64
SEL_TOP = 16
MAX_SEL_BLOCKS = 64
WINDOW = 512
MLA_HEADS = 6
MLA_NOPE = 64
MLA_ROPE = 32
Q_LORA = 384
KV_LORA = 128
ROPE_THETA = 10000.0
SSM_WIDTH = 256
SSM_GROUPS = 16
SSM_GROUP_CH = 16
SSM_STATE = 64
SSM_LANES = SSM_GROUPS * SSM_STATE
D_FF = 2816
EPS = 1e-6
NEG = -1e30

ROW_TILE = 512
ATTN_TILE = 256
KV_CHUNK = 512
FLASH_ROWS = 768
FLASH_CHAINS = 3
V_ROWS = 80
SSM_CHUNK = 64
FF_CHUNK = 256
VMEM_LIMIT = 56 * 1024 * 1024


def _params(sem, vmem=None):
    return pltpu.CompilerParams(dimension_semantics=sem, vmem_limit_bytes=vmem)


def _rms(x, g):
    return x * lax.rsqrt(jnp.mean(x * x, axis=-1, keepdims=True) + EPS) * g


def _gelu(x):
    c = math.sqrt(2.0 / math.pi)
    return 0.5 * x * (1.0 + jnp.tanh(c * (x + 0.044715 * (x * x * x))))


def _sigmoid(x):
    return 1.0 / (1.0 + jnp.exp(-x))


def _mod_pow2(x, n):
    assert n & (n - 1) == 0
    return jnp.bitwise_and(x, n - 1)


def _div_pow2(x, n):
    assert n & (n - 1) == 0
    return jnp.right_shift(x, n.bit_length() - 1)


def _ones_lane(shape):
    lane = _mod_pow2(lax.broadcasted_iota(jnp.int32, shape, 1), LANES)
    return jnp.where(lane == HEAD_DIM, 1.0, 0.0)


def _const_spec(shape):
    nd = len(shape)
    return pl.BlockSpec(shape, lambda *_: (0,) * nd)


def _trig_kernel(pos_ref, invf_ref, sign_ref, c_ref, s_ref):
    ang = pos_ref[...] * invf_ref[...]
    c_ref[...] = jnp.cos(ang)
    s_ref[...] = jnp.sin(ang) * sign_ref[...]


def _trig(pos_rows, invf_row, sign_row):
    t = pos_rows.shape[0]
    spec = pl.BlockSpec((ROW_TILE, LANES), lambda i: (i, 0))
    return pl.pallas_call(
        _trig_kernel,
        grid=(t // ROW_TILE,),
        in_specs=[spec, _const_spec((1, LANES)), _const_spec((1, LANES))],
        out_specs=[spec, spec],
        out_shape=[jax.ShapeDtypeStruct((t, LANES), F32)] * 2,
        compiler_params=_params(("parallel",)),
        name="rope_trig",
    )(pos_rows, invf_row, sign_row)


_IN_SEGS = (
    ("q", NSA_HEADS * LANES, BF16),
    ("kvc", 4 * HEAD_DIM, F32),
    ("ks", NSA_KV_HEADS * LANES, BF16),
    ("vs", NSA_KV_HEADS * LANES, BF16),
    ("kw", NSA_KV_HEADS * LANES, BF16),
    ("vw", NSA_KV_HEADS * LANES, BF16),
    ("gate", LANES, F32),
    ("cq", Q_LORA, F32),
    ("ckv", KV_LORA, F32),
    ("kra", LANES, F32),
    ("krb", LANES, F32),
    ("u", SSM_WIDTH, F32),
)
_IN_COLS = sum(n for _, n, _ in _IN_SEGS)


def _inproj_kernel(x_ref, g_ref, w_ref, *o_refs, seq, tm):
    h = _rms(x_ref[...], g_ref[...]).astype(BF16)
    off = 0
    for (name, n, dt), o_ref in zip(_IN_SEGS, o_refs):
        y = jnp.dot(h, w_ref[:, off:off + n], preferred_element_type=F32)
        if name == "ks":
            s0 = lax.rem(pl.program_id(0) * tm, seq)
            blk = _div_pow2(s0 + lax.broadcasted_iota(jnp.int32, (tm, n), 0), SEL_BLOCK)
            lane = _mod_pow2(lax.broadcasted_iota(jnp.int32, (tm, n), 1), LANES)
            y = y + jnp.where(lane - HEAD_DIM == blk, 1.0, 0.0)
        if name in ("vs", "vw"):
            y = y + _ones_lane((tm, n))
        o_ref[...] = y.astype(dt)
        off += n


def _inproj(x, g, w, seq):
    t, d = x.shape
    tm = ROW_TILE
    return pl.pallas_call(
        functools.partial(_inproj_kernel, seq=seq, tm=tm),
        grid=(t // tm,),
        in_specs=[pl.BlockSpec((tm, d), lambda i: (i, 0)), _const_spec((1, d)),
                  _const_spec((d, _IN_COLS))],
        out_specs=[pl.BlockSpec((tm, n), lambda i: (i, 0)) for _, n, _ in _IN_SEGS],
        out_shape=[jax.ShapeDtypeStruct((t, n), dt) for _, n, dt in _IN_SEGS],
        compiler_params=_params(("parallel",), VMEM_LIMIT),
        name="in_proj",
    )(x, g, w)


def _compress_kernel(a_ref, pe_ref, w1_ref, b1_ref, w2_ref, b2_ref, o_ref):
    half = CMP_STRIDE * HEAD_DIM
    a = a_ref[0, 0]
    top = jnp.dot((a + pe_ref[0, :, 0:half]).astype(BF16), w1_ref[0, 0:half, :],
                  preferred_element_type=F32)
    bot = jnp.dot((a + pe_ref[0, :, half:2 * half]).astype(BF16), w1_ref[0, half:2 * half, :],
                  preferred_element_type=F32)
    n = a.shape[0]
    hid = _gelu(top + pltpu.roll(bot, n - 1, 0) + b1_ref[0])
    o_ref[0, 0] = jnp.dot(hid.astype(BF16), w2_ref[0], preferred_element_type=F32) + b2_ref[0]


def _compress(a, pe, w1, b1, w2, b2):
    b, four, n, width = a.shape
    wspec = lambda shape: pl.BlockSpec((1,) + shape, lambda i, j: (j // NSA_KV_HEADS,) + (0,) * len(shape))
    return pl.pallas_call(
        _compress_kernel,
        grid=(b, four),
        in_specs=[pl.BlockSpec((1, 1, n, width), lambda i, j: (i, j, 0, 0)),
                  wspec((1, width * 2)), wspec((width * 2, CMP_HIDDEN)), wspec((1, CMP_HIDDEN)),
                  wspec((CMP_HIDDEN, LANES)), wspec((1, LANES))],
        out_specs=pl.BlockSpec((1, 1, n, LANES), lambda i, j: (i, j, 0, 0)),
        out_shape=jax.ShapeDtypeStruct((b, four, n, LANES), F32),
        compiler_params=_params(("parallel", "parallel")),
        name="nsa_compress",
    )(a, pe, w1, b1, w2, b2)


def _stack_group(q_ref, h, groups):
    return jnp.concatenate([q_ref[:, (h * groups + g) * LANES:(h * groups + g + 1) * LANES]
                            for g in range(groups)], axis=0)


def _store_heads_t(o_ref, o_t, h, groups, tq):
    o_t = jnp.concatenate([o_t, jnp.zeros_like(o_t)], axis=0)
    for g in range(groups):
        lane0 = (h * groups + g) * LANES
        o_ref[:, lane0:lane0 + LANES] = o_t[:, g * tq:(g + 1) * tq].T.astype(o_ref.dtype)


def _stable_rank(val):
    sub = 8
    n_blk = val.shape[0] // sub
    blocks = [val[r * sub:(r + 1) * sub] for r in range(n_blk)]
    ranks = [jnp.zeros(blocks[0].shape, F32) for _ in range(n_blk)]
    row_in = lax.broadcasted_iota(jnp.int32, blocks[0].shape, 0)
    for i in range(val.shape[0]):
        row = val[i:i + 1, :]
        for r in range(n_blk):
            if r * sub > i:
                ahead = jnp.where(row >= blocks[r], 1.0, 0.0)
            elif (r + 1) * sub <= i:
                ahead = jnp.where(row > blocks[r], 1.0, 0.0)
            else:
                ahead = jnp.where(row_in + r * sub > i, jnp.where(row >= blocks[r], 1.0, 0.0),
                                  jnp.where(row > blocks[r], 1.0, 0.0))
            ranks[r] = ranks[r] + ahead
    return jnp.concatenate(ranks, axis=0)


def _cmp_topk_kernel(q_ref, kc_ref, vct_ref, ovl_ref, o_ref, qa_ref, *, tq, n_top):
    q0 = pl.program_id(1) * tq
    m_cols = NSA_GROUP * tq
    nc = kc_ref.shape[2]
    nt = (((1,), (1,)), ((), ()))
    qs = [_stack_group(q_ref, h, NSA_GROUP) for h in range(NSA_KV_HEADS)]
    ss = [lax.dot_general(kc_ref[0, h].astype(BF16), qs[h], nt, preferred_element_type=F32)
          for h in range(NSA_KV_HEADS)]
    t_col = q0 + _mod_pow2(lax.broadcasted_iota(jnp.int32, (nc, m_cols), 1), tq)
    n_row = lax.broadcasted_iota(jnp.int32, (nc, m_cols), 0)
    mask = n_row * CMP_STRIDE + (CMP_BLOCK - 1) <= t_col
    j_idx = lax.broadcasted_iota(jnp.int32, (MAX_SEL_BLOCKS, tq), 0)
    cur = _div_pow2(q0 + lax.broadcasted_iota(jnp.int32, (MAX_SEL_BLOCKS, tq), 1), SEL_BLOCK)
    forced = (j_idx == 0) | (j_idx == cur) | (j_idx == cur - 1)
    for h in range(NSA_KV_HEADS):
        sm = jnp.where(mask, ss[h], NEG)
        e = jnp.exp(sm - jnp.max(sm, axis=0, keepdims=True))
        p = jnp.where(mask, e * (1.0 / jnp.sum(e, axis=0, keepdims=True)), 0.0)
        o_t = jnp.dot(vct_ref[0, h, 0:HEAD_DIM, :], p.astype(BF16), preferred_element_type=F32)
        _store_heads_t(o_ref, o_t, h, NSA_GROUP, tq)

        psum = p[:, 0:tq] + p[:, tq:2 * tq] + p[:, 2 * tq:3 * tq]
        hi = psum.astype(BF16)
        lo = (psum - hi.astype(F32)).astype(BF16)
        imp = (jnp.dot(ovl_ref[...], hi, preferred_element_type=F32)
               + jnp.dot(ovl_ref[...], lo, preferred_element_type=F32))
        val = jnp.where(forced, jnp.inf, jnp.where(j_idx > cur, -jnp.inf, imp))
        bias_t = jnp.where(_stable_rank(val) < n_top, 0.0, NEG)
        bias = jnp.concatenate([jnp.zeros_like(bias_t), bias_t], axis=0).T
        for g in range(NSA_GROUP):
            lane0 = (h * NSA_GROUP + g) * LANES
            qa_ref[:, lane0:lane0 + LANES] = (q_ref[:, lane0:lane0 + LANES].astype(F32)
                                              + bias).astype(qa_ref.dtype)


def _cmp_topk(q, kcmp, vcmp_t, ovl, batch, seq, n_top):
    t = q.shape[0]
    tq = ATTN_TILE
    nq = seq // tq
    nc = kcmp.shape[2]
    wide = NSA_HEADS * LANES
    qspec = pl.BlockSpec((tq, wide), lambda b, i: (b * nq + i, 0))
    return pl.pallas_call(
        functools.partial(_cmp_topk_kernel, tq=tq, n_top=n_top),
        grid=(batch, nq),
        in_specs=[qspec,
                  pl.BlockSpec((1, NSA_KV_HEADS, nc, LANES), lambda b, i: (b, 0, 0, 0)),
                  pl.BlockSpec((1, NSA_KV_HEADS, LANES, nc), lambda b, i: (b, 0, 0, 0)),
                  _const_spec((MAX_SEL_BLOCKS, nc))],
        out_specs=[qspec, qspec],
        out_shape=[jax.ShapeDtypeStruct((t, wide), BF16)] * 2,
        compiler_params=_params(("parallel", "parallel")),
        name="nsa_cmp_topk",
    )(q, kcmp, vcmp_t, ovl)


def _flash_kernel(q_ref, k_ref, vt_ref, o_ref, *, chains, groups, tq, ck):
    q0 = pl.program_id(2) * tq
    m_cols = groups * tq
    qs = [_stack_group(q_ref, h, groups) for h in range(chains)]

    def scores(c):
        k0 = pl.multiple_of(c * ck, ck)
        return tuple(lax.dot_general(k_ref[pl.ds(k0, ck), h * LANES:(h + 1) * LANES], qs[h],
                                     (((1,), (1,)), ((), ())), preferred_element_type=F32)
                     for h in range(chains))

    def absorb(c, s, stats):
        out = []
        for h in range(chains):
            m_old, acc = stats[h]
            m_new = jnp.maximum(m_old, jnp.max(s[h], axis=0, keepdims=True))
            alpha = jnp.exp(m_old - m_new)
            p = jnp.exp(s[h] - m_new).astype(BF16)
            vt = vt_ref[0, h, c, 0:V_ROWS, :]
            out.append((m_new, alpha * acc + jnp.dot(vt, p, preferred_element_type=F32)))
        return tuple(out)

    last = (q0 + tq - 1) // ck
    stats = tuple((jnp.full((1, m_cols), NEG, F32), jnp.zeros((V_ROWS, m_cols), F32))
                  for _ in range(chains))
    stats = lax.fori_loop(0, last, lambda c, st: absorb(c, scores(c), st), stats)
    key_pos = last * ck + lax.broadcasted_iota(jnp.int32, (ck, m_cols), 0)
    q_pos = q0 + _mod_pow2(lax.broadcasted_iota(jnp.int32, (ck, m_cols), 1), tq)
    visible = key_pos <= q_pos
    stats = absorb(last, tuple(jnp.where(visible, s, NEG) for s in scores(last)), stats)
    for h in range(chains):
        acc = stats[h][1]
        _store_heads_t(o_ref, acc[0:HEAD_DIM] / acc[HEAD_DIM:HEAD_DIM + 1], h, groups, tq)


def _flash(q, k, v, batch, seq, groups, name):
    t, qcols = q.shape
    kv_heads = k.shape[1] // LANES
    ck = min(KV_CHUNK, seq)
    tq = min(FLASH_ROWS // groups, ck)
    assert tq & (tq - 1) == 0 and ck % tq == 0 and seq % ck == 0
    nq = seq // tq
    n_chunks = seq // ck
    v_t = _chunked_vt(v, batch, seq, ck)
    chains =max(c for c in range(1, FLASH_CHAINS + 1) if kv_heads % c == 0)
    qspec = pl.BlockSpec((tq, chains * groups * LANES), lambda b, h, i: (b * nq + i, h))
    return pl.pallas_call(
        functools.partial(_flash_kernel, chains=chains, groups=groups, tq=tq, ck=ck),
        grid=(batch, kv_heads // chains, nq),
        in_specs=[qspec, pl.BlockSpec((seq, chains * LANES), lambda b, h, i: (b, h)),
                  pl.BlockSpec((1, chains, n_chunks, LANES, ck), lambda b, h, i: (b, h, 0, 0, 0))],
        out_specs=qspec,
        out_shape=jax.ShapeDtypeStruct((t, qcols), BF16),
        compiler_params=_params(("parallel", "parallel", "parallel")),
        name=name,
    )(q, k, v_t)


def _window_kernel(q_ref, k_ref, vt_ref, o_ref, *, tq):
    i = pl.program_id(1)
    q0 = i * tq
    n_back = WINDOW // tq
    n_span = n_back + 1
    m_cols = NSA_GROUP * tq
    qs = [_stack_group(q_ref, h, NSA_GROUP) for h in range(NSA_KV_HEADS)]
    q_pos = q0 + _mod_pow2(lax.broadcasted_iota(jnp.int32, (tq, m_cols), 1), tq)
    key_in = lax.broadcasted_iota(jnp.int32, (tq, m_cols), 0)
    nt = (((1,), (1,)), ((), ()))

    def run(c0, clamped):
        start = c0 * tq if clamped else pl.multiple_of(c0 * tq, tq)
        ss = [lax.dot_general(k_ref[pl.ds(start, n_span * tq), h * LANES:(h + 1) * LANES], qs[h],
                              nt, preferred_element_type=F32) for h in range(NSA_KV_HEADS)]
        for h in range(NSA_KV_HEADS):
            blocks = []
            for j in range(n_span):
                blk = ss[h][j * tq:(j + 1) * tq]
                key_pos = start + j * tq + key_in
                if clamped:
                    blk = jnp.where((key_pos <= q_pos) & (key_pos > q_pos - WINDOW), blk, NEG)
                elif j == 0:
                    blk = jnp.where(key_pos > q_pos - WINDOW, blk, NEG)
                elif j == n_span - 1:
                    blk = jnp.where(key_pos <= q_pos, blk, NEG)
                blocks.append(blk)
            m = functools.reduce(jnp.maximum, [jnp.max(b, axis=0, keepdims=True) for b in blocks])
            acc = 0.0
            for j in range(n_span):
                p = jnp.exp(blocks[j] - m).astype(BF16)
                acc = acc + jnp.dot(vt_ref[0, h, c0 + j, 0:V_ROWS, :], p, preferred_element_type=F32)
            _store_heads_t(o_ref, acc[0:HEAD_DIM] / acc[HEAD_DIM:HEAD_DIM + 1], h, NSA_GROUP, tq)

    @pl.when(i < n_back)
    def _():
        run(0, True)

    @pl.when(i >= n_back)
    def _():
        run(i - n_back, False)


def _chunked_vt(v, batch, seq, ck):
    heads = v.shape[1] // LANES
    return v.reshape(batch, seq // ck, ck, heads, LANES).transpose(0, 3, 1, 4, 2)


def _window(q, k, v, batch, seq):
    t = q.shape[0]
    tq = ATTN_TILE
    nq = seq // tq
    assert WINDOW % tq == 0 and seq >= WINDOW + tq
    wide = NSA_HEADS * LANES
    qspec = pl.BlockSpec((tq, wide), lambda b, i: (b * nq + i, 0))
    return pl.pallas_call(
        functools.partial(_window_kernel, tq=tq),
        grid=(batch, nq),
        in_specs=[qspec, pl.BlockSpec((seq, NSA_KV_HEADS * LANES), lambda b, i: (b, 0)),
                  pl.BlockSpec((1, NSA_KV_HEADS, nq, LANES, tq), lambda b, i: (b, 0, 0, 0, 0))],
        out_specs=qspec,
        out_shape=jax.ShapeDtypeStruct((t, wide), BF16),
        compiler_params=_params(("parallel", "parallel")),
        name="nsa_window",
    )(q, k, _chunked_vt(v, batch, seq, tq))


def _mla_proj_kernel(cq_ref, ckv_ref, kra_ref, krb_ref, cm_ref, sm_ref, qg_ref, kg_ref,
                     wa_ref, wb_ref, wk_ref, wv_ref, q_ref, k_ref, v_ref, *, scale):
    qn = _rms(cq_ref[...], qg_ref[...]).astype(BF16)
    cn = _rms(ckv_ref[...], kg_ref[...]).astype(BF16)
    cm = cm_ref[...]
    sm = sm_ref[...]
    k_rot = kra_ref[...] * cm + krb_ref[...] * sm
    for h in range(MLA_HEADS):
        sl = slice(h * LANES, (h + 1) * LANES)
        qa = jnp.dot(qn, wa_ref[:, sl], preferred_element_type=F32)
        qb = jnp.dot(qn, wb_ref[:, sl], preferred_element_type=F32)
        q_ref[:, sl] = ((qa * cm + qb * sm) * scale).astype(q_ref.dtype)
        k_ref[:, sl] = (jnp.dot(cn, wk_ref[:, sl], preferred_element_type=F32) + k_rot).astype(k_ref.dtype)
        v_ref[:, sl] = (jnp.dot(cn, wv_ref[:, sl], preferred_element_type=F32)
                        + _ones_lane((cn.shape[0], LANES))).astype(v_ref.dtype)


def _mla_proj(cq, ckv, kra, krb, cmul, smul, qg, kg, wa, wb, wk, wv):
    t = cq.shape[0]
    tm = ROW_TILE
    width = MLA_HEADS * LANES
    rows = lambda n: pl.BlockSpec((tm, n), lambda i: (i, 0))
    scale = (MLA_NOPE + MLA_ROPE) ** -0.5
    return pl.pallas_call(
        functools.partial(_mla_proj_kernel, scale=scale),
        grid=(t // tm,),
        in_specs=[rows(Q_LORA), rows(KV_LORA), rows(LANES), rows(LANES), rows(LANES), rows(LANES),
                  _const_spec((1, Q_LORA)), _const_spec((1, KV_LORA)),
                  _const_spec((Q_LORA, width)), _const_spec((Q_LORA, width)),
                  _const_spec((KV_LORA, width)), _const_spec((KV_LORA, width))],
        out_specs=[rows(width)] * 3,
        out_shape=[jax.ShapeDtypeStruct((t, width), BF16)] * 3,
        compiler_params=_params(("parallel",)),
        name="mla_proj",
    )(cq, ckv, kra, krb, cmul, smul, qg, kg, wa, wb, wk, wv)


def _ssm_kernel(u_ref, bblk_ref, a_ref, ccat_ref, d_ref, wglu_ref, bglu_ref, o_ref,
                h_sc, st_sc, *, tc, nb):
    @pl.when(pl.program_id(0) == 0)
    def _():
        st_sc[...] = jnp.zeros_like(st_sc)

    n = SSM_LANES
    u = u_ref[...].reshape(tc * nb, SSM_WIDTH)
    h_sc[...] = jnp.dot(u.astype(BF16), bblk_ref[...], preferred_element_type=F32)
    ar = jnp.broadcast_to(a_ref[0:1, :], (nb, n))
    ai = jnp.broadcast_to(a_ref[1:2, :], (nb, n))

    def step(t, carry):
        hr, hi = carry
        r0 = pl.multiple_of(t * nb, nb)
        nr = ar * hr - ai * hi + h_sc[pl.ds(r0, nb), 0:n]
        ni = ar * hi + ai * hr + h_sc[pl.ds(r0, nb), n:2 * n]
        h_sc[pl.ds(r0, nb), 0:n] = nr
        h_sc[pl.ds(r0, nb), n:2 * n] = ni
        return nr, ni

    hr, hi = lax.fori_loop(0, tc, step, (st_sc[0], st_sc[1]), unroll=4)
    st_sc[0] = hr
    st_sc[1] = hi
    y = jnp.dot(h_sc[...].astype(BF16), ccat_ref[...], preferred_element_type=F32) + d_ref[...] * u
    z = _gelu(y)
    gate = jnp.dot(z.astype(BF16), wglu_ref[...], preferred_element_type=F32) + bglu_ref[...]
    o_ref[...] = (z * _sigmoid(gate)).reshape(tc, nb, SSM_WIDTH)


def _ssm(u_t, bblk, a_rows, ccat, d_row, wglu, bglu):
    seq, nb, _ = u_t.shape
    tc = SSM_CHUNK
    n = SSM_LANES
    uspec = pl.BlockSpec((tc, nb, SSM_WIDTH), lambda i: (i, 0, 0))
    return pl.pallas_call(
        functools.partial(_ssm_kernel, tc=tc, nb=nb),
        grid=(seq // tc,),
        in_specs=[uspec, _const_spec((SSM_WIDTH, 2 * n)), _const_spec((2, n)),
                  _const_spec((2 * n, SSM_WIDTH)), _const_spec((1, SSM_WIDTH)),
                  _const_spec((SSM_WIDTH, SSM_WIDTH)), _const_spec((1, SSM_WIDTH))],
        out_specs=uspec,
        out_shape=jax.ShapeDtypeStruct(u_t.shape, F32),
        scratch_shapes=[pltpu.VMEM((tc * nb, 2 * n), F32), pltpu.VMEM((2, nb, n), F32)],
        compiler_params=_params(("arbitrary",), VMEM_LIMIT),
        name="s5_scan",
    )(u_t, bblk, a_rows, ccat, d_row, wglu, bglu)


def _outproj_kernel(oc_ref, os_ref, ow_ref, gate_ref, gb_ref, ex_ref, om_ref, oz_ref, x_ref,
                    gn_ref, gm_ref, gz_ref, wn_ref, wm_ref, wz_ref, o_ref):
    g = _sigmoid(gate_ref[...] + gb_ref[...])
    g_hi = g.astype(BF16)
    g_lo = (g - g_hi.astype(F32)).astype(BF16)
    o_a = 0.0
    for r, br_ref in enumerate((oc_ref, os_ref, ow_ref)):
        spread = (jnp.dot(g_hi, ex_ref[r], preferred_element_type=F32)
                  + jnp.dot(g_lo, ex_ref[r], preferred_element_type=F32))
        o_a = o_a + spread * br_ref[...].astype(F32)
    inv = lax.rsqrt(jnp.sum(o_a * o_a, axis=-1, keepdims=True) * (1.0 / (NSA_HEADS * HEAD_DIM)) + EPS)
    acc = x_ref[...] + jnp.dot((o_a * inv * gn_ref[...]).astype(BF16), wn_ref[...],
                               preferred_element_type=F32)
    om = om_ref[...].astype(F32)
    inv = lax.rsqrt(jnp.sum(om * om, axis=-1, keepdims=True) * (1.0 / (MLA_HEADS * HEAD_DIM)) + EPS)
    acc = acc + jnp.dot((om * inv * gm_ref[...]).astype(BF16), wm_ref[...], preferred_element_type=F32)
    acc = acc + jnp.dot(_rms(oz_ref[...], gz_ref[...]).astype(BF16), wz_ref[...],
                        preferred_element_type=F32)
    o_ref[...] = acc


def _gate_spread():
    col = jnp.arange(LANES)[None, :, None]
    lane = jnp.arange(NSA_HEADS * LANES)[None, None, :]
    r = jnp.arange(N_BRANCH)[:, None, None]
    hit = (col == N_BRANCH * (lane // LANES) + r) & (lane % LANES < HEAD_DIM)
    return hit.astype(BF16)


def _outproj(oc, osel, ow, gate, gate_b, om, oz, x, gn, gm, gz, wn, wm, wz):
    t, d = x.shape
    tm = ROW_TILE
    wide = NSA_HEADS * LANES
    rows = lambda n: pl.BlockSpec((tm, n), lambda i: (i, 0))
    return pl.pallas_call(
        _outproj_kernel,
        grid=(t // tm,),
        in_specs=[rows(wide), rows(wide), rows(wide), rows(LANES), _const_spec((1, LANES)),
                  _const_spec((N_BRANCH, LANES, wide)), rows(wide), rows(SSM_WIDTH), rows(d),
                  _const_spec((1, wide)), _const_spec((1, wide)), _const_spec((1, SSM_WIDTH)),
                  _const_spec((wide, d)), _const_spec((wide, d)), _const_spec((SSM_WIDTH, d))],
        out_specs=rows(d),
        out_shape=jax.ShapeDtypeStruct((t, d), F32),
        compiler_params=_params(("parallel",), VMEM_LIMIT),
        name="out_proj",
    )(oc, osel, ow, gate, gate_b, _gate_spread(), om, oz, x, gn, gm, gz, wn, wm, wz)


def _ffn_kernel(x_ref, g_ref, wg_ref, wv_ref, cw_ref, cb_ref, wd_ref, o_ref, carry_sc,
                *, tm, tiles_per_seq, cf):
    @pl.when(lax.rem(pl.program_id(0), tiles_per_seq) == 0)
    def _():
        carry_sc[...] = jnp.zeros_like(carry_sc)

    x = x_ref[...]
    h = _rms(x, g_ref[...]).astype(BF16)
    o_ref[...] = x
    row = lax.broadcasted_iota(jnp.int32, (tm, cf), 0)
    n_chunks = D_FF // cf

    def up(c):
        sl = slice(c * cf, (c + 1) * cf)
        return (jnp.dot(h, wg_ref[:, sl], preferred_element_type=F32),
                jnp.dot(h, wv_ref[:, sl], preferred_element_type=F32))

    nxt = up(0)
    for c in range(n_chunks):
        sl = slice(c * cf, (c + 1) * cf)
        gate, val = nxt
        if c + 1 < n_chunks:
            nxt = up(c + 1)
        tail = carry_sc[:, sl]
        p1 = tail[7:8, :]
        p2 = tail[6:7, :]
        g1 = jnp.where(row == 0, p1, pltpu.roll(gate, 1, 0))
        g2 = jnp.where(row == 0, p2, jnp.where(row == 1, p1, pltpu.roll(gate, 2, 0)))
        carry_sc[:, sl] = gate[tm - 8:tm, :]
        gc = cw_ref[0:1, sl] * g2 + cw_ref[1:2, sl] * g1 + cw_ref[2:3, sl] * gate + cb_ref[:, sl]
        act = gc * _sigmoid(gc) * val
        o_ref[...] += jnp.dot(act.astype(BF16), wd_ref[sl, :], preferred_element_type=F32)


def _ffn(x, g, wg, wv, cw, cb, wd, seq):
    t, d = x.shape
    tm = ROW_TILE
    rows = pl.BlockSpec((tm, d), lambda i: (i, 0))
    return pl.pallas_call(
        functools.partial(_ffn_kernel, tm=tm, tiles_per_seq=seq // tm, cf=FF_CHUNK),
        grid=(t // tm,),
        in_specs=[rows, _const_spec((1, d)), _const_spec((d, D_FF)), _const_spec((d, D_FF)),
                  _const_spec((8, D_FF)), _const_spec((1, D_FF)), _const_spec((D_FF, d))],
        out_specs=rows,
        out_shape=jax.ShapeDtypeStruct((t, d), F32),
        scratch_shapes=[pltpu.VMEM((8, D_FF), F32)],
        compiler_params=_params(("arbitrary",), VMEM_LIMIT),
        name="conv_ffn",
    )(x, g, wg, wv, cw, cb, wd)


def _final_norm_kernel(x_ref, g_ref, o_ref):
    o_ref[...] = _rms(x_ref[...], g_ref[...])


def _final_norm(x, g):
    t, d = x.shape
    rows = pl.BlockSpec((ROW_TILE, d), lambda i: (i, 0))
    return pl.pallas_call(
        _final_norm_kernel,
        grid=(t // ROW_TILE,),
        in_specs=[rows, _const_spec((1, d))],
        out_specs=rows,
        out_shape=jax.ShapeDtypeStruct((t, d), F32),
        compiler_params=_params(("parallel",)),
        name="final_norm",
    )(x, g)


def _pad_heads(w, heads, width):
    lead = w.shape[:-1]
    w = w.reshape(lead + (heads, width))
    w = jnp.pad(w, [(0, 0)] * len(lead) + [(0, 0), (0, LANES - width)])
    return w.reshape(lead + (heads * LANES,))


def _pad_head_rows(w, heads, width):
    n = w.shape[-1]
    w = w.reshape(heads, width, n)
    w = jnp.pad(w, [(0, 0), (0, LANES - width), (0, 0)])
    return w.reshape(heads * LANES, n)


def _inproj_weight(w):
    sizes = (384, 128, 128, 128, 128, 128, 128, 18, Q_LORA, KV_LORA, MLA_ROPE, SSM_WIDTH)
    offs = [0]
    for n in sizes:
        offs.append(offs[-1] + n)
    (w_q, w_kc, w_vc, w_ks, w_vs, w_kw, w_vw, w_g, w_cq, w_ckv, w_kr, w_u) = [
        w[:, a:b] for a, b in zip(offs[:-1], offs[1:])]
    d = w.shape[0]
    half = MLA_ROPE // 2
    z64 = jnp.zeros((d, HEAD_DIM), w.dtype)
    z32 = jnp.zeros((d, LANES - HEAD_DIM - MLA_ROPE), w.dtype)
    r1, r2 = w_kr[:, :half], w_kr[:, half:]
    cols = [
        _pad_heads(w_q * (HEAD_DIM ** -0.5), NSA_HEADS, HEAD_DIM),
        w_kc, w_vc,
        _pad_heads(w_ks, NSA_KV_HEADS, HEAD_DIM), _pad_heads(w_vs, NSA_KV_HEADS, HEAD_DIM),
        _pad_heads(w_kw, NSA_KV_HEADS, HEAD_DIM), _pad_heads(w_vw, NSA_KV_HEADS, HEAD_DIM),
        jnp.pad(w_g, ((0, 0), (0, LANES - w_g.shape[1]))),
        w_cq, w_ckv,
        jnp.concatenate([z64, r1, r2, z32], axis=1),
        jnp.concatenate([z64, r2, r1, z32], axis=1),
        w_u,
    ]
    return jnp.concatenate(cols, axis=1).astype(BF16)


def _mla_weights(w_uq, w_uk, w_uv):
    half = MLA_ROPE // 2
    w = w_uq.reshape(Q_LORA, MLA_HEADS, MLA_NOPE + MLA_ROPE)
    nope, r1, r2 = w[..., :MLA_NOPE], w[..., MLA_NOPE:MLA_NOPE + half], w[..., MLA_NOPE + half:]
    z32 = jnp.zeros((Q_LORA, MLA_HEADS, LANES - MLA_NOPE - MLA_ROPE), w.dtype)
    wa = jnp.concatenate([nope, r1, r2, z32], axis=-1).reshape(Q_LORA, MLA_HEADS * LANES)
    wb = jnp.concatenate([jnp.zeros_like(nope), r2, r1, z32], axis=-1).reshape(Q_LORA, MLA_HEADS * LANES)
    wk = _pad_heads(w_uk, MLA_HEADS, MLA_NOPE)
    wv = _pad_heads(w_uv, MLA_HEADS, HEAD_DIM)
    return wa.astype(BF16), wb.astype(BF16), wk.astype(BF16), wv.astype(BF16)


def _ssm_weights(log_dt, a_re, a_im, b_re, b_im, c_re, c_im, d):
    dt = jnp.exp(log_dt.astype(F32))[:, None]
    ar, ai = a_re.astype(F32), a_im.astype(F32)
    mag = jnp.exp(ar * dt)
    abr, abi = mag * jnp.cos(ai * dt), mag * jnp.sin(ai * dt)
    den = ar * ar + ai * ai
    fr = ((abr - 1.0) * ar + abi * ai) / den
    fi = (abi * ar - (abr - 1.0) * ai) / den
    br, bi = b_re.astype(F32), b_im.astype(F32)
    bbr = fr[..., None] * br - fi[..., None] * bi
    bbi = fr[..., None] * bi + fi[..., None] * br
    eye = jnp.eye(SSM_GROUPS, dtype=F32)
    blk_in = lambda m: jnp.einsum("gpc,gh->gchp", m, eye).reshape(SSM_WIDTH, SSM_LANES)
    blk_out = lambda m: jnp.einsum("gcp,gh->gphc", m, eye).reshape(SSM_LANES, SSM_WIDTH)
    bblk = jnp.concatenate([blk_in(bbr), blk_in(bbi)], axis=1)
    ccat = jnp.concatenate([blk_out(c_re.astype(F32)), -blk_out(c_im.astype(F32))], axis=0)
    a_rows = jnp.stack([abr.reshape(SSM_LANES), abi.reshape(SSM_LANES)])
    return bblk.astype(BF16), a_rows, ccat.astype(BF16), d.astype(F32).reshape(1, SSM_WIDTH)


def _overlap_t(nc_pad):
    start = jnp.arange(nc_pad) * CMP_STRIDE
    lo = jnp.arange(MAX_SEL_BLOCKS) * SEL_BLOCK
    hit = (start[None, :] < lo[:, None] + SEL_BLOCK) & (start[None, :] + CMP_BLOCK > lo[:, None])
    return hit.astype(BF16)


def _rope_rows():
    half = MLA_ROPE // 2
    inv_freq = ROPE_THETA ** (-jnp.arange(half, dtype=F32) / half)
    z64 = jnp.zeros((HEAD_DIM,), F32)
    z32 = jnp.zeros((LANES - HEAD_DIM - MLA_ROPE,), F32)
    invf = jnp.concatenate([z64, inv_freq, inv_freq, z32]).reshape(1, LANES)
    sign = jnp.concatenate([z64, -jnp.ones((half,), F32), jnp.ones((half,), F32), z32]).reshape(1, LANES)
    return invf, sign


def kernel(x, positions, attn_norm, w_in, nsa_pe, nsa_ck_w1, nsa_ck_b1, nsa_ck_w2, nsa_ck_b2, nsa_cv_w1, nsa_cv_b1, nsa_cv_w2, nsa_cv_b2, nsa_gate_b, mla_q_norm, mla_kv_norm, mla_w_uq, mla_w_uk, mla_w_uv, ssm_log_dt, ssm_a_re, ssm_a_im, ssm_b_re, ssm_b_im, ssm_c_re, ssm_c_im, ssm_d, ssm_w_glu, ssm_b_glu, out_norm_nsa, out_norm_mla, out_norm_ssm, w_out, ffn_norm, ffn_w_up, ffn_conv_w, ffn_conv_b, ffn_w_down, final_norm):
    batch, seq, d_model = x.shape
    depth = w_in.shape[0]
    t = batch * seq
    n_half = seq // CMP_STRIDE
    n_sel = seq // SEL_BLOCK
    assert seq % ROW_TILE == 0 and n_sel <= MAX_SEL_BLOCKS and n_half % LANES == 0
    n_top = min(SEL_TOP, n_sel)
    row = lambda v: v.astype(F32).reshape(1, -1)

    invf, sign = _rope_rows()
    pos_rows = jnp.broadcast_to(positions.astype(F32).reshape(t, 1), (t, LANES))
    cmul, smul = _trig(pos_rows, invf, sign)
    ovl_t = _overlap_t(n_half)

    xf = x.reshape(t, d_model)
    for l in range(depth):
        (q, kvc, ks, vs, kw, vw, gate, cq, ckv, kra, krb, u) = _inproj(
            xf, row(attn_norm[l]), _inproj_weight(w_in[l]), seq)

        a = kvc.reshape(batch, n_half, CMP_STRIDE, 4, HEAD_DIM).transpose(0, 3, 1, 2, 4)
        a = a.reshape(batch, 4, n_half, CMP_STRIDE * HEAD_DIM)
        flat = lambda w: w.reshape(CMP_BLOCK * HEAD_DIM, CMP_HIDDEN)
        pad2 = lambda w: jnp.pad(w, ((0, 0), (0, LANES - HEAD_DIM)))
        kvcmp = _compress(
            a,
            jnp.broadcast_to(nsa_pe[l].reshape(1, 1, -1), (2, 1, CMP_BLOCK * HEAD_DIM)).astype(F32),
            jnp.stack([flat(nsa_ck_w1[l]), flat(nsa_cv_w1[l])]).astype(BF16),
            jnp.stack([row(nsa_ck_b1[l]), row(nsa_cv_b1[l])]),
            jnp.stack([pad2(nsa_ck_w2[l]), pad2(nsa_cv_w2[l])]).astype(BF16),
            jnp.stack([pad2(row(nsa_ck_b2[l])), pad2(row(nsa_cv_b2[l]))]))
        vcmp_t = kvcmp[:, NSA_KV_HEADS:].transpose(0, 1, 3, 2).astype(BF16)
        o_cmp, q_aug = _cmp_topk(q, kvcmp, vcmp_t, ovl_t, batch, seq, n_top)
        o_sel = _flash(q_aug, ks, vs, batch, seq, NSA_GROUP, "nsa_selected")
        o_win = _window(q, kw, vw, batch, seq)

        wa, wb, wk, wv = _mla_weights(mla_w_uq[l], mla_w_uk[l], mla_w_uv[l])
        q_m, k_m, v_m = _mla_proj(cq, ckv, kra, krb, cmul, smul, row(mla_q_norm[l]),
                                  row(mla_kv_norm[l]), wa, wb, wk, wv)
        o_mla = _flash(q_m, k_m, v_m, batch, seq, 1, "mla_attention")

        bblk, a_rows, ccat, d_row = _ssm_weights(
            ssm_log_dt[l], ssm_a_re[l], ssm_a_im[l], ssm_b_re[l], ssm_b_im[l],
            ssm_c_re[l], ssm_c_im[l], ssm_d[l])
        u_t = u.reshape(batch, seq, SSM_WIDTH).transpose(1, 0, 2)
        o_ssm = _ssm(u_t, bblk, a_rows, ccat, d_row, ssm_w_glu[l].astype(BF16), row(ssm_b_glu[l]))
        o_ssm = o_ssm.transpose(1, 0, 2).reshape(t, SSM_WIDTH)

        w_o = w_out[l].astype(BF16)
        n_a = NSA_HEADS * HEAD_DIM
        n_b = n_a + MLA_HEADS * HEAD_DIM
        xf = _outproj(
            o_cmp, o_sel, o_win, gate,
            jnp.pad(row(nsa_gate_b[l]), ((0, 0), (0, LANES - NSA_HEADS * N_BRANCH))),
            o_mla, o_ssm, xf,
            _pad_heads(row(out_norm_nsa[l]), NSA_HEADS, HEAD_DIM),
            _pad_heads(row(out_norm_mla[l]), MLA_HEADS, HEAD_DIM),
            row(out_norm_ssm[l]),
            _pad_head_rows(w_o[:n_a], NSA_HEADS, HEAD_DIM),
            _pad_head_rows(w_o[n_a:n_b], MLA_HEADS, HEAD_DIM),
            w_o[n_b:])

        w_up = ffn_w_up[l].astype(BF16)
        xf = _ffn(xf, row(ffn_norm[l]), w_up[:, :D_FF], w_up[:, D_FF:],
                  jnp.pad(ffn_conv_w[l].astype(F32), ((0, 8 - ffn_conv_w.shape[1]), (0, 0))),
                  row(ffn_conv_b[l]), ffn_w_down[l].astype(BF16), seq)

    return _final_norm(xf, row(final_norm)).reshape(batch, seq, d_model)
```

```python
import functools
import math

import jax
import jax.numpy as jnp
from jax import lax
from jax.experimental import pallas as pl
from jax.experimental.pallas import tpu as pltpu

F32 = jnp.float32
BF16 = jnp.bfloat16

LANES = 128
HEAD_DIM = 64
NSA_HEADS = 6
NSA_KV_HEADS = 2
NSA_GROUP = NSA_HEADS // NSA_KV_HEADS
N_BRANCH = 3
CMP_BLOCK = 32
CMP_STRIDE = 16
CMP_HIDDEN = 128
SEL_BLOCK = 64
SEL_TOP = 16
MAX_SEL_BLOCKS = 64
WINDOW = 512
MLA_HEADS = 6
MLA_NOPE = 64
MLA_ROPE = 32
Q_LORA = 384
KV_LORA = 128
ROPE_THETA = 10000.0
SSM_WIDTH = 256
SSM_GROUPS = 16
SSM_GROUP_CH = 16
SSM_STATE = 64
SSM_LANES = SSM_GROUPS * SSM_STATE
D_FF = 2816
EPS = 1e-6
NEG = -1e30

ROW_TILE = 512
ATTN_TILE = 256
KV_CHUNK = 512
FLASH_ROWS = 768
FLASH_CHAINS = 3
V_ROWS = 80
SSM_CHUNK = 64
FF_CHUNK = 256
VMEM_LIMIT = 56 * 1024 * 1024


def _params(sem, vmem=None):
    return pltpu.CompilerParams(dimension_semantics=sem, vmem_limit_bytes=vmem)


def _rms(x, g):
    return x * lax.rsqrt(jnp.mean(x * x, axis=-1, keepdims=True) + EPS) * g


def _gelu(x):
    c = math.sqrt(2.0 / math.pi)
    return 0.5 * x * (1.0 + jnp.tanh(c * (x + 0.044715 * (x * x * x))))


def _sigmoid(x):
    return 1.0 / (1.0 + jnp.exp(-x))


def _mod_pow2(x, n):
    assert n & (n - 1) == 0
    return jnp.bitwise_and(x, n - 1)


def _div_pow2(x, n):
    assert n & (n - 1) == 0
    return jnp.right_shift(x, n.bit_length() - 1)


def _ones_lane(shape):
    lane = _mod_pow2(lax.broadcasted_iota(jnp.int32, shape, 1), LANES)
    return jnp.where(lane == HEAD_DIM, 1.0, 0.0)


def _const_spec(shape):
    nd = len(shape)
    return pl.BlockSpec(shape, lambda *_: (0,) * nd)


def _trig_kernel(pos_ref, invf_ref, sign_ref, c_ref, s_ref):
    ang = pos_ref[...] * invf_ref[...]
    c_ref[...] = jnp.cos(ang)
    s_ref[...] = jnp.sin(ang) * sign_ref[...]


def _trig(pos_rows, invf_row, sign_row):
    t = pos_rows.shape[0]
    spec = pl.BlockSpec((ROW_TILE, LANES), lambda i: (i, 0))
    return pl.pallas_call(
        _trig_kernel,
        grid=(t // ROW_TILE,),
        in_specs=[spec, _const_spec((1, LANES)), _const_spec((1, LANES))],
        out_specs=[spec, spec],
        out_shape=[jax.ShapeDtypeStruct((t, LANES), F32)] * 2,
        compiler_params=_params(("parallel",)),
        name="rope_trig",
    )(pos_rows, invf_row, sign_row)


_IN_SEGS = (
    ("q", NSA_HEADS * LANES, BF16),
    ("kvc", 4 * HEAD_DIM, F32),
    ("ks", NSA_KV_HEADS * LANES, BF16),
    ("kw", NSA_KV_HEADS * LANES, BF16),
    ("gate", LANES, F32),
    ("cq", Q_LORA, F32),
    ("ckv", KV_LORA, F32),
    ("kra", LANES, F32),
    ("krb", LANES, F32),
    ("u", SSM_WIDTH, F32),
)
_IN_COLS = sum(n for _, n, _ in _IN_SEGS)
_VT_ROWS = 2 * NSA_KV_HEADS * LANES


def _ones_row(shape):
    row = _mod_pow2(lax.broadcasted_iota(jnp.int32, shape, 0), LANES)
    return jnp.where(row == HEAD_DIM, 1.0, 0.0)


def _inproj_kernel(x_ref, g_ref, w_ref, wvt_ref, *o_refs, seq, tm):
    h = _rms(x_ref[...], g_ref[...]).astype(BF16)
    off = 0
    for (name, n, dt), o_ref in zip(_IN_SEGS, o_refs):
        y = jnp.dot(h, w_ref[:, off:off + n], preferred_element_type=F32)
        if name == "ks":
            s0 = lax.rem(pl.program_id(0) * tm, seq)
            blk = _div_pow2(s0 + lax.broadcasted_iota(jnp.int32, (tm, n), 0), SEL_BLOCK)
            lane = _mod_pow2(lax.broadcasted_iota(jnp.int32, (tm, n), 1), LANES)
            y = y + jnp.where(lane - HEAD_DIM == blk, 1.0, 0.0)
        o_ref[...] = y.astype(dt)
        off += n
    vst_ref, vwt_ref = o_refs[len(_IN_SEGS):]
    v_t = lax.dot_general(wvt_ref[...], h, (((1,), (1,)), ((), ())), preferred_element_type=F32)
    v_t = (v_t + _ones_row(v_t.shape)).astype(BF16)
    wc = vwt_ref.shape[-1]
    for hh in range(NSA_KV_HEADS):
        vst_ref[0, hh, 0] = v_t[hh * LANES:(hh + 1) * LANES, :]
        r0 = (NSA_KV_HEADS + hh) * LANES
        for c in range(tm // wc):
            vwt_ref[0, hh, c] = v_t[r0:r0 + LANES, c * wc:(c + 1) * wc]


def _inproj(x, g, w, wvt, batch, seq):
    t, d = x.shape
    tm = ROW_TILE
    assert tm == min(KV_CHUNK, seq) and tm % ATTN_TILE == 0
    nt = seq // tm
    rows = lambda n: pl.BlockSpec((tm, n), lambda i: (i, 0))
    out_specs = [rows(n) for _, n, _ in _IN_SEGS]
    out_shape = [jax.ShapeDtypeStruct((t, n), dt) for _, n, dt in _IN_SEGS]
    out_specs[-1] = pl.BlockSpec((tm, SSM_WIDTH), lambda i: (i % nt, i // nt))
    out_shape[-1] = jax.ShapeDtypeStruct((seq, batch * SSM_WIDTH), F32)
    wpt = tm // ATTN_TILE
    out_specs += [pl.BlockSpec((1, NSA_KV_HEADS, 1, LANES, tm), lambda i: (i // nt, 0, i % nt, 0, 0)),
                  pl.BlockSpec((1, NSA_KV_HEADS, wpt, LANES, ATTN_TILE),
                               lambda i: (i // nt, 0, i % nt, 0, 0))]
    out_shape += [jax.ShapeDtypeStruct((batch, NSA_KV_HEADS, nt, LANES, tm), BF16),
                  jax.ShapeDtypeStruct((batch, NSA_KV_HEADS, seq // ATTN_TILE, LANES, ATTN_TILE), BF16)]
    return pl.pallas_call(
        functools.partial(_inproj_kernel, seq=seq, tm=tm),
        grid=(t // tm,),
        in_specs=[pl.BlockSpec((tm, d), lambda i: (i, 0)), _const_spec((1, d)),
                  _const_spec((d, _IN_COLS)), _const_spec((_VT_ROWS, d))],
        out_specs=out_specs,
        out_shape=out_shape,
        compiler_params=_params(("parallel",), VMEM_LIMIT),
        name="in_proj",
    )(x, g, w, wvt)


_CMP_STREAMS = 2 * NSA_KV_HEADS


def _compress_kernel(x_ref, pe_ref, w1_ref, b1_ref, w2_ref, b2_ref, o_ref):
    x = x_ref[0]
    n = x.shape[0]
    top = jnp.dot((x + pe_ref[0:1, :]).astype(BF16), w1_ref[0], preferred_element_type=F32)
    bot = jnp.dot((x + pe_ref[1:2, :]).astype(BF16), w1_ref[1], preferred_element_type=F32)
    hid = _gelu(top + pltpu.roll(bot, n - 1, 0) + b1_ref[...]).astype(BF16)
    for j in range(_CMP_STREAMS):
        kind = j // NSA_KV_HEADS
        o_ref[0, j] = (jnp.dot(hid[:, j * CMP_HIDDEN:(j + 1) * CMP_HIDDEN], w2_ref[kind],
                               preferred_element_type=F32) + b2_ref[kind])


def _compress(x, pe, w1, b1, w2, b2):
    b, n, width = x.shape
    hid = _CMP_STREAMS * CMP_HIDDEN
    return pl.pallas_call(
        _compress_kernel,
        grid=(b,),
        in_specs=[pl.BlockSpec((1, n, width), lambda i: (i, 0, 0)),
                  _const_spec((2, width)), _const_spec((2, width, hid)), _const_spec((1, hid)),
                  _const_spec((2, CMP_HIDDEN, LANES)), _const_spec((2, 1, LANES))],
        out_specs=pl.BlockSpec((1, _CMP_STREAMS, n, LANES), lambda i: (i, 0, 0, 0)),
        out_shape=jax.ShapeDtypeStruct((b, _CMP_STREAMS, n, LANES), F32),
        compiler_params=_params(("parallel",), VMEM_LIMIT),
        name="nsa_compress",
    )(x, pe, w1, b1, w2, b2)


def _compress_weights(pe, ck_w1, ck_b1, ck_w2, ck_b2, cv_w1, cv_b1, cv_w2, cv_b2):
    eye = jnp.eye(_CMP_STREAMS, dtype=F32)
    per_stream = jnp.stack([ck_w1, ck_w1, cv_w1, cv_w1]).astype(F32)

    def place(w):
        return jnp.einsum("jldf,jk->ljdkf", w, eye).reshape(
            CMP_STRIDE * _CMP_STREAMS * HEAD_DIM, _CMP_STREAMS * CMP_HIDDEN)

    w1 = jnp.stack([place(per_stream[:, :CMP_STRIDE]), place(per_stream[:, CMP_STRIDE:])])
    spread = lambda p: jnp.broadcast_to(p[:, None, :], (CMP_STRIDE, _CMP_STREAMS, HEAD_DIM)).reshape(-1)
    pe2 = jnp.stack([spread(pe[:CMP_STRIDE]), spread(pe[CMP_STRIDE:])]).astype(F32)
    b1 = jnp.concatenate([ck_b1, ck_b1, cv_b1, cv_b1]).astype(F32).reshape(1, -1)
    pad2 = lambda w: jnp.pad(w.astype(F32), ((0, 0), (0, LANES - HEAD_DIM)))
    w2 = jnp.stack([pad2(ck_w2), pad2(cv_w2)]).astype(BF16)
    b2 = jnp.stack([pad2(ck_b2.reshape(1, -1)), pad2(cv_b2.reshape(1, -1))])
    return pe2, w1.astype(BF16), b1, w2, b2


def _stack_group(q_ref, h, groups):
    return jnp.concatenate([q_ref[:, (h * groups + g) * LANES:(h * groups + g + 1) * LANES]
                            for g in range(groups)], axis=0)


def _store_heads_t(o_ref, o_t, h, groups, tq):
    o_t = jnp.concatenate([o_t, jnp.zeros_like(o_t)], axis=0)
    for g in range(groups):
        lane0 = (h * groups + g) * LANES
        o_ref[:, lane0:lane0 + LANES] = o_t[:, g * tq:(g + 1) * tq].T.astype(o_ref.dtype)


def _stable_rank(val):
    sub = 8
    n_blk = val.shape[0] // sub
    blocks = [val[r * sub:(r + 1) * sub] for r in range(n_blk)]
    ranks = [jnp.zeros(blocks[0].shape, F32) for _ in range(n_blk)]
    row_in = lax.broadcasted_iota(jnp.int32, blocks[0].shape, 0)
    for i in range(val.shape[0]):
        row = val[i:i + 1, :]
        for r in range(n_blk):
            if r * sub > i:
                ahead = jnp.where(row >= blocks[r], 1.0, 0.0)
            elif (r + 1) * sub <= i:
                ahead = jnp.where(row > blocks[r], 1.0, 0.0)
            else:
                ahead = jnp.where(row_in + r * sub > i, jnp.where(row >= blocks[r], 1.0, 0.0),
                                  jnp.where(row > blocks[r], 1.0, 0.0))
            ranks[r] = ranks[r] + ahead
    return jnp.concatenate(ranks, axis=0)


def _cmp_topk_kernel(q_ref, kc_ref, vct_ref, ovl_ref, o_ref, qa_ref, *, tq, n_top):
    q0 = pl.program_id(1) * tq
    m_cols = NSA_GROUP * tq
    nc = kc_ref.shape[2]
    nt = (((1,), (1,)), ((), ()))
    qs = [_stack_group(q_ref, h, NSA_GROUP) for h in range(NSA_KV_HEADS)]
    ss = [lax.dot_general(kc_ref[0, h].astype(BF16), qs[h], nt, preferred_element_type=F32)
          for h in range(NSA_KV_HEADS)]
    t_col = q0 + _mod_pow2(lax.broadcasted_iota(jnp.int32, (nc, m_cols), 1), tq)
    n_row = lax.broadcasted_iota(jnp.int32, (nc, m_cols), 0)
    mask = n_row * CMP_STRIDE + (CMP_BLOCK - 1) <= t_col
    j_idx = lax.broadcasted_iota(jnp.int32, (MAX_SEL_BLOCKS, tq), 0)
    cur = _div_pow2(q0 + lax.broadcasted_iota(jnp.int32, (MAX_SEL_BLOCKS, tq), 1), SEL_BLOCK)
    forced = (j_idx == 0) | (j_idx == cur) | (j_idx == cur - 1)
    for h in range(NSA_KV_HEADS):
        sm = jnp.where(mask, ss[h], NEG)
        e = jnp.exp(sm - jnp.max(sm, axis=0, keepdims=True))
        p = jnp.where(mask, e * (1.0 / jnp.sum(e, axis=0, keepdims=True)), 0.0)
        o_t = jnp.dot(vct_ref[0, h, 0:HEAD_DIM, :], p.astype(BF16), preferred_element_type=F32)
        _store_heads_t(o_ref, o_t, h, NSA_GROUP, tq)

        psum = p[:, 0:tq] + p[:, tq:2 * tq] + p[:, 2 * tq:3 * tq]
        hi = psum.astype(BF16)
        lo = (psum - hi.astype(F32)).astype(BF16)
        imp = (jnp.dot(ovl_ref[...], hi, preferred_element_type=F32)
               + jnp.dot(ovl_ref[...], lo, preferred_element_type=F32))
        val = jnp.where(forced, jnp.inf, jnp.where(j_idx > cur, -jnp.inf, imp))
        bias_t = jnp.where(_stable_rank(val) < n_top, 0.0, NEG)
        bias = jnp.concatenate([jnp.zeros_like(bias_t), bias_t], axis=0).T
        for g in range(NSA_GROUP):
            lane0 = (h * NSA_GROUP + g) * LANES
            qa_ref[:, lane0:lane0 + LANES] = (q_ref[:, lane0:lane0 + LANES].astype(F32)
                                              + bias).astype(qa_ref.dtype)


def _cmp_topk(q, kcmp, vcmp_t, ovl, batch, seq, n_top):
    t = q.shape[0]
    tq = ATTN_TILE
    nq = seq // tq
    nc = kcmp.shape[2]
    wide = NSA_HEADS * LANES
    qspec = pl.BlockSpec((tq, wide), lambda b, i: (b * nq + i, 0))
    return pl.pallas_call(
        functools.partial(_cmp_topk_kernel, tq=tq, n_top=n_top),
        grid=(batch, nq),
        in_specs=[qspec,
                  pl.BlockSpec((1, NSA_KV_HEADS, nc, LANES), lambda b, i: (b, 0, 0, 0)),
                  pl.BlockSpec((1, NSA_KV_HEADS, LANES, nc), lambda b, i: (b, 0, 0, 0)),
                  _const_spec((MAX_SEL_BLOCKS, nc))],
        out_specs=[qspec, qspec],
        out_shape=[jax.ShapeDtypeStruct((t, wide), BF16)] * 2,
        compiler_params=_params(("parallel", "parallel")),
        name="nsa_cmp_topk",
    )(q, kcmp, vcmp_t, ovl)


def _flash_kernel(q_ref, k_ref, vt_ref, o_ref, s_a, s_b, *, chains, groups, tq, ck):
    q0 = pl.program_id(2) * tq
    m_cols = groups * tq
    qs = [_stack_group(q_ref, h, groups) for h in range(chains)]

    def issue_scores(c, s_ref):
        k0 = pl.multiple_of(c * ck, ck)
        for h in range(chains):
            s_ref[h] = lax.dot_general(k_ref[pl.ds(k0, ck), h * LANES:(h + 1) * LANES], qs[h],
                                       (((1,), (1,)), ((), ())), preferred_element_type=F32)

    def absorb(c, s_ref, stats, visible=None):
        out = []
        for h in range(chains):
            s = s_ref[h]
            if visible is not None:
                s = jnp.where(visible, s, NEG)
            m_old, acc = stats[h]
            m_new = jnp.maximum(m_old, jnp.max(s, axis=0, keepdims=True))
            alpha = jnp.exp(m_old - m_new)
            p = jnp.exp(s - m_new).astype(BF16)
            vt = vt_ref[0, h, c, 0:V_ROWS, :]
            out.append((m_new, alpha * acc + jnp.dot(vt, p, preferred_element_type=F32)))
        return tuple(out)

    def finish(stats):
        for h in range(chains):
            acc = stats[h][1]
            _store_heads_t(o_ref, acc[0:HEAD_DIM] / acc[HEAD_DIM:HEAD_DIM + 1], h, groups, tq)

    last = (q0 + tq - 1) // ck
    key_pos = last * ck + lax.broadcasted_iota(jnp.int32, (ck, m_cols), 0)
    q_pos = q0 + _mod_pow2(lax.broadcasted_iota(jnp.int32, (ck, m_cols), 1), tq)
    visible = key_pos <= q_pos

    def pair(j, stats):
        c = 2 * j
        issue_scores(c + 1, s_b)
        stats = absorb(c, s_a, stats)
        issue_scores(c + 2, s_a)
        return absorb(c + 1, s_b, stats)

    issue_scores(0, s_a)
    stats = tuple((jnp.full((1, m_cols), NEG, F32), jnp.zeros((V_ROWS, m_cols), F32))
                  for _ in range(chains))
    stats = lax.fori_loop(0, last // 2, pair, stats)

    @pl.when(lax.rem(last, 2) == 0)
    def _():
        finish(absorb(last, s_a, stats, visible))

    @pl.when(lax.rem(last, 2) == 1)
    def _():
        issue_scores(last, s_b)
        finish(absorb(last, s_b, absorb(last - 1, s_a, stats), visible))


def _flash(q, k, v_t, batch, seq, groups, name):
    t, qcols = q.shape
    kv_heads = k.shape[1] // LANES
    ck = min(KV_CHUNK, seq)
    tq = min(FLASH_ROWS // groups, ck)
    assert tq & (tq - 1) == 0 and ck % tq == 0 and seq % ck == 0
    nq = seq // tq
    n_chunks = seq // ck
    assert v_t.shape == (batch, kv_heads, n_chunks, LANES, ck)
    chains =max(c for c in range(1, FLASH_CHAINS + 1) if kv_heads % c == 0)
    qspec = pl.BlockSpec((tq, chains * groups * LANES), lambda b, h, i: (b * nq + i, h))
    return pl.pallas_call(
        functools.partial(_flash_kernel, chains=chains, groups=groups, tq=tq, ck=ck),
        grid=(batch, kv_heads // chains, nq),
        in_specs=[qspec, pl.BlockSpec((seq, chains * LANES), lambda b, h, i: (b, h)),
                  pl.BlockSpec((1, chains, n_chunks, LANES, ck), lambda b, h, i: (b, h, 0, 0, 0))],
        out_specs=qspec,
        out_shape=jax.ShapeDtypeStruct((t, qcols), BF16),
        scratch_shapes=[pltpu.VMEM((chains, ck, groups * tq), F32)] * 2,
        compiler_params=_params(("parallel", "parallel", "parallel"), VMEM_LIMIT),
        name=name,
    )(q, k, v_t)


def _window_kernel(q_ref, k_ref, vt_ref, o_ref, *, tq):
    i = pl.program_id(1)
    q0 = i * tq
    n_back = WINDOW // tq
    n_span = n_back + 1
    m_cols = NSA_GROUP * tq
    qs = [_stack_group(q_ref, h, NSA_GROUP) for h in range(NSA_KV_HEADS)]
    q_pos = q0 + _mod_pow2(lax.broadcasted_iota(jnp.int32, (tq, m_cols), 1), tq)
    key_in = lax.broadcasted_iota(jnp.int32, (tq, m_cols), 0)
    nt = (((1,), (1,)), ((), ()))

    def run(c0, clamped):
        start = c0 * tq if clamped else pl.multiple_of(c0 * tq, tq)
        ss = [lax.dot_general(k_ref[pl.ds(start, n_span * tq), h * LANES:(h + 1) * LANES], qs[h],
                              nt, preferred_element_type=F32) for h in range(NSA_KV_HEADS)]
        for h in range(NSA_KV_HEADS):
            blocks = []
            for j in range(n_span):
                blk = ss[h][j * tq:(j + 1) * tq]
                key_pos = start + j * tq + key_in
                if clamped:
                    blk = jnp.where((key_pos <= q_pos) & (key_pos > q_pos - WINDOW), blk, NEG)
                elif j == 0:
                    blk = jnp.where(key_pos > q_pos - WINDOW, blk, NEG)
                elif j == n_span - 1:
                    blk = jnp.where(key_pos <= q_pos, blk, NEG)
                blocks.append(blk)
            m = functools.reduce(jnp.maximum, [jnp.max(b, axis=0, keepdims=True) for b in blocks])
            acc = 0.0
            for j in range(n_span):
                p = jnp.exp(blocks[j] - m).astype(BF16)
                acc = acc + jnp.dot(vt_ref[0, h, c0 + j, 0:V_ROWS, :], p, preferred_element_type=F32)
            _store_heads_t(o_ref, acc[0:HEAD_DIM] / acc[HEAD_DIM:HEAD_DIM + 1], h, NSA_GROUP, tq)

    @pl.when(i < n_back)
    def _():
        run(0, True)

    @pl.when(i >= n_back)
    def _():
        run(i - n_back, False)


def _window(q, k, v_t, batch, seq):
    t = q.shape[0]
    tq = ATTN_TILE
    nq = seq // tq
    assert WINDOW % tq == 0 and seq >= WINDOW + tq
    wide = NSA_HEADS * LANES
    qspec = pl.BlockSpec((tq, wide), lambda b, i: (b * nq + i, 0))
    return pl.pallas_call(
        functools.partial(_window_kernel, tq=tq),
        grid=(batch, nq),
        in_specs=[qspec, pl.BlockSpec((seq, NSA_KV_HEADS * LANES), lambda b, i: (b, 0)),
                  pl.BlockSpec((1, NSA_KV_HEADS, nq, LANES, tq), lambda b, i: (b, 0, 0, 0, 0))],
        out_specs=qspec,
        out_shape=jax.ShapeDtypeStruct((t, wide), BF16),
        compiler_params=_params(("parallel", "parallel")),
        name="nsa_window",
    )(q, k, v_t)


def _mla_proj_kernel(cq_ref, ckv_ref, kra_ref, krb_ref, cm_ref, sm_ref, qg_ref, kg_ref,
                     wa_ref, wb_ref, wk_ref, wvt_ref, q_ref, k_ref, vt_ref, *, scale):
    qn = _rms(cq_ref[...], qg_ref[...]).astype(BF16)
    cn = _rms(ckv_ref[...], kg_ref[...]).astype(BF16)
    cm = cm_ref[...]
    sm = sm_ref[...]
    k_rot = kra_ref[...] * cm + krb_ref[...] * sm
    for h in range(MLA_HEADS):
        sl = slice(h * LANES, (h + 1) * LANES)
        qa = jnp.dot(qn, wa_ref[:, sl], preferred_element_type=F32)
        qb = jnp.dot(qn, wb_ref[:, sl], preferred_element_type=F32)
        q_ref[:, sl] = ((qa * cm + qb * sm) * scale).astype(q_ref.dtype)
        k_ref[:, sl] = (jnp.dot(cn, wk_ref[:, sl], preferred_element_type=F32) + k_rot).astype(k_ref.dtype)
    v_t = lax.dot_general(wvt_ref[...], cn, (((1,), (1,)), ((), ())), preferred_element_type=F32)
    v_t = (v_t + _ones_row(v_t.shape)).astype(vt_ref.dtype)
    for h in range(MLA_HEADS):
        vt_ref[0, h, 0] = v_t[h * LANES:(h + 1) * LANES, :]


def _mla_proj(cq, ckv, kra, krb, cmul, smul, qg, kg, wa, wb, wk, wvt, batch, seq):
    t = cq.shape[0]
    tm = ROW_TILE
    assert tm == min(KV_CHUNK, seq)
    nt = seq // tm
    width = MLA_HEADS * LANES
    rows = lambda n: pl.BlockSpec((tm, n), lambda i: (i, 0))
    scale = (MLA_NOPE + MLA_ROPE) ** -0.5
    return pl.pallas_call(
        functools.partial(_mla_proj_kernel, scale=scale),
        grid=(t // tm,),
        in_specs=[rows(Q_LORA), rows(KV_LORA), rows(LANES), rows(LANES), rows(LANES), rows(LANES),
                  _const_spec((1, Q_LORA)), _const_spec((1, KV_LORA)),
                  _const_spec((Q_LORA, width)), _const_spec((Q_LORA, width)),
                  _const_spec((KV_LORA, width)), _const_spec((width, KV_LORA))],
        out_specs=[rows(width), rows(width),
                   pl.BlockSpec((1, MLA_HEADS, 1, LANES, tm), lambda i: (i // nt, 0, i % nt, 0, 0))],
        out_shape=[jax.ShapeDtypeStruct((t, width), BF16)] * 2
        + [jax.ShapeDtypeStruct((batch, MLA_HEADS, nt, LANES, tm), BF16)],
        compiler_params=_params(("parallel",)),
        name="mla_proj",
    )(cq, ckv, kra, krb, cmul, smul, qg, kg, wa, wb, wk, wvt)


def _ssm_kernel(u_ref, bblk_ref, a_ref, ccat_ref, d_ref, wglu_ref, bglu_ref, o_ref,
                h_sc, st_sc, *, tc, nb):
    @pl.when(pl.program_id(0) == 0)
    def _():
        st_sc[...] = jnp.zeros_like(st_sc)

    n = SSM_LANES
    u = u_ref[...].reshape(tc * nb, SSM_WIDTH)
    h_sc[...] = jnp.dot(u.astype(BF16), bblk_ref[...], preferred_element_type=F32)
    ar = jnp.broadcast_to(a_ref[0:1, :], (nb, n))
    ai = jnp.broadcast_to(a_ref[1:2, :], (nb, n))

    def step(t, carry):
        hr, hi = carry
        r0 = pl.multiple_of(t * nb, nb)
        nr = ar * hr - ai * hi + h_sc[pl.ds(r0, nb), 0:n]
        ni = ar * hi + ai * hr + h_sc[pl.ds(r0, nb), n:2 * n]
        h_sc[pl.ds(r0, nb), 0:n] = nr
        h_sc[pl.ds(r0, nb), n:2 * n] = ni
        return nr, ni

    hr, hi = lax.fori_loop(0, tc, step, (st_sc[0], st_sc[1]), unroll=4)
    st_sc[0] = hr
    st_sc[1] = hi
    y = jnp.dot(h_sc[...].astype(BF16), ccat_ref[...], preferred_element_type=F32) + d_ref[...] * u
    z = _gelu(y)
    gate = jnp.dot(z.astype(BF16), wglu_ref[...], preferred_element_type=F32) + bglu_ref[...]
    o_ref[...] = (z * _sigmoid(gate)).reshape(tc, nb, SSM_WIDTH)


def _ssm(u_t, bblk, a_rows, ccat, d_row, wglu, bglu):
    seq, nb, _ = u_t.shape
    tc = SSM_CHUNK
    n = SSM_LANES
    uspec = pl.BlockSpec((tc, nb, SSM_WIDTH), lambda i: (i, 0, 0))
    return pl.pallas_call(
        functools.partial(_ssm_kernel, tc=tc, nb=nb),
        grid=(seq // tc,),
        in_specs=[uspec, _const_spec((SSM_WIDTH, 2 * n)), _const_spec((2, n)),
                  _const_spec((2 * n, SSM_WIDTH)), _const_spec((1, SSM_WIDTH)),
                  _const_spec((SSM_WIDTH, SSM_WIDTH)), _const_spec((1, SSM_WIDTH))],
        out_specs=uspec,
        out_shape=jax.ShapeDtypeStruct(u_t.shape, F32),
        scratch_shapes=[pltpu.VMEM((tc * nb, 2 * n), F32), pltpu.VMEM((2, nb, n), F32)],
        compiler_params=_params(("arbitrary",), VMEM_LIMIT),
        name="s5_scan",
    )(u_t, bblk, a_rows, ccat, d_row, wglu, bglu)


def _outproj_kernel(oc_ref, os_ref, ow_ref, gate_ref, gb_ref, ex_ref, om_ref, oz_ref, x_ref,
                    gn_ref, gm_ref, gz_ref, wn_ref, wm_ref, wz_ref, o_ref):
    g = _sigmoid(gate_ref[...] + gb_ref[...])
    g_hi = g.astype(BF16)
    g_lo = (g - g_hi.astype(F32)).astype(BF16)
    o_a = 0.0
    for r, br_ref in enumerate((oc_ref, os_ref, ow_ref)):
        spread = (jnp.dot(g_hi, ex_ref[r], preferred_element_type=F32)
                  + jnp.dot(g_lo, ex_ref[r], preferred_element_type=F32))
        o_a = o_a + spread * br_ref[...].astype(F32)
    inv = lax.rsqrt(jnp.sum(o_a * o_a, axis=-1, keepdims=True) * (1.0 / (NSA_HEADS * HEAD_DIM)) + EPS)
    acc = x_ref[...] + jnp.dot((o_a * inv * gn_ref[...]).astype(BF16), wn_ref[...],
                               preferred_element_type=F32)
    om = om_ref[...].astype(F32)
    inv = lax.rsqrt(jnp.sum(om * om, axis=-1, keepdims=True) * (1.0 / (MLA_HEADS * HEAD_DIM)) + EPS)
    acc = acc + jnp.dot((om * inv * gm_ref[...]).astype(BF16), wm_ref[...], preferred_element_type=F32)
    acc = acc + jnp.dot(_rms(oz_ref[...], gz_ref[...]).astype(BF16), wz_ref[...],
                        preferred_element_type=F32)
    o_ref[...] = acc


def _gate_spread():
    col = jnp.arange(LANES)[None, :, None]
    lane = jnp.arange(NSA_HEADS * LANES)[None, None, :]
    r = jnp.arange(N_BRANCH)[:, None, None]
    hit = (col == N_BRANCH * (lane // LANES) + r) & (lane % LANES < HEAD_DIM)
    return hit.astype(BF16)


def _outproj(oc, osel, ow, gate, gate_b, om, oz, x, gn, gm, gz, wn, wm, wz):
    t, d = x.shape
    tm = ROW_TILE
    nt = oz.shape[0] // tm
    wide = NSA_HEADS * LANES
    rows = lambda n: pl.BlockSpec((tm, n), lambda i: (i, 0))
    return pl.pallas_call(
        _outproj_kernel,
        grid=(t // tm,),
        in_specs=[rows(wide), rows(wide), rows(wide), rows(LANES), _const_spec((1, LANES)),
                  _const_spec((N_BRANCH, LANES, wide)), rows(wide),
                  pl.BlockSpec((tm, SSM_WIDTH), lambda i: (i % nt, i // nt)), rows(d),
                  _const_spec((1, wide)), _const_spec((1, wide)), _const_spec((1, SSM_WIDTH)),
                  _const_spec((wide, d)), _const_spec((wide, d)), _const_spec((SSM_WIDTH, d))],
        out_specs=rows(d),
        out_shape=jax.ShapeDtypeStruct((t, d), F32),
        compiler_params=_params(("parallel",), VMEM_LIMIT),
        name="out_proj",
    )(oc, osel, ow, gate, gate_b, _gate_spread(), om, oz, x, gn, gm, gz, wn, wm, wz)


def _ffn_kernel(x_ref, g_ref, wg_ref, wv_ref, cw_ref, cb_ref, wd_ref, o_ref, carry_sc,
                *, tm, tiles_per_seq, cf):
    @pl.when(lax.rem(pl.program_id(0), tiles_per_seq) == 0)
    def _():
        carry_sc[...] = jnp.zeros_like(carry_sc)

    x = x_ref[...]
    h = _rms(x, g_ref[...]).astype(BF16)
    o_ref[...] = x
    row = lax.broadcasted_iota(jnp.int32, (tm, cf), 0)
    n_chunks = D_FF // cf

    def up(c):
        sl = slice(c * cf, (c + 1) * cf)
        return (jnp.dot(h, wg_ref[:, sl], preferred_element_type=F32),
                jnp.dot(h, wv_ref[:, sl], preferred_element_type=F32))

    nxt = up(0)
    for c in range(n_chunks):
        sl = slice(c * cf, (c + 1) * cf)
        gate, val = nxt
        if c + 1 < n_chunks:
            nxt = up(c + 1)
        tail = carry_sc[:, sl]
        p1 = tail[7:8, :]
        p2 = tail[6:7, :]
        g1 = jnp.where(row == 0, p1, pltpu.roll(gate, 1, 0))
        g2 = jnp.where(row == 0, p2, jnp.where(row == 1, p1, pltpu.roll(gate, 2, 0)))
        carry_sc[:, sl] = gate[tm - 8:tm, :]
        gc = cw_ref[0:1, sl] * g2 + cw_ref[1:2, sl] * g1 + cw_ref[2:3, sl] * gate + cb_ref[:, sl]
        act = gc * _sigmoid(gc) * val
        o_ref[...] += jnp.dot(act.astype(BF16), wd_ref[sl, :], preferred_element_type=F32)


def _ffn(x, g, wg, wv, cw, cb, wd, seq):
    t, d = x.shape
    tm = ROW_TILE
    rows = pl.BlockSpec((tm, d), lambda i: (i, 0))
    return pl.pallas_call(
        functools.partial(_ffn_kernel, tm=tm, tiles_per_seq=seq // tm, cf=FF_CHUNK),
        grid=(t // tm,),
        in_specs=[rows, _const_spec((1, d)), _const_spec((d, D_FF)), _const_spec((d, D_FF)),
                  _const_spec((8, D_FF)), _const_spec((1, D_FF)), _const_spec((D_FF, d))],
        out_specs=rows,
        out_shape=jax.ShapeDtypeStruct((t, d), F32),
        scratch_shapes=[pltpu.VMEM((8, D_FF), F32)],
        compiler_params=_params(("arbitrary",), VMEM_LIMIT),
        name="conv_ffn",
    )(x, g, wg, wv, cw, cb, wd)


def _final_norm_kernel(x_ref, g_ref, o_ref):
    o_ref[...] = _rms(x_ref[...], g_ref[...])


def _final_norm(x, g):
    t, d = x.shape
    rows = pl.BlockSpec((ROW_TILE, d), lambda i: (i, 0))
    return pl.pallas_call(
        _final_norm_kernel,
        grid=(t // ROW_TILE,),
        in_specs=[rows, _const_spec((1, d))],
        out_specs=rows,
        out_shape=jax.ShapeDtypeStruct((t, d), F32),
        compiler_params=_params(("parallel",)),
        name="final_norm",
    )(x, g)


def _pad_heads(w, heads, width):
    lead = w.shape[:-1]
    w = w.reshape(lead + (heads, width))
    w = jnp.pad(w, [(0, 0)] * len(lead) + [(0, 0), (0, LANES - width)])
    return w.reshape(lead + (heads * LANES,))


def _pad_head_rows(w, heads, width):
    n = w.shape[-1]
    w = w.reshape(heads, width, n)
    w = jnp.pad(w, [(0, 0), (0, LANES - width), (0, 0)])
    return w.reshape(heads * LANES, n)


def _inproj_weight(w):
    sizes = (384, 128, 128, 128, 128, 128, 128, 18, Q_LORA, KV_LORA, MLA_ROPE, SSM_WIDTH)
    offs = [0]
    for n in sizes:
        offs.append(offs[-1] + n)
    (w_q, w_kc, w_vc, w_ks, w_vs, w_kw, w_vw, w_g, w_cq, w_ckv, w_kr, w_u) = [
        w[:, a:b] for a, b in zip(offs[:-1], offs[1:])]
    d = w.shape[0]
    half = MLA_ROPE // 2
    z64 = jnp.zeros((d, HEAD_DIM), w.dtype)
    z32 = jnp.zeros((d, LANES - HEAD_DIM - MLA_ROPE), w.dtype)
    r1, r2 = w_kr[:, :half], w_kr[:, half:]
    cols = [
        _pad_heads(w_q * (HEAD_DIM ** -0.5), NSA_HEADS, HEAD_DIM),
        w_kc, w_vc,
        _pad_heads(w_ks, NSA_KV_HEADS, HEAD_DIM),
        _pad_heads(w_kw, NSA_KV_HEADS, HEAD_DIM),
        jnp.pad(w_g, ((0, 0), (0, LANES - w_g.shape[1]))),
        w_cq, w_ckv,
        jnp.concatenate([z64, r1, r2, z32], axis=1),
        jnp.concatenate([z64, r2, r1, z32], axis=1),
        w_u,
    ]
    w_vt = jnp.concatenate([_pad_heads(w_vs, NSA_KV_HEADS, HEAD_DIM),
                            _pad_heads(w_vw, NSA_KV_HEADS, HEAD_DIM)], axis=1).T
    return jnp.concatenate(cols, axis=1).astype(BF16), w_vt.astype(BF16)


def _mla_weights(w_uq, w_uk, w_uv):
    half = MLA_ROPE // 2
    w = w_uq.reshape(Q_LORA, MLA_HEADS, MLA_NOPE + MLA_ROPE)
    nope, r1, r2 = w[..., :MLA_NOPE], w[..., MLA_NOPE:MLA_NOPE + half], w[..., MLA_NOPE + half:]
    z32 = jnp.zeros((Q_LORA, MLA_HEADS, LANES - MLA_NOPE - MLA_ROPE), w.dtype)
    wa = jnp.concatenate([nope, r1, r2, z32], axis=-1).reshape(Q_LORA, MLA_HEADS * LANES)
    wb = jnp.concatenate([jnp.zeros_like(nope), r2, r1, z32], axis=-1).reshape(Q_LORA, MLA_HEADS * LANES)
    wk = _pad_heads(w_uk, MLA_HEADS, MLA_NOPE)
    wvt = _pad_heads(w_uv, MLA_HEADS, HEAD_DIM).T
    return wa.astype(BF16), wb.astype(BF16), wk.astype(BF16), wvt.astype(BF16)


def _ssm_weights(log_dt, a_re, a_im, b_re, b_im, c_re, c_im, d):
    dt = jnp.exp(log_dt.astype(F32))[:, None]
    ar, ai = a_re.astype(F32), a_im.astype(F32)
    mag = jnp.exp(ar * dt)
    abr, abi = mag * jnp.cos(ai * dt), mag * jnp.sin(ai * dt)
    den = ar * ar + ai * ai
    fr = ((abr - 1.0) * ar + abi * ai) / den
    fi = (abi * ar - (abr - 1.0) * ai) / den
    br, bi = b_re.astype(F32), b_im.astype(F32)
    bbr = fr[..., None] * br - fi[..., None] * bi
    bbi = fr[..., None] * bi + fi[..., None] * br
    eye = jnp.eye(SSM_GROUPS, dtype=F32)
    blk_in = lambda m: jnp.einsum("gpc,gh->gchp", m, eye).reshape(SSM_WIDTH, SSM_LANES)
    blk_out = lambda m: jnp.einsum("gcp,gh->gphc", m, eye).reshape(SSM_LANES, SSM_WIDTH)
    bblk = jnp.concatenate([blk_in(bbr), blk_in(bbi)], axis=1)
    ccat = jnp.concatenate([blk_out(c_re.astype(F32)), -blk_out(c_im.astype(F32))], axis=0)
    a_rows = jnp.stack([abr.reshape(SSM_LANES), abi.reshape(SSM_LANES)])
    return bblk.astype(BF16), a_rows, ccat.astype(BF16), d.astype(F32).reshape(1, SSM_WIDTH)


def _overlap_t(nc_pad):
    start = jnp.arange(nc_pad) * CMP_STRIDE
    lo = jnp.arange(MAX_SEL_BLOCKS) * SEL_BLOCK
    hit = (start[None, :] < lo[:, None] + SEL_BLOCK) & (start[None, :] + CMP_BLOCK > lo[:, None])
    return hit.astype(BF16)


def _rope_rows():
    half = MLA_ROPE // 2
    inv_freq = ROPE_THETA ** (-jnp.arange(half, dtype=F32) / half)
    z64 = jnp.zeros((HEAD_DIM,), F32)
    z32 = jnp.zeros((LANES - HEAD_DIM - MLA_ROPE,), F32)
    invf = jnp.concatenate([z64, inv_freq, inv_freq, z32]).reshape(1, LANES)
    sign = jnp.concatenate([z64, -jnp.ones((half,), F32), jnp.ones((half,), F32), z32]).reshape(1, LANES)
    return invf, sign


def kernel(x, positions, attn_norm, w_in, nsa_pe, nsa_ck_w1, nsa_ck_b1, nsa_ck_w2, nsa_ck_b2, nsa_cv_w1, nsa_cv_b1, nsa_cv_w2, nsa_cv_b2, nsa_gate_b, mla_q_norm, mla_kv_norm, mla_w_uq, mla_w_uk, mla_w_uv, ssm_log_dt, ssm_a_re, ssm_a_im, ssm_b_re, ssm_b_im, ssm_c_re, ssm_c_im, ssm_d, ssm_w_glu, ssm_b_glu, out_norm_nsa, out_norm_mla, out_norm_ssm, w_out, ffn_norm, ffn_w_up, ffn_conv_w, ffn_conv_b, ffn_w_down, final_norm):
    batch, seq, d_model = x.shape
    depth = w_in.shape[0]
    t = batch * seq
    n_half = seq // CMP_STRIDE
    n_sel = seq // SEL_BLOCK
    assert seq % ROW_TILE == 0 and n_sel <= MAX_SEL_BLOCKS and n_half % LANES == 0
    n_top = min(SEL_TOP, n_sel)
    row = lambda v: v.astype(F32).reshape(1, -1)

    invf, sign = _rope_rows()
    pos_rows = jnp.broadcast_to(positions.astype(F32).reshape(t, 1), (t, LANES))
    cmul, smul = _trig(pos_rows, invf, sign)
    ovl_t = _overlap_t(n_half)

    xf = x.reshape(t, d_model)
    for l in range(depth):
        (q, kvc, ks, kw, gate, cq, ckv, kra, krb, u, vs_t, vw_t) = _inproj(
            xf, row(attn_norm[l]), *_inproj_weight(w_in[l]), batch, seq)

        kvcmp = _compress(
            kvc.reshape(batch, n_half, CMP_STRIDE * _CMP_STREAMS * HEAD_DIM),
            *_compress_weights(nsa_pe[l], nsa_ck_w1[l], nsa_ck_b1[l], nsa_ck_w2[l], nsa_ck_b2[l],
                               nsa_cv_w1[l], nsa_cv_b1[l], nsa_cv_w2[l], nsa_cv_b2[l]))
        vcmp_t = kvcmp[:, NSA_KV_HEADS:].transpose(0, 1, 3, 2).astype(BF16)
        o_cmp, q_aug = _cmp_topk(q, kvcmp, vcmp_t, ovl_t, batch, seq, n_top)
        o_sel = _flash(q_aug, ks, vs_t, batch, seq, NSA_GROUP, "nsa_selected")
        o_win = _window(q, kw, vw_t, batch, seq)

        wa, wb, wk, wvt = _mla_weights(mla_w_uq[l], mla_w_uk[l], mla_w_uv[l])
        q_m, k_m, vm_t = _mla_proj(cq, ckv, kra, krb, cmul, smul, row(mla_q_norm[l]),
                                   row(mla_kv_norm[l]), wa, wb, wk, wvt, batch, seq)
        o_mla = _flash(q_m, k_m, vm_t, batch, seq, 1, "mla_attention")

        bblk, a_rows, ccat, d_row = _ssm_weights(
            ssm_log_dt[l], ssm_a_re[l], ssm_a_im[l], ssm_b_re[l], ssm_b_im[l],
            ssm_c_re[l], ssm_c_im[l], ssm_d[l])
        o_ssm = _ssm(u.reshape(seq, batch, SSM_WIDTH), bblk, a_rows, ccat, d_row,
                     ssm_w_glu[l].astype(BF16), row(ssm_b_glu[l]))
        o_ssm = o_ssm.reshape(seq, batch * SSM_WIDTH)

        w_o = w_out[l].astype(BF16)
        n_a = NSA_HEADS * HEAD_DIM
        n_b = n_a + MLA_HEADS * HEAD_DIM
        xf = _outproj(
            o_cmp, o_sel, o_win, gate,
            jnp.pad(row(nsa_gate_b[l]), ((0, 0), (0, LANES - NSA_HEADS * N_BRANCH))),
            o_mla, o_ssm, xf,
            _pad_heads(row(out_norm_nsa[l]), NSA_HEADS, HEAD_DIM),
            _pad_heads(row(out_norm_mla[l]), MLA_HEADS, HEAD_DIM),
            row(out_norm_ssm[l]),
            _pad_head_rows(w_o[:n_a], NSA_HEADS, HEAD_DIM),
            _pad_head_rows(w_o[n_a:n_b], MLA_HEADS, HEAD_DIM),
            w_o[n_b:])

        w_up = ffn_w_up[l].astype(BF16)
        xf = _ffn(xf, row(ffn_norm[l]), w_up[:, :D_FF], w_up[:, D_FF:],
                  jnp.pad(ffn_conv_w[l].astype(F32), ((0, 8 - ffn_conv_w.shape[1]), (0, 0))),
                  row(ffn_conv_b[l]), ffn_w_down[l].astype(BF16), seq)

    return _final_norm(xf, row(final_norm)).reshape(batch, seq, d_model)
```

```python
import functools
import math

import jax
import jax.numpy as jnp
from jax import lax
from jax.experimental import pallas as pl
from jax.experimental.pallas import tpu as pltpu

F32 = jnp.float32
BF16 = jnp.bfloat16

LANES = 128
HEAD_DIM = 64
NSA_HEADS = 6
NSA_KV_HEADS = 2
NSA_GROUP = NSA_HEADS // NSA_KV_HEADS
N_BRANCH = 3
CMP_BLOCK = 32
CMP_STRIDE = 16
CMP_HIDDEN = 128
SEL_BLOCK = 64
SEL_TOP = 16
MAX_SEL_BLOCKS = 64
WINDOW = 512
MLA_HEADS = 6
MLA_NOPE = 64
MLA_ROPE = 32
Q_LORA = 384
KV_LORA = 128
ROPE_THETA = 10000.0
SSM_WIDTH = 256
SSM_GROUPS = 16
SSM_GROUP_CH = 16
SSM_STATE = 64
SSM_LANES = SSM_GROUPS * SSM_STATE
D_FF = 2816
EPS = 1e-6
NEG = -1e30
LOG2E = math.log2(math.e)

ROW_TILE = 512
ATTN_TILE = 256
KV_CHUNK = 512
FLASH_ROWS = 768
FLASH_CHAINS = 6
V_ROWS = 80
SSM_CHUNK = 64
FF_CHUNK = 256
VMEM_LIMIT = 56 * 1024 * 1024


def _params(sem, vmem=None):
    return pltpu.CompilerParams(dimension_semantics=sem, vmem_limit_bytes=vmem)


def _rms(x, g):
    return x * lax.rsqrt(jnp.mean(x * x, axis=-1, keepdims=True) + EPS) * g


def _gelu(x):
    c = math.sqrt(2.0 / math.pi)
    return 0.5 * x * (1.0 + jnp.tanh(c * (x + 0.044715 * (x * x * x))))


def _sigmoid(x):
    return 1.0 / (1.0 + jnp.exp(-x))


def _mod_pow2(x, n):
    assert n & (n - 1) == 0
    return jnp.bitwise_and(x, n - 1)


def _div_pow2(x, n):
    assert n & (n - 1) == 0
    return jnp.right_shift(x, n.bit_length() - 1)


def _const_spec(shape):
    nd = len(shape)
    return pl.BlockSpec(shape, lambda *_: (0,) * nd)


def _trig_kernel(pos_ref, invf_ref, sign_ref, c_ref, s_ref):
    ang = pos_ref[...] * invf_ref[...]
    c_ref[...] = jnp.cos(ang)
    s_ref[...] = jnp.sin(ang) * sign_ref[...]


def _trig(pos_rows, invf_row, sign_row):
    t = pos_rows.shape[0]
    spec = pl.BlockSpec((ROW_TILE, LANES), lambda i: (i, 0))
    return pl.pallas_call(
        _trig_kernel,
        grid=(t // ROW_TILE,),
        in_specs=[spec, _const_spec((1, LANES)), _const_spec((1, LANES))],
        out_specs=[spec, spec],
        out_shape=[jax.ShapeDtypeStruct((t, LANES), F32)] * 2,
        compiler_params=_params(("parallel",)),
        name="rope_trig",
    )(pos_rows, invf_row, sign_row)


_IN_SEGS = (
    ("q", NSA_HEADS * LANES, BF16),
    ("kvc", 4 * HEAD_DIM, F32),
    ("ks", NSA_KV_HEADS * LANES, BF16),
    ("kw", NSA_KV_HEADS * LANES, BF16),
    ("gate", LANES, F32),
    ("cq", Q_LORA, F32),
    ("ckv", KV_LORA, F32),
    ("kra", LANES, F32),
    ("krb", LANES, F32),
    ("u", SSM_WIDTH, F32),
)
_IN_COLS = sum(n for _, n, _ in _IN_SEGS)
_VT_ROWS = 2 * NSA_KV_HEADS * LANES


def _ones_row(shape):
    row = _mod_pow2(lax.broadcasted_iota(jnp.int32, shape, 0), LANES)
    return jnp.where(row == HEAD_DIM, 1.0, 0.0)


def _inproj_kernel(x_ref, g_ref, w_ref, wvt_ref, *o_refs, seq, tm):
    h = _rms(x_ref[...], g_ref[...]).astype(BF16)
    off = 0
    for (name, n, dt), o_ref in zip(_IN_SEGS, o_refs):
        y = jnp.dot(h, w_ref[:, off:off + n], preferred_element_type=F32)
        if name == "ks":
            s0 = lax.rem(pl.program_id(0) * tm, seq)
            blk = _div_pow2(s0 + lax.broadcasted_iota(jnp.int32, (tm, n), 0), SEL_BLOCK)
            lane = _mod_pow2(lax.broadcasted_iota(jnp.int32, (tm, n), 1), LANES)
            y = y + jnp.where(lane - HEAD_DIM == blk, 1.0, 0.0)
        o_ref[...] = y.astype(dt)
        off += n
    vst_ref, vwt_ref = o_refs[len(_IN_SEGS):]
    v_t = lax.dot_general(wvt_ref[...], h, (((1,), (1,)), ((), ())), preferred_element_type=F32)
    v_t = (v_t + _ones_row(v_t.shape)).astype(BF16)
    wc = vwt_ref.shape[-1]
    for hh in range(NSA_KV_HEADS):
        vst_ref[0, hh, 0] = v_t[hh * LANES:(hh + 1) * LANES, :]
        r0 = (NSA_KV_HEADS + hh) * LANES
        for c in range(tm // wc):
            vwt_ref[0, hh, c] = v_t[r0:r0 + LANES, c * wc:(c + 1) * wc]


def _inproj(x, g, w, wvt, batch, seq):
    t, d = x.shape
    tm = ROW_TILE
    assert tm == min(KV_CHUNK, seq) and tm % ATTN_TILE == 0
    nt = seq // tm
    rows = lambda n: pl.BlockSpec((tm, n), lambda i: (i, 0))
    out_specs = [rows(n) for _, n, _ in _IN_SEGS]
    out_shape = [jax.ShapeDtypeStruct((t, n), dt) for _, n, dt in _IN_SEGS]
    out_specs[-1] = pl.BlockSpec((tm, SSM_WIDTH), lambda i: (i % nt, i // nt))
    out_shape[-1] = jax.ShapeDtypeStruct((seq, batch * SSM_WIDTH), F32)
    wpt = tm // ATTN_TILE
    out_specs += [pl.BlockSpec((1, NSA_KV_HEADS, 1, LANES, tm), lambda i: (i // nt, 0, i % nt, 0, 0)),
                  pl.BlockSpec((1, NSA_KV_HEADS, wpt, LANES, ATTN_TILE),
                               lambda i: (i // nt, 0, i % nt, 0, 0))]
    out_shape += [jax.ShapeDtypeStruct((batch, NSA_KV_HEADS, nt, LANES, tm), BF16),
                  jax.ShapeDtypeStruct((batch, NSA_KV_HEADS, seq // ATTN_TILE, LANES, ATTN_TILE), BF16)]
    return pl.pallas_call(
        functools.partial(_inproj_kernel, seq=seq, tm=tm),
        grid=(t // tm,),
        in_specs=[pl.BlockSpec((tm, d), lambda i: (i, 0)), _const_spec((1, d)),
                  _const_spec((d, _IN_COLS)), _const_spec((_VT_ROWS, d))],
        out_specs=out_specs,
        out_shape=out_shape,
        compiler_params=_params(("parallel",), VMEM_LIMIT),
        name="in_proj",
    )(x, g, w, wvt)


_CMP_STREAMS = 2 * NSA_KV_HEADS


def _compress_kernel(x_ref, pe_ref, w1_ref, b1_ref, w2_ref, b2_ref, o_ref):
    x = x_ref[0]
    n = x.shape[0]
    top = jnp.dot((x + pe_ref[0:1, :]).astype(BF16), w1_ref[0], preferred_element_type=F32)
    bot = jnp.dot((x + pe_ref[1:2, :]).astype(BF16), w1_ref[1], preferred_element_type=F32)
    hid = _gelu(top + pltpu.roll(bot, n - 1, 0) + b1_ref[...]).astype(BF16)
    for j in range(_CMP_STREAMS):
        kind = j // NSA_KV_HEADS
        o_ref[0, j] = (jnp.dot(hid[:, j * CMP_HIDDEN:(j + 1) * CMP_HIDDEN], w2_ref[kind],
                               preferred_element_type=F32) + b2_ref[kind])


def _compress(x, pe, w1, b1, w2, b2):
    b, n, width = x.shape
    hid = _CMP_STREAMS * CMP_HIDDEN
    return pl.pallas_call(
        _compress_kernel,
        grid=(b,),
        in_specs=[pl.BlockSpec((1, n, width), lambda i: (i, 0, 0)),
                  _const_spec((2, width)), _const_spec((2, width, hid)), _const_spec((1, hid)),
                  _const_spec((2, CMP_HIDDEN, LANES)), _const_spec((2, 1, LANES))],
        out_specs=pl.BlockSpec((1, _CMP_STREAMS, n, LANES), lambda i: (i, 0, 0, 0)),
        out_shape=jax.ShapeDtypeStruct((b, _CMP_STREAMS, n, LANES), F32),
        compiler_params=_params(("parallel",), VMEM_LIMIT),
        name="nsa_compress",
    )(x, pe, w1, b1, w2, b2)


def _compress_weights(pe, ck_w1, ck_b1, ck_w2, ck_b2, cv_w1, cv_b1, cv_w2, cv_b2):
    eye = jnp.eye(_CMP_STREAMS, dtype=F32)
    per_stream = jnp.stack([ck_w1, ck_w1, cv_w1, cv_w1]).astype(F32)

    def place(w):
        return jnp.einsum("jldf,jk->ljdkf", w, eye).reshape(
            CMP_STRIDE * _CMP_STREAMS * HEAD_DIM, _CMP_STREAMS * CMP_HIDDEN)

    w1 = jnp.stack([place(per_stream[:, :CMP_STRIDE]), place(per_stream[:, CMP_STRIDE:])])
    spread = lambda p: jnp.broadcast_to(p[:, None, :], (CMP_STRIDE, _CMP_STREAMS, HEAD_DIM)).reshape(-1)
    pe2 = jnp.stack([spread(pe[:CMP_STRIDE]), spread(pe[CMP_STRIDE:])]).astype(F32)
    b1 = jnp.concatenate([ck_b1, ck_b1, cv_b1, cv_b1]).astype(F32).reshape(1, -1)
    pad2 = lambda w: jnp.pad(w.astype(F32), ((0, 0), (0, LANES - HEAD_DIM)))
    w2 = jnp.stack([pad2(ck_w2), pad2(cv_w2)]).astype(BF16)
    b2 = jnp.stack([pad2(ck_b2.reshape(1, -1)), pad2(cv_b2.reshape(1, -1))])
    return pe2, w1.astype(BF16), b1, w2, b2


def _stack_group(q_ref, h, groups):
    return jnp.concatenate([q_ref[:, (h * groups + g) * LANES:(h * groups + g + 1) * LANES]
                            for g in range(groups)], axis=0)


def _store_heads_t(o_ref, chains_t, tq):
    heads = [c[:, g * tq:(g + 1) * tq] for c in chains_t for g in range(c.shape[1] // tq)]
    for p in range(len(heads) // 2):
        pair = jnp.concatenate([heads[2 * p], heads[2 * p + 1]], axis=0)
        o_ref[:, p * LANES:(p + 1) * LANES] = pair.T.astype(o_ref.dtype)


def _stable_rank(val):
    sub = 8
    n_blk = val.shape[0] // sub
    blocks = [val[r * sub:(r + 1) * sub] for r in range(n_blk)]
    ranks = [jnp.zeros(blocks[0].shape, F32) for _ in range(n_blk)]
    row_in = lax.broadcasted_iota(jnp.int32, blocks[0].shape, 0)
    for i in range(val.shape[0]):
        row = val[i:i + 1, :]
        for r in range(n_blk):
            if r * sub > i:
                ahead = jnp.where(row >= blocks[r], 1.0, 0.0)
            elif (r + 1) * sub <= i:
                ahead = jnp.where(row > blocks[r], 1.0, 0.0)
            else:
                ahead = jnp.where(row_in + r * sub > i, jnp.where(row >= blocks[r], 1.0, 0.0),
                                  jnp.where(row > blocks[r], 1.0, 0.0))
            ranks[r] = ranks[r] + ahead
    return jnp.concatenate(ranks, axis=0)


def _cmp_topk_kernel(q_ref, kc_ref, vct_ref, ovl_ref, o_ref, qa_ref, *, tq, n_top):
    q0 = pl.program_id(1) * tq
    m_cols = NSA_GROUP * tq
    nc = kc_ref.shape[2]
    nt = (((1,), (1,)), ((), ()))
    qs = [_stack_group(q_ref, h, NSA_GROUP) for h in range(NSA_KV_HEADS)]
    ss = [lax.dot_general(kc_ref[0, h].astype(BF16), qs[h], nt, preferred_element_type=F32)
          for h in range(NSA_KV_HEADS)]
    t_col = q0 + _mod_pow2(lax.broadcasted_iota(jnp.int32, (nc, m_cols), 1), tq)
    n_row = lax.broadcasted_iota(jnp.int32, (nc, m_cols), 0)
    mask = n_row * CMP_STRIDE + (CMP_BLOCK - 1) <= t_col
    j_idx = lax.broadcasted_iota(jnp.int32, (MAX_SEL_BLOCKS, tq), 0)
    cur = _div_pow2(q0 + lax.broadcasted_iota(jnp.int32, (MAX_SEL_BLOCKS, tq), 1), SEL_BLOCK)
    forced = (j_idx == 0) | (j_idx == cur) | (j_idx == cur - 1)
    outs = []
    for h in range(NSA_KV_HEADS):
        sm = jnp.where(mask, ss[h], NEG)
        e = jnp.exp2(sm - jnp.max(sm, axis=0, keepdims=True))
        p = jnp.where(mask, e * (1.0 / jnp.sum(e, axis=0, keepdims=True)), 0.0)
        outs.append(jnp.dot(vct_ref[0, h, 0:HEAD_DIM, :], p.astype(BF16),
                            preferred_element_type=F32))

        psum = p[:, 0:tq] + p[:, tq:2 * tq] + p[:, 2 * tq:3 * tq]
        hi = psum.astype(BF16)
        lo = (psum - hi.astype(F32)).astype(BF16)
        imp = (jnp.dot(ovl_ref[...], hi, preferred_element_type=F32)
               + jnp.dot(ovl_ref[...], lo, preferred_element_type=F32))
        val = jnp.where(forced, jnp.inf, jnp.where(j_idx > cur, -jnp.inf, imp))
        bias_t = jnp.where(_stable_rank(val) < n_top, 0.0, NEG)
        bias = jnp.concatenate([jnp.zeros_like(bias_t), bias_t], axis=0).T
        for g in range(NSA_GROUP):
            lane0 = (h * NSA_GROUP + g) * LANES
            qa_ref[:, lane0:lane0 + LANES] = (q_ref[:, lane0:lane0 + LANES].astype(F32)
                                              + bias).astype(qa_ref.dtype)
    _store_heads_t(o_ref, outs, tq)


def _cmp_topk(q, kcmp, vcmp_t, ovl, batch, seq, n_top):
    t = q.shape[0]
    tq = ATTN_TILE
    nq = seq // tq
    nc = kcmp.shape[2]
    wide = NSA_HEADS * LANES
    qspec = pl.BlockSpec((tq, wide), lambda b, i: (b * nq + i, 0))
    return pl.pallas_call(
        functools.partial(_cmp_topk_kernel, tq=tq, n_top=n_top),
        grid=(batch, nq),
        in_specs=[qspec,
                  pl.BlockSpec((1, NSA_KV_HEADS, nc, LANES), lambda b, i: (b, 0, 0, 0)),
                  pl.BlockSpec((1, NSA_KV_HEADS, LANES, nc), lambda b, i: (b, 0, 0, 0)),
                  _const_spec((MAX_SEL_BLOCKS, nc))],
        out_specs=[pl.BlockSpec((tq, NSA_HEADS * HEAD_DIM), lambda b, i: (b * nq + i, 0)), qspec],
        out_shape=[jax.ShapeDtypeStruct((t, NSA_HEADS * HEAD_DIM), BF16),
                   jax.ShapeDtypeStruct((t, wide), BF16)],
        compiler_params=_params(("parallel", "parallel")),
        name="nsa_cmp_topk",
    )(q, kcmp, vcmp_t, ovl)


def _flash_kernel(q_ref, k_ref, vt_ref, o_ref, s_a, s_b, *, chains, groups, tq, ck):
    q0 = pl.program_id(2) * tq
    m_cols = groups * tq
    qs = [_stack_group(q_ref, h, groups) for h in range(chains)]

    def issue_scores(c, s_ref):
        k0 = pl.multiple_of(c * ck, ck)
        for h in range(chains):
            s_ref[h] = lax.dot_general(k_ref[pl.ds(k0, ck), h * LANES:(h + 1) * LANES], qs[h],
                                       (((1,), (1,)), ((), ())), preferred_element_type=F32)

    def absorb(c, s_ref, stats, visible=None):
        out = []
        for h in range(chains):
            s = s_ref[h]
            if visible is not None:
                s = jnp.where(visible, s, NEG)
            m_old, acc = stats[h]
            m_new = jnp.maximum(m_old, jnp.max(s, axis=0, keepdims=True))
            alpha = jnp.exp2(m_old - m_new)
            p = jnp.exp2(s - m_new).astype(BF16)
            vt = vt_ref[0, h, c, 0:V_ROWS, :]
            out.append((m_new, alpha * acc + jnp.dot(vt, p, preferred_element_type=F32)))
        return tuple(out)

    def finish(stats):
        _store_heads_t(o_ref, [acc[0:HEAD_DIM] / acc[HEAD_DIM:HEAD_DIM + 1] for _, acc in stats], tq)

    last = (q0 + tq - 1) // ck
    key_pos = last * ck + lax.broadcasted_iota(jnp.int32, (ck, m_cols), 0)
    q_pos = q0 + _mod_pow2(lax.broadcasted_iota(jnp.int32, (ck, m_cols), 1), tq)
    visible = key_pos <= q_pos

    def pair(j, stats):
        c = 2 * j
        issue_scores(c + 1, s_b)
        stats = absorb(c, s_a, stats)
        issue_scores(c + 2, s_a)
        return absorb(c + 1, s_b, stats)

    issue_scores(0, s_a)
    stats = tuple((jnp.full((1, m_cols), NEG, F32), jnp.zeros((V_ROWS, m_cols), F32))
                  for _ in range(chains))
    stats = lax.fori_loop(0, last // 2, pair, stats)

    @pl.when(lax.rem(last, 2) == 0)
    def _():
        finish(absorb(last, s_a, stats, visible))

    @pl.when(lax.rem(last, 2) == 1)
    def _():
        issue_scores(last, s_b)
        finish(absorb(last, s_b, absorb(last - 1, s_a, stats), visible))


def _flash(q, k, v_t, batch, seq, groups, name):
    t, qcols = q.shape
    kv_heads = k.shape[1] // LANES
    ck = min(KV_CHUNK, seq)
    tq = min(FLASH_ROWS // groups, ck)
    assert tq & (tq - 1) == 0 and ck % tq == 0 and seq % ck == 0
    nq = seq // tq
    n_chunks = seq // ck
    assert v_t.shape == (batch, kv_heads, n_chunks, LANES, ck)
    chains = max(c for c in range(1, FLASH_CHAINS + 1)
                 if kv_heads % c == 0 and (c * groups) % 2 == 0)
    heads = kv_heads * groups
    return pl.pallas_call(
        functools.partial(_flash_kernel, chains=chains, groups=groups, tq=tq, ck=ck),
        grid=(batch, kv_heads // chains, nq),
        in_specs=[pl.BlockSpec((tq, chains * groups * LANES), lambda b, h, i: (b * nq + i, h)),
                  pl.BlockSpec((seq, chains * LANES), lambda b, h, i: (b, h)),
                  pl.BlockSpec((1, chains, n_chunks, LANES, ck), lambda b, h, i: (b, h, 0, 0, 0))],
        out_specs=pl.BlockSpec((tq, chains * groups * HEAD_DIM), lambda b, h, i: (b * nq + i, h)),
        out_shape=jax.ShapeDtypeStruct((t, heads * HEAD_DIM), BF16),
        scratch_shapes=[pltpu.VMEM((chains, ck, groups * tq), F32)] * 2,
        compiler_params=_params(("parallel", "parallel", "parallel"), VMEM_LIMIT),
        name=name,
    )(q, k, v_t)


def _window_kernel(q_ref, k_ref, vt_ref, o_ref, *, tq):
    i = pl.program_id(1)
    q0 = i * tq
    n_back = WINDOW // tq
    n_span = n_back + 1
    m_cols = NSA_GROUP * tq
    qs = [_stack_group(q_ref, h, NSA_GROUP) for h in range(NSA_KV_HEADS)]
    q_pos = q0 + _mod_pow2(lax.broadcasted_iota(jnp.int32, (tq, m_cols), 1), tq)
    key_in = lax.broadcasted_iota(jnp.int32, (tq, m_cols), 0)
    nt = (((1,), (1,)), ((), ()))

    def run(c0, clamped):
        start = c0 * tq if clamped else pl.multiple_of(c0 * tq, tq)
        ss = [lax.dot_general(k_ref[pl.ds(start, n_span * tq), h * LANES:(h + 1) * LANES], qs[h],
                              nt, preferred_element_type=F32) for h in range(NSA_KV_HEADS)]
        outs = []
        for h in range(NSA_KV_HEADS):
            blocks = []
            for j in range(n_span):
                blk = ss[h][j * tq:(j + 1) * tq]
                key_pos = start + j * tq + key_in
                if clamped:
                    blk = jnp.where((key_pos <= q_pos) & (key_pos > q_pos - WINDOW), blk, NEG)
                elif j == 0:
                    blk = jnp.where(key_pos > q_pos - WINDOW, blk, NEG)
                elif j == n_span - 1:
                    blk = jnp.where(key_pos <= q_pos, blk, NEG)
                blocks.append(blk)
            m = functools.reduce(jnp.maximum, [jnp.max(b, axis=0, keepdims=True) for b in blocks])
            acc = 0.0
            for j in range(n_span):
                p = jnp.exp2(blocks[j] - m).astype(BF16)
                acc = acc + jnp.dot(vt_ref[0, h, c0 + j, 0:V_ROWS, :], p, preferred_element_type=F32)
            outs.append(acc[0:HEAD_DIM] / acc[HEAD_DIM:HEAD_DIM + 1])
        _store_heads_t(o_ref, outs, tq)

    @pl.when(i < n_back)
    def _():
        run(0, True)

    @pl.when(i >= n_back)
    def _():
        run(i - n_back, False)


def _window(q, k, v_t, batch, seq):
    t = q.shape[0]
    tq = ATTN_TILE
    nq = seq // tq
    assert WINDOW % tq == 0 and seq >= WINDOW + tq
    wide = NSA_HEADS * LANES
    qspec = pl.BlockSpec((tq, wide), lambda b, i: (b * nq + i, 0))
    return pl.pallas_call(
        functools.partial(_window_kernel, tq=tq),
        grid=(batch, nq),
        in_specs=[qspec, pl.BlockSpec((seq, NSA_KV_HEADS * LANES), lambda b, i: (b, 0)),
                  pl.BlockSpec((1, NSA_KV_HEADS, nq, LANES, tq), lambda b, i: (b, 0, 0, 0, 0))],
        out_specs=pl.BlockSpec((tq, NSA_HEADS * HEAD_DIM), lambda b, i: (b * nq + i, 0)),
        out_shape=jax.ShapeDtypeStruct((t, NSA_HEADS * HEAD_DIM), BF16),
        compiler_params=_params(("parallel", "parallel")),
        name="nsa_window",
    )(q, k, v_t)


def _mla_proj_kernel(cq_ref, ckv_ref, kra_ref, krb_ref, cm_ref, sm_ref, qg_ref, kg_ref,
                     wa_ref, wb_ref, wk_ref, wvt_ref, q_ref, k_ref, vt_ref, *, scale):
    qn = _rms(cq_ref[...], qg_ref[...]).astype(BF16)
    cn = _rms(ckv_ref[...], kg_ref[...]).astype(BF16)
    cm = cm_ref[...]
    sm = sm_ref[...]
    k_rot = kra_ref[...] * cm + krb_ref[...] * sm
    for h in range(MLA_HEADS):
        sl = slice(h * LANES, (h + 1) * LANES)
        qa = jnp.dot(qn, wa_ref[:, sl], preferred_element_type=F32)
        qb = jnp.dot(qn, wb_ref[:, sl], preferred_element_type=F32)
        q_ref[:, sl] = ((qa * cm + qb * sm) * scale).astype(q_ref.dtype)
        k_ref[:, sl] = (jnp.dot(cn, wk_ref[:, sl], preferred_element_type=F32) + k_rot).astype(k_ref.dtype)
    v_t = lax.dot_general(wvt_ref[...], cn, (((1,), (1,)), ((), ())), preferred_element_type=F32)
    v_t = (v_t + _ones_row(v_t.shape)).astype(vt_ref.dtype)
    for h in range(MLA_HEADS):
        vt_ref[0, h, 0] = v_t[h * LANES:(h + 1) * LANES, :]


def _mla_proj(cq, ckv, kra, krb, cmul, smul, qg, kg, wa, wb, wk, wvt, batch, seq):
    t = cq.shape[0]
    tm = ROW_TILE
    assert tm == min(KV_CHUNK, seq)
    nt = seq // tm
    width = MLA_HEADS * LANES
    rows = lambda n: pl.BlockSpec((tm, n), lambda i: (i, 0))
    scale = (MLA_NOPE + MLA_ROPE) ** -0.5 * LOG2E
    return pl.pallas_call(
        functools.partial(_mla_proj_kernel, scale=scale),
        grid=(t // tm,),
        in_specs=[rows(Q_LORA), rows(KV_LORA), rows(LANES), rows(LANES), rows(LANES), rows(LANES),
                  _const_spec((1, Q_LORA)), _const_spec((1, KV_LORA)),
                  _const_spec((Q_LORA, width)), _const_spec((Q_LORA, width)),
                  _const_spec((KV_LORA, width)), _const_spec((width, KV_LORA))],
        out_specs=[rows(width), rows(width),
                   pl.BlockSpec((1, MLA_HEADS, 1, LANES, tm), lambda i: (i // nt, 0, i % nt, 0, 0))],
        out_shape=[jax.ShapeDtypeStruct((t, width), BF16)] * 2
        + [jax.ShapeDtypeStruct((batch, MLA_HEADS, nt, LANES, tm), BF16)],
        compiler_params=_params(("parallel",)),
        name="mla_proj",
    )(cq, ckv, kra, krb, cmul, smul, qg, kg, wa, wb, wk, wvt)


def _ssm_kernel(u_ref, bblk_ref, a_ref, ccat_ref, d_ref, wglu_ref, bglu_ref, o_ref,
                h_sc, st_sc, *, tc, nb):
    @pl.when(pl.program_id(0) == 0)
    def _():
        st_sc[...] = jnp.zeros_like(st_sc)

    n = SSM_LANES
    u = u_ref[...].reshape(tc * nb, SSM_WIDTH)
    h_sc[...] = jnp.dot(u.astype(BF16), bblk_ref[...], preferred_element_type=F32)
    ar = jnp.broadcast_to(a_ref[0:1, :], (nb, n))
    ai = jnp.broadcast_to(a_ref[1:2, :], (nb, n))

    def step(t, carry):
        hr, hi = carry
        r0 = pl.multiple_of(t * nb, nb)
        nr = ar * hr - ai * hi + h_sc[pl.ds(r0, nb), 0:n]
        ni = ar * hi + ai * hr + h_sc[pl.ds(r0, nb), n:2 * n]
        h_sc[pl.ds(r0, nb), 0:n] = nr
        h_sc[pl.ds(r0, nb), n:2 * n] = ni
        return nr, ni

    hr, hi = lax.fori_loop(0, tc, step, (st_sc[0], st_sc[1]), unroll=4)
    st_sc[0] = hr
    st_sc[1] = hi
    y = jnp.dot(h_sc[...].astype(BF16), ccat_ref[...], preferred_element_type=F32) + d_ref[...] * u
    z = _gelu(y)
    gate = jnp.dot(z.astype(BF16), wglu_ref[...], preferred_element_type=F32) + bglu_ref[...]
    o_ref[...] = (z * _sigmoid(gate)).reshape(tc, nb, SSM_WIDTH)


def _ssm(u_t, bblk, a_rows, ccat, d_row, wglu, bglu):
    seq, nb, _ = u_t.shape
    tc = SSM_CHUNK
    n = SSM_LANES
    uspec = pl.BlockSpec((tc, nb, SSM_WIDTH), lambda i: (i, 0, 0))
    return pl.pallas_call(
        functools.partial(_ssm_kernel, tc=tc, nb=nb),
        grid=(seq // tc,),
        in_specs=[uspec, _const_spec((SSM_WIDTH, 2 * n)), _const_spec((2, n)),
                  _const_spec((2 * n, SSM_WIDTH)), _const_spec((1, SSM_WIDTH)),
                  _const_spec((SSM_WIDTH, SSM_WIDTH)), _const_spec((1, SSM_WIDTH))],
        out_specs=uspec,
        out_shape=jax.ShapeDtypeStruct(u_t.shape, F32),
        scratch_shapes=[pltpu.VMEM((tc * nb, 2 * n), F32), pltpu.VMEM((2, nb, n), F32)],
        compiler_params=_params(("arbitrary",), VMEM_LIMIT),
        name="s5_scan",
    )(u_t, bblk, a_rows, ccat, d_row, wglu, bglu)


def _outproj_kernel(oc_ref, os_ref, ow_ref, gate_ref, gb_ref, ex_ref, om_ref, oz_ref, x_ref,
                    gn_ref, gm_ref, gz_ref, wn_ref, wm_ref, wz_ref, o_ref):
    g = _sigmoid(gate_ref[...] + gb_ref[...])
    g_hi = g.astype(BF16)
    g_lo = (g - g_hi.astype(F32)).astype(BF16)
    o_a = 0.0
    for r, br_ref in enumerate((oc_ref, os_ref, ow_ref)):
        spread = (jnp.dot(g_hi, ex_ref[r], preferred_element_type=F32)
                  + jnp.dot(g_lo, ex_ref[r], preferred_element_type=F32))
        o_a = o_a + spread * br_ref[...].astype(F32)
    inv = lax.rsqrt(jnp.sum(o_a * o_a, axis=-1, keepdims=True) * (1.0 / (NSA_HEADS * HEAD_DIM)) + EPS)
    acc = x_ref[...] + jnp.dot((o_a * inv * gn_ref[...]).astype(BF16), wn_ref[...],
                               preferred_element_type=F32)
    om = om_ref[...].astype(F32)
    inv = lax.rsqrt(jnp.sum(om * om, axis=-1, keepdims=True) * (1.0 / (MLA_HEADS * HEAD_DIM)) + EPS)
    acc = acc + jnp.dot((om * inv * gm_ref[...]).astype(BF16), wm_ref[...], preferred_element_type=F32)
    acc = acc + jnp.dot(_rms(oz_ref[...], gz_ref[...]).astype(BF16), wz_ref[...],
                        preferred_element_type=F32)
    o_ref[...] = acc


def _gate_spread():
    col = jnp.arange(LANES)[None, :, None]
    lane = jnp.arange(NSA_HEADS * HEAD_DIM)[None, None, :]
    r = jnp.arange(N_BRANCH)[:, None, None]
    return (col == N_BRANCH * (lane // HEAD_DIM) + r).astype(BF16)


def _outproj(oc, osel, ow, gate, gate_b, om, oz, x, gn, gm, gz, wn, wm, wz):
    t, d = x.shape
    tm = ROW_TILE
    nt = oz.shape[0] // tm
    wide = NSA_HEADS * HEAD_DIM
    assert wide == MLA_HEADS * HEAD_DIM
    rows = lambda n: pl.BlockSpec((tm, n), lambda i: (i, 0))
    return pl.pallas_call(
        _outproj_kernel,
        grid=(t // tm,),
        in_specs=[rows(wide), rows(wide), rows(wide), rows(LANES), _const_spec((1, LANES)),
                  _const_spec((N_BRANCH, LANES, wide)), rows(wide),
                  pl.BlockSpec((tm, SSM_WIDTH), lambda i: (i % nt, i // nt)), rows(d),
                  _const_spec((1, wide)), _const_spec((1, wide)), _const_spec((1, SSM_WIDTH)),
                  _const_spec((wide, d)), _const_spec((wide, d)), _const_spec((SSM_WIDTH, d))],
        out_specs=rows(d),
        out_shape=jax.ShapeDtypeStruct((t, d), F32),
        compiler_params=_params(("parallel",), VMEM_LIMIT),
        name="out_proj",
    )(oc, osel, ow, gate, gate_b, _gate_spread(), om, oz, x, gn, gm, gz, wn, wm, wz)


def _ffn_kernel(x_ref, g_ref, wg_ref, wv_ref, cw_ref, cb_ref, wd_ref, o_ref, carry_sc,
                *, tm, tiles_per_seq, cf):
    @pl.when(lax.rem(pl.program_id(0), tiles_per_seq) == 0)
    def _():
        carry_sc[...] = jnp.zeros_like(carry_sc)

    x = x_ref[...]
    h = _rms(x, g_ref[...]).astype(BF16)
    o_ref[...] = x
    row = lax.broadcasted_iota(jnp.int32, (tm, cf), 0)
    n_chunks = D_FF // cf

    def up(c):
        sl = slice(c * cf, (c + 1) * cf)
        return (jnp.dot(h, wg_ref[:, sl], preferred_element_type=F32),
                jnp.dot(h, wv_ref[:, sl], preferred_element_type=F32))

    nxt = up(0)
    for c in range(n_chunks):
        sl = slice(c * cf, (c + 1) * cf)
        gate, val = nxt
        if c + 1 < n_chunks:
            nxt = up(c + 1)
        tail = carry_sc[:, sl]
        p1 = tail[7:8, :]
        p2 = tail[6:7, :]
        g1 = jnp.where(row == 0, p1, pltpu.roll(gate, 1, 0))
        g2 = jnp.where(row == 0, p2, jnp.where(row == 1, p1, pltpu.roll(gate, 2, 0)))
        carry_sc[:, sl] = gate[tm - 8:tm, :]
        gc = cw_ref[0:1, sl] * g2 + cw_ref[1:2, sl] * g1 + cw_ref[2:3, sl] * gate + cb_ref[:, sl]
        act = gc * _sigmoid(gc) * val
        o_ref[...] += jnp.dot(act.astype(BF16), wd_ref[sl, :], preferred_element_type=F32)


def _ffn(x, g, wg, wv, cw, cb, wd, seq):
    t, d = x.shape
    tm = ROW_TILE
    rows = pl.BlockSpec((tm, d), lambda i: (i, 0))
    return pl.pallas_call(
        functools.partial(_ffn_kernel, tm=tm, tiles_per_seq=seq // tm, cf=FF_CHUNK),
        grid=(t // tm,),
        in_specs=[rows, _const_spec((1, d)), _const_spec((d, D_FF)), _const_spec((d, D_FF)),
                  _const_spec((8, D_FF)), _const_spec((1, D_FF)), _const_spec((D_FF, d))],
        out_specs=rows,
        out_shape=jax.ShapeDtypeStruct((t, d), F32),
        scratch_shapes=[pltpu.VMEM((8, D_FF), F32)],
        compiler_params=_params(("arbitrary",), VMEM_LIMIT),
        name="conv_ffn",
    )(x, g, wg, wv, cw, cb, wd)


def _final_norm_kernel(x_ref, g_ref, o_ref):
    o_ref[...] = _rms(x_ref[...], g_ref[...])


def _final_norm(x, g):
    t, d = x.shape
    rows = pl.BlockSpec((ROW_TILE, d), lambda i: (i, 0))
    return pl.pallas_call(
        _final_norm_kernel,
        grid=(t // ROW_TILE,),
        in_specs=[rows, _const_spec((1, d))],
        out_specs=rows,
        out_shape=jax.ShapeDtypeStruct((t, d), F32),
        compiler_params=_params(("parallel",)),
        name="final_norm",
    )(x, g)


def _pad_heads(w, heads, width):
    lead = w.shape[:-1]
    w = w.reshape(lead + (heads, width))
    w = jnp.pad(w, [(0, 0)] * len(lead) + [(0, 0), (0, LANES - width)])
    return w.reshape(lead + (heads * LANES,))


def _inproj_weight(w):
    sizes = (384, 128, 128, 128, 128, 128, 128, 18, Q_LORA, KV_LORA, MLA_ROPE, SSM_WIDTH)
    offs = [0]
    for n in sizes:
        offs.append(offs[-1] + n)
    (w_q, w_kc, w_vc, w_ks, w_vs, w_kw, w_vw, w_g, w_cq, w_ckv, w_kr, w_u) = [
        w[:, a:b] for a, b in zip(offs[:-1], offs[1:])]
    d = w.shape[0]
    half = MLA_ROPE // 2
    z64 = jnp.zeros((d, HEAD_DIM), w.dtype)
    z32 = jnp.zeros((d, LANES - HEAD_DIM - MLA_ROPE), w.dtype)
    r1, r2 = w_kr[:, :half], w_kr[:, half:]
    cols = [
        _pad_heads(w_q * (HEAD_DIM ** -0.5 * LOG2E), NSA_HEADS, HEAD_DIM),
        w_kc, w_vc,
        _pad_heads(w_ks, NSA_KV_HEADS, HEAD_DIM),
        _pad_heads(w_kw, NSA_KV_HEADS, HEAD_DIM),
        jnp.pad(w_g, ((0, 0), (0, LANES - w_g.shape[1]))),
        w_cq, w_ckv,
        jnp.concatenate([z64, r1, r2, z32], axis=1),
        jnp.concatenate([z64, r2, r1, z32], axis=1),
        w_u,
    ]
    w_vt = jnp.concatenate([_pad_heads(w_vs, NSA_KV_HEADS, HEAD_DIM),
                            _pad_heads(w_vw, NSA_KV_HEADS, HEAD_DIM)], axis=1).T
    return jnp.concatenate(cols, axis=1).astype(BF16), w_vt.astype(BF16)


def _mla_weights(w_uq, w_uk, w_uv):
    half = MLA_ROPE // 2
    w = w_uq.reshape(Q_LORA, MLA_HEADS, MLA_NOPE + MLA_ROPE)
    nope, r1, r2 = w[..., :MLA_NOPE], w[..., MLA_NOPE:MLA_NOPE + half], w[..., MLA_NOPE + half:]
    z32 = jnp.zeros((Q_LORA, MLA_HEADS, LANES - MLA_NOPE - MLA_ROPE), w.dtype)
    wa = jnp.concatenate([nope, r1, r2, z32], axis=-1).reshape(Q_LORA, MLA_HEADS * LANES)
    wb = jnp.concatenate([jnp.zeros_like(nope), r2, r1, z32], axis=-1).reshape(Q_LORA, MLA_HEADS * LANES)
    wk = _pad_heads(w_uk, MLA_HEADS, MLA_NOPE)
    wvt = _pad_heads(w_uv, MLA_HEADS, HEAD_DIM).T
    return wa.astype(BF16), wb.astype(BF16), wk.astype(BF16), wvt.astype(BF16)


def _ssm_weights(log_dt, a_re, a_im, b_re, b_im, c_re, c_im, d):
    dt = jnp.exp(log_dt.astype(F32))[:, None]
    ar, ai = a_re.astype(F32), a_im.astype(F32)
    mag = jnp.exp(ar * dt)
    abr, abi = mag * jnp.cos(ai * dt), mag * jnp.sin(ai * dt)
    den = ar * ar + ai * ai
    fr = ((abr - 1.0) * ar + abi * ai) / den
    fi = (abi * ar - (abr - 1.0) * ai) / den
    br, bi = b_re.astype(F32), b_im.astype(F32)
    bbr = fr[..., None] * br - fi[..., None] * bi
    bbi = fr[..., None] * bi + fi[..., None] * br
    eye = jnp.eye(SSM_GROUPS, dtype=F32)
    blk_in = lambda m: jnp.einsum("gpc,gh->gchp", m, eye).reshape(SSM_WIDTH, SSM_LANES)
    blk_out = lambda m: jnp.einsum("gcp,gh->gphc", m, eye).reshape(SSM_LANES, SSM_WIDTH)
    bblk = jnp.concatenate([blk_in(bbr), blk_in(bbi)], axis=1)
    ccat = jnp.concatenate([blk_out(c_re.astype(F32)), -blk_out(c_im.astype(F32))], axis=0)
    a_rows = jnp.stack([abr.reshape(SSM_LANES), abi.reshape(SSM_LANES)])
    return bblk.astype(BF16), a_rows, ccat.astype(BF16), d.astype(F32).reshape(1, SSM_WIDTH)


def _overlap_t(nc_pad):
    start = jnp.arange(nc_pad) * CMP_STRIDE
    lo = jnp.arange(MAX_SEL_BLOCKS) * SEL_BLOCK
    hit = (start[None, :] < lo[:, None] + SEL_BLOCK) & (start[None, :] + CMP_BLOCK > lo[:, None])
    return hit.astype(BF16)


def _rope_rows():
    half = MLA_ROPE // 2
    inv_freq = ROPE_THETA ** (-jnp.arange(half, dtype=F32) / half)
    z64 = jnp.zeros((HEAD_DIM,), F32)
    z32 = jnp.zeros((LANES - HEAD_DIM - MLA_ROPE,), F32)
    invf = jnp.concatenate([z64, inv_freq, inv_freq, z32]).reshape(1, LANES)
    sign = jnp.concatenate([z64, -jnp.ones((half,), F32), jnp.ones((half,), F32), z32]).reshape(1, LANES)
    return invf, sign


def kernel(x, positions, attn_norm, w_in, nsa_pe, nsa_ck_w1, nsa_ck_b1, nsa_ck_w2, nsa_ck_b2, nsa_cv_w1, nsa_cv_b1, nsa_cv_w2, nsa_cv_b2, nsa_gate_b, mla_q_norm, mla_kv_norm, mla_w_uq, mla_w_uk, mla_w_uv, ssm_log_dt, ssm_a_re, ssm_a_im, ssm_b_re, ssm_b_im, ssm_c_re, ssm_c_im, ssm_d, ssm_w_glu, ssm_b_glu, out_norm_nsa, out_norm_mla, out_norm_ssm, w_out, ffn_norm, ffn_w_up, ffn_conv_w, ffn_conv_b, ffn_w_down, final_norm):
    batch, seq, d_model = x.shape
    depth = w_in.shape[0]
    t = batch * seq
    n_half = seq // CMP_STRIDE
    n_sel = seq // SEL_BLOCK
    assert seq % ROW_TILE == 0 and n_sel <= MAX_SEL_BLOCKS and n_half % LANES == 0
    n_top = min(SEL_TOP, n_sel)
    row = lambda v: v.astype(F32).reshape(1, -1)

    invf, sign = _rope_rows()
    pos_rows = jnp.broadcast_to(positions.astype(F32).reshape(t, 1), (t, LANES))
    cmul, smul = _trig(pos_rows, invf, sign)
    ovl_t = _overlap_t(n_half)

    xf = x.reshape(t, d_model)
    for l in range(depth):
        (q, kvc, ks, kw, gate, cq, ckv, kra, krb, u, vs_t, vw_t) = _inproj(
            xf, row(attn_norm[l]), *_inproj_weight(w_in[l]), batch, seq)

        kvcmp = _compress(
            kvc.reshape(batch, n_half, CMP_STRIDE * _CMP_STREAMS * HEAD_DIM),
            *_compress_weights(nsa_pe[l], nsa_ck_w1[l], nsa_ck_b1[l], nsa_ck_w2[l], nsa_ck_b2[l],
                               nsa_cv_w1[l], nsa_cv_b1[l], nsa_cv_w2[l], nsa_cv_b2[l]))
        vcmp_t = kvcmp[:, NSA_KV_HEADS:].transpose(0, 1, 3, 2).astype(BF16)
        o_cmp, q_aug = _cmp_topk(q, kvcmp, vcmp_t, ovl_t, batch, seq, n_top)
        o_sel = _flash(q_aug, ks, vs_t, batch, seq, NSA_GROUP, "nsa_selected")
        o_win = _window(q, kw, vw_t, batch, seq)

        wa, wb, wk, wvt = _mla_weights(mla_w_uq[l], mla_w_uk[l], mla_w_uv[l])
        q_m, k_m, vm_t = _mla_proj(cq, ckv, kra, krb, cmul, smul, row(mla_q_norm[l]),
                                   row(mla_kv_norm[l]), wa, wb, wk, wvt, batch, seq)
        o_mla = _flash(q_m, k_m, vm_t, batch, seq, 1, "mla_attention")

        bblk, a_rows, ccat, d_row = _ssm_weights(
            ssm_log_dt[l], ssm_a_re[l], ssm_a_im[l], ssm_b_re[l], ssm_b_im[l],
            ssm_c_re[l], ssm_c_im[l], ssm_d[l])
        o_ssm = _ssm(u.reshape(seq, batch, SSM_WIDTH), bblk, a_rows, ccat, d_row,
                     ssm_w_glu[l].astype(BF16), row(ssm_b_glu[l]))
        o_ssm = o_ssm.reshape(seq, batch * SSM_WIDTH)

        w_o = w_out[l].astype(BF16)
        n_a = NSA_HEADS * HEAD_DIM
        n_b = n_a + MLA_HEADS * HEAD_DIM
        xf = _outproj(
            o_cmp, o_sel, o_win, gate,
            jnp.pad(row(nsa_gate_b[l]), ((0, 0), (0, LANES - NSA_HEADS * N_BRANCH))),
            o_mla, o_ssm, xf,
            row(out_norm_nsa[l]), row(out_norm_mla[l]), row(out_norm_ssm[l]),
            w_o[:n_a], w_o[n_a:n_b], w_o[n_b:])

        w_up = ffn_w_up[l].astype(BF16)
        xf = _ffn(xf, row(ffn_norm[l]), w_up[:, :D_FF], w_up[:, D_FF:],
                  jnp.pad(ffn_conv_w[l].astype(F32), ((0, 8 - ffn_conv_w.shape[1]), (0, 0))),
                  row(ffn_conv_b[l]), ffn_w_down[l].astype(BF16), seq)

    return _final_norm(xf, row(final_norm)).reshape(batch, seq, d_model)
```

```python
import functools
import math

import jax
import jax.numpy as jnp
from jax import lax
from jax.experimental import pallas as pl
from jax.experimental.pallas import tpu as pltpu

F32 = jnp.float32
BF16 = jnp.bfloat16

LANES = 128
HEAD_DIM = 64
NSA_HEADS = 6
NSA_KV_HEADS = 2
NSA_GROUP = NSA_HEADS // NSA_KV_HEADS
N_BRANCH = 3
CMP_BLOCK = 32
CMP_STRIDE = 16
CMP_HIDDEN = 128
SEL_BLOCK = 64
SEL_TOP = 16
MAX_SEL_BLOCKS = 64
WINDOW = 512
MLA_HEADS = 6
MLA_NOPE = 64
MLA_ROPE = 32
Q_LORA = 384
KV_LORA = 128
ROPE_THETA = 10000.0
SSM_WIDTH = 256
SSM_GROUPS = 16
SSM_GROUP_CH = 16
SSM_STATE = 64
SSM_LANES = SSM_GROUPS * SSM_STATE
D_FF = 2816
EPS = 1e-6
NEG = -1e30
LOG2E = math.log2(math.e)

ROW_TILE = 512
ATTN_TILE = 256
KV_CHUNK = 512
FLASH_ROWS = 768
FLASH_CHAINS = 6
V_ROWS = 80
SSM_CHUNK = 64
SSM_SUB = 16
FF_CHUNK = 256
FF_DOWN_GROUP = 6
VMEM_LIMIT = 56 * 1024 * 1024


def _params(sem, vmem=None):
    return pltpu.CompilerParams(dimension_semantics=sem, vmem_limit_bytes=vmem)


def _rms(x, g):
    return x * lax.rsqrt(jnp.mean(x * x, axis=-1, keepdims=True) + EPS) * g


def _gelu(x):
    c = math.sqrt(2.0 / math.pi)
    return 0.5 * x * (1.0 + jnp.tanh(c * (x + 0.044715 * (x * x * x))))


def _sigmoid(x):
    return 1.0 / (1.0 + jnp.exp(-x))


def _mod_pow2(x, n):
    assert n & (n - 1) == 0
    return jnp.bitwise_and(x, n - 1)


def _div_pow2(x, n):
    assert n & (n - 1) == 0
    return jnp.right_shift(x, n.bit_length() - 1)


def _const_spec(shape):
    nd = len(shape)
    return pl.BlockSpec(shape, lambda *_: (0,) * nd)


def _trig_kernel(pos_ref, invf_ref, sign_ref, c_ref, s_ref):
    ang = pos_ref[...] * invf_ref[...]
    c_ref[...] = jnp.cos(ang)
    s_ref[...] = jnp.sin(ang) * sign_ref[...]


def _trig(pos_rows, invf_row, sign_row):
    t = pos_rows.shape[0]
    spec = pl.BlockSpec((ROW_TILE, LANES), lambda i: (i, 0))
    return pl.pallas_call(
        _trig_kernel,
        grid=(t // ROW_TILE,),
        in_specs=[spec, _const_spec((1, LANES)), _const_spec((1, LANES))],
        out_specs=[spec, spec],
        out_shape=[jax.ShapeDtypeStruct((t, LANES), F32)] * 2,
        compiler_params=_params(("parallel",)),
        name="rope_trig",
    )(pos_rows, invf_row, sign_row)


_IN_SEGS = (
    ("q", NSA_HEADS * LANES, BF16),
    ("kvc", 4 * HEAD_DIM, F32),
    ("ks", NSA_KV_HEADS * LANES, BF16),
    ("kw", NSA_KV_HEADS * LANES, BF16),
    ("gate", LANES, F32),
    ("cq", Q_LORA, F32),
    ("ckv", KV_LORA, F32),
    ("kra", LANES, F32),
    ("krb", LANES, F32),
    ("u", SSM_WIDTH, F32),
)
_IN_COLS = sum(n for _, n, _ in _IN_SEGS)
_VT_ROWS = 2 * NSA_KV_HEADS * LANES


def _ones_row(shape):
    row = _mod_pow2(lax.broadcasted_iota(jnp.int32, shape, 0), LANES)
    return jnp.where(row == HEAD_DIM, 1.0, 0.0)


def _inproj_kernel(x_ref, g_ref, w_ref, wvt_ref, *o_refs, seq, tm):
    h = _rms(x_ref[...], g_ref[...]).astype(BF16)
    off = 0
    for (name, n, dt), o_ref in zip(_IN_SEGS, o_refs):
        y = jnp.dot(h, w_ref[:, off:off + n], preferred_element_type=F32)
        if name == "ks":
            s0 = lax.rem(pl.program_id(0) * tm, seq)
            blk = _div_pow2(s0 + lax.broadcasted_iota(jnp.int32, (tm, n), 0), SEL_BLOCK)
            lane = _mod_pow2(lax.broadcasted_iota(jnp.int32, (tm, n), 1), LANES)
            y = y + jnp.where(lane - HEAD_DIM == blk, 1.0, 0.0)
        o_ref[...] = y.astype(dt)
        off += n
    vst_ref, vwt_ref = o_refs[len(_IN_SEGS):]
    v_t = lax.dot_general(wvt_ref[...], h, (((1,), (1,)), ((), ())), preferred_element_type=F32)
    v_t = (v_t + _ones_row(v_t.shape)).astype(BF16)
    wc = vwt_ref.shape[-1]
    for hh in range(NSA_KV_HEADS):
        vst_ref[0, hh, 0] = v_t[hh * LANES:(hh + 1) * LANES, :]
        r0 = (NSA_KV_HEADS + hh) * LANES
        for c in range(tm // wc):
            vwt_ref[0, hh, c] = v_t[r0:r0 + LANES, c * wc:(c + 1) * wc]


def _inproj(x, g, w, wvt, batch, seq):
    t, d = x.shape
    tm = ROW_TILE
    assert tm == min(KV_CHUNK, seq) and tm % ATTN_TILE == 0
    nt = seq // tm
    rows = lambda n: pl.BlockSpec((tm, n), lambda i: (i, 0))
    out_specs = [rows(n) for _, n, _ in _IN_SEGS]
    out_shape = [jax.ShapeDtypeStruct((t, n), dt) for _, n, dt in _IN_SEGS]
    out_specs[-1] = pl.BlockSpec((tm, SSM_WIDTH), lambda i: (i % nt, i // nt))
    out_shape[-1] = jax.ShapeDtypeStruct((seq, batch * SSM_WIDTH), F32)
    wpt = tm // ATTN_TILE
    out_specs += [pl.BlockSpec((1, NSA_KV_HEADS, 1, LANES, tm), lambda i: (i // nt, 0, i % nt, 0, 0)),
                  pl.BlockSpec((1, NSA_KV_HEADS, wpt, LANES, ATTN_TILE),
                               lambda i: (i // nt, 0, i % nt, 0, 0))]
    out_shape += [jax.ShapeDtypeStruct((batch, NSA_KV_HEADS, nt, LANES, tm), BF16),
                  jax.ShapeDtypeStruct((batch, NSA_KV_HEADS, seq // ATTN_TILE, LANES, ATTN_TILE), BF16)]
    return pl.pallas_call(
        functools.partial(_inproj_kernel, seq=seq, tm=tm),
        grid=(t // tm,),
        in_specs=[pl.BlockSpec((tm, d), lambda i: (i, 0)), _const_spec((1, d)),
                  _const_spec((d, _IN_COLS)), _const_spec((_VT_ROWS, d))],
        out_specs=out_specs,
        out_shape=out_shape,
        compiler_params=_params(("parallel",), VMEM_LIMIT),
        name="in_proj",
    )(x, g, w, wvt)


_CMP_STREAMS = 2 * NSA_KV_HEADS


def _compress_kernel(x_ref, pe_ref, w1_ref, b1_ref, w2_ref, b2_ref, o_ref):
    x = x_ref[0]
    n = x.shape[0]
    top = jnp.dot((x + pe_ref[0:1, :]).astype(BF16), w1_ref[0], preferred_element_type=F32)
    bot = jnp.dot((x + pe_ref[1:2, :]).astype(BF16), w1_ref[1], preferred_element_type=F32)
    hid = _gelu(top + pltpu.roll(bot, n - 1, 0) + b1_ref[...]).astype(BF16)
    for j in range(_CMP_STREAMS):
        kind = j // NSA_KV_HEADS
        o_ref[0, j] = (jnp.dot(hid[:, j * CMP_HIDDEN:(j + 1) * CMP_HIDDEN], w2_ref[kind],
                               preferred_element_type=F32) + b2_ref[kind])


def _compress(x, pe, w1, b1, w2, b2):
    b, n, width = x.shape
    hid = _CMP_STREAMS * CMP_HIDDEN
    return pl.pallas_call(
        _compress_kernel,
        grid=(b,),
        in_specs=[pl.BlockSpec((1, n, width), lambda i: (i, 0, 0)),
                  _const_spec((2, width)), _const_spec((2, width, hid)), _const_spec((1, hid)),
                  _const_spec((2, CMP_HIDDEN, LANES)), _const_spec((2, 1, LANES))],
        out_specs=pl.BlockSpec((1, _CMP_STREAMS, n, LANES), lambda i: (i, 0, 0, 0)),
        out_shape=jax.ShapeDtypeStruct((b, _CMP_STREAMS, n, LANES), F32),
        compiler_params=_params(("parallel",), VMEM_LIMIT),
        name="nsa_compress",
    )(x, pe, w1, b1, w2, b2)


def _compress_weights(pe, ck_w1, ck_b1, ck_w2, ck_b2, cv_w1, cv_b1, cv_w2, cv_b2):
    eye = jnp.eye(_CMP_STREAMS, dtype=F32)
    per_stream = jnp.stack([ck_w1, ck_w1, cv_w1, cv_w1]).astype(F32)

    def place(w):
        return jnp.einsum("jldf,jk->ljdkf", w, eye).reshape(
            CMP_STRIDE * _CMP_STREAMS * HEAD_DIM, _CMP_STREAMS * CMP_HIDDEN)

    w1 = jnp.stack([place(per_stream[:, :CMP_STRIDE]), place(per_stream[:, CMP_STRIDE:])])
    spread = lambda p: jnp.broadcast_to(p[:, None, :], (CMP_STRIDE, _CMP_STREAMS, HEAD_DIM)).reshape(-1)
    pe2 = jnp.stack([spread(pe[:CMP_STRIDE]), spread(pe[CMP_STRIDE:])]).astype(F32)
    b1 = jnp.concatenate([ck_b1, ck_b1, cv_b1, cv_b1]).astype(F32).reshape(1, -1)
    pad2 = lambda w: jnp.pad(w.astype(F32), ((0, 0), (0, LANES - HEAD_DIM)))
    w2 = jnp.stack([pad2(ck_w2), pad2(cv_w2)]).astype(BF16)
    b2 = jnp.stack([pad2(ck_b2.reshape(1, -1)), pad2(cv_b2.reshape(1, -1))])
    return pe2, w1.astype(BF16), b1, w2, b2


def _stack_group(q_ref, h, groups):
    return jnp.concatenate([q_ref[:, (h * groups + g) * LANES:(h * groups + g + 1) * LANES]
                            for g in range(groups)], axis=0)


def _store_heads_t(o_ref, chains_t, tq):
    heads = [c[:, g * tq:(g + 1) * tq] for c in chains_t for g in range(c.shape[1] // tq)]
    for p in range(len(heads) // 2):
        pair = jnp.concatenate([heads[2 * p], heads[2 * p + 1]], axis=0)
        o_ref[:, p * LANES:(p + 1) * LANES] = pair.T.astype(o_ref.dtype)


def _stable_rank(val):
    sub = 8
    n_blk = val.shape[0] // sub
    blocks = [val[r * sub:(r + 1) * sub] for r in range(n_blk)]
    ranks = [jnp.zeros(blocks[0].shape, F32) for _ in range(n_blk)]
    row_in = lax.broadcasted_iota(jnp.int32, blocks[0].shape, 0)
    for i in range(val.shape[0]):
        row = val[i:i + 1, :]
        for r in range(n_blk):
            if r * sub > i:
                ahead = jnp.where(row >= blocks[r], 1.0, 0.0)
            elif (r + 1) * sub <= i:
                ahead = jnp.where(row > blocks[r], 1.0, 0.0)
            else:
                ahead = jnp.where(row_in + r * sub > i, jnp.where(row >= blocks[r], 1.0, 0.0),
                                  jnp.where(row > blocks[r], 1.0, 0.0))
            ranks[r] = ranks[r] + ahead
    return jnp.concatenate(ranks, axis=0)


def _cmp_scores(kc_ref, qs):
    nt = (((1,), (1,)), ((), ()))
    return [lax.dot_general(kc_ref[0, h].astype(BF16), qs[h], nt, preferred_element_type=F32)
            for h in range(NSA_KV_HEADS)]


def _cmp_finish(ss, q0, q_ref, vct_ref, ovl_ref, o_ref, qa_ref, *, tq, n_top):
    m_cols = NSA_GROUP * tq
    nc = ss[0].shape[0]
    t_col = q0 + _mod_pow2(lax.broadcasted_iota(jnp.int32, (nc, m_cols), 1), tq)
    n_row = lax.broadcasted_iota(jnp.int32, (nc, m_cols), 0)
    mask = n_row * CMP_STRIDE + (CMP_BLOCK - 1) <= t_col
    j_idx = lax.broadcasted_iota(jnp.int32, (MAX_SEL_BLOCKS, tq), 0)
    cur = _div_pow2(q0 + lax.broadcasted_iota(jnp.int32, (MAX_SEL_BLOCKS, tq), 1), SEL_BLOCK)
    forced = (j_idx == 0) | (j_idx == cur) | (j_idx == cur - 1)
    outs = []
    for h in range(NSA_KV_HEADS):
        sm = jnp.where(mask, ss[h], NEG)
        e = jnp.exp2(sm - jnp.max(sm, axis=0, keepdims=True))
        p = jnp.where(mask, e * (1.0 / jnp.sum(e, axis=0, keepdims=True)), 0.0)
        outs.append(jnp.dot(vct_ref[0, h, 0:HEAD_DIM, :], p.astype(BF16),
                            preferred_element_type=F32))

        psum = p[:, 0:tq] + p[:, tq:2 * tq] + p[:, 2 * tq:3 * tq]
        hi = psum.astype(BF16)
        lo = (psum - hi.astype(F32)).astype(BF16)
        imp = (jnp.dot(ovl_ref[...], hi, preferred_element_type=F32)
               + jnp.dot(ovl_ref[...], lo, preferred_element_type=F32))
        val = jnp.where(forced, jnp.inf, jnp.where(j_idx > cur, -jnp.inf, imp))
        bias_t = jnp.where(_stable_rank(val) < n_top, 0.0, NEG)
        bias = jnp.concatenate([jnp.zeros_like(bias_t), bias_t], axis=0).T
        for g in range(NSA_GROUP):
            lane0 = (h * NSA_GROUP + g) * LANES
            qa_ref[:, lane0:lane0 + LANES] = (q_ref[:, lane0:lane0 + LANES].astype(F32)
                                              + bias).astype(qa_ref.dtype)
    _store_heads_t(o_ref, outs, tq)


def _flash_kernel(q_ref, k_ref, vt_ref, o_ref, s_a, s_b, *, chains, groups, tq, ck):
    q0 = pl.program_id(2) * tq
    m_cols = groups * tq
    qs = [_stack_group(q_ref, h, groups) for h in range(chains)]

    def issue_scores(c, s_ref):
        k0 = pl.multiple_of(c * ck, ck)
        for h in range(chains):
            s_ref[h] = lax.dot_general(k_ref[pl.ds(k0, ck), h * LANES:(h + 1) * LANES], qs[h],
                                       (((1,), (1,)), ((), ())), preferred_element_type=F32)

    def absorb(c, s_ref, stats, visible=None):
        out = []
        for h in range(chains):
            s = s_ref[h]
            if visible is not None:
                s = jnp.where(visible, s, NEG)
            m_old, acc = stats[h]
            m_new = jnp.maximum(m_old, jnp.max(s, axis=0, keepdims=True))
            alpha = jnp.exp2(m_old - m_new)
            p = jnp.exp2(s - m_new).astype(BF16)
            vt = vt_ref[0, h, c, 0:V_ROWS, :]
            out.append((m_new, alpha * acc + jnp.dot(vt, p, preferred_element_type=F32)))
        return tuple(out)

    def finish(stats):
        _store_heads_t(o_ref, [acc[0:HEAD_DIM] / acc[HEAD_DIM:HEAD_DIM + 1] for _, acc in stats], tq)

    last = (q0 + tq - 1) // ck
    key_pos = last * ck + lax.broadcasted_iota(jnp.int32, (ck, m_cols), 0)
    q_pos = q0 + _mod_pow2(lax.broadcasted_iota(jnp.int32, (ck, m_cols), 1), tq)
    visible = key_pos <= q_pos

    def pair(j, stats):
        c = 2 * j
        issue_scores(c + 1, s_b)
        stats = absorb(c, s_a, stats)
        issue_scores(c + 2, s_a)
        return absorb(c + 1, s_b, stats)

    issue_scores(0, s_a)
    stats = tuple((jnp.full((1, m_cols), NEG, F32), jnp.zeros((V_ROWS, m_cols), F32))
                  for _ in range(chains))
    stats = lax.fori_loop(0, last // 2, pair, stats)

    @pl.when(lax.rem(last, 2) == 0)
    def _():
        finish(absorb(last, s_a, stats, visible))

    @pl.when(lax.rem(last, 2) == 1)
    def _():
        issue_scores(last, s_b)
        finish(absorb(last, s_b, absorb(last - 1, s_a, stats), visible))


def _flash(q, k, v_t, batch, seq, groups, name):
    t, qcols = q.shape
    kv_heads = k.shape[1] // LANES
    ck = min(KV_CHUNK, seq)
    tq = min(FLASH_ROWS // groups, ck)
    assert tq & (tq - 1) == 0 and ck % tq == 0 and seq % ck == 0
    nq = seq // tq
    n_chunks = seq // ck
    assert v_t.shape == (batch, kv_heads, n_chunks, LANES, ck)
    chains = max(c for c in range(1, FLASH_CHAINS + 1)
                 if kv_heads % c == 0 and (c * groups) % 2 == 0)
    heads = kv_heads * groups
    return pl.pallas_call(
        functools.partial(_flash_kernel, chains=chains, groups=groups, tq=tq, ck=ck),
        grid=(batch, kv_heads // chains, nq),
        in_specs=[pl.BlockSpec((tq, chains * groups * LANES), lambda b, h, i: (b * nq + i, h)),
                  pl.BlockSpec((seq, chains * LANES), lambda b, h, i: (b, h)),
                  pl.BlockSpec((1, chains, n_chunks, LANES, ck), lambda b, h, i: (b, h, 0, 0, 0))],
        out_specs=pl.BlockSpec((tq, chains * groups * HEAD_DIM), lambda b, h, i: (b * nq + i, h)),
        out_shape=jax.ShapeDtypeStruct((t, heads * HEAD_DIM), BF16),
        scratch_shapes=[pltpu.VMEM((chains, ck, groups * tq), F32)] * 2,
        compiler_params=_params(("parallel", "parallel", "parallel"), VMEM_LIMIT),
        name=name,
    )(q, k, v_t)


def _nsa_local_kernel(q_ref, kc_ref, vct_ref, ovl_ref, k_ref, vt_ref, oc_ref, qa_ref, ow_ref,
                      *, tq, n_top):
    i = pl.program_id(1)
    q0 = i * tq
    n_back = WINDOW // tq
    n_span = n_back + 1
    m_cols = NSA_GROUP * tq
    qs = [_stack_group(q_ref, h, NSA_GROUP) for h in range(NSA_KV_HEADS)]
    q_pos = q0 + _mod_pow2(lax.broadcasted_iota(jnp.int32, (tq, m_cols), 1), tq)
    key_in = lax.broadcasted_iota(jnp.int32, (tq, m_cols), 0)
    nt = (((1,), (1,)), ((), ()))

    def run(c0, clamped):
        start = c0 * tq if clamped else pl.multiple_of(c0 * tq, tq)
        ss = [lax.dot_general(k_ref[pl.ds(start, n_span * tq), h * LANES:(h + 1) * LANES], qs[h],
                              nt, preferred_element_type=F32) for h in range(NSA_KV_HEADS)]
        ss_cmp = _cmp_scores(kc_ref, qs)
        outs = []
        for h in range(NSA_KV_HEADS):
            blocks = []
            for j in range(n_span):
                blk = ss[h][j * tq:(j + 1) * tq]
                key_pos = start + j * tq + key_in
                if clamped:
                    blk = jnp.where((key_pos <= q_pos) & (key_pos > q_pos - WINDOW), blk, NEG)
                elif j == 0:
                    blk = jnp.where(key_pos > q_pos - WINDOW, blk, NEG)
                elif j == n_span - 1:
                    blk = jnp.where(key_pos <= q_pos, blk, NEG)
                blocks.append(blk)
            m = functools.reduce(jnp.maximum, [jnp.max(b, axis=0, keepdims=True) for b in blocks])
            acc = 0.0
            for j in range(n_span):
                p = jnp.exp2(blocks[j] - m).astype(BF16)
                acc = acc + jnp.dot(vt_ref[0, h, c0 + j, 0:V_ROWS, :], p, preferred_element_type=F32)
            outs.append(acc[0:HEAD_DIM] / acc[HEAD_DIM:HEAD_DIM + 1])
        _store_heads_t(ow_ref, outs, tq)
        _cmp_finish(ss_cmp, q0, q_ref, vct_ref, ovl_ref, oc_ref, qa_ref, tq=tq, n_top=n_top)

    @pl.when(i < n_back)
    def _():
        run(0, True)

    @pl.when(i >= n_back)
    def _():
        run(i - n_back, False)


def _nsa_local(q, kcmp, vcmp_t, ovl, k, v_t, batch, seq, n_top):
    t = q.shape[0]
    tq = ATTN_TILE
    nq = seq // tq
    nc = kcmp.shape[2]
    assert WINDOW % tq == 0 and seq >= WINDOW + tq
    wide = NSA_HEADS * LANES
    qspec = pl.BlockSpec((tq, wide), lambda b, i: (b * nq + i, 0))
    ospec = pl.BlockSpec((tq, NSA_HEADS * HEAD_DIM), lambda b, i: (b * nq + i, 0))
    o_sds = jax.ShapeDtypeStruct((t, NSA_HEADS * HEAD_DIM), BF16)
    return pl.pallas_call(
        functools.partial(_nsa_local_kernel, tq=tq, n_top=n_top),
        grid=(batch, nq),
        in_specs=[qspec,
                  pl.BlockSpec((1, NSA_KV_HEADS, nc, LANES), lambda b, i: (b, 0, 0, 0)),
                  pl.BlockSpec((1, NSA_KV_HEADS, LANES, nc), lambda b, i: (b, 0, 0, 0)),
                  _const_spec((MAX_SEL_BLOCKS, nc)),
                  pl.BlockSpec((seq, NSA_KV_HEADS * LANES), lambda b, i: (b, 0)),
                  pl.BlockSpec((1, NSA_KV_HEADS, nq, LANES, tq), lambda b, i: (b, 0, 0, 0, 0))],
        out_specs=[ospec, qspec, ospec],
        out_shape=[o_sds, jax.ShapeDtypeStruct((t, wide), BF16), o_sds],
        compiler_params=_params(("parallel", "parallel"), VMEM_LIMIT),
        name="nsa_local",
    )(q, kcmp, vcmp_t, ovl, k, v_t)


def _mla_proj_kernel(cq_ref, ckv_ref, kra_ref, krb_ref, cm_ref, sm_ref, qg_ref, kg_ref,
                     wa_ref, wb_ref, wk_ref, wvt_ref, q_ref, k_ref, vt_ref, *, scale):
    qn = _rms(cq_ref[...], qg_ref[...]).astype(BF16)
    cn = _rms(ckv_ref[...], kg_ref[...]).astype(BF16)
    cm = cm_ref[...]
    sm = sm_ref[...]
    k_rot = kra_ref[...] * cm + krb_ref[...] * sm
    for h in range(MLA_HEADS):
        sl = slice(h * LANES, (h + 1) * LANES)
        qa = jnp.dot(qn, wa_ref[:, sl], preferred_element_type=F32)
        qb = jnp.dot(qn, wb_ref[:, sl], preferred_element_type=F32)
        q_ref[:, sl] = ((qa * cm + qb * sm) * scale).astype(q_ref.dtype)
        k_ref[:, sl] = (jnp.dot(cn, wk_ref[:, sl], preferred_element_type=F32) + k_rot).astype(k_ref.dtype)
    v_t = lax.dot_general(wvt_ref[...], cn, (((1,), (1,)), ((), ())), preferred_element_type=F32)
    v_t = (v_t + _ones_row(v_t.shape)).astype(vt_ref.dtype)
    for h in range(MLA_HEADS):
        vt_ref[0, h, 0] = v_t[h * LANES:(h + 1) * LANES, :]


def _mla_proj(cq, ckv, kra, krb, cmul, smul, qg, kg, wa, wb, wk, wvt, batch, seq):
    t = cq.shape[0]
    tm = ROW_TILE
    assert tm == min(KV_CHUNK, seq)
    nt = seq // tm
    width = MLA_HEADS * LANES
    rows = lambda n: pl.BlockSpec((tm, n), lambda i: (i, 0))
    scale = (MLA_NOPE + MLA_ROPE) ** -0.5 * LOG2E
    return pl.pallas_call(
        functools.partial(_mla_proj_kernel, scale=scale),
        grid=(t // tm,),
        in_specs=[rows(Q_LORA), rows(KV_LORA), rows(LANES), rows(LANES), rows(LANES), rows(LANES),
                  _const_spec((1, Q_LORA)), _const_spec((1, KV_LORA)),
                  _const_spec((Q_LORA, width)), _const_spec((Q_LORA, width)),
                  _const_spec((KV_LORA, width)), _const_spec((width, KV_LORA))],
        out_specs=[rows(width), rows(width),
                   pl.BlockSpec((1, MLA_HEADS, 1, LANES, tm), lambda i: (i // nt, 0, i % nt, 0, 0))],
        out_shape=[jax.ShapeDtypeStruct((t, width), BF16)] * 2
        + [jax.ShapeDtypeStruct((batch, MLA_HEADS, nt, LANES, tm), BF16)],
        compiler_params=_params(("parallel",)),
        name="mla_proj",
    )(cq, ckv, kra, krb, cmul, smul, qg, kg, wa, wb, wk, wvt)


def _ssm_kernel(u_ref, bblk_ref, a_ref, ccat_ref, d_ref, wglu_ref, bglu_ref, o_ref,
                h_sc, st_sc, *, tc, sub, nb):
    @pl.when(pl.program_id(0) == 0)
    def _():
        st_sc[...] = jnp.zeros_like(st_sc)

    n = SSM_LANES
    ar = jnp.broadcast_to(a_ref[0:1, :], (nb, n))
    ai = jnp.broadcast_to(a_ref[1:2, :], (nb, n))
    n_sub = tc // sub
    rows = sub * nb
    us = []
    for j in range(n_sub):
        u = u_ref[j * sub:(j + 1) * sub].reshape(rows, SSM_WIDTH)
        us.append(u)
        h_sc[j * rows:(j + 1) * rows, :] = jnp.dot(u.astype(BF16), bblk_ref[...],
                                                   preferred_element_type=F32)
    hr, hi = st_sc[0], st_sc[1]
    ys = []
    for j in range(n_sub):
        for t in range(j * sub, (j + 1) * sub):
            r0 = t * nb
            nr = ar * hr - ai * hi + h_sc[r0:r0 + nb, 0:n]
            ni = ar * hi + ai * hr + h_sc[r0:r0 + nb, n:2 * n]
            h_sc[r0:r0 + nb, 0:n] = nr
            h_sc[r0:r0 + nb, n:2 * n] = ni
            hr, hi = nr, ni
        ys.append(jnp.dot(h_sc[j * rows:(j + 1) * rows, :].astype(BF16), ccat_ref[...],
                          preferred_element_type=F32) + d_ref[...] * us[j])
    st_sc[0] = hr
    st_sc[1] = hi
    z = _gelu(jnp.concatenate(ys, axis=0))
    gate = jnp.dot(z.astype(BF16), wglu_ref[...], preferred_element_type=F32) + bglu_ref[...]
    o_ref[...] = (z * _sigmoid(gate)).reshape(tc, nb, SSM_WIDTH)


def _ssm(u_t, bblk, a_rows, ccat, d_row, wglu, bglu):
    seq, nb, _ = u_t.shape
    tc = SSM_CHUNK
    n = SSM_LANES
    uspec = pl.BlockSpec((tc, nb, SSM_WIDTH), lambda i: (i, 0, 0))
    return pl.pallas_call(
        functools.partial(_ssm_kernel, tc=tc, sub=SSM_SUB, nb=nb),
        grid=(seq // tc,),
        in_specs=[uspec, _const_spec((SSM_WIDTH, 2 * n)), _const_spec((2, n)),
                  _const_spec((2 * n, SSM_WIDTH)), _const_spec((1, SSM_WIDTH)),
                  _const_spec((SSM_WIDTH, SSM_WIDTH)), _const_spec((1, SSM_WIDTH))],
        out_specs=uspec,
        out_shape=jax.ShapeDtypeStruct(u_t.shape, F32),
        scratch_shapes=[pltpu.VMEM((tc * nb, 2 * n), F32), pltpu.VMEM((2, nb, n), F32)],
        compiler_params=_params(("arbitrary",), VMEM_LIMIT),
        name="s5_scan",
    )(u_t, bblk, a_rows, ccat, d_row, wglu, bglu)


def _outproj_kernel(oc_ref, os_ref, ow_ref, gate_ref, gb_ref, ex_ref, om_ref, oz_ref, x_ref,
                    gn_ref, gm_ref, gz_ref, wn_ref, wm_ref, wz_ref, o_ref):
    g = _sigmoid(gate_ref[...] + gb_ref[...])
    g_hi = g.astype(BF16)
    g_lo = (g - g_hi.astype(F32)).astype(BF16)
    o_a = 0.0
    for r, br_ref in enumerate((oc_ref, os_ref, ow_ref)):
        spread = (jnp.dot(g_hi, ex_ref[r], preferred_element_type=F32)
                  + jnp.dot(g_lo, ex_ref[r], preferred_element_type=F32))
        o_a = o_a + spread * br_ref[...].astype(F32)
    inv = lax.rsqrt(jnp.sum(o_a * o_a, axis=-1, keepdims=True) * (1.0 / (NSA_HEADS * HEAD_DIM)) + EPS)
    acc = x_ref[...] + jnp.dot((o_a * inv * gn_ref[...]).astype(BF16), wn_ref[...],
                               preferred_element_type=F32)
    om = om_ref[...].astype(F32)
    inv = lax.rsqrt(jnp.sum(om * om, axis=-1, keepdims=True) * (1.0 / (MLA_HEADS * HEAD_DIM)) + EPS)
    acc = acc + jnp.dot((om * inv * gm_ref[...]).astype(BF16), wm_ref[...], preferred_element_type=F32)
    acc = acc + jnp.dot(_rms(oz_ref[...], gz_ref[...]).astype(BF16), wz_ref[...],
                        preferred_element_type=F32)
    o_ref[...] = acc


def _gate_spread():
    col = jnp.arange(LANES)[None, :, None]
    lane = jnp.arange(NSA_HEADS * HEAD_DIM)[None, None, :]
    r = jnp.arange(N_BRANCH)[:, None, None]
    return (col == N_BRANCH * (lane // HEAD_DIM) + r).astype(BF16)


def _outproj(oc, osel, ow, gate, gate_b, om, oz, x, gn, gm, gz, wn, wm, wz):
    t, d = x.shape
    tm = ROW_TILE
    nt = oz.shape[0] // tm
    wide = NSA_HEADS * HEAD_DIM
    assert wide == MLA_HEADS * HEAD_DIM
    rows = lambda n: pl.BlockSpec((tm, n), lambda i: (i, 0))
    return pl.pallas_call(
        _outproj_kernel,
        grid=(t // tm,),
        in_specs=[rows(wide), rows(wide), rows(wide), rows(LANES), _const_spec((1, LANES)),
                  _const_spec((N_BRANCH, LANES, wide)), rows(wide),
                  pl.BlockSpec((tm, SSM_WIDTH), lambda i: (i % nt, i // nt)), rows(d),
                  _const_spec((1, wide)), _const_spec((1, wide)), _const_spec((1, SSM_WIDTH)),
                  _const_spec((wide, d)), _const_spec((wide, d)), _const_spec((SSM_WIDTH, d))],
        out_specs=rows(d),
        out_shape=jax.ShapeDtypeStruct((t, d), F32),
        compiler_params=_params(("parallel",), VMEM_LIMIT),
        name="out_proj",
    )(oc, osel, ow, gate, gate_b, _gate_spread(), om, oz, x, gn, gm, gz, wn, wm, wz)


def _ffn_kernel(x_ref, g_ref, wg_ref, wv_ref, cw_ref, cb_ref, wd_ref, o_ref, carry_sc, act_sc,
                *, tm, tiles_per_seq, cf, down_group):
    @pl.when(lax.rem(pl.program_id(0), tiles_per_seq) == 0)
    def _():
        carry_sc[...] = jnp.zeros_like(carry_sc)

    x = x_ref[...]
    h = _rms(x, g_ref[...]).astype(BF16)
    row = lax.broadcasted_iota(jnp.int32, (tm, cf), 0)
    n_chunks = D_FF // cf
    acc = x

    def up(c):
        sl = slice(c * cf, (c + 1) * cf)
        return (jnp.dot(h, wg_ref[:, sl], preferred_element_type=F32),
                jnp.dot(h, wv_ref[:, sl], preferred_element_type=F32))

    nxt = up(0)
    for c in range(n_chunks):
        sl = slice(c * cf, (c + 1) * cf)
        gate, val = nxt
        if c + 1 < n_chunks:
            nxt = up(c + 1)
        tail = carry_sc[:, sl]
        p1 = tail[7:8, :]
        p2 = tail[6:7, :]
        g1 = jnp.where(row == 0, p1, pltpu.roll(gate, 1, 0))
        g2 = jnp.where(row == 0, p2, jnp.where(row == 1, p1, pltpu.roll(gate, 2, 0)))
        carry_sc[:, sl] = gate[tm - 8:tm, :]
        gc = cw_ref[0:1, sl] * g2 + cw_ref[1:2, sl] * g1 + cw_ref[2:3, sl] * gate + cb_ref[:, sl]
        act_sc[:, sl] = (gc * _sigmoid(gc) * val).astype(BF16)
        if (c + 1) % down_group == 0 or c + 1 == n_chunks:
            lo = (c // down_group) * down_group * cf
            acc = acc + jnp.dot(act_sc[:, lo:(c + 1) * cf], wd_ref[lo:(c + 1) * cf, :],
                                preferred_element_type=F32)
    o_ref[...] = acc


def _ffn(x, g, wg, wv, cw, cb, wd, seq):
    t, d = x.shape
    tm = ROW_TILE
    rows = pl.BlockSpec((tm, d), lambda i: (i, 0))
    return pl.pallas_call(
        functools.partial(_ffn_kernel, tm=tm, tiles_per_seq=seq // tm, cf=FF_CHUNK,
                          down_group=FF_DOWN_GROUP),
        grid=(t // tm,),
        in_specs=[rows, _const_spec((1, d)), _const_spec((d, D_FF)), _const_spec((d, D_FF)),
                  _const_spec((8, D_FF)), _const_spec((1, D_FF)), _const_spec((D_FF, d))],
        out_specs=rows,
        out_shape=jax.ShapeDtypeStruct((t, d), F32),
        scratch_shapes=[pltpu.VMEM((8, D_FF), F32), pltpu.VMEM((tm, D_FF), BF16)],
        compiler_params=_params(("arbitrary",), VMEM_LIMIT),
        name="conv_ffn",
    )(x, g, wg, wv, cw, cb, wd)


def _final_norm_kernel(x_ref, g_ref, o_ref):
    o_ref[...] = _rms(x_ref[...], g_ref[...])


def _final_norm(x, g):
    t, d = x.shape
    rows = pl.BlockSpec((ROW_TILE, d), lambda i: (i, 0))
    return pl.pallas_call(
        _final_norm_kernel,
        grid=(t // ROW_TILE,),
        in_specs=[rows, _const_spec((1, d))],
        out_specs=rows,
        out_shape=jax.ShapeDtypeStruct((t, d), F32),
        compiler_params=_params(("parallel",)),
        name="final_norm",
    )(x, g)


def _pad_heads(w, heads, width):
    lead = w.shape[:-1]
    w = w.reshape(lead + (heads, width))
    w = jnp.pad(w, [(0, 0)] * len(lead) + [(0, 0), (0, LANES - width)])
    return w.reshape(lead + (heads * LANES,))


def _inproj_weight(w):
    sizes = (384, 128, 128, 128, 128, 128, 128, 18, Q_LORA, KV_LORA, MLA_ROPE, SSM_WIDTH)
    offs = [0]
    for n in sizes:
        offs.append(offs[-1] + n)
    (w_q, w_kc, w_vc, w_ks, w_vs, w_kw, w_vw, w_g, w_cq, w_ckv, w_kr, w_u) = [
        w[:, a:b] for a, b in zip(offs[:-1], offs[1:])]
    d = w.shape[0]
    half = MLA_ROPE // 2
    z64 = jnp.zeros((d, HEAD_DIM), w.dtype)
    z32 = jnp.zeros((d, LANES - HEAD_DIM - MLA_ROPE), w.dtype)
    r1, r2 = w_kr[:, :half], w_kr[:, half:]
    cols = [
        _pad_heads(w_q * (HEAD_DIM ** -0.5 * LOG2E), NSA_HEADS, HEAD_DIM),
        w_kc, w_vc,
        _pad_heads(w_ks, NSA_KV_HEADS, HEAD_DIM),
        _pad_heads(w_kw, NSA_KV_HEADS, HEAD_DIM),
        jnp.pad(w_g, ((0, 0), (0, LANES - w_g.shape[1]))),
        w_cq, w_ckv,
        jnp.concatenate([z64, r1, r2, z32], axis=1),
        jnp.concatenate([z64, r2, r1, z32], axis=1),
        w_u,
    ]
    w_vt = jnp.concatenate([_pad_heads(w_vs, NSA_KV_HEADS, HEAD_DIM),
                            _pad_heads(w_vw, NSA_KV_HEADS, HEAD_DIM)], axis=1).T
    return jnp.concatenate(cols, axis=1).astype(BF16), w_vt.astype(BF16)


def _mla_weights(w_uq, w_uk, w_uv):
    half = MLA_ROPE // 2
    w = w_uq.reshape(Q_LORA, MLA_HEADS, MLA_NOPE + MLA_ROPE)
    nope, r1, r2 = w[..., :MLA_NOPE], w[..., MLA_NOPE:MLA_NOPE + half], w[..., MLA_NOPE + half:]
    z32 = jnp.zeros((Q_LORA, MLA_HEADS, LANES - MLA_NOPE - MLA_ROPE), w.dtype)
    wa = jnp.concatenate([nope, r1, r2, z32], axis=-1).reshape(Q_LORA, MLA_HEADS * LANES)
    wb = jnp.concatenate([jnp.zeros_like(nope), r2, r1, z32], axis=-1).reshape(Q_LORA, MLA_HEADS * LANES)
    wk = _pad_heads(w_uk, MLA_HEADS, MLA_NOPE)
    wvt = _pad_heads(w_uv, MLA_HEADS, HEAD_DIM).T
    return wa.astype(BF16), wb.astype(BF16), wk.astype(BF16), wvt.astype(BF16)


def _ssm_weights(log_dt, a_re, a_im, b_re, b_im, c_re, c_im, d):
    dt = jnp.exp(log_dt.astype(F32))[:, None]
    ar, ai = a_re.astype(F32), a_im.astype(F32)
    mag = jnp.exp(ar * dt)
    abr, abi = mag * jnp.cos(ai * dt), mag * jnp.sin(ai * dt)
    den = ar * ar + ai * ai
    fr = ((abr - 1.0) * ar + abi * ai) / den
    fi = (abi * ar - (abr - 1.0) * ai) / den
    br, bi = b_re.astype(F32), b_im.astype(F32)
    bbr = fr[..., None] * br - fi[..., None] * bi
    bbi = fr[..., None] * bi + fi[..., None] * br
    eye = jnp.eye(SSM_GROUPS, dtype=F32)
    blk_in = lambda m: jnp.einsum("gpc,gh->gchp", m, eye).reshape(SSM_WIDTH, SSM_LANES)
    blk_out = lambda m: jnp.einsum("gcp,gh->gphc", m, eye).reshape(SSM_LANES, SSM_WIDTH)
    bblk = jnp.concatenate([blk_in(bbr), blk_in(bbi)], axis=1)
    ccat = jnp.concatenate([blk_out(c_re.astype(F32)), -blk_out(c_im.astype(F32))], axis=0)
    a_rows = jnp.stack([abr.reshape(SSM_LANES), abi.reshape(SSM_LANES)])
    return bblk.astype(BF16), a_rows, ccat.astype(BF16), d.astype(F32).reshape(1, SSM_WIDTH)


def _overlap_t(nc_pad):
    start = jnp.arange(nc_pad) * CMP_STRIDE
    lo = jnp.arange(MAX_SEL_BLOCKS) * SEL_BLOCK
    hit = (start[None, :] < lo[:, None] + SEL_BLOCK) & (start[None, :] + CMP_BLOCK > lo[:, None])
    return hit.astype(BF16)


def _rope_rows():
    half = MLA_ROPE // 2
    inv_freq = ROPE_THETA ** (-jnp.arange(half, dtype=F32) / half)
    z64 = jnp.zeros((HEAD_DIM,), F32)
    z32 = jnp.zeros((LANES - HEAD_DIM - MLA_ROPE,), F32)
    invf = jnp.concatenate([z64, inv_freq, inv_freq, z32]).reshape(1, LANES)
    sign = jnp.concatenate([z64, -jnp.ones((half,), F32), jnp.ones((half,), F32), z32]).reshape(1, LANES)
    return invf, sign


def kernel(x, positions, attn_norm, w_in, nsa_pe, nsa_ck_w1, nsa_ck_b1, nsa_ck_w2, nsa_ck_b2, nsa_cv_w1, nsa_cv_b1, nsa_cv_w2, nsa_cv_b2, nsa_gate_b, mla_q_norm, mla_kv_norm, mla_w_uq, mla_w_uk, mla_w_uv, ssm_log_dt, ssm_a_re, ssm_a_im, ssm_b_re, ssm_b_im, ssm_c_re, ssm_c_im, ssm_d, ssm_w_glu, ssm_b_glu, out_norm_nsa, out_norm_mla, out_norm_ssm, w_out, ffn_norm, ffn_w_up, ffn_conv_w, ffn_conv_b, ffn_w_down, final_norm):
    batch, seq, d_model = x.shape
    depth = w_in.shape[0]
    t = batch * seq
    n_half = seq // CMP_STRIDE
    n_sel = seq // SEL_BLOCK
    assert seq % ROW_TILE == 0 and n_sel <= MAX_SEL_BLOCKS and n_half % LANES == 0
    n_top = min(SEL_TOP, n_sel)
    row = lambda v: v.astype(F32).reshape(1, -1)

    invf, sign = _rope_rows()
    pos_rows = jnp.broadcast_to(positions.astype(F32).reshape(t, 1), (t, LANES))
    cmul, smul = _trig(pos_rows, invf, sign)
    ovl_t = _overlap_t(n_half)

    xf = x.reshape(t, d_model)
    for l in range(depth):
        (q, kvc, ks, kw, gate, cq, ckv, kra, krb, u, vs_t, vw_t) = _inproj(
            xf, row(attn_norm[l]), *_inproj_weight(w_in[l]), batch, seq)

        kvcmp = _compress(
            kvc.reshape(batch, n_half, CMP_STRIDE * _CMP_STREAMS * HEAD_DIM),
            *_compress_weights(nsa_pe[l], nsa_ck_w1[l], nsa_ck_b1[l], nsa_ck_w2[l], nsa_ck_b2[l],
                               nsa_cv_w1[l], nsa_cv_b1[l], nsa_cv_w2[l], nsa_cv_b2[l]))
        vcmp_t = kvcmp[:, NSA_KV_HEADS:].transpose(0, 1, 3, 2).astype(BF16)
        o_cmp, q_aug, o_win = _nsa_local(q, kvcmp, vcmp_t, ovl_t, kw, vw_t, batch, seq, n_top)
        o_sel = _flash(q_aug, ks, vs_t, batch, seq, NSA_GROUP, "nsa_selected")

        wa, wb, wk, wvt = _mla_weights(mla_w_uq[l], mla_w_uk[l], mla_w_uv[l])
        q_m, k_m, vm_t = _mla_proj(cq, ckv, kra, krb, cmul, smul, row(mla_q_norm[l]),
                                   row(mla_kv_norm[l]), wa, wb, wk, wvt, batch, seq)
        o_mla = _flash(q_m, k_m, vm_t, batch, seq, 1, "mla_attention")

        bblk, a_rows, ccat, d_row = _ssm_weights(
            ssm_log_dt[l], ssm_a_re[l], ssm_a_im[l], ssm_b_re[l], ssm_b_im[l],
            ssm_c_re[l], ssm_c_im[l], ssm_d[l])
        o_ssm = _ssm(u.reshape(seq, batch, SSM_WIDTH), bblk, a_rows, ccat, d_row,
                     ssm_w_glu[l].astype(BF16), row(ssm_b_glu[l]))
        o_ssm = o_ssm.reshape(seq, batch * SSM_WIDTH)

        w_o = w_out[l].astype(BF16)
        n_a = NSA_HEADS * HEAD_DIM
        n_b = n_a + MLA_HEADS * HEAD_DIM
        xf = _outproj(
            o_cmp, o_sel, o_win, gate,
            jnp.pad(row(nsa_gate_b[l]), ((0, 0), (0, LANES - NSA_HEADS * N_BRANCH))),
            o_mla, o_ssm, xf,
            row(out_norm_nsa[l]), row(out_norm_mla[l]), row(out_norm_ssm[l]),
            w_o[:n_a], w_o[n_a:n_b], w_o[n_b:])

        w_up = ffn_w_up[l].astype(BF16)
        xf = _ffn(xf, row(ffn_norm[l]), w_up[:, :D_FF], w_up[:, D_FF:],
                  jnp.pad(ffn_conv_w[l].astype(F32), ((0, 8 - ffn_conv_w.shape[1]), (0, 0))),
                  row(ffn_conv_b[l]), ffn_w_down[l].astype(BF16), seq)

    return _final_norm(xf, row(final_norm)).reshape(batch, seq, d_model)
```

```python
import functools
import math

import jax
import jax.numpy as jnp
from jax import lax
from jax.experimental import pallas as pl
from jax.experimental.pallas import tpu as pltpu

F32 = jnp.float32
BF16 = jnp.bfloat16

LANES = 128
HEAD_DIM = 64
NSA_HEADS = 6
NSA_KV_HEADS = 2
NSA_GROUP = NSA_HEADS // NSA_KV_HEADS
N_BRANCH = 3
CMP_BLOCK = 32
CMP_STRIDE = 16
CMP_HIDDEN = 128
SEL_BLOCK = 64
SEL_TOP = 16
MAX_SEL_BLOCKS = 64
WINDOW = 512
MLA_HEADS = 6
MLA_NOPE = 64
MLA_ROPE = 32
Q_LORA = 384
KV_LORA = 128
ROPE_THETA = 10000.0
SSM_WIDTH = 256
SSM_GROUPS = 16
SSM_GROUP_CH = 16
SSM_STATE = 64
SSM_LANES = SSM_GROUPS * SSM_STATE
D_FF = 2816
EPS = 1e-6
NEG = -1e30
LOG2E = math.log2(math.e)

ROW_TILE = 512
ATTN_TILE = 256
KV_CHUNK = 512
FLASH_ROWS = 1536
FLASH_CHAINS = 6
V_ROWS = 80
SSM_CHUNK = 64
SSM_SUB = 16
FF_CHUNK = 256
FF_DOWN_GROUP = 6
VMEM_LIMIT = 56 * 1024 * 1024


def _params(sem, vmem=None):
    return pltpu.CompilerParams(dimension_semantics=sem, vmem_limit_bytes=vmem)


def _rms(x, g):
    return x * lax.rsqrt(jnp.mean(x * x, axis=-1, keepdims=True) + EPS) * g


def _gelu(x):
    c = math.sqrt(2.0 / math.pi)
    return 0.5 * x * (1.0 + jnp.tanh(c * (x + 0.044715 * (x * x * x))))


def _sigmoid(x):
    return 1.0 / (1.0 + jnp.exp(-x))


def _mod_pow2(x, n):
    assert n & (n - 1) == 0
    return jnp.bitwise_and(x, n - 1)


def _div_pow2(x, n):
    assert n & (n - 1) == 0
    return jnp.right_shift(x, n.bit_length() - 1)


def _const_spec(shape):
    nd = len(shape)
    return pl.BlockSpec(shape, lambda *_: (0,) * nd)


def _trig_kernel(pos_ref, invf_ref, sign_ref, c_ref, s_ref):
    ang = pos_ref[...] * invf_ref[...]
    c_ref[...] = jnp.cos(ang)
    s_ref[...] = jnp.sin(ang) * sign_ref[...]


def _trig(pos_rows, invf_row, sign_row):
    t = pos_rows.shape[0]
    spec = pl.BlockSpec((ROW_TILE, LANES), lambda i: (i, 0))
    return pl.pallas_call(
        _trig_kernel,
        grid=(t // ROW_TILE,),
        in_specs=[spec, _const_spec((1, LANES)), _const_spec((1, LANES))],
        out_specs=[spec, spec],
        out_shape=[jax.ShapeDtypeStruct((t, LANES), F32)] * 2,
        compiler_params=_params(("parallel",)),
        name="rope_trig",
    )(pos_rows, invf_row, sign_row)


_IN_SEGS = (
    ("q", NSA_HEADS * LANES, BF16),
    ("kvc", 4 * HEAD_DIM, F32),
    ("ks", NSA_KV_HEADS * LANES, BF16),
    ("kw", NSA_KV_HEADS * LANES, BF16),
    ("cq", Q_LORA, F32),
    ("gate", LANES, F32),
    ("ckv", KV_LORA, F32),
    ("kra", LANES, F32),
    ("krb", LANES, F32),
    ("u", SSM_WIDTH, F32),
)
_IN_GROUPS = ((0,), (1,), (2,), (3,), (4, 5), (6, 7), (8,), (9,))
_IN_COLS = sum(n for _, n, _ in _IN_SEGS)
_VT_ROWS = 2 * NSA_KV_HEADS * LANES


def _ones_row(shape):
    row = _mod_pow2(lax.broadcasted_iota(jnp.int32, shape, 0), LANES)
    return jnp.where(row == HEAD_DIM, 1.0, 0.0)


def _inproj_kernel(x_ref, g_ref, w_ref, wvt_ref, *o_refs, seq, tm):
    h = _rms(x_ref[...], g_ref[...]).astype(BF16)
    off = 0
    for group in _IN_GROUPS:
        width = sum(_IN_SEGS[s][1] for s in group)
        y_all = jnp.dot(h, w_ref[:, off:off + width], preferred_element_type=F32)
        off += width
        lo = 0
        for s in group:
            name, n, dt = _IN_SEGS[s]
            y = y_all[:, lo:lo + n]
            lo += n
            if name == "ks":
                s0 = lax.rem(pl.program_id(0) * tm, seq)
                blk = _div_pow2(s0 + lax.broadcasted_iota(jnp.int32, (tm, n), 0), SEL_BLOCK)
                lane = _mod_pow2(lax.broadcasted_iota(jnp.int32, (tm, n), 1), LANES)
                y = y + jnp.where(lane - HEAD_DIM == blk, 1.0, 0.0)
            o_refs[s][...] = y.astype(dt)
    vst_ref, vwt_ref = o_refs[len(_IN_SEGS):]
    v_t = lax.dot_general(wvt_ref[...], h, (((1,), (1,)), ((), ())), preferred_element_type=F32)
    v_t = (v_t + _ones_row(v_t.shape)).astype(BF16)
    wc = vwt_ref.shape[-1]
    for hh in range(NSA_KV_HEADS):
        vst_ref[0, hh, 0] = v_t[hh * LANES:(hh + 1) * LANES, :]
        r0 = (NSA_KV_HEADS + hh) * LANES
        for c in range(tm // wc):
            vwt_ref[0, hh, c] = v_t[r0:r0 + LANES, c * wc:(c + 1) * wc]


def _inproj(x, g, w, wvt, batch, seq):
    t, d = x.shape
    tm = ROW_TILE
    assert tm == min(KV_CHUNK, seq) and tm % ATTN_TILE == 0
    nt = seq // tm
    rows = lambda n: pl.BlockSpec((tm, n), lambda i: (i, 0))
    out_specs = [rows(n) for _, n, _ in _IN_SEGS]
    out_shape = [jax.ShapeDtypeStruct((t, n), dt) for _, n, dt in _IN_SEGS]
    out_specs[-1] = pl.BlockSpec((tm, SSM_WIDTH), lambda i: (i % nt, i // nt))
    out_shape[-1] = jax.ShapeDtypeStruct((seq, batch * SSM_WIDTH), F32)
    wpt = tm // ATTN_TILE
    out_specs += [pl.BlockSpec((1, NSA_KV_HEADS, 1, LANES, tm), lambda i: (i // nt, 0, i % nt, 0, 0)),
                  pl.BlockSpec((1, NSA_KV_HEADS, wpt, LANES, ATTN_TILE),
                               lambda i: (i // nt, 0, i % nt, 0, 0))]
    out_shape += [jax.ShapeDtypeStruct((batch, NSA_KV_HEADS, nt, LANES, tm), BF16),
                  jax.ShapeDtypeStruct((batch, NSA_KV_HEADS, seq // ATTN_TILE, LANES, ATTN_TILE), BF16)]
    return pl.pallas_call(
        functools.partial(_inproj_kernel, seq=seq, tm=tm),
        grid=(t // tm,),
        in_specs=[pl.BlockSpec((tm, d), lambda i: (i, 0)), _const_spec((1, d)),
                  _const_spec((d, _IN_COLS)), _const_spec((_VT_ROWS, d))],
        out_specs=out_specs,
        out_shape=out_shape,
        compiler_params=_params(("parallel",), VMEM_LIMIT),
        name="in_proj",
    )(x, g, w, wvt)


_CMP_STREAMS = 2 * NSA_KV_HEADS


def _compress_kernel(x_ref, pe_ref, w1_ref, b1_ref, w2_ref, b2_ref, o_ref):
    x = x_ref[0]
    n = x.shape[0]
    top = jnp.dot((x + pe_ref[0:1, :]).astype(BF16), w1_ref[0], preferred_element_type=F32)
    bot = jnp.dot((x + pe_ref[1:2, :]).astype(BF16), w1_ref[1], preferred_element_type=F32)
    hid = _gelu(top + pltpu.roll(bot, n - 1, 0) + b1_ref[...]).astype(BF16)
    for j in range(_CMP_STREAMS):
        kind = j // NSA_KV_HEADS
        o_ref[0, j] = (jnp.dot(hid[:, j * CMP_HIDDEN:(j + 1) * CMP_HIDDEN], w2_ref[kind],
                               preferred_element_type=F32) + b2_ref[kind])


def _compress(x, pe, w1, b1, w2, b2):
    b, n, width = x.shape
    hid = _CMP_STREAMS * CMP_HIDDEN
    return pl.pallas_call(
        _compress_kernel,
        grid=(b,),
        in_specs=[pl.BlockSpec((1, n, width), lambda i: (i, 0, 0)),
                  _const_spec((2, width)), _const_spec((2, width, hid)), _const_spec((1, hid)),
                  _const_spec((2, CMP_HIDDEN, LANES)), _const_spec((2, 1, LANES))],
        out_specs=pl.BlockSpec((1, _CMP_STREAMS, n, LANES), lambda i: (i, 0, 0, 0)),
        out_shape=jax.ShapeDtypeStruct((b, _CMP_STREAMS, n, LANES), F32),
        compiler_params=_params(("parallel",), VMEM_LIMIT),
        name="nsa_compress",
    )(x, pe, w1, b1, w2, b2)


def _compress_weights(pe, ck_w1, ck_b1, ck_w2, ck_b2, cv_w1, cv_b1, cv_w2, cv_b2):
    eye = jnp.eye(_CMP_STREAMS, dtype=F32)
    per_stream = jnp.stack([ck_w1, ck_w1, cv_w1, cv_w1]).astype(F32)

    def place(w):
        return jnp.einsum("jldf,jk->ljdkf", w, eye).reshape(
            CMP_STRIDE * _CMP_STREAMS * HEAD_DIM, _CMP_STREAMS * CMP_HIDDEN)

    w1 = jnp.stack([place(per_stream[:, :CMP_STRIDE]), place(per_stream[:, CMP_STRIDE:])])
    spread = lambda p: jnp.broadcast_to(p[:, None, :], (CMP_STRIDE, _CMP_STREAMS, HEAD_DIM)).reshape(-1)
    pe2 = jnp.stack([spread(pe[:CMP_STRIDE]), spread(pe[CMP_STRIDE:])]).astype(F32)
    b1 = jnp.concatenate([ck_b1, ck_b1, cv_b1, cv_b1]).astype(F32).reshape(1, -1)
    pad2 = lambda w: jnp.pad(w.astype(F32), ((0, 0), (0, LANES - HEAD_DIM)))
    w2 = jnp.stack([pad2(ck_w2), pad2(cv_w2)]).astype(BF16)
    b2 = jnp.stack([pad2(ck_b2.reshape(1, -1)), pad2(cv_b2.reshape(1, -1))])
    return pe2, w1.astype(BF16), b1, w2, b2


def _stack_group(q_ref, h, groups):
    return jnp.concatenate([q_ref[:, (h * groups + g) * LANES:(h * groups + g + 1) * LANES]
                            for g in range(groups)], axis=0)


def _store_heads_t(o_ref, chains_t, tq):
    heads = [c[:, g * tq:(g + 1) * tq] for c in chains_t for g in range(c.shape[1] // tq)]
    for p in range(len(heads) // 2):
        pair = jnp.concatenate([heads[2 * p], heads[2 * p + 1]], axis=0)
        o_ref[:, p * LANES:(p + 1) * LANES] = pair.T.astype(o_ref.dtype)


def _stable_rank(val):
    sub = 8
    n_blk = val.shape[0] // sub
    blocks = [val[r * sub:(r + 1) * sub] for r in range(n_blk)]
    ranks = [jnp.zeros(blocks[0].shape, F32) for _ in range(n_blk)]
    row_in = lax.broadcasted_iota(jnp.int32, blocks[0].shape, 0)
    for i in range(val.shape[0]):
        row = val[i:i + 1, :]
        for r in range(n_blk):
            if r * sub > i:
                ahead = jnp.where(row >= blocks[r], 1.0, 0.0)
            elif (r + 1) * sub <= i:
                ahead = jnp.where(row > blocks[r], 1.0, 0.0)
            else:
                ahead = jnp.where(row_in + r * sub > i, jnp.where(row >= blocks[r], 1.0, 0.0),
                                  jnp.where(row > blocks[r], 1.0, 0.0))
            ranks[r] = ranks[r] + ahead
    return jnp.concatenate(ranks, axis=0)


def _cmp_scores(kc_ref, qs):
    nt = (((1,), (1,)), ((), ()))
    return [lax.dot_general(kc_ref[0, h].astype(BF16), qs[h], nt, preferred_element_type=F32)
            for h in range(NSA_KV_HEADS)]


def _cmp_finish(ss, q0, q_ref, vct_ref, ovl_ref, o_ref, qa_ref, *, tq, n_top):
    m_cols = NSA_GROUP * tq
    nc = ss[0].shape[0]
    t_col = q0 + _mod_pow2(lax.broadcasted_iota(jnp.int32, (nc, m_cols), 1), tq)
    n_row = lax.broadcasted_iota(jnp.int32, (nc, m_cols), 0)
    mask = n_row * CMP_STRIDE + (CMP_BLOCK - 1) <= t_col
    j_idx = lax.broadcasted_iota(jnp.int32, (MAX_SEL_BLOCKS, tq), 0)
    cur = _div_pow2(q0 + lax.broadcasted_iota(jnp.int32, (MAX_SEL_BLOCKS, tq), 1), SEL_BLOCK)
    forced = (j_idx == 0) | (j_idx == cur) | (j_idx == cur - 1)
    outs = []
    for h in range(NSA_KV_HEADS):
        sm = jnp.where(mask, ss[h], NEG)
        e = jnp.exp2(sm - jnp.max(sm, axis=0, keepdims=True))
        p = jnp.where(mask, e * (1.0 / jnp.sum(e, axis=0, keepdims=True)), 0.0)
        outs.append(jnp.dot(vct_ref[0, h, 0:HEAD_DIM, :], p.astype(BF16),
                            preferred_element_type=F32))

        psum = p[:, 0:tq] + p[:, tq:2 * tq] + p[:, 2 * tq:3 * tq]
        hi = psum.astype(BF16)
        lo = (psum - hi.astype(F32)).astype(BF16)
        imp = (jnp.dot(ovl_ref[...], hi, preferred_element_type=F32)
               + jnp.dot(ovl_ref[...], lo, preferred_element_type=F32))
        val = jnp.where(forced, jnp.inf, jnp.where(j_idx > cur, -jnp.inf, imp))
        bias_t = jnp.where(_stable_rank(val) < n_top, 0.0, NEG)
        bias = jnp.concatenate([jnp.zeros_like(bias_t), bias_t], axis=0).T
        for g in range(NSA_GROUP):
            lane0 = (h * NSA_GROUP + g) * LANES
            qa_ref[:, lane0:lane0 + LANES] = (q_ref[:, lane0:lane0 + LANES].astype(F32)
                                              + bias).astype(qa_ref.dtype)
    _store_heads_t(o_ref, outs, tq)


def _flash_kernel(q_ref, k_ref, vt_ref, o_ref, s_a, s_b, *, chains, groups, tq, ck):
    q0 = pl.program_id(2) * tq
    m_cols = groups * tq
    qs = [_stack_group(q_ref, h, groups) for h in range(chains)]

    def issue_scores(c, s_ref):
        k0 = pl.multiple_of(c * ck, ck)
        for h in range(chains):
            s_ref[h] = lax.dot_general(k_ref[pl.ds(k0, ck), h * LANES:(h + 1) * LANES], qs[h],
                                       (((1,), (1,)), ((), ())), preferred_element_type=F32)

    def absorb(c, s_ref, stats, visible=None):
        out = []
        for h in range(chains):
            s = s_ref[h]
            if visible is not None:
                s = jnp.where(visible, s, NEG)
            m_old, acc = stats[h]
            m_new = jnp.maximum(m_old, jnp.max(s, axis=0, keepdims=True))
            alpha = jnp.exp2(m_old - m_new)
            p = jnp.exp2(s - m_new).astype(BF16)
            vt = vt_ref[0, h, c, 0:V_ROWS, :]
            out.append((m_new, alpha * acc + jnp.dot(vt, p, preferred_element_type=F32)))
        return tuple(out)

    def finish(stats):
        _store_heads_t(o_ref, [acc[0:HEAD_DIM] / acc[HEAD_DIM:HEAD_DIM + 1] for _, acc in stats], tq)

    last = (q0 + tq - 1) // ck
    key_pos = last * ck + lax.broadcasted_iota(jnp.int32, (ck, m_cols), 0)
    q_pos = q0 + _mod_pow2(lax.broadcasted_iota(jnp.int32, (ck, m_cols), 1), tq)
    visible = key_pos <= q_pos

    def pair(j, stats):
        c = 2 * j
        issue_scores(c + 1, s_b)
        stats = absorb(c, s_a, stats)
        issue_scores(c + 2, s_a)
        return absorb(c + 1, s_b, stats)

    issue_scores(0, s_a)
    stats = tuple((jnp.full((1, m_cols), NEG, F32), jnp.zeros((V_ROWS, m_cols), F32))
                  for _ in range(chains))
    stats = lax.fori_loop(0, last // 2, pair, stats)

    @pl.when(lax.rem(last, 2) == 0)
    def _():
        finish(absorb(last, s_a, stats, visible))

    @pl.when(lax.rem(last, 2) == 1)
    def _():
        issue_scores(last, s_b)
        finish(absorb(last, s_b, absorb(last - 1, s_a, stats), visible))


def _flash(q, k, v_t, batch, seq, groups, name):
    t, qcols = q.shape
    kv_heads = k.shape[1] // LANES
    ck = min(KV_CHUNK, seq)
    tq = min(FLASH_ROWS // groups, ck)
    assert tq & (tq - 1) == 0 and ck % tq == 0 and seq % ck == 0
    nq = seq // tq
    n_chunks = seq // ck
    assert v_t.shape == (batch, kv_heads, n_chunks, LANES, ck)
    chains = max(c for c in range(1, FLASH_CHAINS + 1)
                 if kv_heads % c == 0 and (c * groups) % 2 == 0)
    heads = kv_heads * groups
    return pl.pallas_call(
        functools.partial(_flash_kernel, chains=chains, groups=groups, tq=tq, ck=ck),
        grid=(batch, kv_heads // chains, nq),
        in_specs=[pl.BlockSpec((tq, chains * groups * LANES), lambda b, h, i: (b * nq + i, h)),
                  pl.BlockSpec((seq, chains * LANES), lambda b, h, i: (b, h)),
                  pl.BlockSpec((1, chains, n_chunks, LANES, ck), lambda b, h, i: (b, h, 0, 0, 0))],
        out_specs=pl.BlockSpec((tq, chains * groups * HEAD_DIM), lambda b, h, i: (b * nq + i, h)),
        out_shape=jax.ShapeDtypeStruct((t, heads * HEAD_DIM), BF16),
        scratch_shapes=[pltpu.VMEM((chains, ck, groups * tq), F32)] * 2,
        compiler_params=_params(("parallel", "parallel", "parallel"), VMEM_LIMIT),
        name=name,
    )(q, k, v_t)


def _nsa_local_kernel(q_ref, kc_ref, vct_ref, ovl_ref, k_ref, vt_ref, oc_ref, qa_ref, ow_ref,
                      *, tq, n_top):
    i = pl.program_id(1)
    q0 = i * tq
    n_back = WINDOW // tq
    n_span = n_back + 1
    m_cols = NSA_GROUP * tq
    qs = [_stack_group(q_ref, h, NSA_GROUP) for h in range(NSA_KV_HEADS)]
    q_pos = q0 + _mod_pow2(lax.broadcasted_iota(jnp.int32, (tq, m_cols), 1), tq)
    key_in = lax.broadcasted_iota(jnp.int32, (tq, m_cols), 0)
    nt = (((1,), (1,)), ((), ()))

    def run(c0, clamped):
        start = c0 * tq if clamped else pl.multiple_of(c0 * tq, tq)
        ss = [lax.dot_general(k_ref[pl.ds(start, n_span * tq), h * LANES:(h + 1) * LANES], qs[h],
                              nt, preferred_element_type=F32) for h in range(NSA_KV_HEADS)]
        ss_cmp = _cmp_scores(kc_ref, qs)
        outs = []
        for h in range(NSA_KV_HEADS):
            blocks = []
            for j in range(n_span):
                blk = ss[h][j * tq:(j + 1) * tq]
                key_pos = start + j * tq + key_in
                if clamped:
                    blk = jnp.where((key_pos <= q_pos) & (key_pos > q_pos - WINDOW), blk, NEG)
                elif j == 0:
                    blk = jnp.where(key_pos > q_pos - WINDOW, blk, NEG)
                elif j == n_span - 1:
                    blk = jnp.where(key_pos <= q_pos, blk, NEG)
                blocks.append(blk)
            m = functools.reduce(jnp.maximum, [jnp.max(b, axis=0, keepdims=True) for b in blocks])
            acc = 0.0
            for j in range(n_span):
                p = jnp.exp2(blocks[j] - m).astype(BF16)
                acc = acc + jnp.dot(vt_ref[0, h, c0 + j, 0:V_ROWS, :], p, preferred_element_type=F32)
            outs.append(acc[0:HEAD_DIM] / acc[HEAD_DIM:HEAD_DIM + 1])
        _store_heads_t(ow_ref, outs, tq)
        _cmp_finish(ss_cmp, q0, q_ref, vct_ref, ovl_ref, oc_ref, qa_ref, tq=tq, n_top=n_top)

    @pl.when(i < n_back)
    def _():
        run(0, True)

    @pl.when(i >= n_back)
    def _():
        run(i - n_back, False)


def _nsa_local(q, kcmp, vcmp_t, ovl, k, v_t, batch, seq, n_top):
    t = q.shape[0]
    tq = ATTN_TILE
    nq = seq // tq
    nc = kcmp.shape[2]
    assert WINDOW % tq == 0 and seq >= WINDOW + tq
    wide = NSA_HEADS * LANES
    qspec = pl.BlockSpec((tq, wide), lambda b, i: (b * nq + i, 0))
    ospec = pl.BlockSpec((tq, NSA_HEADS * HEAD_DIM), lambda b, i: (b * nq + i, 0))
    o_sds = jax.ShapeDtypeStruct((t, NSA_HEADS * HEAD_DIM), BF16)
    return pl.pallas_call(
        functools.partial(_nsa_local_kernel, tq=tq, n_top=n_top),
        grid=(batch, nq),
        in_specs=[qspec,
                  pl.BlockSpec((1, NSA_KV_HEADS, nc, LANES), lambda b, i: (b, 0, 0, 0)),
                  pl.BlockSpec((1, NSA_KV_HEADS, LANES, nc), lambda b, i: (b, 0, 0, 0)),
                  _const_spec((MAX_SEL_BLOCKS, nc)),
                  pl.BlockSpec((seq, NSA_KV_HEADS * LANES), lambda b, i: (b, 0)),
                  pl.BlockSpec((1, NSA_KV_HEADS, nq, LANES, tq), lambda b, i: (b, 0, 0, 0, 0))],
        out_specs=[ospec, qspec, ospec],
        out_shape=[o_sds, jax.ShapeDtypeStruct((t, wide), BF16), o_sds],
        compiler_params=_params(("parallel", "parallel"), VMEM_LIMIT),
        name="nsa_local",
    )(q, kcmp, vcmp_t, ovl, k, v_t)


def _mla_proj_kernel(cq_ref, ckv_ref, kra_ref, krb_ref, cm_ref, sm_ref, qg_ref, kg_ref,
                     wa_ref, wb_ref, wk_ref, wvt_ref, q_ref, k_ref, vt_ref, *, scale):
    qn = _rms(cq_ref[...], qg_ref[...]).astype(BF16)
    cn = _rms(ckv_ref[...], kg_ref[...]).astype(BF16)
    cm = cm_ref[...]
    sm = sm_ref[...]
    k_rot = kra_ref[...] * cm + krb_ref[...] * sm
    cm2 = jnp.concatenate([cm, cm], axis=1)
    sm2 = jnp.concatenate([sm, sm], axis=1)
    k_rot2 = jnp.concatenate([k_rot, k_rot], axis=1)
    for h in range(0, MLA_HEADS, 2):
        sl = slice(h * LANES, (h + 2) * LANES)
        qa = jnp.dot(qn, wa_ref[:, sl], preferred_element_type=F32)
        qb = jnp.dot(qn, wb_ref[:, sl], preferred_element_type=F32)
        q_ref[:, sl] = ((qa * cm2 + qb * sm2) * scale).astype(q_ref.dtype)
        k_ref[:, sl] = (jnp.dot(cn, wk_ref[:, sl], preferred_element_type=F32) + k_rot2).astype(k_ref.dtype)
    v_t = lax.dot_general(wvt_ref[...], cn, (((1,), (1,)), ((), ())), preferred_element_type=F32)
    v_t = (v_t + _ones_row(v_t.shape)).astype(vt_ref.dtype)
    for h in range(MLA_HEADS):
        vt_ref[0, h, 0] = v_t[h * LANES:(h + 1) * LANES, :]


def _mla_proj(cq, ckv, kra, krb, cmul, smul, qg, kg, wa, wb, wk, wvt, batch, seq):
    t = cq.shape[0]
    tm = ROW_TILE
    assert tm == min(KV_CHUNK, seq)
    nt = seq // tm
    width = MLA_HEADS * LANES
    rows = lambda n: pl.BlockSpec((tm, n), lambda i: (i, 0))
    scale = (MLA_NOPE + MLA_ROPE) ** -0.5 * LOG2E
    return pl.pallas_call(
        functools.partial(_mla_proj_kernel, scale=scale),
        grid=(t // tm,),
        in_specs=[rows(Q_LORA), rows(KV_LORA), rows(LANES), rows(LANES), rows(LANES), rows(LANES),
                  _const_spec((1, Q_LORA)), _const_spec((1, KV_LORA)),
                  _const_spec((Q_LORA, width)), _const_spec((Q_LORA, width)),
                  _const_spec((KV_LORA, width)), _const_spec((width, KV_LORA))],
        out_specs=[rows(width), rows(width),
                   pl.BlockSpec((1, MLA_HEADS, 1, LANES, tm), lambda i: (i // nt, 0, i % nt, 0, 0))],
        out_shape=[jax.ShapeDtypeStruct((t, width), BF16)] * 2
        + [jax.ShapeDtypeStruct((batch, MLA_HEADS, nt, LANES, tm), BF16)],
        compiler_params=_params(("parallel",)),
        name="mla_proj",
    )(cq, ckv, kra, krb, cmul, smul, qg, kg, wa, wb, wk, wvt)


def _ssm_kernel(u_ref, bblk_ref, a_ref, ccat_ref, d_ref, wglu_ref, bglu_ref, o_ref,
                h_sc, st_sc, *, tc, sub, nb):
    @pl.when(pl.program_id(0) == 0)
    def _():
        st_sc[...] = jnp.zeros_like(st_sc)

    n = SSM_LANES
    ar = jnp.broadcast_to(a_ref[0:1, :], (nb, n))
    ai = jnp.broadcast_to(a_ref[1:2, :], (nb, n))
    n_sub = tc // sub
    rows = sub * nb
    us = []
    for j in range(n_sub):
        u = u_ref[j * sub:(j + 1) * sub].reshape(rows, SSM_WIDTH)
        us.append(u)
        h_sc[j * rows:(j + 1) * rows, :] = jnp.dot(u.astype(BF16), bblk_ref[...],
                                                   preferred_element_type=F32)
    hr, hi = st_sc[0], st_sc[1]
    ys = []
    for j in range(n_sub):
        for t in range(j * sub, (j + 1) * sub):
            r0 = t * nb
            nr = ar * hr - ai * hi + h_sc[r0:r0 + nb, 0:n]
            ni = ar * hi + ai * hr + h_sc[r0:r0 + nb, n:2 * n]
            h_sc[r0:r0 + nb, 0:n] = nr
            h_sc[r0:r0 + nb, n:2 * n] = ni
            hr, hi = nr, ni
        ys.append(jnp.dot(h_sc[j * rows:(j + 1) * rows, :].astype(BF16), ccat_ref[...],
                          preferred_element_type=F32) + d_ref[...] * us[j])
    st_sc[0] = hr
    st_sc[1] = hi
    z = _gelu(jnp.concatenate(ys, axis=0))
    gate = jnp.dot(z.astype(BF16), wglu_ref[...], preferred_element_type=F32) + bglu_ref[...]
    o_ref[...] = (z * _sigmoid(gate)).reshape(tc, nb, SSM_WIDTH)


def _ssm(u_t, bblk, a_rows, ccat, d_row, wglu, bglu):
    seq, nb, _ = u_t.shape
    tc = SSM_CHUNK
    n = SSM_LANES
    uspec = pl.BlockSpec((tc, nb, SSM_WIDTH), lambda i: (i, 0, 0))
    return pl.pallas_call(
        functools.partial(_ssm_kernel, tc=tc, sub=SSM_SUB, nb=nb),
        grid=(seq // tc,),
        in_specs=[uspec, _const_spec((SSM_WIDTH, 2 * n)), _const_spec((2, n)),
                  _const_spec((2 * n, SSM_WIDTH)), _const_spec((1, SSM_WIDTH)),
                  _const_spec((SSM_WIDTH, SSM_WIDTH)), _const_spec((1, SSM_WIDTH))],
        out_specs=uspec,
        out_shape=jax.ShapeDtypeStruct(u_t.shape, F32),
        scratch_shapes=[pltpu.VMEM((tc * nb, 2 * n), F32), pltpu.VMEM((2, nb, n), F32)],
        compiler_params=_params(("arbitrary",), VMEM_LIMIT),
        name="s5_scan",
    )(u_t, bblk, a_rows, ccat, d_row, wglu, bglu)


def _outproj_kernel(oc_ref, os_ref, ow_ref, gate_ref, gb_ref, ex_ref, om_ref, oz_ref, x_ref,
                    gn_ref, gm_ref, gz_ref, wn_ref, wm_ref, wz_ref, o_ref):
    g = _sigmoid(gate_ref[...] + gb_ref[...])
    g_hi = g.astype(BF16)
    g_lo = (g - g_hi.astype(F32)).astype(BF16)
    spread = jnp.dot(jnp.concatenate([g_hi, g_lo], axis=1), ex_ref[...],
                     preferred_element_type=F32)
    o_a = 0.0
    wide = oc_ref.shape[1]
    for r, br_ref in enumerate((oc_ref, os_ref, ow_ref)):
        o_a = o_a + spread[:, r * wide:(r + 1) * wide] * br_ref[...].astype(F32)
    inv = lax.rsqrt(jnp.sum(o_a * o_a, axis=-1, keepdims=True) * (1.0 / (NSA_HEADS * HEAD_DIM)) + EPS)
    acc = x_ref[...] + jnp.dot((o_a * inv * gn_ref[...]).astype(BF16), wn_ref[...],
                               preferred_element_type=F32)
    om = om_ref[...].astype(F32)
    inv = lax.rsqrt(jnp.sum(om * om, axis=-1, keepdims=True) * (1.0 / (MLA_HEADS * HEAD_DIM)) + EPS)
    acc = acc + jnp.dot((om * inv * gm_ref[...]).astype(BF16), wm_ref[...], preferred_element_type=F32)
    acc = acc + jnp.dot(_rms(oz_ref[...], gz_ref[...]).astype(BF16), wz_ref[...],
                        preferred_element_type=F32)
    o_ref[...] = acc


def _gate_spread():
    col = jnp.arange(LANES)[:, None]
    out = jnp.arange(N_BRANCH * NSA_HEADS * HEAD_DIM)[None, :]
    lane, r = out % (NSA_HEADS * HEAD_DIM), out // (NSA_HEADS * HEAD_DIM)
    hit = (col == N_BRANCH * (lane // HEAD_DIM) + r).astype(BF16)
    return jnp.concatenate([hit, hit], axis=0)


def _outproj(oc, osel, ow, gate, gate_b, om, oz, x, gn, gm, gz, wn, wm, wz):
    t, d = x.shape
    tm = ROW_TILE
    nt = oz.shape[0] // tm
    wide = NSA_HEADS * HEAD_DIM
    assert wide == MLA_HEADS * HEAD_DIM
    rows = lambda n: pl.BlockSpec((tm, n), lambda i: (i, 0))
    return pl.pallas_call(
        _outproj_kernel,
        grid=(t // tm,),
        in_specs=[rows(wide), rows(wide), rows(wide), rows(LANES), _const_spec((1, LANES)),
                  _const_spec((2 * LANES, N_BRANCH * wide)), rows(wide),
                  pl.BlockSpec((tm, SSM_WIDTH), lambda i: (i % nt, i // nt)), rows(d),
                  _const_spec((1, wide)), _const_spec((1, wide)), _const_spec((1, SSM_WIDTH)),
                  _const_spec((wide, d)), _const_spec((wide, d)), _const_spec((SSM_WIDTH, d))],
        out_specs=rows(d),
        out_shape=jax.ShapeDtypeStruct((t, d), F32),
        compiler_params=_params(("parallel",), VMEM_LIMIT),
        name="out_proj",
    )(oc, osel, ow, gate, gate_b, _gate_spread(), om, oz, x, gn, gm, gz, wn, wm, wz)


def _ffn_kernel(x_ref, g_ref, wg_ref, wv_ref, cw_ref, cb_ref, wd_ref, *rest,
                tm, tiles_per_seq, cf, down_group):
    fg_ref = rest[0] if len(rest) == 4 else None
    o_ref, carry_sc, act_sc = rest[-3:]

    @pl.when(lax.rem(pl.program_id(0), tiles_per_seq) == 0)
    def _():
        carry_sc[...] = jnp.zeros_like(carry_sc)

    x = x_ref[...]
    h = _rms(x, g_ref[...]).astype(BF16)
    row = lax.broadcasted_iota(jnp.int32, (tm, cf), 0)
    n_chunks = D_FF // cf
    acc = x

    def up(c):
        sl = slice(c * cf, (c + 1) * cf)
        return (jnp.dot(h, wg_ref[:, sl], preferred_element_type=F32),
                jnp.dot(h, wv_ref[:, sl], preferred_element_type=F32))

    nxt = up(0)
    for c in range(n_chunks):
        sl = slice(c * cf, (c + 1) * cf)
        gate, val = nxt
        if c + 1 < n_chunks:
            nxt = up(c + 1)
        tail = carry_sc[:, sl]
        p1 = tail[7:8, :]
        p2 = tail[6:7, :]
        g1 = jnp.where(row == 0, p1, pltpu.roll(gate, 1, 0))
        g2 = jnp.where(row == 0, p2, jnp.where(row == 1, p1, pltpu.roll(gate, 2, 0)))
        carry_sc[:, sl] = gate[tm - 8:tm, :]
        gc = cw_ref[0:1, sl] * g2 + cw_ref[1:2, sl] * g1 + cw_ref[2:3, sl] * gate + cb_ref[:, sl]
        act_sc[:, sl] = (gc * _sigmoid(gc) * val).astype(BF16)
        if (c + 1) % down_group == 0 or c + 1 == n_chunks:
            lo = (c // down_group) * down_group * cf
            acc = acc + jnp.dot(act_sc[:, lo:(c + 1) * cf], wd_ref[lo:(c + 1) * cf, :],
                                preferred_element_type=F32)
    o_ref[...] = acc if fg_ref is None else _rms(acc, fg_ref[...])


def _ffn(x, g, wg, wv, cw, cb, wd, seq, final_gain=None):
    t, d = x.shape
    tm = ROW_TILE
    rows = pl.BlockSpec((tm, d), lambda i: (i, 0))
    extra = [] if final_gain is None else [final_gain]
    return pl.pallas_call(
        functools.partial(_ffn_kernel, tm=tm, tiles_per_seq=seq // tm, cf=FF_CHUNK,
                          down_group=FF_DOWN_GROUP),
        grid=(t // tm,),
        in_specs=[rows, _const_spec((1, d)), _const_spec((d, D_FF)), _const_spec((d, D_FF)),
                  _const_spec((8, D_FF)), _const_spec((1, D_FF)), _const_spec((D_FF, d))]
        + [_const_spec((1, d))] * len(extra),
        out_specs=rows,
        out_shape=jax.ShapeDtypeStruct((t, d), F32),
        scratch_shapes=[pltpu.VMEM((8, D_FF), F32), pltpu.VMEM((tm, D_FF), BF16)],
        compiler_params=_params(("arbitrary",), VMEM_LIMIT),
        name="conv_ffn",
    )(x, g, wg, wv, cw, cb, wd, *extra)


def _pad_heads(w, heads, width):
    lead = w.shape[:-1]
    w = w.reshape(lead + (heads, width))
    w = jnp.pad(w, [(0, 0)] * len(lead) + [(0, 0), (0, LANES - width)])
    return w.reshape(lead + (heads * LANES,))


def _inproj_weight(w):
    sizes = (384, 128, 128, 128, 128, 128, 128, 18, Q_LORA, KV_LORA, MLA_ROPE, SSM_WIDTH)
    offs = [0]
    for n in sizes:
        offs.append(offs[-1] + n)
    (w_q, w_kc, w_vc, w_ks, w_vs, w_kw, w_vw, w_g, w_cq, w_ckv, w_kr, w_u) = [
        w[:, a:b] for a, b in zip(offs[:-1], offs[1:])]
    d = w.shape[0]
    half = MLA_ROPE // 2
    z64 = jnp.zeros((d, HEAD_DIM), w.dtype)
    z32 = jnp.zeros((d, LANES - HEAD_DIM - MLA_ROPE), w.dtype)
    r1, r2 = w_kr[:, :half], w_kr[:, half:]
    cols = [
        _pad_heads(w_q * (HEAD_DIM ** -0.5 * LOG2E), NSA_HEADS, HEAD_DIM),
        w_kc, w_vc,
        _pad_heads(w_ks, NSA_KV_HEADS, HEAD_DIM),
        _pad_heads(w_kw, NSA_KV_HEADS, HEAD_DIM),
        w_cq,
        jnp.pad(w_g, ((0, 0), (0, LANES - w_g.shape[1]))),
        w_ckv,
        jnp.concatenate([z64, r1, r2, z32], axis=1),
        jnp.concatenate([z64, r2, r1, z32], axis=1),
        w_u,
    ]
    w_vt = jnp.concatenate([_pad_heads(w_vs, NSA_KV_HEADS, HEAD_DIM),
                            _pad_heads(w_vw, NSA_KV_HEADS, HEAD_DIM)], axis=1).T
    return jnp.concatenate(cols, axis=1).astype(BF16), w_vt.astype(BF16)


def _mla_weights(w_uq, w_uk, w_uv):
    half = MLA_ROPE // 2
    w = w_uq.reshape(Q_LORA, MLA_HEADS, MLA_NOPE + MLA_ROPE)
    nope, r1, r2 = w[..., :MLA_NOPE], w[..., MLA_NOPE:MLA_NOPE + half], w[..., MLA_NOPE + half:]
    z32 = jnp.zeros((Q_LORA, MLA_HEADS, LANES - MLA_NOPE - MLA_ROPE), w.dtype)
    wa = jnp.concatenate([nope, r1, r2, z32], axis=-1).reshape(Q_LORA, MLA_HEADS * LANES)
    wb = jnp.concatenate([jnp.zeros_like(nope), r2, r1, z32], axis=-1).reshape(Q_LORA, MLA_HEADS * LANES)
    wk = _pad_heads(w_uk, MLA_HEADS, MLA_NOPE)
    wvt = _pad_heads(w_uv, MLA_HEADS, HEAD_DIM).T
    return wa.astype(BF16), wb.astype(BF16), wk.astype(BF16), wvt.astype(BF16)


def _ssm_weights(log_dt, a_re, a_im, b_re, b_im, c_re, c_im, d):
    dt = jnp.exp(log_dt.astype(F32))[:, None]
    ar, ai = a_re.astype(F32), a_im.astype(F32)
    mag = jnp.exp(ar * dt)
    abr, abi = mag * jnp.cos(ai * dt), mag * jnp.sin(ai * dt)
    den = ar * ar + ai * ai
    fr = ((abr - 1.0) * ar + abi * ai) / den
    fi = (abi * ar - (abr - 1.0) * ai) / den
    br, bi = b_re.astype(F32), b_im.astype(F32)
    bbr = fr[..., None] * br - fi[..., None] * bi
    bbi = fr[..., None] * bi + fi[..., None] * br
    eye = jnp.eye(SSM_GROUPS, dtype=F32)
    blk_in = lambda m: jnp.einsum("gpc,gh->gchp", m, eye).reshape(SSM_WIDTH, SSM_LANES)
    blk_out = lambda m: jnp.einsum("gcp,gh->gphc", m, eye).reshape(SSM_LANES, SSM_WIDTH)
    bblk = jnp.concatenate([blk_in(bbr), blk_in(bbi)], axis=1)
    ccat = jnp.concatenate([blk_out(c_re.astype(F32)), -blk_out(c_im.astype(F32))], axis=0)
    a_rows = jnp.stack([abr.reshape(SSM_LANES), abi.reshape(SSM_LANES)])
    return bblk.astype(BF16), a_rows, ccat.astype(BF16), d.astype(F32).reshape(1, SSM_WIDTH)


def _overlap_t(nc_pad):
    start = jnp.arange(nc_pad) * CMP_STRIDE
    lo = jnp.arange(MAX_SEL_BLOCKS) * SEL_BLOCK
    hit = (start[None, :] < lo[:, None] + SEL_BLOCK) & (start[None, :] + CMP_BLOCK > lo[:, None])
    return hit.astype(BF16)


def _rope_rows():
    half = MLA_ROPE // 2
    inv_freq = ROPE_THETA ** (-jnp.arange(half, dtype=F32) / half)
    z64 = jnp.zeros((HEAD_DIM,), F32)
    z32 = jnp.zeros((LANES - HEAD_DIM - MLA_ROPE,), F32)
    invf = jnp.concatenate([z64, inv_freq, inv_freq, z32]).reshape(1, LANES)
    sign = jnp.concatenate([z64, -jnp.ones((half,), F32), jnp.ones((half,), F32), z32]).reshape(1, LANES)
    return invf, sign


def kernel(x, positions, attn_norm, w_in, nsa_pe, nsa_ck_w1, nsa_ck_b1, nsa_ck_w2, nsa_ck_b2, nsa_cv_w1, nsa_cv_b1, nsa_cv_w2, nsa_cv_b2, nsa_gate_b, mla_q_norm, mla_kv_norm, mla_w_uq, mla_w_uk, mla_w_uv, ssm_log_dt, ssm_a_re, ssm_a_im, ssm_b_re, ssm_b_im, ssm_c_re, ssm_c_im, ssm_d, ssm_w_glu, ssm_b_glu, out_norm_nsa, out_norm_mla, out_norm_ssm, w_out, ffn_norm, ffn_w_up, ffn_conv_w, ffn_conv_b, ffn_w_down, final_norm):
    batch, seq, d_model = x.shape
    depth = w_in.shape[0]
    t = batch * seq
    n_half = seq // CMP_STRIDE
    n_sel = seq // SEL_BLOCK
    assert depth >= 1 and seq % ROW_TILE == 0 and n_sel <= MAX_SEL_BLOCKS and n_half % LANES == 0
    n_top = min(SEL_TOP, n_sel)
    row = lambda v: v.astype(F32).reshape(1, -1)

    invf, sign = _rope_rows()
    pos_rows = jnp.broadcast_to(positions.astype(F32).reshape(t, 1), (t, LANES))
    cmul, smul = _trig(pos_rows, invf, sign)
    ovl_t = _overlap_t(n_half)

    xf = x.reshape(t, d_model)
    for l in range(depth):
        (q, kvc, ks, kw, cq, gate, ckv, kra, krb, u, vs_t, vw_t) = _inproj(
            xf, row(attn_norm[l]), *_inproj_weight(w_in[l]), batch, seq)

        kvcmp = _compress(
            kvc.reshape(batch, n_half, CMP_STRIDE * _CMP_STREAMS * HEAD_DIM),
            *_compress_weights(nsa_pe[l], nsa_ck_w1[l], nsa_ck_b1[l], nsa_ck_w2[l], nsa_ck_b2[l],
                               nsa_cv_w1[l], nsa_cv_b1[l], nsa_cv_w2[l], nsa_cv_b2[l]))
        vcmp_t = kvcmp[:, NSA_KV_HEADS:].transpose(0, 1, 3, 2).astype(BF16)
        o_cmp, q_aug, o_win = _nsa_local(q, kvcmp, vcmp_t, ovl_t, kw, vw_t, batch, seq, n_top)
        o_sel = _flash(q_aug, ks, vs_t, batch, seq, NSA_GROUP, "nsa_selected")

        wa, wb, wk, wvt = _mla_weights(mla_w_uq[l], mla_w_uk[l], mla_w_uv[l])
        q_m, k_m, vm_t = _mla_proj(cq, ckv, kra, krb, cmul, smul, row(mla_q_norm[l]),
                                   row(mla_kv_norm[l]), wa, wb, wk, wvt, batch, seq)
        o_mla = _flash(q_m, k_m, vm_t, batch, seq, 1, "mla_attention")

        bblk, a_rows, ccat, d_row = _ssm_weights(
            ssm_log_dt[l], ssm_a_re[l], ssm_a_im[l], ssm_b_re[l], ssm_b_im[l],
            ssm_c_re[l], ssm_c_im[l], ssm_d[l])
        o_ssm = _ssm(u.reshape(seq, batch, SSM_WIDTH), bblk, a_rows, ccat, d_row,
                     ssm_w_glu[l].astype(BF16), row(ssm_b_glu[l]))
        o_ssm = o_ssm.reshape(seq, batch * SSM_WIDTH)

        w_o = w_out[l].astype(BF16)
        n_a = NSA_HEADS * HEAD_DIM
        n_b = n_a + MLA_HEADS * HEAD_DIM
        xf = _outproj(
            o_cmp, o_sel, o_win, gate,
            jnp.pad(row(nsa_gate_b[l]), ((0, 0), (0, LANES - NSA_HEADS * N_BRANCH))),
            o_mla, o_ssm, xf,
            row(out_norm_nsa[l]), row(out_norm_mla[l]), row(out_norm_ssm[l]),
            w_o[:n_a], w_o[n_a:n_b], w_o[n_b:])

        w_up = ffn_w_up[l].astype(BF16)
        xf = _ffn(xf, row(ffn_norm[l]), w_up[:, :D_FF], w_up[:, D_FF:],
                  jnp.pad(ffn_conv_w[l].astype(F32), ((0, 8 - ffn_conv_w.shape[1]), (0, 0))),
                  row(ffn_conv_b[l]), ffn_w_down[l].astype(BF16), seq,
                  final_gain=row(final_norm) if l == depth - 1 else None)

    return xf.reshape(batch, seq, d_model)
```

```python
import functools
import math

import jax
import jax.numpy as jnp
from jax import lax
from jax.experimental import pallas as pl
from jax.experimental.pallas import tpu as pltpu

F32 = jnp.float32
BF16 = jnp.bfloat16

LANES = 128
HEAD_DIM = 64
NSA_HEADS = 6
NSA_KV_HEADS = 2
NSA_GROUP = NSA_HEADS // NSA_KV_HEADS
N_BRANCH = 3
CMP_BLOCK = 32
CMP_STRIDE = 16
CMP_HIDDEN = 128
SEL_BLOCK = 64
SEL_TOP = 16
MAX_SEL_BLOCKS = 64
WINDOW = 512
MLA_HEADS = 6
MLA_NOPE = 64
MLA_ROPE = 32
Q_LORA = 384
KV_LORA = 128
ROPE_THETA = 10000.0
SSM_WIDTH = 256
SSM_GROUPS = 16
SSM_GROUP_CH = 16
SSM_STATE = 64
SSM_LANES = SSM_GROUPS * SSM_STATE
D_FF = 2816
EPS = 1e-6
NEG = -1e30
LOG2E = math.log2(math.e)

ROW_TILE = 512
ATTN_TILE = 256
KV_CHUNK = 512
FLASH_ROWS = 1536
FLASH_CHAINS = 6
V_ROWS = 80
SSM_CHUNK = 64
SSM_SUB = 16
FF_CHUNK = 256
FF_DOWN_GROUP = 6
VMEM_LIMIT = 56 * 1024 * 1024


def _params(sem, vmem=None):
    return pltpu.CompilerParams(dimension_semantics=sem, vmem_limit_bytes=vmem)


def _rms(x, g):
    return x * lax.rsqrt(jnp.mean(x * x, axis=-1, keepdims=True) + EPS) * g


def _gelu(x):
    c = math.sqrt(2.0 / math.pi)
    return 0.5 * x * (1.0 + jnp.tanh(c * (x + 0.044715 * (x * x * x))))


def _sigmoid(x):
    return 1.0 / (1.0 + jnp.exp(-x))


def _mod_pow2(x, n):
    assert n & (n - 1) == 0
    return jnp.bitwise_and(x, n - 1)


def _div_pow2(x, n):
    assert n & (n - 1) == 0
    return jnp.right_shift(x, n.bit_length() - 1)


def _const_spec(shape):
    nd = len(shape)
    return pl.BlockSpec(shape, lambda *_: (0,) * nd)


def _trig_kernel(pos_ref, invf_ref, c_ref, s_ref, ns_ref):
    ang = pos_ref[...] * invf_ref[...]
    sin = jnp.sin(ang)
    c_ref[...] = jnp.cos(ang)
    s_ref[...] = sin
    ns_ref[...] = -sin


def _trig(pos_rows, invf_row):
    rows = pos_rows.shape[0]
    tile = min(ROW_TILE, rows)
    spec = pl.BlockSpec((tile, LANES), lambda i: (i, 0))
    return pl.pallas_call(
        _trig_kernel,
        grid=(rows // tile,),
        in_specs=[spec, _const_spec((1, LANES))],
        out_specs=[spec] * 3,
        out_shape=[jax.ShapeDtypeStruct((rows, LANES), F32)] * 3,
        compiler_params=_params(("parallel",)),
        name="rope_trig",
    )(pos_rows, invf_row)


_IN_SEGS = (
    ("q", NSA_HEADS * LANES, BF16),
    ("kc", NSA_KV_HEADS * HEAD_DIM, F32),
    ("vc", NSA_KV_HEADS * HEAD_DIM, F32),
    ("ks", NSA_KV_HEADS * LANES, BF16),
    ("kw", NSA_KV_HEADS * LANES, BF16),
    ("cq", Q_LORA, F32),
    ("gate", LANES, F32),
    ("ckv", KV_LORA, F32),
    ("kra", LANES, F32),
    ("krb", LANES, F32),
    ("u", SSM_WIDTH, F32),
)
_IN_GROUPS = ((0,), (1, 2), (3,), (4,), (5, 6), (7, 8), (9,), (10,))
_IN_COLS = sum(n for _, n, _ in _IN_SEGS)
_VT_ROWS = 2 * NSA_KV_HEADS * LANES


def _ones_row(shape):
    row = _mod_pow2(lax.broadcasted_iota(jnp.int32, shape, 0), LANES)
    return jnp.where(row == HEAD_DIM, 1.0, 0.0)


def _inproj_kernel(x_ref, g_ref, w_ref, wvt_ref, *o_refs, seq, tm):
    h = _rms(x_ref[...], g_ref[...]).astype(BF16)
    off = 0
    for group in _IN_GROUPS:
        width = sum(_IN_SEGS[s][1] for s in group)
        y_all = jnp.dot(h, w_ref[:, off:off + width], preferred_element_type=F32)
        off += width
        lo = 0
        for s in group:
            name, n, dt = _IN_SEGS[s]
            y = y_all[:, lo:lo + n]
            lo += n
            if name == "ks":
                s0 = lax.rem(pl.program_id(0) * tm, seq)
                blk = _div_pow2(s0 + lax.broadcasted_iota(jnp.int32, (tm, n), 0), SEL_BLOCK)
                lane = _mod_pow2(lax.broadcasted_iota(jnp.int32, (tm, n), 1), LANES)
                y = y + jnp.where(lane - HEAD_DIM == blk, 1.0, 0.0)
            o_refs[s][...] = y.astype(dt)
    vst_ref, vwt_ref = o_refs[len(_IN_SEGS):]
    v_t = lax.dot_general(wvt_ref[...], h, (((1,), (1,)), ((), ())), preferred_element_type=F32)
    v_t = (v_t + _ones_row(v_t.shape)).astype(BF16)
    wc = vwt_ref.shape[-1]
    for hh in range(NSA_KV_HEADS):
        vst_ref[0, hh, 0] = v_t[hh * LANES:(hh + 1) * LANES, :]
        r0 = (NSA_KV_HEADS + hh) * LANES
        for c in range(tm // wc):
            vwt_ref[0, hh, c] = v_t[r0:r0 + LANES, c * wc:(c + 1) * wc]


def _inproj(x, g, w, wvt, batch, seq):
    t, d = x.shape
    tm = ROW_TILE
    assert tm == min(KV_CHUNK, seq) and tm % ATTN_TILE == 0
    nt = seq // tm
    rows = lambda n: pl.BlockSpec((tm, n), lambda i: (i, 0))
    out_specs = [rows(n) for _, n, _ in _IN_SEGS]
    out_shape = [jax.ShapeDtypeStruct((t, n), dt) for _, n, dt in _IN_SEGS]
    out_specs[-1] = pl.BlockSpec((tm, SSM_WIDTH), lambda i: (i % nt, i // nt))
    out_shape[-1] = jax.ShapeDtypeStruct((seq, batch * SSM_WIDTH), F32)
    wpt = tm // ATTN_TILE
    out_specs += [pl.BlockSpec((1, NSA_KV_HEADS, 1, LANES, tm), lambda i: (i // nt, 0, i % nt, 0, 0)),
                  pl.BlockSpec((1, NSA_KV_HEADS, wpt, LANES, ATTN_TILE),
                               lambda i: (i // nt, 0, i % nt, 0, 0))]
    out_shape += [jax.ShapeDtypeStruct((batch, NSA_KV_HEADS, nt, LANES, tm), BF16),
                  jax.ShapeDtypeStruct((batch, NSA_KV_HEADS, seq // ATTN_TILE, LANES, ATTN_TILE), BF16)]
    return pl.pallas_call(
        functools.partial(_inproj_kernel, seq=seq, tm=tm),
        grid=(t // tm,),
        in_specs=[pl.BlockSpec((tm, d), lambda i: (i, 0)), _const_spec((1, d)),
                  _const_spec((d, _IN_COLS)), _const_spec((_VT_ROWS, d))],
        out_specs=out_specs,
        out_shape=out_shape,
        compiler_params=_params(("parallel",), VMEM_LIMIT),
        name="in_proj",
    )(x, g, w, wvt)


_CMP_STREAMS = 2 * NSA_KV_HEADS


def _compress_kernel(kc_ref, vc_ref, pe_ref, w1_ref, b1_ref, w2_ref, b2_ref, o_ref):
    n = kc_ref.shape[0] // CMP_STRIDE
    x = jnp.concatenate([r[pl.ds(l, n, stride=CMP_STRIDE), :]
                         for l in range(CMP_STRIDE) for r in (kc_ref, vc_ref)], axis=1)
    top = jnp.dot((x + pe_ref[0:1, :]).astype(BF16), w1_ref[0], preferred_element_type=F32)
    bot = jnp.dot((x + pe_ref[1:2, :]).astype(BF16), w1_ref[1], preferred_element_type=F32)
    hid = _gelu(top + pltpu.roll(bot, n - 1, 0) + b1_ref[...]).astype(BF16)
    for j in range(_CMP_STREAMS):
        kind = j // NSA_KV_HEADS
        o_ref[0, j] = (jnp.dot(hid[:, j * CMP_HIDDEN:(j + 1) * CMP_HIDDEN], w2_ref[kind],
                               preferred_element_type=F32) + b2_ref[kind])


def _compress(kc, vc, batch, pe, w1, b1, w2, b2):
    b = batch
    seq = kc.shape[0] // b
    n = seq // CMP_STRIDE
    width = CMP_STRIDE * (kc.shape[1] + vc.shape[1])
    hid = _CMP_STREAMS * CMP_HIDDEN
    xspec = pl.BlockSpec((seq, kc.shape[1]), lambda i: (i, 0))
    return pl.pallas_call(
        _compress_kernel,
        grid=(b,),
        in_specs=[xspec, xspec,
                  _const_spec((2, width)), _const_spec((2, width, hid)), _const_spec((1, hid)),
                  _const_spec((2, CMP_HIDDEN, LANES)), _const_spec((2, 1, LANES))],
        out_specs=pl.BlockSpec((1, _CMP_STREAMS, n, LANES), lambda i: (i, 0, 0, 0)),
        out_shape=jax.ShapeDtypeStruct((b, _CMP_STREAMS, n, LANES), F32),
        compiler_params=_params(("parallel",), VMEM_LIMIT),
        name="nsa_compress",
    )(kc, vc, pe, w1, b1, w2, b2)


def _compress_weights(pe, ck_w1, ck_b1, ck_w2, ck_b2, cv_w1, cv_b1, cv_w2, cv_b2):
    eye = jnp.eye(_CMP_STREAMS, dtype=F32)
    per_stream = jnp.stack([ck_w1, ck_w1, cv_w1, cv_w1]).astype(F32)

    def place(w):
        return jnp.einsum("jldf,jk->ljdkf", w, eye).reshape(
            CMP_STRIDE * _CMP_STREAMS * HEAD_DIM, _CMP_STREAMS * CMP_HIDDEN)

    w1 = jnp.stack([place(per_stream[:, :CMP_STRIDE]), place(per_stream[:, CMP_STRIDE:])])
    spread = lambda p: jnp.broadcast_to(p[:, None, :], (CMP_STRIDE, _CMP_STREAMS, HEAD_DIM)).reshape(-1)
    pe2 = jnp.stack([spread(pe[:CMP_STRIDE]), spread(pe[CMP_STRIDE:])]).astype(F32)
    b1 = jnp.concatenate([ck_b1, ck_b1, cv_b1, cv_b1]).astype(F32).reshape(1, -1)
    pad2 = lambda w: jnp.pad(w.astype(F32), ((0, 0), (0, LANES - HEAD_DIM)))
    w2 = jnp.stack([pad2(ck_w2), pad2(cv_w2)]).astype(BF16)
    b2 = jnp.stack([pad2(ck_b2.reshape(1, -1)), pad2(cv_b2.reshape(1, -1))])
    return pe2, w1.astype(BF16), b1, w2, b2


def _stack_group(q_ref, h, groups):
    return jnp.concatenate([q_ref[:, (h * groups + g) * LANES:(h * groups + g + 1) * LANES]
                            for g in range(groups)], axis=0)


def _store_heads_t(o_ref, chains_t, tq):
    heads = [c[:, g * tq:(g + 1) * tq] for c in chains_t for g in range(c.shape[1] // tq)]
    for p in range(len(heads) // 2):
        pair = jnp.concatenate([heads[2 * p], heads[2 * p + 1]], axis=0)
        o_ref[:, p * LANES:(p + 1) * LANES] = pair.T.astype(o_ref.dtype)


def _stable_rank(val):
    sub = 8
    n_blk = val.shape[0] // sub
    blocks = [val[r * sub:(r + 1) * sub] for r in range(n_blk)]
    ranks = [jnp.zeros(blocks[0].shape, F32) for _ in range(n_blk)]
    row_in = lax.broadcasted_iota(jnp.int32, blocks[0].shape, 0)
    for i in range(val.shape[0]):
        row = val[i:i + 1, :]
        for r in range(n_blk):
            if r * sub > i:
                ahead = jnp.where(row >= blocks[r], 1.0, 0.0)
            elif (r + 1) * sub <= i:
                ahead = jnp.where(row > blocks[r], 1.0, 0.0)
            else:
                ahead = jnp.where(row_in + r * sub > i, jnp.where(row >= blocks[r], 1.0, 0.0),
                                  jnp.where(row > blocks[r], 1.0, 0.0))
            ranks[r] = ranks[r] + ahead
    return jnp.concatenate(ranks, axis=0)


def _cmp_scores(kc_ref, qs):
    nt = (((1,), (1,)), ((), ()))
    return [lax.dot_general(kc_ref[0, h].astype(BF16), qs[h], nt, preferred_element_type=F32)
            for h in range(NSA_KV_HEADS)]


def _cmp_finish(ss, q0, q_ref, vct_ref, ovl_ref, o_ref, qa_ref, *, tq, n_top):
    m_cols = NSA_GROUP * tq
    nc = ss[0].shape[0]
    t_col = q0 + _mod_pow2(lax.broadcasted_iota(jnp.int32, (nc, m_cols), 1), tq)
    n_row = lax.broadcasted_iota(jnp.int32, (nc, m_cols), 0)
    mask = n_row * CMP_STRIDE + (CMP_BLOCK - 1) <= t_col
    has_valid = t_col[0:1, :] >= CMP_BLOCK - 1
    j_idx =lax.broadcasted_iota(jnp.int32, (MAX_SEL_BLOCKS, tq), 0)
    cur = _div_pow2(q0 + lax.broadcasted_iota(jnp.int32, (MAX_SEL_BLOCKS, tq), 1), SEL_BLOCK)
    forced = (j_idx == 0) | (j_idx == cur) | (j_idx == cur - 1)
    outs = []
    for h in range(NSA_KV_HEADS):
        sm = jnp.where(mask, ss[h], NEG)
        e = jnp.exp2(sm - jnp.max(sm, axis=0, keepdims=True))
        scale = jnp.where(has_valid, 1.0 / jnp.sum(e, axis=0, keepdims=True), 0.0)
        p = e * scale
        outs.append(jnp.dot(vct_ref[0, h, 0:HEAD_DIM, :], p.astype(BF16),
                            preferred_element_type=F32))

        psum = p[:, 0:tq] + p[:, tq:2 * tq] + p[:, 2 * tq:3 * tq]
        hi = psum.astype(BF16)
        lo = (psum - hi.astype(F32)).astype(BF16)
        imp = (jnp.dot(ovl_ref[...], hi, preferred_element_type=F32)
               + jnp.dot(ovl_ref[...], lo, preferred_element_type=F32))
        val = jnp.where(forced, jnp.inf, jnp.where(j_idx > cur, -jnp.inf, imp))
        bias_t = jnp.where(_stable_rank(val) < n_top, 0.0, NEG)
        bias = jnp.concatenate([jnp.zeros_like(bias_t), bias_t], axis=0).T
        for g in range(NSA_GROUP):
            lane0 = (h * NSA_GROUP + g) * LANES
            qa_ref[:, lane0:lane0 + LANES] = (q_ref[:, lane0:lane0 + LANES].astype(F32)
                                              + bias).astype(qa_ref.dtype)
    _store_heads_t(o_ref, outs, tq)


def _flash_kernel(q_ref, k_ref, vt_ref, o_ref, s_a, s_b, *, chains, groups, tq, ck):
    q0 = pl.program_id(2) * tq
    m_cols = groups * tq
    qs = [_stack_group(q_ref, h, groups) for h in range(chains)]

    def issue_scores(c, s_ref):
        k0 = pl.multiple_of(c * ck, ck)
        for h in range(chains):
            s_ref[h] = lax.dot_general(k_ref[pl.ds(k0, ck), h * LANES:(h + 1) * LANES], qs[h],
                                       (((1,), (1,)), ((), ())), preferred_element_type=F32)

    def absorb(c, s_ref, stats, visible=None):
        out = []
        for h in range(chains):
            s = s_ref[h]
            if visible is not None:
                s = jnp.where(visible, s, NEG)
            m_old, acc = stats[h]
            m_new = jnp.maximum(m_old, jnp.max(s, axis=0, keepdims=True))
            alpha = jnp.exp2(m_old - m_new)
            p = jnp.exp2(s - m_new).astype(BF16)
            vt = vt_ref[0, h, c, 0:V_ROWS, :]
            out.append((m_new, alpha * acc + jnp.dot(vt, p, preferred_element_type=F32)))
        return tuple(out)

    def finish(stats):
        _store_heads_t(o_ref, [acc[0:HEAD_DIM] / acc[HEAD_DIM:HEAD_DIM + 1] for _, acc in stats], tq)

    last = (q0 + tq - 1) // ck
    key_pos = last * ck + lax.broadcasted_iota(jnp.int32, (ck, m_cols), 0)
    q_pos = q0 + _mod_pow2(lax.broadcasted_iota(jnp.int32, (ck, m_cols), 1), tq)
    visible = key_pos <= q_pos

    def pair(j, stats):
        c = 2 * j
        issue_scores(c + 1, s_b)
        stats = absorb(c, s_a, stats)
        issue_scores(c + 2, s_a)
        return absorb(c + 1, s_b, stats)

    issue_scores(0, s_a)
    stats = tuple((jnp.full((1, m_cols), NEG, F32), jnp.zeros((V_ROWS, m_cols), F32))
                  for _ in range(chains))
    stats = lax.fori_loop(0, last // 2, pair, stats)

    @pl.when(lax.rem(last, 2) == 0)
    def _():
        finish(absorb(last, s_a, stats, visible))

    @pl.when(lax.rem(last, 2) == 1)
    def _():
        issue_scores(last, s_b)
        finish(absorb(last, s_b, absorb(last - 1, s_a, stats), visible))


def _flash(q, k, v_t, batch, seq, groups, name):
    t, qcols = q.shape
    kv_heads = k.shape[1] // LANES
    ck = min(KV_CHUNK, seq)
    tq = min(FLASH_ROWS // groups, ck)
    assert tq & (tq - 1) == 0 and ck % tq == 0 and seq % ck == 0
    nq = seq // tq
    n_chunks = seq // ck
    assert v_t.shape == (batch, kv_heads, n_chunks, LANES, ck)
    chains = max(c for c in range(1, FLASH_CHAINS + 1)
                 if kv_heads % c == 0 and (c * groups) % 2 == 0)
    heads = kv_heads * groups
    return pl.pallas_call(
        functools.partial(_flash_kernel, chains=chains, groups=groups, tq=tq, ck=ck),
        grid=(batch, kv_heads // chains, nq),
        in_specs=[pl.BlockSpec((tq, chains * groups * LANES), lambda b, h, i: (b * nq + i, h)),
                  pl.BlockSpec((seq, chains * LANES), lambda b, h, i: (b, h)),
                  pl.BlockSpec((1, chains, n_chunks, LANES, ck), lambda b, h, i: (b, h, 0, 0, 0))],
        out_specs=pl.BlockSpec((tq, chains * groups * HEAD_DIM), lambda b, h, i: (b * nq + i, h)),
        out_shape=jax.ShapeDtypeStruct((t, heads * HEAD_DIM), BF16),
        scratch_shapes=[pltpu.VMEM((chains, ck, groups * tq), F32)] * 2,
        compiler_params=_params(("parallel", "parallel", "parallel"), VMEM_LIMIT),
        name=name,
    )(q, k, v_t)


def _nsa_local_kernel(q_ref, kc_ref, vct_ref, ovl_ref, k_ref, vt_ref, oc_ref, qa_ref, ow_ref,
                      *, tq, n_top):
    i = pl.program_id(1)
    q0 = i * tq
    n_back = WINDOW // tq
    n_span = n_back + 1
    m_cols = NSA_GROUP * tq
    qs = [_stack_group(q_ref, h, NSA_GROUP) for h in range(NSA_KV_HEADS)]
    q_pos = q0 + _mod_pow2(lax.broadcasted_iota(jnp.int32, (tq, m_cols), 1), tq)
    key_in = lax.broadcasted_iota(jnp.int32, (tq, m_cols), 0)
    nt = (((1,), (1,)), ((), ()))

    def run(c0, clamped):
        start = c0 * tq if clamped else pl.multiple_of(c0 * tq, tq)
        ss = [lax.dot_general(k_ref[pl.ds(start, n_span * tq), h * LANES:(h + 1) * LANES], qs[h],
                              nt, preferred_element_type=F32) for h in range(NSA_KV_HEADS)]
        ss_cmp = _cmp_scores(kc_ref, qs)
        outs = []
        for h in range(NSA_KV_HEADS):
            blocks = []
            for j in range(n_span):
                blk = ss[h][j * tq:(j + 1) * tq]
                key_pos = start + j * tq + key_in
                if clamped:
                    blk = jnp.where((key_pos <= q_pos) & (key_pos > q_pos - WINDOW), blk, NEG)
                elif j == 0:
                    blk = jnp.where(key_pos > q_pos - WINDOW, blk, NEG)
                elif j == n_span - 1:
                    blk = jnp.where(key_pos <= q_pos, blk, NEG)
                blocks.append(blk)
            m = functools.reduce(jnp.maximum, [jnp.max(b, axis=0, keepdims=True) for b in blocks])
            acc = 0.0
            for j in range(n_span):
                p = jnp.exp2(blocks[j] - m).astype(BF16)
                acc = acc + jnp.dot(vt_ref[0, h, c0 + j, 0:V_ROWS, :], p, preferred_element_type=F32)
            outs.append(acc[0:HEAD_DIM] / acc[HEAD_DIM:HEAD_DIM + 1])
        _store_heads_t(ow_ref, outs, tq)
        _cmp_finish(ss_cmp, q0, q_ref, vct_ref, ovl_ref, oc_ref, qa_ref, tq=tq, n_top=n_top)

    @pl.when(i < n_back)
    def _():
        run(0, True)

    @pl.when(i >= n_back)
    def _():
        run(i - n_back, False)


def _nsa_local(q, kcmp, vcmp_t, ovl, k, v_t, batch, seq, n_top):
    t = q.shape[0]
    tq = ATTN_TILE
    nq = seq // tq
    nc = kcmp.shape[2]
    assert WINDOW % tq == 0 and seq >= WINDOW + tq
    wide = NSA_HEADS * LANES
    qspec = pl.BlockSpec((tq, wide), lambda b, i: (b * nq + i, 0))
    ospec = pl.BlockSpec((tq, NSA_HEADS * HEAD_DIM), lambda b, i: (b * nq + i, 0))
    o_sds = jax.ShapeDtypeStruct((t, NSA_HEADS * HEAD_DIM), BF16)
    return pl.pallas_call(
        functools.partial(_nsa_local_kernel, tq=tq, n_top=n_top),
        grid=(batch, nq),
        in_specs=[qspec,
                  pl.BlockSpec((1, NSA_KV_HEADS, nc, LANES), lambda b, i: (b, 0, 0, 0)),
                  pl.BlockSpec((1, NSA_KV_HEADS, LANES, nc), lambda b, i: (b, 0, 0, 0)),
                  _const_spec((MAX_SEL_BLOCKS, nc)),
                  pl.BlockSpec((seq, NSA_KV_HEADS * LANES), lambda b, i: (b, 0)),
                  pl.BlockSpec((1, NSA_KV_HEADS, nq, LANES, tq), lambda b, i: (b, 0, 0, 0, 0))],
        out_specs=[ospec, qspec, ospec],
        out_shape=[o_sds, jax.ShapeDtypeStruct((t, wide), BF16), o_sds],
        compiler_params=_params(("parallel", "parallel"), VMEM_LIMIT),
        name="nsa_local",
    )(q, kcmp, vcmp_t, ovl, k, v_t)


def _mla_proj_kernel(cq_ref, ckv_ref, kra_ref, krb_ref, cm_ref, sm_ref, qg_ref, kg_ref,
                     wa_ref, wb_ref, wk_ref, wvt_ref, q_ref, k_ref, vt_ref, *, scale):
    qn = _rms(cq_ref[...], qg_ref[...]).astype(BF16)
    cn = _rms(ckv_ref[...], kg_ref[...]).astype(BF16)
    cm = cm_ref[...]
    sm = sm_ref[...]
    k_rot = kra_ref[...] * cm + krb_ref[...] * sm
    cm2 = jnp.concatenate([cm, cm], axis=1)
    sm2 = jnp.concatenate([sm, sm], axis=1)
    k_rot2 = jnp.concatenate([k_rot, k_rot], axis=1)
    for h in range(0, MLA_HEADS, 2):
        sl = slice(h * LANES, (h + 2) * LANES)
        qa = jnp.dot(qn, wa_ref[:, sl], preferred_element_type=F32)
        qb = jnp.dot(qn, wb_ref[:, sl], preferred_element_type=F32)
        q_ref[:, sl] = ((qa * cm2 + qb * sm2) * scale).astype(q_ref.dtype)
        k_ref[:, sl] = (jnp.dot(cn, wk_ref[:, sl], preferred_element_type=F32) + k_rot2).astype(k_ref.dtype)
    v_t = lax.dot_general(wvt_ref[...], cn, (((1,), (1,)), ((), ())), preferred_element_type=F32)
    v_t = (v_t + _ones_row(v_t.shape)).astype(vt_ref.dtype)
    for h in range(MLA_HEADS):
        vt_ref[0, h, 0] = v_t[h * LANES:(h + 1) * LANES, :]


def _mla_proj(cq, ckv, kra, krb, cmul, smul, qg, kg, wa, wb, wk, wvt, batch, seq):
    t = cq.shape[0]
    tm = ROW_TILE
    assert tm == min(KV_CHUNK, seq)
    nt = seq // tm
    width = MLA_HEADS * LANES
    rows = lambda n: pl.BlockSpec((tm, n), lambda i: (i, 0))
    scale = (MLA_NOPE + MLA_ROPE) ** -0.5 * LOG2E
    return pl.pallas_call(
        functools.partial(_mla_proj_kernel, scale=scale),
        grid=(t // tm,),
        in_specs=[rows(Q_LORA), rows(KV_LORA), rows(LANES), rows(LANES), rows(LANES), rows(LANES),
                  _const_spec((1, Q_LORA)), _const_spec((1, KV_LORA)),
                  _const_spec((Q_LORA, width)), _const_spec((Q_LORA, width)),
                  _const_spec((KV_LORA, width)), _const_spec((width, KV_LORA))],
        out_specs=[rows(width), rows(width),
                   pl.BlockSpec((1, MLA_HEADS, 1, LANES, tm), lambda i: (i // nt, 0, i % nt, 0, 0))],
        out_shape=[jax.ShapeDtypeStruct((t, width), BF16)] * 2
        + [jax.ShapeDtypeStruct((batch, MLA_HEADS, nt, LANES, tm), BF16)],
        compiler_params=_params(("parallel",)),
        name="mla_proj",
    )(cq, ckv, kra, krb, cmul, smul, qg, kg, wa, wb, wk, wvt)


def _ssm_kernel(u_ref, bblk_ref, a_ref, ccat_ref, d_ref, wglu_ref, bglu_ref, o_ref,
                h_sc, st_sc, *, tc, sub, nb):
    @pl.when(pl.program_id(0) == 0)
    def _():
        st_sc[...] = jnp.zeros_like(st_sc)

    n = SSM_LANES
    ar = jnp.broadcast_to(a_ref[0:1, :], (nb, n))
    ai = jnp.broadcast_to(a_ref[1:2, :], (nb, n))
    n_sub = tc // sub
    rows = sub * nb
    us = []
    for j in range(n_sub):
        u = u_ref[j * sub:(j + 1) * sub].reshape(rows, SSM_WIDTH)
        us.append(u)
        h_sc[j * rows:(j + 1) * rows, :] = jnp.dot(u.astype(BF16), bblk_ref[...],
                                                   preferred_element_type=F32)
    hr, hi = st_sc[0], st_sc[1]
    ys = []
    for j in range(n_sub):
        for t in range(j * sub, (j + 1) * sub):
            r0 = t * nb
            nr = ar * hr - ai * hi + h_sc[r0:r0 + nb, 0:n]
            ni = ar * hi + ai * hr + h_sc[r0:r0 + nb, n:2 * n]
            h_sc[r0:r0 + nb, 0:n] = nr
            h_sc[r0:r0 + nb, n:2 * n] = ni
            hr, hi = nr, ni
        ys.append(jnp.dot(h_sc[j * rows:(j + 1) * rows, :].astype(BF16), ccat_ref[...],
                          preferred_element_type=F32) + d_ref[...] * us[j])
    st_sc[0] = hr
    st_sc[1] = hi
    z = _gelu(jnp.concatenate(ys, axis=0))
    gate = jnp.dot(z.astype(BF16), wglu_ref[...], preferred_element_type=F32) + bglu_ref[...]
    o_ref[...] = (z * _sigmoid(gate)).reshape(tc, nb, SSM_WIDTH)


def _ssm(u_t, bblk, a_rows, ccat, d_row, wglu, bglu):
    seq, nb, _ = u_t.shape
    tc = SSM_CHUNK
    n = SSM_LANES
    uspec = pl.BlockSpec((tc, nb, SSM_WIDTH), lambda i: (i, 0, 0))
    return pl.pallas_call(
        functools.partial(_ssm_kernel, tc=tc, sub=SSM_SUB, nb=nb),
        grid=(seq // tc,),
        in_specs=[uspec, _const_spec((SSM_WIDTH, 2 * n)), _const_spec((2, n)),
                  _const_spec((2 * n, SSM_WIDTH)), _const_spec((1, SSM_WIDTH)),
                  _const_spec((SSM_WIDTH, SSM_WIDTH)), _const_spec((1, SSM_WIDTH))],
        out_specs=uspec,
        out_shape=jax.ShapeDtypeStruct(u_t.shape, F32),
        scratch_shapes=[pltpu.VMEM((tc * nb, 2 * n), F32), pltpu.VMEM((2, nb, n), F32)],
        compiler_params=_params(("arbitrary",), VMEM_LIMIT),
        name="s5_scan",
    )(u_t, bblk, a_rows, ccat, d_row, wglu, bglu)


def _outproj_kernel(oc_ref, os_ref, ow_ref, gate_ref, gb_ref, ex_ref, om_ref, oz_ref, x_ref,
                    gn_ref, gm_ref, gz_ref, wn_ref, wm_ref, wz_ref, o_ref):
    g = _sigmoid(gate_ref[...] + gb_ref[...])
    g_hi = g.astype(BF16)
    g_lo = (g - g_hi.astype(F32)).astype(BF16)
    spread = jnp.dot(jnp.concatenate([g_hi, g_lo], axis=1), ex_ref[...],
                     preferred_element_type=F32)
    o_a = 0.0
    wide = oc_ref.shape[1]
    for r, br_ref in enumerate((oc_ref, os_ref, ow_ref)):
        o_a = o_a + spread[:, r * wide:(r + 1) * wide] * br_ref[...].astype(F32)
    inv = lax.rsqrt(jnp.sum(o_a * o_a, axis=-1, keepdims=True) * (1.0 / (NSA_HEADS * HEAD_DIM)) + EPS)
    acc = x_ref[...] + jnp.dot((o_a * inv * gn_ref[...]).astype(BF16), wn_ref[...],
                               preferred_element_type=F32)
    om = om_ref[...].astype(F32)
    inv = lax.rsqrt(jnp.sum(om * om, axis=-1, keepdims=True) * (1.0 / (MLA_HEADS * HEAD_DIM)) + EPS)
    acc = acc + jnp.dot((om * inv * gm_ref[...]).astype(BF16), wm_ref[...], preferred_element_type=F32)
    acc = acc + jnp.dot(_rms(oz_ref[...], gz_ref[...]).astype(BF16), wz_ref[...],
                        preferred_element_type=F32)
    o_ref[...] = acc


def _gate_spread():
    col = jnp.arange(LANES)[:, None]
    out = jnp.arange(N_BRANCH * NSA_HEADS * HEAD_DIM)[None, :]
    lane, r = out % (NSA_HEADS * HEAD_DIM), out // (NSA_HEADS * HEAD_DIM)
    hit = (col == N_BRANCH * (lane // HEAD_DIM) + r).astype(BF16)
    return jnp.concatenate([hit, hit], axis=0)


def _outproj(oc, osel, ow, gate, gate_b, om, oz, x, gn, gm, gz, wn, wm, wz):
    t, d = x.shape
    tm = ROW_TILE
    nt = oz.shape[0] // tm
    wide = NSA_HEADS * HEAD_DIM
    assert wide == MLA_HEADS * HEAD_DIM
    rows = lambda n: pl.BlockSpec((tm, n), lambda i: (i, 0))
    return pl.pallas_call(
        _outproj_kernel,
        grid=(t // tm,),
        in_specs=[rows(wide), rows(wide), rows(wide), rows(LANES), _const_spec((1, LANES)),
                  _const_spec((2 * LANES, N_BRANCH * wide)), rows(wide),
                  pl.BlockSpec((tm, SSM_WIDTH), lambda i: (i % nt, i // nt)), rows(d),
                  _const_spec((1, wide)), _const_spec((1, wide)), _const_spec((1, SSM_WIDTH)),
                  _const_spec((wide, d)), _const_spec((wide, d)), _const_spec((SSM_WIDTH, d))],
        out_specs=rows(d),
        out_shape=jax.ShapeDtypeStruct((t, d), F32),
        compiler_params=_params(("parallel",), VMEM_LIMIT),
        name="out_proj",
    )(oc, osel, ow, gate, gate_b, _gate_spread(), om, oz, x, gn, gm, gz, wn, wm, wz)


def _ffn_kernel(x_ref, g_ref, wg_ref, wv_ref, cw_ref, cb_ref, wd_ref, *rest,
                tm, tiles_per_seq, cf, down_group):
    fg_ref = rest[0] if len(rest) == 4 else None
    o_ref, carry_sc, act_sc = rest[-3:]

    @pl.when(lax.rem(pl.program_id(0), tiles_per_seq) == 0)
    def _():
        carry_sc[...] = jnp.zeros_like(carry_sc)

    x = x_ref[...]
    h = _rms(x, g_ref[...]).astype(BF16)
    row = lax.broadcasted_iota(jnp.int32, (tm, cf), 0)
    n_chunks = D_FF // cf
    acc = x

    def up(c):
        sl = slice(c * cf, (c + 1) * cf)
        return (jnp.dot(h, wg_ref[:, sl], preferred_element_type=F32),
                jnp.dot(h, wv_ref[:, sl], preferred_element_type=F32))

    nxt = up(0)
    for c in range(n_chunks):
        sl = slice(c * cf, (c + 1) * cf)
        gate, val = nxt
        if c + 1 < n_chunks:
            nxt = up(c + 1)
        tail = carry_sc[:, sl]
        p1 = tail[7:8, :]
        p2 = tail[6:7, :]
        g1 = jnp.where(row == 0, p1, pltpu.roll(gate, 1, 0))
        g2 = jnp.where(row == 0, p2, jnp.where(row == 1, p1, pltpu.roll(gate, 2, 0)))
        carry_sc[:, sl] = gate[tm - 8:tm, :]
        gc = cw_ref[0:1, sl] * g2 + cw_ref[1:2, sl] * g1 + cw_ref[2:3, sl] * gate + cb_ref[:, sl]
        act_sc[:, sl] = (gc * _sigmoid(gc) * val).astype(BF16)
        if (c + 1) % down_group == 0 or c + 1 == n_chunks:
            lo = (c // down_group) * down_group * cf
            acc = acc + jnp.dot(act_sc[:, lo:(c + 1) * cf], wd_ref[lo:(c + 1) * cf, :],
                                preferred_element_type=F32)
    o_ref[...] = acc if fg_ref is None else _rms(acc, fg_ref[...])


def _ffn(x, g, wg, wv, cw, cb, wd, seq, final_gain=None):
    t, d = x.shape
    tm = ROW_TILE
    rows = pl.BlockSpec((tm, d), lambda i: (i, 0))
    extra = [] if final_gain is None else [final_gain]
    return pl.pallas_call(
        functools.partial(_ffn_kernel, tm=tm, tiles_per_seq=seq // tm, cf=FF_CHUNK,
                          down_group=FF_DOWN_GROUP),
        grid=(t // tm,),
        in_specs=[rows, _const_spec((1, d)), _const_spec((d, D_FF)), _const_spec((d, D_FF)),
                  _const_spec((8, D_FF)), _const_spec((1, D_FF)), _const_spec((D_FF, d))]
        + [_const_spec((1, d))] * len(extra),
        out_specs=rows,
        out_shape=jax.ShapeDtypeStruct((t, d), F32),
        scratch_shapes=[pltpu.VMEM((8, D_FF), F32), pltpu.VMEM((tm, D_FF), BF16)],
        compiler_params=_params(("arbitrary",), VMEM_LIMIT),
        name="conv_ffn",
    )(x, g, wg, wv, cw, cb, wd, *extra)


def _pad_heads(w, heads, width):
    lead = w.shape[:-1]
    w = w.reshape(lead + (heads, width))
    w = jnp.pad(w, [(0, 0)] * len(lead) + [(0, 0), (0, LANES - width)])
    return w.reshape(lead + (heads * LANES,))


def _inproj_weight(w):
    sizes = (384, 128, 128, 128, 128, 128, 128, 18, Q_LORA, KV_LORA, MLA_ROPE, SSM_WIDTH)
    offs = [0]
    for n in sizes:
        offs.append(offs[-1] + n)
    (w_q, w_kc, w_vc, w_ks, w_vs, w_kw, w_vw, w_g, w_cq, w_ckv, w_kr, w_u) = [
        w[:, a:b] for a, b in zip(offs[:-1], offs[1:])]
    d = w.shape[0]
    half = MLA_ROPE // 2
    z64 = jnp.zeros((d, HEAD_DIM), w.dtype)
    z32 = jnp.zeros((d, LANES - HEAD_DIM - MLA_ROPE), w.dtype)
    r1, r2 = w_kr[:, :half], w_kr[:, half:]
    cols = [
        _pad_heads(w_q * (HEAD_DIM ** -0.5 * LOG2E), NSA_HEADS, HEAD_DIM),
        w_kc, w_vc,
        _pad_heads(w_ks, NSA_KV_HEADS, HEAD_DIM),
        _pad_heads(w_kw, NSA_KV_HEADS, HEAD_DIM),
        w_cq,
        jnp.pad(w_g, ((0, 0), (0, LANES - w_g.shape[1]))),
        w_ckv,
        jnp.concatenate([z64, r1, r2, z32], axis=1),
        jnp.concatenate([z64, r2, r1, z32], axis=1),
        w_u,
    ]
    w_vt = jnp.concatenate([_pad_heads(w_vs, NSA_KV_HEADS, HEAD_DIM),
                            _pad_heads(w_vw, NSA_KV_HEADS, HEAD_DIM)], axis=1).T
    return jnp.concatenate(cols, axis=1).astype(BF16), w_vt.astype(BF16)


def _mla_weights(w_uq, w_uk, w_uv):
    half = MLA_ROPE // 2
    w = w_uq.reshape(Q_LORA, MLA_HEADS, MLA_NOPE + MLA_ROPE)
    nope, r1, r2 = w[..., :MLA_NOPE], w[..., MLA_NOPE:MLA_NOPE + half], w[..., MLA_NOPE + half:]
    z32 = jnp.zeros((Q_LORA, MLA_HEADS, LANES - MLA_NOPE - MLA_ROPE), w.dtype)
    wa = jnp.concatenate([nope, r1, r2, z32], axis=-1).reshape(Q_LORA, MLA_HEADS * LANES)
    wb = jnp.concatenate([jnp.zeros_like(nope), r2, r1, z32], axis=-1).reshape(Q_LORA, MLA_HEADS * LANES)
    wk = _pad_heads(w_uk, MLA_HEADS, MLA_NOPE)
    wvt = _pad_heads(w_uv, MLA_HEADS, HEAD_DIM).T
    return wa.astype(BF16), wb.astype(BF16), wk.astype(BF16), wvt.astype(BF16)


def _ssm_weights(log_dt, a_re, a_im, b_re, b_im, c_re, c_im, d):
    dt = jnp.exp(log_dt.astype(F32))[:, None]
    ar, ai = a_re.astype(F32), a_im.astype(F32)
    mag = jnp.exp(ar * dt)
    abr, abi = mag * jnp.cos(ai * dt), mag * jnp.sin(ai * dt)
    den = ar * ar + ai * ai
    fr = ((abr - 1.0) * ar + abi * ai) / den
    fi = (abi * ar - (abr - 1.0) * ai) / den
    br, bi = b_re.astype(F32), b_im.astype(F32)
    bbr = fr[..., None] * br - fi[..., None] * bi
    bbi = fr[..., None] * bi + fi[..., None] * br
    eye = jnp.eye(SSM_GROUPS, dtype=F32)
    blk_in = lambda m: jnp.einsum("gpc,gh->gchp", m, eye).reshape(SSM_WIDTH, SSM_LANES)
    blk_out = lambda m: jnp.einsum("gcp,gh->gphc", m, eye).reshape(SSM_LANES, SSM_WIDTH)
    bblk = jnp.concatenate([blk_in(bbr), blk_in(bbi)], axis=1)
    ccat = jnp.concatenate([blk_out(c_re.astype(F32)), -blk_out(c_im.astype(F32))], axis=0)
    a_rows = jnp.stack([abr.reshape(SSM_LANES), abi.reshape(SSM_LANES)])
    return bblk.astype(BF16), a_rows, ccat.astype(BF16), d.astype(F32).reshape(1, SSM_WIDTH)


def _overlap_t(nc_pad):
    start = jnp.arange(nc_pad) * CMP_STRIDE
    lo = jnp.arange(MAX_SEL_BLOCKS) * SEL_BLOCK
    hit = (start[None, :] < lo[:, None] + SEL_BLOCK) & (start[None, :] + CMP_BLOCK > lo[:, None])
    return hit.astype(BF16)


def _rope_multipliers(positions):
    half = MLA_ROPE // 2
    per_row = LANES // half
    t = positions.size
    inv_freq = ROPE_THETA ** (-jnp.arange(half, dtype=F32) / half)
    pos = jnp.broadcast_to(positions.astype(F32).reshape(t // per_row, per_row, 1),
                           (t // per_row, per_row, half)).reshape(t // per_row, LANES)
    cos, sin, nsin = [a.reshape(t, half)
                      for a in _trig(pos, jnp.tile(inv_freq, per_row).reshape(1, LANES))]
    one = jnp.ones((t, HEAD_DIM), F32)
    zero = jnp.zeros((t, HEAD_DIM), F32)
    pad = jnp.zeros((t, LANES - HEAD_DIM - MLA_ROPE), F32)
    return (jnp.concatenate([one, cos, cos, pad], axis=1),
            jnp.concatenate([zero, nsin, sin, pad], axis=1))


def kernel(x, positions, attn_norm, w_in, nsa_pe, nsa_ck_w1, nsa_ck_b1, nsa_ck_w2, nsa_ck_b2, nsa_cv_w1, nsa_cv_b1, nsa_cv_w2, nsa_cv_b2, nsa_gate_b, mla_q_norm, mla_kv_norm, mla_w_uq, mla_w_uk, mla_w_uv, ssm_log_dt, ssm_a_re, ssm_a_im, ssm_b_re, ssm_b_im, ssm_c_re, ssm_c_im, ssm_d, ssm_w_glu, ssm_b_glu, out_norm_nsa, out_norm_mla, out_norm_ssm, w_out, ffn_norm, ffn_w_up, ffn_conv_w, ffn_conv_b, ffn_w_down, final_norm):
    batch, seq, d_model = x.shape
    depth = w_in.shape[0]
    t = batch * seq
    n_half = seq // CMP_STRIDE
    n_sel = seq // SEL_BLOCK
    assert depth >= 1 and seq % ROW_TILE == 0 and n_sel <= MAX_SEL_BLOCKS and n_half % LANES == 0
    n_top = min(SEL_TOP, n_sel)
    row = lambda v: v.astype(F32).reshape(1, -1)

    cmul, smul = _rope_multipliers(positions)
    ovl_t = _overlap_t(n_half)

    xf = x.reshape(t, d_model)
    for l in range(depth):
        (q, kc, vc, ks, kw, cq, gate, ckv, kra, krb, u, vs_t, vw_t) = _inproj(
            xf, row(attn_norm[l]), *_inproj_weight(w_in[l]), batch, seq)

        kvcmp = _compress(
            kc, vc, batch,
            *_compress_weights(nsa_pe[l], nsa_ck_w1[l], nsa_ck_b1[l], nsa_ck_w2[l], nsa_ck_b2[l],
                               nsa_cv_w1[l], nsa_cv_b1[l], nsa_cv_w2[l], nsa_cv_b2[l]))
        vcmp_t = kvcmp[:, NSA_KV_HEADS:].transpose(0, 1, 3, 2).astype(BF16)
        o_cmp, q_aug, o_win = _nsa_local(q, kvcmp, vcmp_t, ovl_t, kw, vw_t, batch, seq, n_top)
        o_sel = _flash(q_aug, ks, vs_t, batch, seq, NSA_GROUP, "nsa_selected")

        wa, wb, wk, wvt = _mla_weights(mla_w_uq[l], mla_w_uk[l], mla_w_uv[l])
        q_m, k_m, vm_t = _mla_proj(cq, ckv, kra, krb, cmul, smul, row(mla_q_norm[l]),
                                   row(mla_kv_norm[l]), wa, wb, wk, wvt, batch, seq)
        o_mla = _flash(q_m, k_m, vm_t, batch, seq, 1, "mla_attention")

        bblk, a_rows, ccat, d_row = _ssm_weights(
            ssm_log_dt[l], ssm_a_re[l], ssm_a_im[l], ssm_b_re[l], ssm_b_im[l],
            ssm_c_re[l], ssm_c_im[l], ssm_d[l])
        o_ssm = _ssm(u.reshape(seq, batch, SSM_WIDTH), bblk, a_rows, ccat, d_row,
                     ssm_w_glu[l].astype(BF16), row(ssm_b_glu[l]))
        o_ssm = o_ssm.reshape(seq, batch * SSM_WIDTH)

        w_o = w_out[l].astype(BF16)
        n_a = NSA_HEADS * HEAD_DIM
        n_b = n_a + MLA_HEADS * HEAD_DIM
        xf = _outproj(
            o_cmp, o_sel, o_win, gate,
            jnp.pad(row(nsa_gate_b[l]), ((0, 0), (0, LANES - NSA_HEADS * N_BRANCH))),
            o_mla, o_ssm, xf,
            row(out_norm_nsa[l]), row(out_norm_mla[l]), row(out_norm_ssm[l]),
            w_o[:n_a], w_o[n_a:n_b], w_o[n_b:])

        w_up = ffn_w_up[l].astype(BF16)
        xf = _ffn(xf, row(ffn_norm[l]), w_up[:, :D_FF], w_up[:, D_FF:],
                  jnp.pad(ffn_conv_w[l].astype(F32), ((0, 8 - ffn_conv_w.shape[1]), (0, 0))),
                  row(ffn_conv_b[l]), ffn_w_down[l].astype(BF16), seq,
                  final_gain=row(final_norm) if l == depth - 1 else None)

    return xf.reshape(batch, seq, d_model)
```

```python
import functools
import math

import jax
import jax.numpy as jnp
from jax import lax
from jax.experimental import pallas as pl
from jax.experimental.pallas import tpu as pltpu

F32 = jnp.float32
BF16 = jnp.bfloat16

LANES = 128
SUBLANES = 8
HEAD_DIM = 64
NSA_HEADS = 6
NSA_KV_HEADS = 2
NSA_GROUP = NSA_HEADS // NSA_KV_HEADS
N_BRANCH = 3
CMP_BLOCK = 32
CMP_STRIDE = 16
CMP_HIDDEN = 128
SEL_BLOCK = 64
SEL_TOP = 16
MAX_SEL_BLOCKS = 64
WINDOW = 512
MLA_HEADS = 6
MLA_NOPE = 64
MLA_ROPE = 32
Q_LORA = 384
KV_LORA = 128
ROPE_THETA = 10000.0
SSM_WIDTH = 256
SSM_GROUPS = 16
SSM_GROUP_CH = 16
SSM_STATE = 64
SSM_LANES = SSM_GROUPS * SSM_STATE
D_FF = 2816
EPS = 1e-6
NEG = -1e30
LOG2E = math.log2(math.e)

ROW_TILE = 512
ATTN_TILE = 256
KV_CHUNK = 512
FLASH_ROWS = 1536
FLASH_CHAINS = 6
V_ROWS = 80
SSM_CHUNK = 64
SSM_SUB = 16
FF_CHUNK = 256
FF_DOWN_GROUP = 6
VMEM_LIMIT = 56 * 1024 * 1024


def _params(sem, vmem=None):
    return pltpu.CompilerParams(dimension_semantics=sem, vmem_limit_bytes=vmem)


def _rms(x, g):
    return x * lax.rsqrt(jnp.mean(x * x, axis=-1, keepdims=True) + EPS) * g


def _gelu(x):
    c = math.sqrt(2.0 / math.pi)
    return 0.5 * x * (1.0 + jnp.tanh(c * (x + 0.044715 * (x * x * x))))


def _sigmoid(x):
    return 1.0 / (1.0 + jnp.exp(-x))


def _mod_pow2(x, n):
    assert n & (n - 1) == 0
    return jnp.bitwise_and(x, n - 1)


def _div_pow2(x, n):
    assert n & (n - 1) == 0
    return jnp.right_shift(x, n.bit_length() - 1)


def _const_spec(shape):
    nd = len(shape)
    return pl.BlockSpec(shape, lambda *_: (0,) * nd)


def _trig_kernel(pos_ref, invf_ref, c_ref, s_ref, ns_ref):
    ang = pos_ref[...] * invf_ref[...]
    sin = jnp.sin(ang)
    c_ref[...] = jnp.cos(ang)
    s_ref[...] = sin
    ns_ref[...] = -sin


def _trig(pos_rows, invf_row):
    rows = pos_rows.shape[0]
    tile = min(ROW_TILE, rows)
    spec = pl.BlockSpec((tile, LANES), lambda i: (i, 0))
    return pl.pallas_call(
        _trig_kernel,
        grid=(rows // tile,),
        in_specs=[spec, _const_spec((1, LANES))],
        out_specs=[spec] * 3,
        out_shape=[jax.ShapeDtypeStruct((rows, LANES), F32)] * 3,
        compiler_params=_params(("parallel",)),
        name="rope_trig",
    )(pos_rows, invf_row)


_IN_SEGS = (
    ("q", NSA_HEADS * LANES, BF16),
    ("kc", NSA_KV_HEADS * HEAD_DIM, F32),
    ("vc", NSA_KV_HEADS * HEAD_DIM, F32),
    ("ks", NSA_KV_HEADS * LANES, BF16),
    ("kw", NSA_KV_HEADS * LANES, BF16),
    ("cq", Q_LORA, F32),
    ("gate", LANES, F32),
    ("ckv", KV_LORA, F32),
    ("kra", LANES, F32),
    ("krb", LANES, F32),
    ("u", SSM_WIDTH, F32),
)
_IN_GROUPS = ((0,), (1, 2), (3,), (4,), (5, 6), (7, 8), (9,), (10,))
_IN_COLS = sum(n for _, n, _ in _IN_SEGS)
_VT_ROWS = 2 * NSA_KV_HEADS * LANES


def _ones_row(shape):
    row = _mod_pow2(lax.broadcasted_iota(jnp.int32, shape, 0), LANES)
    return jnp.where(row == HEAD_DIM, 1.0, 0.0)


def _inproj_kernel(x_ref, g_ref, w_ref, wvt_ref, *o_refs, seq, tm):
    h = _rms(x_ref[...], g_ref[...]).astype(BF16)
    off = 0
    for group in _IN_GROUPS:
        width = sum(_IN_SEGS[s][1] for s in group)
        y_all = jnp.dot(h, w_ref[:, off:off + width], preferred_element_type=F32)
        off += width
        lo = 0
        for s in group:
            name, n, dt = _IN_SEGS[s]
            y = y_all[:, lo:lo + n]
            lo += n
            if name == "ks":
                s0 = lax.rem(pl.program_id(0) * tm, seq)
                blk = _div_pow2(s0 + lax.broadcasted_iota(jnp.int32, (tm, n), 0), SEL_BLOCK)
                lane = _mod_pow2(lax.broadcasted_iota(jnp.int32, (tm, n), 1), LANES)
                y = y + jnp.where(lane - HEAD_DIM == blk, 1.0, 0.0)
            o_refs[s][...] = y.astype(dt)
    vst_ref, vwt_ref = o_refs[len(_IN_SEGS):]
    v_t = lax.dot_general(wvt_ref[...], h, (((1,), (1,)), ((), ())), preferred_element_type=F32)
    v_t = (v_t + _ones_row(v_t.shape)).astype(BF16)
    wc = vwt_ref.shape[-1]
    for hh in range(NSA_KV_HEADS):
        vst_ref[0, hh, 0] = v_t[hh * LANES:(hh + 1) * LANES, :]
        r0 = (NSA_KV_HEADS + hh) * LANES
        for c in range(tm // wc):
            vwt_ref[0, hh, c] = v_t[r0:r0 + LANES, c * wc:(c + 1) * wc]


def _inproj(x, g, w, wvt, batch, seq):
    t, d = x.shape
    tm = ROW_TILE
    assert tm == min(KV_CHUNK, seq) and tm % ATTN_TILE == 0
    nt = seq // tm
    rows = lambda n: pl.BlockSpec((tm, n), lambda i: (i, 0))
    out_specs = [rows(n) for _, n, _ in _IN_SEGS]
    out_shape = [jax.ShapeDtypeStruct((t, n), dt) for _, n, dt in _IN_SEGS]
    out_specs[-1] = pl.BlockSpec((tm, SSM_WIDTH), lambda i: (i % nt, i // nt))
    out_shape[-1] = jax.ShapeDtypeStruct((seq, batch * SSM_WIDTH), F32)
    wpt = tm // ATTN_TILE
    out_specs += [pl.BlockSpec((1, NSA_KV_HEADS, 1, LANES, tm), lambda i: (i // nt, 0, i % nt, 0, 0)),
                  pl.BlockSpec((1, NSA_KV_HEADS, wpt, LANES, ATTN_TILE),
                               lambda i: (i // nt, 0, i % nt, 0, 0))]
    out_shape += [jax.ShapeDtypeStruct((batch, NSA_KV_HEADS, nt, LANES, tm), BF16),
                  jax.ShapeDtypeStruct((batch, NSA_KV_HEADS, seq // ATTN_TILE, LANES, ATTN_TILE), BF16)]
    return pl.pallas_call(
        functools.partial(_inproj_kernel, seq=seq, tm=tm),
        grid=(t // tm,),
        in_specs=[pl.BlockSpec((tm, d), lambda i: (i, 0)), _const_spec((1, d)),
                  _const_spec((d, _IN_COLS)), _const_spec((_VT_ROWS, d))],
        out_specs=out_specs,
        out_shape=out_shape,
        compiler_params=_params(("parallel",), VMEM_LIMIT),
        name="in_proj",
    )(x, g, w, wvt)


_CMP_STREAMS = 2 * NSA_KV_HEADS


def _compress_kernel(kc_ref, vc_ref, pe_ref, w1_ref, b1_ref, w2_ref, b2_ref, o_ref):
    n = kc_ref.shape[0] // CMP_STRIDE
    x = jnp.concatenate([r[pl.ds(l, n, stride=CMP_STRIDE), :]
                         for l in range(CMP_STRIDE) for r in (kc_ref, vc_ref)], axis=1)
    top = jnp.dot((x + pe_ref[0:1, :]).astype(BF16), w1_ref[0], preferred_element_type=F32)
    bot = jnp.dot((x + pe_ref[1:2, :]).astype(BF16), w1_ref[1], preferred_element_type=F32)
    hid = _gelu(top + pltpu.roll(bot, n - 1, 0) + b1_ref[...]).astype(BF16)
    for j in range(_CMP_STREAMS):
        kind = j // NSA_KV_HEADS
        o_ref[0, j] = (jnp.dot(hid[:, j * CMP_HIDDEN:(j + 1) * CMP_HIDDEN], w2_ref[kind],
                               preferred_element_type=F32) + b2_ref[kind])


def _compress(kc, vc, batch, pe, w1, b1, w2, b2):
    b = batch
    seq = kc.shape[0] // b
    n = seq // CMP_STRIDE
    width = CMP_STRIDE * (kc.shape[1] + vc.shape[1])
    hid = _CMP_STREAMS * CMP_HIDDEN
    xspec = pl.BlockSpec((seq, kc.shape[1]), lambda i: (i, 0))
    return pl.pallas_call(
        _compress_kernel,
        grid=(b,),
        in_specs=[xspec, xspec,
                  _const_spec((2, width)), _const_spec((2, width, hid)), _const_spec((1, hid)),
                  _const_spec((2, CMP_HIDDEN, LANES)), _const_spec((2, 1, LANES))],
        out_specs=pl.BlockSpec((1, _CMP_STREAMS, n, LANES), lambda i: (i, 0, 0, 0)),
        out_shape=jax.ShapeDtypeStruct((b, _CMP_STREAMS, n, LANES), F32),
        compiler_params=_params(("parallel",), VMEM_LIMIT),
        name="nsa_compress",
    )(kc, vc, pe, w1, b1, w2, b2)


def _compress_weights(pe, ck_w1, ck_b1, ck_w2, ck_b2, cv_w1, cv_b1, cv_w2, cv_b2):
    eye = jnp.eye(_CMP_STREAMS, dtype=F32)
    per_stream = jnp.stack([ck_w1, ck_w1, cv_w1, cv_w1]).astype(F32)

    def place(w):
        return jnp.einsum("jldf,jk->ljdkf", w, eye).reshape(
            CMP_STRIDE * _CMP_STREAMS * HEAD_DIM, _CMP_STREAMS * CMP_HIDDEN)

    w1 = jnp.stack([place(per_stream[:, :CMP_STRIDE]), place(per_stream[:, CMP_STRIDE:])])
    spread = lambda p: jnp.broadcast_to(p[:, None, :], (CMP_STRIDE, _CMP_STREAMS, HEAD_DIM)).reshape(-1)
    pe2 = jnp.stack([spread(pe[:CMP_STRIDE]), spread(pe[CMP_STRIDE:])]).astype(F32)
    b1 = jnp.concatenate([ck_b1, ck_b1, cv_b1, cv_b1]).astype(F32).reshape(1, -1)
    pad2 = lambda w: jnp.pad(w.astype(F32), ((0, 0), (0, LANES - HEAD_DIM)))
    w2 = jnp.stack([pad2(ck_w2), pad2(cv_w2)]).astype(BF16)
    b2 = jnp.stack([pad2(ck_b2.reshape(1, -1)), pad2(cv_b2.reshape(1, -1))])
    return pe2, w1.astype(BF16), b1, w2, b2


def _stack_group(q_ref, h, groups):
    return jnp.concatenate([q_ref[:, (h * groups + g) * LANES:(h * groups + g + 1) * LANES]
                            for g in range(groups)], axis=0)


def _store_heads_t(o_ref, chains_t, tq):
    heads = [c[:, g * tq:(g + 1) * tq] for c in chains_t for g in range(c.shape[1] // tq)]
    for p in range(len(heads) // 2):
        pair = jnp.concatenate([heads[2 * p], heads[2 * p + 1]], axis=0)
        o_ref[:, p * LANES:(p + 1) * LANES] = pair.T.astype(o_ref.dtype)


def _stable_rank(val):
    sub = SUBLANES
    n_blk = val.shape[0] // sub
    blocks = [val[r * sub:(r + 1) * sub] for r in range(n_blk)]
    ranks = [jnp.zeros(blocks[0].shape, F32) for _ in range(n_blk)]
    row_in = lax.broadcasted_iota(jnp.int32, blocks[0].shape, 0)
    for i in range(val.shape[0]):
        row = val[i:i + 1, :]
        for r in range(n_blk):
            if r * sub > i:
                ahead = jnp.where(row >= blocks[r], 1.0, 0.0)
            elif (r + 1) * sub <= i:
                ahead = jnp.where(row > blocks[r], 1.0, 0.0)
            else:
                ahead = jnp.where(row_in + r * sub > i, jnp.where(row >= blocks[r], 1.0, 0.0),
                                  jnp.where(row > blocks[r], 1.0, 0.0))
            ranks[r] = ranks[r] + ahead
    return jnp.concatenate(ranks, axis=0)


def _cmp_scores(kc_ref, qs):
    nt = (((1,), (1,)), ((), ()))
    return [lax.dot_general(kc_ref[0, h].astype(BF16), qs[h], nt, preferred_element_type=F32)
            for h in range(NSA_KV_HEADS)]


def _cmp_finish(ss, q0, q_ref, vct_ref, ovl_ref, o_ref, qa_ref, *, tq, n_top):
    m_cols = NSA_GROUP * tq
    nc = ss[0].shape[0]
    t_col = q0 + _mod_pow2(lax.broadcasted_iota(jnp.int32, (nc, m_cols), 1), tq)
    n_row = lax.broadcasted_iota(jnp.int32, (nc, m_cols), 0)
    mask = n_row * CMP_STRIDE + (CMP_BLOCK - 1) <= t_col
    has_valid = t_col[0:1, :] >= CMP_BLOCK - 1
    j_idx =lax.broadcasted_iota(jnp.int32, (MAX_SEL_BLOCKS, tq), 0)
    cur = _div_pow2(q0 + lax.broadcasted_iota(jnp.int32, (MAX_SEL_BLOCKS, tq), 1), SEL_BLOCK)
    forced = (j_idx == 0) | (j_idx == cur) | (j_idx == cur - 1)
    outs = []
    for h in range(NSA_KV_HEADS):
        sm = jnp.where(mask, ss[h], NEG)
        e = jnp.exp2(sm - jnp.max(sm, axis=0, keepdims=True))
        scale = jnp.where(has_valid, 1.0 / jnp.sum(e, axis=0, keepdims=True), 0.0)
        p = e * scale
        outs.append(jnp.dot(vct_ref[0, h, 0:HEAD_DIM, :], p.astype(BF16),
                            preferred_element_type=F32))

        psum = p[:, 0:tq] + p[:, tq:2 * tq] + p[:, 2 * tq:3 * tq]
        hi = psum.astype(BF16)
        lo = (psum - hi.astype(F32)).astype(BF16)
        imp = (jnp.dot(ovl_ref[...], hi, preferred_element_type=F32)
               + jnp.dot(ovl_ref[...], lo, preferred_element_type=F32))
        val = jnp.where(forced, jnp.inf, jnp.where(j_idx > cur, -jnp.inf, imp))
        bias_t = jnp.where(_stable_rank(val) < n_top, 0.0, NEG)
        bias = jnp.concatenate([jnp.zeros_like(bias_t), bias_t], axis=0).T
        for g in range(NSA_GROUP):
            lane0 = (h * NSA_GROUP + g) * LANES
            qa_ref[:, lane0:lane0 + LANES] = (q_ref[:, lane0:lane0 + LANES].astype(F32)
                                              + bias).astype(qa_ref.dtype)
    _store_heads_t(o_ref, outs, tq)


def _flash_kernel(q_ref, k_ref, vt_ref, o_ref, s_a, s_b, *, chains, groups, tq, ck):
    q0 = pl.program_id(2) * tq
    m_cols = groups * tq
    qs = [_stack_group(q_ref, h, groups) for h in range(chains)]

    def issue_scores(c, s_ref):
        k0 = pl.multiple_of(c * ck, ck)
        for h in range(chains):
            s_ref[h] = lax.dot_general(k_ref[pl.ds(k0, ck), h * LANES:(h + 1) * LANES], qs[h],
                                       (((1,), (1,)), ((), ())), preferred_element_type=F32)

    def absorb(c, s_ref, stats, visible=None):
        out = []
        for h in range(chains):
            s = s_ref[h]
            if visible is not None:
                s = jnp.where(visible, s, NEG)
            m_old, acc = stats[h]
            m_new = jnp.maximum(m_old, jnp.max(s, axis=0, keepdims=True))
            alpha = jnp.exp2(m_old - m_new)
            p = jnp.exp2(s - m_new).astype(BF16)
            vt = vt_ref[0, h, c, 0:V_ROWS, :]
            out.append((m_new, alpha * acc + jnp.dot(vt, p, preferred_element_type=F32)))
        return tuple(out)

    def finish(stats):
        _store_heads_t(o_ref, [acc[0:HEAD_DIM] / acc[HEAD_DIM:HEAD_DIM + 1] for _, acc in stats], tq)

    last = (q0 + tq - 1) // ck
    key_pos = last * ck + lax.broadcasted_iota(jnp.int32, (ck, m_cols), 0)
    q_pos = q0 + _mod_pow2(lax.broadcasted_iota(jnp.int32, (ck, m_cols), 1), tq)
    visible = key_pos <= q_pos

    def pair(j, stats):
        c = 2 * j
        issue_scores(c + 1, s_b)
        stats = absorb(c, s_a, stats)
        issue_scores(c + 2, s_a)
        return absorb(c + 1, s_b, stats)

    issue_scores(0, s_a)
    stats = tuple((jnp.full((1, m_cols), NEG, F32), jnp.zeros((V_ROWS, m_cols), F32))
                  for _ in range(chains))
    stats = lax.fori_loop(0, last // 2, pair, stats)

    @pl.when(lax.rem(last, 2) == 0)
    def _():
        finish(absorb(last, s_a, stats, visible))

    @pl.when(lax.rem(last, 2) == 1)
    def _():
        issue_scores(last, s_b)
        finish(absorb(last, s_b, absorb(last - 1, s_a, stats), visible))


def _flash(q, k, v_t, batch, seq, groups, name):
    t, qcols = q.shape
    kv_heads = k.shape[1] // LANES
    ck = min(KV_CHUNK, seq)
    tq = min(FLASH_ROWS // groups, ck)
    assert tq & (tq - 1) == 0 and ck % tq == 0 and seq % ck == 0
    nq = seq // tq
    n_chunks = seq // ck
    assert v_t.shape == (batch, kv_heads, n_chunks, LANES, ck)
    chains = max(c for c in range(1, FLASH_CHAINS + 1)
                 if kv_heads % c == 0 and (c * groups) % 2 == 0)
    heads = kv_heads * groups
    return pl.pallas_call(
        functools.partial(_flash_kernel, chains=chains, groups=groups, tq=tq, ck=ck),
        grid=(batch, kv_heads // chains, nq),
        in_specs=[pl.BlockSpec((tq, chains * groups * LANES), lambda b, h, i: (b * nq + i, h)),
                  pl.BlockSpec((seq, chains * LANES), lambda b, h, i: (b, h)),
                  pl.BlockSpec((1, chains, n_chunks, LANES, ck), lambda b, h, i: (b, h, 0, 0, 0))],
        out_specs=pl.BlockSpec((tq, chains * groups * HEAD_DIM), lambda b, h, i: (b * nq + i, h)),
        out_shape=jax.ShapeDtypeStruct((t, heads * HEAD_DIM), BF16),
        scratch_shapes=[pltpu.VMEM((chains, ck, groups * tq), F32)] * 2,
        compiler_params=_params(("parallel", "parallel", "parallel"), VMEM_LIMIT),
        name=name,
    )(q, k, v_t)


def _nsa_local_kernel(q_ref, kc_ref, vct_ref, ovl_ref, k_ref, vt_ref, oc_ref, qa_ref, ow_ref,
                      *, tq, n_top):
    i = pl.program_id(1)
    q0 = i * tq
    n_back = WINDOW // tq
    n_span = n_back + 1
    m_cols = NSA_GROUP * tq
    qs = [_stack_group(q_ref, h, NSA_GROUP) for h in range(NSA_KV_HEADS)]
    q_pos = q0 + _mod_pow2(lax.broadcasted_iota(jnp.int32, (tq, m_cols), 1), tq)
    key_in = lax.broadcasted_iota(jnp.int32, (tq, m_cols), 0)
    nt = (((1,), (1,)), ((), ()))

    def run(c0, clamped):
        start = c0 * tq if clamped else pl.multiple_of(c0 * tq, tq)
        ss = [lax.dot_general(k_ref[pl.ds(start, n_span * tq), h * LANES:(h + 1) * LANES], qs[h],
                              nt, preferred_element_type=F32) for h in range(NSA_KV_HEADS)]
        ss_cmp = _cmp_scores(kc_ref, qs)
        outs = []
        for h in range(NSA_KV_HEADS):
            blocks = []
            for j in range(n_span):
                blk = ss[h][j * tq:(j + 1) * tq]
                key_pos = start + j * tq + key_in
                if clamped:
                    blk = jnp.where((key_pos <= q_pos) & (key_pos > q_pos - WINDOW), blk, NEG)
                elif j == 0:
                    blk = jnp.where(key_pos > q_pos - WINDOW, blk, NEG)
                elif j == n_span - 1:
                    blk = jnp.where(key_pos <= q_pos, blk, NEG)
                blocks.append(blk)
            m = functools.reduce(jnp.maximum, [jnp.max(b, axis=0, keepdims=True) for b in blocks])
            acc = 0.0
            for j in range(n_span):
                p = jnp.exp2(blocks[j] - m).astype(BF16)
                acc = acc + jnp.dot(vt_ref[0, h, c0 + j, 0:V_ROWS, :], p, preferred_element_type=F32)
            outs.append(acc[0:HEAD_DIM] / acc[HEAD_DIM:HEAD_DIM + 1])
        _store_heads_t(ow_ref, outs, tq)
        _cmp_finish(ss_cmp, q0, q_ref, vct_ref, ovl_ref, oc_ref, qa_ref, tq=tq, n_top=n_top)

    @pl.when(i < n_back)
    def _():
        run(0, True)

    @pl.when(i >= n_back)
    def _():
        run(i - n_back, False)


def _nsa_local(q, kcmp, vcmp_t, ovl, k, v_t, batch, seq, n_top):
    t = q.shape[0]
    tq = ATTN_TILE
    nq = seq // tq
    nc = kcmp.shape[2]
    assert WINDOW % tq == 0 and seq >= WINDOW + tq
    wide = NSA_HEADS * LANES
    qspec = pl.BlockSpec((tq, wide), lambda b, i: (b * nq + i, 0))
    ospec = pl.BlockSpec((tq, NSA_HEADS * HEAD_DIM), lambda b, i: (b * nq + i, 0))
    o_sds = jax.ShapeDtypeStruct((t, NSA_HEADS * HEAD_DIM), BF16)
    return pl.pallas_call(
        functools.partial(_nsa_local_kernel, tq=tq, n_top=n_top),
        grid=(batch, nq),
        in_specs=[qspec,
                  pl.BlockSpec((1, NSA_KV_HEADS, nc, LANES), lambda b, i: (b, 0, 0, 0)),
                  pl.BlockSpec((1, NSA_KV_HEADS, LANES, nc), lambda b, i: (b, 0, 0, 0)),
                  _const_spec((MAX_SEL_BLOCKS, nc)),
                  pl.BlockSpec((seq, NSA_KV_HEADS * LANES), lambda b, i: (b, 0)),
                  pl.BlockSpec((1, NSA_KV_HEADS, nq, LANES, tq), lambda b, i: (b, 0, 0, 0, 0))],
        out_specs=[ospec, qspec, ospec],
        out_shape=[o_sds, jax.ShapeDtypeStruct((t, wide), BF16), o_sds],
        compiler_params=_params(("parallel", "parallel"), VMEM_LIMIT),
        name="nsa_local",
    )(q, kcmp, vcmp_t, ovl, k, v_t)


def _mla_proj_kernel(cq_ref, ckv_ref, kra_ref, krb_ref, cm_ref, sm_ref, qg_ref, kg_ref,
                     wa_ref, wb_ref, wk_ref, wvt_ref, q_ref, k_ref, vt_ref, *, scale):
    qn = _rms(cq_ref[...], qg_ref[...]).astype(BF16)
    cn = _rms(ckv_ref[...], kg_ref[...]).astype(BF16)
    cm = cm_ref[...]
    sm = sm_ref[...]
    k_rot = kra_ref[...] * cm + krb_ref[...] * sm
    cm2 = jnp.concatenate([cm, cm], axis=1)
    sm2 = jnp.concatenate([sm, sm], axis=1)
    k_rot2 = jnp.concatenate([k_rot, k_rot], axis=1)
    for h in range(0, MLA_HEADS, 2):
        sl = slice(h * LANES, (h + 2) * LANES)
        qa = jnp.dot(qn, wa_ref[:, sl], preferred_element_type=F32)
        qb = jnp.dot(qn, wb_ref[:, sl], preferred_element_type=F32)
        q_ref[:, sl] = ((qa * cm2 + qb * sm2) * scale).astype(q_ref.dtype)
        k_ref[:, sl] = (jnp.dot(cn, wk_ref[:, sl], preferred_element_type=F32) + k_rot2).astype(k_ref.dtype)
    v_t = lax.dot_general(wvt_ref[...], cn, (((1,), (1,)), ((), ())), preferred_element_type=F32)
    v_t = (v_t + _ones_row(v_t.shape)).astype(vt_ref.dtype)
    for h in range(MLA_HEADS):
        vt_ref[0, h, 0] = v_t[h * LANES:(h + 1) * LANES, :]


def _mla_proj(cq, ckv, kra, krb, cmul, smul, qg, kg, wa, wb, wk, wvt, batch, seq):
    t = cq.shape[0]
    tm = ROW_TILE
    assert tm == min(KV_CHUNK, seq)
    nt = seq // tm
    width = MLA_HEADS * LANES
    rows = lambda n: pl.BlockSpec((tm, n), lambda i: (i, 0))
    scale = (MLA_NOPE + MLA_ROPE) ** -0.5 * LOG2E
    return pl.pallas_call(
        functools.partial(_mla_proj_kernel, scale=scale),
        grid=(t // tm,),
        in_specs=[rows(Q_LORA), rows(KV_LORA), rows(LANES), rows(LANES), rows(LANES), rows(LANES),
                  _const_spec((1, Q_LORA)), _const_spec((1, KV_LORA)),
                  _const_spec((Q_LORA, width)), _const_spec((Q_LORA, width)),
                  _const_spec((KV_LORA, width)), _const_spec((width, KV_LORA))],
        out_specs=[rows(width), rows(width),
                   pl.BlockSpec((1, MLA_HEADS, 1, LANES, tm), lambda i: (i // nt, 0, i % nt, 0, 0))],
        out_shape=[jax.ShapeDtypeStruct((t, width), BF16)] * 2
        + [jax.ShapeDtypeStruct((batch, MLA_HEADS, nt, LANES, tm), BF16)],
        compiler_params=_params(("parallel",)),
        name="mla_proj",
    )(cq, ckv, kra, krb, cmul, smul, qg, kg, wa, wb, wk, wvt)


def _ssm_kernel(u_ref, bblk_ref, a_ref, ccat_ref, d_ref, wglu_ref, bglu_ref, o_ref,
                h_sc, st_sc, *, tc, sub, nb):
    @pl.when(pl.program_id(0) == 0)
    def _():
        st_sc[...] = jnp.zeros_like(st_sc)

    n = SSM_LANES
    ar = jnp.broadcast_to(a_ref[0:1, :], (nb, n))
    ai = jnp.broadcast_to(a_ref[1:2, :], (nb, n))
    n_sub = tc // sub
    rows = sub * nb
    us = []
    for j in range(n_sub):
        u = u_ref[j * sub:(j + 1) * sub].reshape(rows, SSM_WIDTH)
        us.append(u)
        h_sc[j * rows:(j + 1) * rows, :] = jnp.dot(u.astype(BF16), bblk_ref[...],
                                                   preferred_element_type=F32)
    hr, hi = st_sc[0], st_sc[1]
    ys = []
    for j in range(n_sub):
        for t in range(j * sub, (j + 1) * sub):
            r0 = t * nb
            nr = ar * hr - ai * hi + h_sc[r0:r0 + nb, 0:n]
            ni = ar * hi + ai * hr + h_sc[r0:r0 + nb, n:2 * n]
            h_sc[r0:r0 + nb, 0:n] = nr
            h_sc[r0:r0 + nb, n:2 * n] = ni
            hr, hi = nr, ni
        ys.append(jnp.dot(h_sc[j * rows:(j + 1) * rows, :].astype(BF16), ccat_ref[...],
                          preferred_element_type=F32) + d_ref[...] * us[j])
    st_sc[0] = hr
    st_sc[1] = hi
    z = _gelu(jnp.concatenate(ys, axis=0))
    gate = jnp.dot(z.astype(BF16), wglu_ref[...], preferred_element_type=F32) + bglu_ref[...]
    o_ref[...] = (z * _sigmoid(gate)).reshape(tc, nb, SSM_WIDTH)


def _ssm(u_t, bblk, a_rows, ccat, d_row, wglu, bglu):
    seq, nb, _ = u_t.shape
    tc = SSM_CHUNK
    n = SSM_LANES
    uspec = pl.BlockSpec((tc, nb, SSM_WIDTH), lambda i: (i, 0, 0))
    return pl.pallas_call(
        functools.partial(_ssm_kernel, tc=tc, sub=SSM_SUB, nb=nb),
        grid=(seq // tc,),
        in_specs=[uspec, _const_spec((SSM_WIDTH, 2 * n)), _const_spec((2, n)),
                  _const_spec((2 * n, SSM_WIDTH)), _const_spec((1, SSM_WIDTH)),
                  _const_spec((SSM_WIDTH, SSM_WIDTH)), _const_spec((1, SSM_WIDTH))],
        out_specs=uspec,
        out_shape=jax.ShapeDtypeStruct(u_t.shape, F32),
        scratch_shapes=[pltpu.VMEM((tc * nb, 2 * n), F32), pltpu.VMEM((2, nb, n), F32)],
        compiler_params=_params(("arbitrary",), VMEM_LIMIT),
        name="s5_scan",
    )(u_t, bblk, a_rows, ccat, d_row, wglu, bglu)


def _outproj_kernel(oc_ref, os_ref, ow_ref, gate_ref, gb_ref, ex_ref, om_ref, oz_ref, x_ref,
                    gn_ref, gm_ref, gz_ref, wn_ref, wm_ref, wz_ref, o_ref):
    g = _sigmoid(gate_ref[...] + gb_ref[...])
    g_hi = g.astype(BF16)
    g_lo = (g - g_hi.astype(F32)).astype(BF16)
    spread = jnp.dot(jnp.concatenate([g_hi, g_lo], axis=1), ex_ref[...],
                     preferred_element_type=F32)
    o_a = 0.0
    wide = oc_ref.shape[1]
    for r, br_ref in enumerate((oc_ref, os_ref, ow_ref)):
        o_a = o_a + spread[:, r * wide:(r + 1) * wide] * br_ref[...].astype(F32)
    inv = lax.rsqrt(jnp.sum(o_a * o_a, axis=-1, keepdims=True) * (1.0 / (NSA_HEADS * HEAD_DIM)) + EPS)
    acc = x_ref[...] + jnp.dot((o_a * inv * gn_ref[...]).astype(BF16), wn_ref[...],
                               preferred_element_type=F32)
    om = om_ref[...].astype(F32)
    inv = lax.rsqrt(jnp.sum(om * om, axis=-1, keepdims=True) * (1.0 / (MLA_HEADS * HEAD_DIM)) + EPS)
    acc = acc + jnp.dot((om * inv * gm_ref[...]).astype(BF16), wm_ref[...], preferred_element_type=F32)
    acc = acc + jnp.dot(_rms(oz_ref[...], gz_ref[...]).astype(BF16), wz_ref[...],
                        preferred_element_type=F32)
    o_ref[...] = acc


def _gate_spread():
    col = jnp.arange(LANES)[:, None]
    out = jnp.arange(N_BRANCH * NSA_HEADS * HEAD_DIM)[None, :]
    lane, r = out % (NSA_HEADS * HEAD_DIM), out // (NSA_HEADS * HEAD_DIM)
    hit = (col == N_BRANCH * (lane // HEAD_DIM) + r).astype(BF16)
    return jnp.concatenate([hit, hit], axis=0)


def _outproj(oc, osel, ow, gate, gate_b, om, oz, x, gn, gm, gz, wn, wm, wz):
    t, d = x.shape
    tm = ROW_TILE
    nt = oz.shape[0] // tm
    wide = NSA_HEADS * HEAD_DIM
    assert wide == MLA_HEADS * HEAD_DIM
    rows = lambda n: pl.BlockSpec((tm, n), lambda i: (i, 0))
    return pl.pallas_call(
        _outproj_kernel,
        grid=(t // tm,),
        in_specs=[rows(wide), rows(wide), rows(wide), rows(LANES), _const_spec((1, LANES)),
                  _const_spec((2 * LANES, N_BRANCH * wide)), rows(wide),
                  pl.BlockSpec((tm, SSM_WIDTH), lambda i: (i % nt, i // nt)), rows(d),
                  _const_spec((1, wide)), _const_spec((1, wide)), _const_spec((1, SSM_WIDTH)),
                  _const_spec((wide, d)), _const_spec((wide, d)), _const_spec((SSM_WIDTH, d))],
        out_specs=rows(d),
        out_shape=jax.ShapeDtypeStruct((t, d), F32),
        compiler_params=_params(("parallel",), VMEM_LIMIT),
        name="out_proj",
    )(oc, osel, ow, gate, gate_b, _gate_spread(), om, oz, x, gn, gm, gz, wn, wm, wz)


def _ffn_kernel(x_ref, g_ref, wg_ref, wv_ref, cw_ref, cb_ref, wd_ref, *rest,
                tm, tiles_per_seq, cf, down_group):
    fg_ref = rest[0] if len(rest) == 4 else None
    o_ref, carry_sc, act_sc = rest[-3:]

    @pl.when(lax.rem(pl.program_id(0), tiles_per_seq) == 0)
    def _():
        carry_sc[...] = jnp.zeros_like(carry_sc)

    x = x_ref[...]
    h = _rms(x, g_ref[...]).astype(BF16)
    row = lax.broadcasted_iota(jnp.int32, (tm, cf), 0)
    n_chunks = D_FF // cf
    acc = x

    def up(c):
        sl = slice(c * cf, (c + 1) * cf)
        return (jnp.dot(h, wg_ref[:, sl], preferred_element_type=F32),
                jnp.dot(h, wv_ref[:, sl], preferred_element_type=F32))

    nxt = up(0)
    for c in range(n_chunks):
        sl = slice(c * cf, (c + 1) * cf)
        gate, val = nxt
        if c + 1 < n_chunks:
            nxt = up(c + 1)
        tail = carry_sc[:, sl]
        p1 = tail[SUBLANES - 1:SUBLANES, :]
        p2 = tail[SUBLANES - 2:SUBLANES - 1, :]
        g1 = jnp.where(row == 0, p1, pltpu.roll(gate, 1, 0))
        g2 = jnp.where(row == 0, p2, jnp.where(row == 1, p1, pltpu.roll(gate, 2, 0)))
        carry_sc[:, sl] = gate[tm - SUBLANES:tm, :]
        gc = cw_ref[0:1, sl] * g2 + cw_ref[1:2, sl] * g1 + cw_ref[2:3, sl] * gate + cb_ref[:, sl]
        act_sc[:, sl] = (gc * _sigmoid(gc) * val).astype(BF16)
        if (c + 1) % down_group == 0 or c + 1 == n_chunks:
            lo = (c // down_group) * down_group * cf
            acc = acc + jnp.dot(act_sc[:, lo:(c + 1) * cf], wd_ref[lo:(c + 1) * cf, :],
                                preferred_element_type=F32)
    o_ref[...] = acc if fg_ref is None else _rms(acc, fg_ref[...])


def _ffn(x, g, wg, wv, cw, cb, wd, seq, final_gain=None):
    t, d = x.shape
    tm = ROW_TILE
    rows = pl.BlockSpec((tm, d), lambda i: (i, 0))
    extra = [] if final_gain is None else [final_gain]
    return pl.pallas_call(
        functools.partial(_ffn_kernel, tm=tm, tiles_per_seq=seq // tm, cf=FF_CHUNK,
                          down_group=FF_DOWN_GROUP),
        grid=(t // tm,),
        in_specs=[rows, _const_spec((1, d)), _const_spec((d, D_FF)), _const_spec((d, D_FF)),
                  _const_spec((SUBLANES, D_FF)), _const_spec((1, D_FF)), _const_spec((D_FF, d))]
        + [_const_spec((1, d))] * len(extra),
        out_specs=rows,
        out_shape=jax.ShapeDtypeStruct((t, d), F32),
        scratch_shapes=[pltpu.VMEM((SUBLANES, D_FF), F32), pltpu.VMEM((tm, D_FF), BF16)],
        compiler_params=_params(("arbitrary",), VMEM_LIMIT),
        name="conv_ffn",
    )(x, g, wg, wv, cw, cb, wd, *extra)


def _pad_heads(w, heads, width):
    lead = w.shape[:-1]
    w = w.reshape(lead + (heads, width))
    w = jnp.pad(w, [(0, 0)] * len(lead) + [(0, 0), (0, LANES - width)])
    return w.reshape(lead + (heads * LANES,))


def _inproj_weight(w):
    kv = NSA_KV_HEADS * HEAD_DIM
    sizes = (NSA_HEADS * HEAD_DIM,) + (kv,) * 6 + (NSA_HEADS * N_BRANCH, Q_LORA, KV_LORA, MLA_ROPE,
                                                   SSM_WIDTH)
    offs = [0]
    for n in sizes:
        offs.append(offs[-1] + n)
    (w_q, w_kc, w_vc, w_ks, w_vs, w_kw, w_vw, w_g, w_cq, w_ckv, w_kr, w_u) = [
        w[:, a:b] for a, b in zip(offs[:-1], offs[1:])]
    d = w.shape[0]
    half = MLA_ROPE // 2
    z64 = jnp.zeros((d, HEAD_DIM), w.dtype)
    z32 = jnp.zeros((d, LANES - HEAD_DIM - MLA_ROPE), w.dtype)
    r1, r2 = w_kr[:, :half], w_kr[:, half:]
    cols = [
        _pad_heads(w_q * (HEAD_DIM ** -0.5 * LOG2E), NSA_HEADS, HEAD_DIM),
        w_kc, w_vc,
        _pad_heads(w_ks, NSA_KV_HEADS, HEAD_DIM),
        _pad_heads(w_kw, NSA_KV_HEADS, HEAD_DIM),
        w_cq,
        jnp.pad(w_g, ((0, 0), (0, LANES - w_g.shape[1]))),
        w_ckv,
        jnp.concatenate([z64, r1, r2, z32], axis=1),
        jnp.concatenate([z64, r2, r1, z32], axis=1),
        w_u,
    ]
    w_vt = jnp.concatenate([_pad_heads(w_vs, NSA_KV_HEADS, HEAD_DIM),
                            _pad_heads(w_vw, NSA_KV_HEADS, HEAD_DIM)], axis=1).T
    return jnp.concatenate(cols, axis=1).astype(BF16), w_vt.astype(BF16)


def _mla_weights(w_uq, w_uk, w_uv):
    half = MLA_ROPE // 2
    w = w_uq.reshape(Q_LORA, MLA_HEADS, MLA_NOPE + MLA_ROPE)
    nope, r1, r2 = w[..., :MLA_NOPE], w[..., MLA_NOPE:MLA_NOPE + half], w[..., MLA_NOPE + half:]
    z32 = jnp.zeros((Q_LORA, MLA_HEADS, LANES - MLA_NOPE - MLA_ROPE), w.dtype)
    wa = jnp.concatenate([nope, r1, r2, z32], axis=-1).reshape(Q_LORA, MLA_HEADS * LANES)
    wb = jnp.concatenate([jnp.zeros_like(nope), r2, r1, z32], axis=-1).reshape(Q_LORA, MLA_HEADS * LANES)
    wk = _pad_heads(w_uk, MLA_HEADS, MLA_NOPE)
    wvt = _pad_heads(w_uv, MLA_HEADS, HEAD_DIM).T
    return wa.astype(BF16), wb.astype(BF16), wk.astype(BF16), wvt.astype(BF16)


def _ssm_weights(log_dt, a_re, a_im, b_re, b_im, c_re, c_im, d):
    dt = jnp.exp(log_dt.astype(F32))[:, None]
    ar, ai = a_re.astype(F32), a_im.astype(F32)
    mag = jnp.exp(ar * dt)
    abr, abi = mag * jnp.cos(ai * dt), mag * jnp.sin(ai * dt)
    den = ar * ar + ai * ai
    fr = ((abr - 1.0) * ar + abi * ai) / den
    fi = (abi * ar - (abr - 1.0) * ai) / den
    br, bi = b_re.astype(F32), b_im.astype(F32)
    bbr = fr[..., None] * br - fi[..., None] * bi
    bbi = fr[..., None] * bi + fi[..., None] * br
    eye = jnp.eye(SSM_GROUPS, dtype=F32)
    blk_in = lambda m: jnp.einsum("gpc,gh->gchp", m, eye).reshape(SSM_WIDTH, SSM_LANES)
    blk_out = lambda m: jnp.einsum("gcp,gh->gphc", m, eye).reshape(SSM_LANES, SSM_WIDTH)
    bblk = jnp.concatenate([blk_in(bbr), blk_in(bbi)], axis=1)
    ccat = jnp.concatenate([blk_out(c_re.astype(F32)), -blk_out(c_im.astype(F32))], axis=0)
    a_rows = jnp.stack([abr.reshape(SSM_LANES), abi.reshape(SSM_LANES)])
    return bblk.astype(BF16), a_rows, ccat.astype(BF16), d.astype(F32).reshape(1, SSM_WIDTH)


def _overlap_t(nc_pad):
    start = jnp.arange(nc_pad) * CMP_STRIDE
    lo = jnp.arange(MAX_SEL_BLOCKS) * SEL_BLOCK
    hit = (start[None, :] < lo[:, None] + SEL_BLOCK) & (start[None, :] + CMP_BLOCK > lo[:, None])
    return hit.astype(BF16)


def _rope_multipliers(positions):
    half = MLA_ROPE // 2
    per_row = LANES // half
    t = positions.size
    inv_freq = ROPE_THETA ** (-jnp.arange(half, dtype=F32) / half)
    pos = jnp.broadcast_to(positions.astype(F32).reshape(t // per_row, per_row, 1),
                           (t // per_row, per_row, half)).reshape(t // per_row, LANES)
    cos, sin, nsin = [a.reshape(t, half)
                      for a in _trig(pos, jnp.tile(inv_freq, per_row).reshape(1, LANES))]
    one = jnp.ones((t, HEAD_DIM), F32)
    zero = jnp.zeros((t, HEAD_DIM), F32)
    pad = jnp.zeros((t, LANES - HEAD_DIM - MLA_ROPE), F32)
    return (jnp.concatenate([one, cos, cos, pad], axis=1),
            jnp.concatenate([zero, nsin, sin, pad], axis=1))


def kernel(x, positions, attn_norm, w_in, nsa_pe, nsa_ck_w1, nsa_ck_b1, nsa_ck_w2, nsa_ck_b2, nsa_cv_w1, nsa_cv_b1, nsa_cv_w2, nsa_cv_b2, nsa_gate_b, mla_q_norm, mla_kv_norm, mla_w_uq, mla_w_uk, mla_w_uv, ssm_log_dt, ssm_a_re, ssm_a_im, ssm_b_re, ssm_b_im, ssm_c_re, ssm_c_im, ssm_d, ssm_w_glu, ssm_b_glu, out_norm_nsa, out_norm_mla, out_norm_ssm, w_out, ffn_norm, ffn_w_up, ffn_conv_w, ffn_conv_b, ffn_w_down, final_norm):
    batch, seq, d_model = x.shape
    depth = w_in.shape[0]
    t = batch * seq
    n_half = seq // CMP_STRIDE
    n_sel = seq // SEL_BLOCK
    assert depth >= 1 and seq % ROW_TILE == 0 and n_sel <= MAX_SEL_BLOCKS and n_half % LANES == 0
    n_top = min(SEL_TOP, n_sel)
    row = lambda v: v.astype(F32).reshape(1, -1)

    cmul, smul = _rope_multipliers(positions)
    ovl_t = _overlap_t(n_half)

    xf = x.reshape(t, d_model)
    for l in range(depth):
        (q, kc, vc, ks, kw, cq, gate, ckv, kra, krb, u, vs_t, vw_t) = _inproj(
            xf, row(attn_norm[l]), *_inproj_weight(w_in[l]), batch, seq)

        kvcmp = _compress(
            kc, vc, batch,
            *_compress_weights(nsa_pe[l], nsa_ck_w1[l], nsa_ck_b1[l], nsa_ck_w2[l], nsa_ck_b2[l],
                               nsa_cv_w1[l], nsa_cv_b1[l], nsa_cv_w2[l], nsa_cv_b2[l]))
        vcmp_t = kvcmp[:, NSA_KV_HEADS:].transpose(0, 1, 3, 2).astype(BF16)
        o_cmp, q_aug, o_win = _nsa_local(q, kvcmp, vcmp_t, ovl_t, kw, vw_t, batch, seq, n_top)
        o_sel = _flash(q_aug, ks, vs_t, batch, seq, NSA_GROUP, "nsa_selected")

        wa, wb, wk, wvt = _mla_weights(mla_w_uq[l], mla_w_uk[l], mla_w_uv[l])
        q_m, k_m, vm_t = _mla_proj(cq, ckv, kra, krb, cmul, smul, row(mla_q_norm[l]),
                                   row(mla_kv_norm[l]), wa, wb, wk, wvt, batch, seq)
        o_mla = _flash(q_m, k_m, vm_t, batch, seq, 1, "mla_attention")

        bblk, a_rows, ccat, d_row = _ssm_weights(
            ssm_log_dt[l], ssm_a_re[l], ssm_a_im[l], ssm_b_re[l], ssm_b_im[l],
            ssm_c_re[l], ssm_c_im[l], ssm_d[l])
        o_ssm = _ssm(u.reshape(seq, batch, SSM_WIDTH), bblk, a_rows, ccat, d_row,
                     ssm_w_glu[l].astype(BF16), row(ssm_b_glu[l]))
        o_ssm = o_ssm.reshape(seq, batch * SSM_WIDTH)

        w_o = w_out[l].astype(BF16)
        n_a = NSA_HEADS * HEAD_DIM
        n_b = n_a + MLA_HEADS * HEAD_DIM
        xf = _outproj(
            o_cmp, o_sel, o_win, gate,
            jnp.pad(row(nsa_gate_b[l]), ((0, 0), (0, LANES - NSA_HEADS * N_BRANCH))),
            o_mla, o_ssm, xf,
            row(out_norm_nsa[l]), row(out_norm_mla[l]), row(out_norm_ssm[l]),
            w_o[:n_a], w_o[n_a:n_b], w_o[n_b:])

        w_up = ffn_w_up[l].astype(BF16)
        xf = _ffn(xf, row(ffn_norm[l]), w_up[:, :D_FF], w_up[:, D_FF:],
                  jnp.pad(ffn_conv_w[l].astype(F32), ((0, SUBLANES - ffn_conv_w.shape[1]), (0, 0))),
                  row(ffn_conv_b[l]), ffn_w_down[l].astype(BF16), seq,
                  final_gain=row(final_norm) if l == depth - 1 else None)

    return xf.reshape(batch, seq, d_model)
```

```python
import functools
import math

import jax
import jax.numpy as jnp
from jax import lax
from jax.experimental import pallas as pl
from jax.experimental.pallas import tpu as pltpu

F32 = jnp.float32
BF16 = jnp.bfloat16

LANES = 128
SUBLANES = 8
HEAD_DIM = 64
NSA_HEADS = 6
NSA_KV_HEADS = 2
NSA_GROUP = NSA_HEADS // NSA_KV_HEADS
N_BRANCH = 3
CMP_BLOCK = 32
CMP_STRIDE = 16
CMP_HIDDEN = 128
SEL_BLOCK = 64
SEL_TOP = 16
MAX_SEL_BLOCKS = 64
WINDOW = 512
MLA_HEADS = 6
MLA_NOPE = 64
MLA_ROPE = 32
Q_LORA = 384
KV_LORA = 128
ROPE_THETA = 10000.0
SSM_WIDTH = 256
SSM_GROUPS = 16
SSM_GROUP_CH = 16
SSM_STATE = 64
SSM_LANES = SSM_GROUPS * SSM_STATE
D_FF = 2816
EPS = 1e-6
NEG = -1e30
LOG2E = math.log2(math.e)

ROW_TILE = 512
ATTN_TILE = 256
KV_CHUNK = 512
FLASH_ROWS = 1536
FLASH_CHAINS = 6
V_ROWS = 80
SSM_CHUNK = 64
SSM_SUB = 16
FF_CHUNK = 256
FF_DOWN_GROUP = 6
VMEM_LIMIT = 56 * 1024 * 1024


def _params(sem, vmem=None):
    return pltpu.CompilerParams(dimension_semantics=sem, vmem_limit_bytes=vmem)


def _rms(x, g):
    return x * lax.rsqrt(jnp.mean(x * x, axis=-1, keepdims=True) + EPS) * g


def _gelu(x):
    c = math.sqrt(2.0 / math.pi)
    return 0.5 * x * (1.0 + jnp.tanh(c * (x + 0.044715 * (x * x * x))))


def _sigmoid(x):
    return 1.0 / (1.0 + jnp.exp(-x))


def _mod_pow2(x, n):
    assert n & (n - 1) == 0
    return jnp.bitwise_and(x, n - 1)


def _div_pow2(x, n):
    assert n & (n - 1) == 0
    return jnp.right_shift(x, n.bit_length() - 1)


def _const_spec(shape):
    nd = len(shape)
    return pl.BlockSpec(shape, lambda *_: (0,) * nd)


def _trig_kernel(pos_ref, invf_ref, c_ref, s_ref, ns_ref):
    ang = pos_ref[...] * invf_ref[...]
    sin = jnp.sin(ang)
    c_ref[...] = jnp.cos(ang)
    s_ref[...] = sin
    ns_ref[...] = -sin


def _trig(pos_rows, invf_row):
    rows = pos_rows.shape[0]
    tile = min(ROW_TILE, rows)
    spec = pl.BlockSpec((tile, LANES), lambda i: (i, 0))
    return pl.pallas_call(
        _trig_kernel,
        grid=(rows // tile,),
        in_specs=[spec, _const_spec((1, LANES))],
        out_specs=[spec] * 3,
        out_shape=[jax.ShapeDtypeStruct((rows, LANES), F32)] * 3,
        compiler_params=_params(("parallel",)),
        name="rope_trig",
    )(pos_rows, invf_row)


_IN_SEGS = (
    ("q", NSA_HEADS * LANES, BF16),
    ("kc", NSA_KV_HEADS * HEAD_DIM, F32),
    ("vc", NSA_KV_HEADS * HEAD_DIM, F32),
    ("ks", NSA_KV_HEADS * LANES, BF16),
    ("kw", NSA_KV_HEADS * LANES, BF16),
    ("cq", Q_LORA, None),
    ("gate", LANES, F32),
    ("ckv", KV_LORA, None),
    ("kra", LANES, None),
    ("krb", LANES, None),
    ("u", SSM_WIDTH, F32),
)
_IN_GROUPS = ((0,), (1, 2), (3,), (4,), (5, 6), (7, 8), (9,), (10,))
_IN_COLS = sum(n for _, n, _ in _IN_SEGS)
_IN_OUTS = tuple((name, n, dt) for name, n, dt in _IN_SEGS if dt is not None)
_VT_ROWS = 2 * NSA_KV_HEADS * LANES


def _ones_row(shape):
    row = _mod_pow2(lax.broadcasted_iota(jnp.int32, shape, 0), LANES)
    return jnp.where(row == HEAD_DIM, 1.0, 0.0)


def _inproj_kernel(x_ref, g_ref, w_ref, wvt_ref, cm_ref, sm_ref, qg_ref, kg_ref,
                   wa_ref, wb_ref, wk_ref, wmvt_ref, *o_refs, seq, tm, mla_scale):
    h = _rms(x_ref[...], g_ref[...]).astype(BF16)
    outs = dict(zip([name for name, _, _ in _IN_OUTS], o_refs))
    latent = {}
    off = 0
    for group in _IN_GROUPS:
        width = sum(_IN_SEGS[s][1] for s in group)
        y_all = jnp.dot(h, w_ref[:, off:off + width], preferred_element_type=F32)
        off += width
        lo = 0
        for s in group:
            name, n, dt = _IN_SEGS[s]
            y = y_all[:, lo:lo + n]
            lo += n
            if name == "ks":
                s0 = lax.rem(pl.program_id(0) * tm, seq)
                blk = _div_pow2(s0 + lax.broadcasted_iota(jnp.int32, (tm, n), 0), SEL_BLOCK)
                lane = _mod_pow2(lax.broadcasted_iota(jnp.int32, (tm, n), 1), LANES)
                y = y + jnp.where(lane - HEAD_DIM == blk, 1.0, 0.0)
            if dt is None:
                latent[name] = y
            else:
                outs[name][...] = y.astype(dt)
    vst_ref, vwt_ref, qm_ref, km_ref, vmt_ref = o_refs[len(_IN_OUTS):]
    _mla_stage(latent["cq"], latent["ckv"], latent["kra"], latent["krb"], cm_ref[...], sm_ref[...],
               qg_ref, kg_ref, wa_ref, wb_ref, wk_ref, wmvt_ref, qm_ref, km_ref, vmt_ref, mla_scale)
    v_t = lax.dot_general(wvt_ref[...], h, (((1,), (1,)), ((), ())), preferred_element_type=F32)
    v_t = (v_t + _ones_row(v_t.shape)).astype(BF16)
    wc = vwt_ref.shape[-1]
    for hh in range(NSA_KV_HEADS):
        vst_ref[0, hh, 0] = v_t[hh * LANES:(hh + 1) * LANES, :]
        r0 = (NSA_KV_HEADS + hh) * LANES
        for c in range(tm // wc):
            vwt_ref[0, hh, c] = v_t[r0:r0 + LANES, c * wc:(c + 1) * wc]


def _inproj(x, g, w, wvt, cmul, smul, qg, kg, wa, wb, wk, wmvt, batch, seq):
    t, d = x.shape
    tm = ROW_TILE
    assert tm == min(KV_CHUNK, seq) and tm % ATTN_TILE == 0
    nt = seq // tm
    rows = lambda n: pl.BlockSpec((tm, n), lambda i: (i, 0))
    chunked = lambda heads, per_tile, ck: pl.BlockSpec((1, heads, per_tile, LANES, ck),
                                                       lambda i: (i // nt, 0, i % nt, 0, 0))
    out_specs = [rows(n) for _, n, _ in _IN_OUTS]
    out_shape = [jax.ShapeDtypeStruct((t, n), dt) for _, n, dt in _IN_OUTS]
    out_specs[-1] = pl.BlockSpec((tm, SSM_WIDTH), lambda i: (i % nt, i // nt))
    out_shape[-1] = jax.ShapeDtypeStruct((seq, batch * SSM_WIDTH), F32)
    mla_wide = MLA_HEADS * LANES
    out_specs += [chunked(NSA_KV_HEADS, 1, tm), chunked(NSA_KV_HEADS, tm // ATTN_TILE, ATTN_TILE),
                  rows(mla_wide), rows(mla_wide), chunked(MLA_HEADS, 1, tm)]
    out_shape += [jax.ShapeDtypeStruct((batch, NSA_KV_HEADS, nt, LANES, tm), BF16),
                  jax.ShapeDtypeStruct((batch, NSA_KV_HEADS, seq // ATTN_TILE, LANES, ATTN_TILE), BF16),
                  jax.ShapeDtypeStruct((t, mla_wide), BF16), jax.ShapeDtypeStruct((t, mla_wide), BF16),
                  jax.ShapeDtypeStruct((batch, MLA_HEADS, nt, LANES, tm), BF16)]
    mla_scale = (MLA_NOPE + MLA_ROPE) ** -0.5 * LOG2E
    return pl.pallas_call(
        functools.partial(_inproj_kernel, seq=seq, tm=tm, mla_scale=mla_scale),
        grid=(t // tm,),
        in_specs=[pl.BlockSpec((tm, d), lambda i: (i, 0)), _const_spec((1, d)),
                  _const_spec((d, _IN_COLS)), _const_spec((_VT_ROWS, d)),
                  rows(LANES), rows(LANES), _const_spec((1, Q_LORA)), _const_spec((1, KV_LORA)),
                  _const_spec((Q_LORA, mla_wide)), _const_spec((Q_LORA, mla_wide)),
                  _const_spec((KV_LORA, mla_wide)), _const_spec((mla_wide, KV_LORA))],
        out_specs=out_specs,
        out_shape=out_shape,
        compiler_params=_params(("parallel",), VMEM_LIMIT),
        name="in_proj",
    )(x, g, w, wvt, cmul, smul, qg, kg, wa, wb, wk, wmvt)


_CMP_STREAMS = 2 * NSA_KV_HEADS


def _compress_kernel(kc_ref, vc_ref, pe_ref, w1_ref, b1_ref, w2_ref, b2_ref, o_ref):
    n = kc_ref.shape[0] // CMP_STRIDE
    x = jnp.concatenate([r[pl.ds(l, n, stride=CMP_STRIDE), :]
                         for l in range(CMP_STRIDE) for r in (kc_ref, vc_ref)], axis=1)
    top = jnp.dot((x + pe_ref[0:1, :]).astype(BF16), w1_ref[0], preferred_element_type=F32)
    bot = jnp.dot((x + pe_ref[1:2, :]).astype(BF16), w1_ref[1], preferred_element_type=F32)
    hid = _gelu(top + pltpu.roll(bot, n - 1, 0) + b1_ref[...]).astype(BF16)
    for j in range(_CMP_STREAMS):
        kind = j // NSA_KV_HEADS
        o_ref[0, j] = (jnp.dot(hid[:, j * CMP_HIDDEN:(j + 1) * CMP_HIDDEN], w2_ref[kind],
                               preferred_element_type=F32) + b2_ref[kind])


def _compress(kc, vc, batch, pe, w1, b1, w2, b2):
    b = batch
    seq = kc.shape[0] // b
    n = seq // CMP_STRIDE
    width = CMP_STRIDE * (kc.shape[1] + vc.shape[1])
    hid = _CMP_STREAMS * CMP_HIDDEN
    xspec = pl.BlockSpec((seq, kc.shape[1]), lambda i: (i, 0))
    return pl.pallas_call(
        _compress_kernel,
        grid=(b,),
        in_specs=[xspec, xspec,
                  _const_spec((2, width)), _const_spec((2, width, hid)), _const_spec((1, hid)),
                  _const_spec((2, CMP_HIDDEN, LANES)), _const_spec((2, 1, LANES))],
        out_specs=pl.BlockSpec((1, _CMP_STREAMS, n, LANES), lambda i: (i, 0, 0, 0)),
        out_shape=jax.ShapeDtypeStruct((b, _CMP_STREAMS, n, LANES), F32),
        compiler_params=_params(("parallel",), VMEM_LIMIT),
        name="nsa_compress",
    )(kc, vc, pe, w1, b1, w2, b2)


def _compress_weights(pe, ck_w1, ck_b1, ck_w2, ck_b2, cv_w1, cv_b1, cv_w2, cv_b2):
    eye = jnp.eye(_CMP_STREAMS, dtype=F32)
    per_stream = jnp.stack([ck_w1, ck_w1, cv_w1, cv_w1]).astype(F32)

    def place(w):
        return jnp.einsum("jldf,jk->ljdkf", w, eye).reshape(
            CMP_STRIDE * _CMP_STREAMS * HEAD_DIM, _CMP_STREAMS * CMP_HIDDEN)

    w1 = jnp.stack([place(per_stream[:, :CMP_STRIDE]), place(per_stream[:, CMP_STRIDE:])])
    spread = lambda p: jnp.broadcast_to(p[:, None, :], (CMP_STRIDE, _CMP_STREAMS, HEAD_DIM)).reshape(-1)
    pe2 = jnp.stack([spread(pe[:CMP_STRIDE]), spread(pe[CMP_STRIDE:])]).astype(F32)
    b1 = jnp.concatenate([ck_b1, ck_b1, cv_b1, cv_b1]).astype(F32).reshape(1, -1)
    pad2 = lambda w: jnp.pad(w.astype(F32), ((0, 0), (0, LANES - HEAD_DIM)))
    w2 = jnp.stack([pad2(ck_w2), pad2(cv_w2)]).astype(BF16)
    b2 = jnp.stack([pad2(ck_b2.reshape(1, -1)), pad2(cv_b2.reshape(1, -1))])
    return pe2, w1.astype(BF16), b1, w2, b2


def _stack_group(q_ref, h, groups):
    return jnp.concatenate([q_ref[:, (h * groups + g) * LANES:(h * groups + g + 1) * LANES]
                            for g in range(groups)], axis=0)


def _store_heads_t(o_ref, chains_t, tq):
    heads = [c[:, g * tq:(g + 1) * tq] for c in chains_t for g in range(c.shape[1] // tq)]
    for p in range(len(heads) // 2):
        pair = jnp.concatenate([heads[2 * p], heads[2 * p + 1]], axis=0)
        o_ref[:, p * LANES:(p + 1) * LANES] = pair.T.astype(o_ref.dtype)


def _stable_rank(val):
    sub = SUBLANES
    n_blk = val.shape[0] // sub
    blocks = [val[r * sub:(r + 1) * sub] for r in range(n_blk)]
    ranks = [jnp.zeros(blocks[0].shape, F32) for _ in range(n_blk)]
    row_in = lax.broadcasted_iota(jnp.int32, blocks[0].shape, 0)
    for i in range(val.shape[0]):
        row = val[i:i + 1, :]
        for r in range(n_blk):
            if r * sub > i:
                ahead = jnp.where(row >= blocks[r], 1.0, 0.0)
            elif (r + 1) * sub <= i:
                ahead = jnp.where(row > blocks[r], 1.0, 0.0)
            else:
                ahead = jnp.where(row_in + r * sub > i, jnp.where(row >= blocks[r], 1.0, 0.0),
                                  jnp.where(row > blocks[r], 1.0, 0.0))
            ranks[r] = ranks[r] + ahead
    return jnp.concatenate(ranks, axis=0)


def _cmp_scores(kc_ref, qs):
    nt = (((1,), (1,)), ((), ()))
    return [lax.dot_general(kc_ref[0, h].astype(BF16), qs[h], nt, preferred_element_type=F32)
            for h in range(NSA_KV_HEADS)]


def _cmp_finish(ss, q0, q_ref, vct_ref, ovl_ref, o_ref, qa_ref, *, tq, n_top):
    m_cols = NSA_GROUP * tq
    nc = ss[0].shape[0]
    t_col = q0 + _mod_pow2(lax.broadcasted_iota(jnp.int32, (nc, m_cols), 1), tq)
    n_row = lax.broadcasted_iota(jnp.int32, (nc, m_cols), 0)
    mask = n_row * CMP_STRIDE + (CMP_BLOCK - 1) <= t_col
    has_valid = t_col[0:1, :] >= CMP_BLOCK - 1
    j_idx =lax.broadcasted_iota(jnp.int32, (MAX_SEL_BLOCKS, tq), 0)
    cur = _div_pow2(q0 + lax.broadcasted_iota(jnp.int32, (MAX_SEL_BLOCKS, tq), 1), SEL_BLOCK)
    forced = (j_idx == 0) | (j_idx == cur) | (j_idx == cur - 1)
    outs = []
    for h in range(NSA_KV_HEADS):
        sm = jnp.where(mask, ss[h], NEG)
        e = jnp.exp2(sm - jnp.max(sm, axis=0, keepdims=True))
        scale = jnp.where(has_valid, 1.0 / jnp.sum(e, axis=0, keepdims=True), 0.0)
        p = e * scale
        outs.append(jnp.dot(vct_ref[0, h, 0:HEAD_DIM, :], p.astype(BF16),
                            preferred_element_type=F32))

        psum = p[:, 0:tq] + p[:, tq:2 * tq] + p[:, 2 * tq:3 * tq]
        hi = psum.astype(BF16)
        lo = (psum - hi.astype(F32)).astype(BF16)
        imp = (jnp.dot(ovl_ref[...], hi, preferred_element_type=F32)
               + jnp.dot(ovl_ref[...], lo, preferred_element_type=F32))
        val = jnp.where(forced, jnp.inf, jnp.where(j_idx > cur, -jnp.inf, imp))
        bias_t = jnp.where(_stable_rank(val) < n_top, 0.0, NEG)
        bias = jnp.concatenate([jnp.zeros_like(bias_t), bias_t], axis=0).T
        for g in range(NSA_GROUP):
            lane0 = (h * NSA_GROUP + g) * LANES
            qa_ref[:, lane0:lane0 + LANES] = (q_ref[:, lane0:lane0 + LANES].astype(F32)
                                              + bias).astype(qa_ref.dtype)
    _store_heads_t(o_ref, outs, tq)


def _flash_kernel(q_ref, k_ref, vt_ref, o_ref, s_a, s_b, *, chains, groups, tq, ck):
    q0 = pl.program_id(2) * tq
    m_cols = groups * tq
    qs = [_stack_group(q_ref, h, groups) for h in range(chains)]

    def issue_scores(c, s_ref):
        k0 = pl.multiple_of(c * ck, ck)
        for h in range(chains):
            s_ref[h] = lax.dot_general(k_ref[pl.ds(k0, ck), h * LANES:(h + 1) * LANES], qs[h],
                                       (((1,), (1,)), ((), ())), preferred_element_type=F32)

    def absorb(c, s_ref, stats, visible=None):
        out = []
        for h in range(chains):
            s = s_ref[h]
            if visible is not None:
                s = jnp.where(visible, s, NEG)
            m_old, acc = stats[h]
            m_new = jnp.maximum(m_old, jnp.max(s, axis=0, keepdims=True))
            alpha = jnp.exp2(m_old - m_new)
            p = jnp.exp2(s - m_new).astype(BF16)
            vt = vt_ref[0, h, c, 0:V_ROWS, :]
            out.append((m_new, alpha * acc + jnp.dot(vt, p, preferred_element_type=F32)))
        return tuple(out)

    def finish(stats):
        _store_heads_t(o_ref, [acc[0:HEAD_DIM] / acc[HEAD_DIM:HEAD_DIM + 1] for _, acc in stats], tq)

    last = (q0 + tq - 1) // ck
    shift = 0 if tq == ck else last * ck - q0
    visible = (shift + lax.broadcasted_iota(jnp.int32, (ck, m_cols), 0)
               <= _mod_pow2(lax.broadcasted_iota(jnp.int32, (ck, m_cols), 1), tq))

    def pair(j, stats):
        c = 2 * j
        issue_scores(c + 1, s_b)
        stats = absorb(c, s_a, stats)
        issue_scores(c + 2, s_a)
        return absorb(c + 1, s_b, stats)

    issue_scores(0, s_a)
    stats = tuple((jnp.full((1, m_cols), NEG, F32), jnp.zeros((V_ROWS, m_cols), F32))
                  for _ in range(chains))
    stats = lax.fori_loop(0, last // 2, pair, stats)

    @pl.when(lax.rem(last, 2) == 0)
    def _():
        finish(absorb(last, s_a, stats, visible))

    @pl.when(lax.rem(last, 2) == 1)
    def _():
        issue_scores(last, s_b)
        finish(absorb(last, s_b, absorb(last - 1, s_a, stats), visible))


def _flash(q, k, v_t, batch, seq, groups, name):
    t, qcols = q.shape
    kv_heads = k.shape[1] // LANES
    ck = min(KV_CHUNK, seq)
    tq = min(FLASH_ROWS // groups, ck)
    assert tq & (tq - 1) == 0 and ck % tq == 0 and seq % ck == 0
    nq = seq // tq
    n_chunks = seq // ck
    assert v_t.shape == (batch, kv_heads, n_chunks, LANES, ck)
    chains = max(c for c in range(1, FLASH_CHAINS + 1)
                 if kv_heads % c == 0 and (c * groups) % 2 == 0)
    heads = kv_heads * groups
    return pl.pallas_call(
        functools.partial(_flash_kernel, chains=chains, groups=groups, tq=tq, ck=ck),
        grid=(batch, kv_heads // chains, nq),
        in_specs=[pl.BlockSpec((tq, chains * groups * LANES), lambda b, h, i: (b * nq + i, h)),
                  pl.BlockSpec((seq, chains * LANES), lambda b, h, i: (b, h)),
                  pl.BlockSpec((1, chains, n_chunks, LANES, ck), lambda b, h, i: (b, h, 0, 0, 0))],
        out_specs=pl.BlockSpec((tq, chains * groups * HEAD_DIM), lambda b, h, i: (b * nq + i, h)),
        out_shape=jax.ShapeDtypeStruct((t, heads * HEAD_DIM), BF16),
        scratch_shapes=[pltpu.VMEM((chains, ck, groups * tq), F32)] * 2,
        compiler_params=_params(("parallel", "parallel", "parallel"), VMEM_LIMIT),
        name=name,
    )(q, k, v_t)


def _nsa_local_kernel(q_ref, kc_ref, vct_ref, ovl_ref, k_ref, vt_ref, oc_ref, qa_ref, ow_ref,
                      *, tq, n_top):
    i = pl.program_id(1)
    q0 = i * tq
    n_back = WINDOW // tq
    n_span = n_back + 1
    m_cols = NSA_GROUP * tq
    qs = [_stack_group(q_ref, h, NSA_GROUP) for h in range(NSA_KV_HEADS)]
    q_pos = q0 + _mod_pow2(lax.broadcasted_iota(jnp.int32, (tq, m_cols), 1), tq)
    key_in = lax.broadcasted_iota(jnp.int32, (tq, m_cols), 0)
    nt = (((1,), (1,)), ((), ()))

    def run(c0, clamped):
        start = c0 * tq if clamped else pl.multiple_of(c0 * tq, tq)
        ss = [lax.dot_general(k_ref[pl.ds(start, n_span * tq), h * LANES:(h + 1) * LANES], qs[h],
                              nt, preferred_element_type=F32) for h in range(NSA_KV_HEADS)]
        ss_cmp = _cmp_scores(kc_ref, qs)
        outs = []
        for h in range(NSA_KV_HEADS):
            blocks = []
            for j in range(n_span):
                blk = ss[h][j * tq:(j + 1) * tq]
                key_pos = start + j * tq + key_in
                if clamped:
                    blk = jnp.where((key_pos <= q_pos) & (key_pos > q_pos - WINDOW), blk, NEG)
                elif j == 0:
                    blk = jnp.where(key_pos > q_pos - WINDOW, blk, NEG)
                elif j == n_span - 1:
                    blk = jnp.where(key_pos <= q_pos, blk, NEG)
                blocks.append(blk)
            m = functools.reduce(jnp.maximum, [jnp.max(b, axis=0, keepdims=True) for b in blocks])
            acc = 0.0
            for j in range(n_span):
                p = jnp.exp2(blocks[j] - m).astype(BF16)
                acc = acc + jnp.dot(vt_ref[0, h, c0 + j, 0:V_ROWS, :], p, preferred_element_type=F32)
            outs.append(acc[0:HEAD_DIM] / acc[HEAD_DIM:HEAD_DIM + 1])
        _store_heads_t(ow_ref, outs, tq)
        _cmp_finish(ss_cmp, q0, q_ref, vct_ref, ovl_ref, oc_ref, qa_ref, tq=tq, n_top=n_top)

    @pl.when(i < n_back)
    def _():
        run(0, True)

    @pl.when(i >= n_back)
    def _():
        run(i - n_back, False)


def _nsa_local(q, kcmp, vcmp_t, ovl, k, v_t, batch, seq, n_top):
    t = q.shape[0]
    tq = ATTN_TILE
    nq = seq // tq
    nc = kcmp.shape[2]
    assert WINDOW % tq == 0 and seq >= WINDOW + tq
    wide = NSA_HEADS * LANES
    qspec = pl.BlockSpec((tq, wide), lambda b, i: (b * nq + i, 0))
    ospec = pl.BlockSpec((tq, NSA_HEADS * HEAD_DIM), lambda b, i: (b * nq + i, 0))
    o_sds = jax.ShapeDtypeStruct((t, NSA_HEADS * HEAD_DIM), BF16)
    return pl.pallas_call(
        functools.partial(_nsa_local_kernel, tq=tq, n_top=n_top),
        grid=(batch, nq),
        in_specs=[qspec,
                  pl.BlockSpec((1, NSA_KV_HEADS, nc, LANES), lambda b, i: (b, 0, 0, 0)),
                  pl.BlockSpec((1, NSA_KV_HEADS, LANES, nc), lambda b, i: (b, 0, 0, 0)),
                  _const_spec((MAX_SEL_BLOCKS, nc)),
                  pl.BlockSpec((seq, NSA_KV_HEADS * LANES), lambda b, i: (b, 0)),
                  pl.BlockSpec((1, NSA_KV_HEADS, nq, LANES, tq), lambda b, i: (b, 0, 0, 0, 0))],
        out_specs=[ospec, qspec, ospec],
        out_shape=[o_sds, jax.ShapeDtypeStruct((t, wide), BF16), o_sds],
        compiler_params=_params(("parallel", "parallel"), VMEM_LIMIT),
        name="nsa_local",
    )(q, kcmp, vcmp_t, ovl, k, v_t)


def _mla_stage(cq, ckv, kra, krb, cm, sm, qg_ref, kg_ref, wa_ref, wb_ref, wk_ref, wvt_ref,
               q_ref, k_ref, vt_ref, scale):
    qn = _rms(cq, qg_ref[...]).astype(BF16)
    cn = _rms(ckv, kg_ref[...]).astype(BF16)
    k_rot = kra * cm + krb * sm
    cm2 = jnp.concatenate([cm, cm], axis=1)
    sm2 = jnp.concatenate([sm, sm], axis=1)
    k_rot2 = jnp.concatenate([k_rot, k_rot], axis=1)
    for h in range(0, MLA_HEADS, 2):
        sl = slice(h * LANES, (h + 2) * LANES)
        qa = jnp.dot(qn, wa_ref[:, sl], preferred_element_type=F32)
        qb = jnp.dot(qn, wb_ref[:, sl], preferred_element_type=F32)
        q_ref[:, sl] = ((qa * cm2 + qb * sm2) * scale).astype(q_ref.dtype)
        k_ref[:, sl] = (jnp.dot(cn, wk_ref[:, sl], preferred_element_type=F32) + k_rot2).astype(k_ref.dtype)
    v_t = lax.dot_general(wvt_ref[...], cn, (((1,), (1,)), ((), ())), preferred_element_type=F32)
    v_t = (v_t + _ones_row(v_t.shape)).astype(vt_ref.dtype)
    for h in range(MLA_HEADS):
        vt_ref[0, h, 0] = v_t[h * LANES:(h + 1) * LANES, :]


def _ssm_kernel(u_ref, bblk_ref, a_ref, ccat_ref, d_ref, wglu_ref, bglu_ref, o_ref,
                h_sc, st_sc, *, tc, sub, nb):
    @pl.when(pl.program_id(0) == 0)
    def _():
        st_sc[...] = jnp.zeros_like(st_sc)

    n = SSM_LANES
    ar = jnp.broadcast_to(a_ref[0:1, :], (nb, n))
    ai = jnp.broadcast_to(a_ref[1:2, :], (nb, n))
    n_sub = tc // sub
    rows = sub * nb
    us = []
    for j in range(n_sub):
        u = u_ref[j * sub:(j + 1) * sub].reshape(rows, SSM_WIDTH)
        us.append(u)
        h_sc[j * rows:(j + 1) * rows, :] = jnp.dot(u.astype(BF16), bblk_ref[...],
                                                   preferred_element_type=F32)
    hr, hi = st_sc[0], st_sc[1]
    ys = []
    for j in range(n_sub):
        for t in range(j * sub, (j + 1) * sub):
            r0 = t * nb
            nr = ar * hr - ai * hi + h_sc[r0:r0 + nb, 0:n]
            ni = ar * hi + ai * hr + h_sc[r0:r0 + nb, n:2 * n]
            h_sc[r0:r0 + nb, 0:n] = nr
            h_sc[r0:r0 + nb, n:2 * n] = ni
            hr, hi = nr, ni
        ys.append(jnp.dot(h_sc[j * rows:(j + 1) * rows, :].astype(BF16), ccat_ref[...],
                          preferred_element_type=F32) + d_ref[...] * us[j])
    st_sc[0] = hr
    st_sc[1] = hi
    z = _gelu(jnp.concatenate(ys, axis=0))
    gate = jnp.dot(z.astype(BF16), wglu_ref[...], preferred_element_type=F32) + bglu_ref[...]
    o_ref[...] = (z * _sigmoid(gate)).reshape(tc, nb, SSM_WIDTH)


def _ssm(u_t, bblk, a_rows, ccat, d_row, wglu, bglu):
    seq, nb, _ = u_t.shape
    tc = SSM_CHUNK
    n = SSM_LANES
    uspec = pl.BlockSpec((tc, nb, SSM_WIDTH), lambda i: (i, 0, 0))
    return pl.pallas_call(
        functools.partial(_ssm_kernel, tc=tc, sub=SSM_SUB, nb=nb),
        grid=(seq // tc,),
        in_specs=[uspec, _const_spec((SSM_WIDTH, 2 * n)), _const_spec((2, n)),
                  _const_spec((2 * n, SSM_WIDTH)), _const_spec((1, SSM_WIDTH)),
                  _const_spec((SSM_WIDTH, SSM_WIDTH)), _const_spec((1, SSM_WIDTH))],
        out_specs=uspec,
        out_shape=jax.ShapeDtypeStruct(u_t.shape, F32),
        scratch_shapes=[pltpu.VMEM((tc * nb, 2 * n), F32), pltpu.VMEM((2, nb, n), F32)],
        compiler_params=_params(("arbitrary",), VMEM_LIMIT),
        name="s5_scan",
    )(u_t, bblk, a_rows, ccat, d_row, wglu, bglu)


def _outproj_kernel(oc_ref, os_ref, ow_ref, gate_ref, gb_ref, ex_ref, om_ref, oz_ref, x_ref,
                    gn_ref, gm_ref, gz_ref, wn_ref, wm_ref, wz_ref, o_ref):
    g = _sigmoid(gate_ref[...] + gb_ref[...])
    g_hi = g.astype(BF16)
    g_lo = (g - g_hi.astype(F32)).astype(BF16)
    spread = jnp.dot(jnp.concatenate([g_hi, g_lo], axis=1), ex_ref[...],
                     preferred_element_type=F32)
    o_a = 0.0
    wide = oc_ref.shape[1]
    for r, br_ref in enumerate((oc_ref, os_ref, ow_ref)):
        o_a = o_a + spread[:, r * wide:(r + 1) * wide] * br_ref[...].astype(F32)
    inv = lax.rsqrt(jnp.sum(o_a * o_a, axis=-1, keepdims=True) * (1.0 / (NSA_HEADS * HEAD_DIM)) + EPS)
    acc = x_ref[...] + jnp.dot((o_a * inv * gn_ref[...]).astype(BF16), wn_ref[...],
                               preferred_element_type=F32)
    om = om_ref[...].astype(F32)
    inv = lax.rsqrt(jnp.sum(om * om, axis=-1, keepdims=True) * (1.0 / (MLA_HEADS * HEAD_DIM)) + EPS)
    acc = acc + jnp.dot((om * inv * gm_ref[...]).astype(BF16), wm_ref[...], preferred_element_type=F32)
    acc = acc + jnp.dot(_rms(oz_ref[...], gz_ref[...]).astype(BF16), wz_ref[...],
                        preferred_element_type=F32)
    o_ref[...] = acc


def _gate_spread():
    col = jnp.arange(LANES)[:, None]
    out = jnp.arange(N_BRANCH * NSA_HEADS * HEAD_DIM)[None, :]
    lane, r = out % (NSA_HEADS * HEAD_DIM), out // (NSA_HEADS * HEAD_DIM)
    hit = (col == N_BRANCH * (lane // HEAD_DIM) + r).astype(BF16)
    return jnp.concatenate([hit, hit], axis=0)


def _outproj(oc, osel, ow, gate, gate_b, om, oz, x, gn, gm, gz, wn, wm, wz):
    t, d = x.shape
    tm = ROW_TILE
    nt = oz.shape[0] // tm
    wide = NSA_HEADS * HEAD_DIM
    assert wide == MLA_HEADS * HEAD_DIM
    rows = lambda n: pl.BlockSpec((tm, n), lambda i: (i, 0))
    return pl.pallas_call(
        _outproj_kernel,
        grid=(t // tm,),
        in_specs=[rows(wide), rows(wide), rows(wide), rows(LANES), _const_spec((1, LANES)),
                  _const_spec((2 * LANES, N_BRANCH * wide)), rows(wide),
                  pl.BlockSpec((tm, SSM_WIDTH), lambda i: (i % nt, i // nt)), rows(d),
                  _const_spec((1, wide)), _const_spec((1, wide)), _const_spec((1, SSM_WIDTH)),
                  _const_spec((wide, d)), _const_spec((wide, d)), _const_spec((SSM_WIDTH, d))],
        out_specs=rows(d),
        out_shape=jax.ShapeDtypeStruct((t, d), F32),
        compiler_params=_params(("parallel",), VMEM_LIMIT),
        name="out_proj",
    )(oc, osel, ow, gate, gate_b, _gate_spread(), om, oz, x, gn, gm, gz, wn, wm, wz)


def _ffn_kernel(x_ref, g_ref, wg_ref, wv_ref, cw_ref, cb_ref, wd_ref, *rest,
                tm, tiles_per_seq, cf, down_group):
    fg_ref = rest[0] if len(rest) == 4 else None
    o_ref, carry_sc, act_sc = rest[-3:]

    @pl.when(lax.rem(pl.program_id(0), tiles_per_seq) == 0)
    def _():
        carry_sc[...] = jnp.zeros_like(carry_sc)

    x = x_ref[...]
    h = _rms(x, g_ref[...]).astype(BF16)
    row = lax.broadcasted_iota(jnp.int32, (tm, cf), 0)
    n_chunks = D_FF // cf
    acc = x

    def up(c):
        sl = slice(c * cf, (c + 1) * cf)
        return (jnp.dot(h, wg_ref[:, sl], preferred_element_type=F32),
                jnp.dot(h, wv_ref[:, sl], preferred_element_type=F32))

    nxt = up(0)
    for c in range(n_chunks):
        sl = slice(c * cf, (c + 1) * cf)
        gate, val = nxt
        if c + 1 < n_chunks:
            nxt = up(c + 1)
        tail = carry_sc[:, sl]
        p1 = tail[SUBLANES - 1:SUBLANES, :]
        p2 = tail[SUBLANES - 2:SUBLANES - 1, :]
        g1 = jnp.where(row == 0, p1, pltpu.roll(gate, 1, 0))
        g2 = jnp.where(row == 0, p2, jnp.where(row == 1, p1, pltpu.roll(gate, 2, 0)))
        carry_sc[:, sl] = gate[tm - SUBLANES:tm, :]
        gc = cw_ref[0:1, sl] * g2 + cw_ref[1:2, sl] * g1 + cw_ref[2:3, sl] * gate + cb_ref[:, sl]
        act_sc[:, sl] = (gc * _sigmoid(gc) * val).astype(BF16)
        if (c + 1) % down_group == 0 or c + 1 == n_chunks:
            lo = (c // down_group) * down_group * cf
            acc = acc + jnp.dot(act_sc[:, lo:(c + 1) * cf], wd_ref[lo:(c + 1) * cf, :],
                                preferred_element_type=F32)
    o_ref[...] = acc if fg_ref is None else _rms(acc, fg_ref[...])


def _ffn(x, g, wg, wv, cw, cb, wd, seq, final_gain=None):
    t, d = x.shape
    tm = ROW_TILE
    rows = pl.BlockSpec((tm, d), lambda i: (i, 0))
    extra = [] if final_gain is None else [final_gain]
    return pl.pallas_call(
        functools.partial(_ffn_kernel, tm=tm, tiles_per_seq=seq // tm, cf=FF_CHUNK,
                          down_group=FF_DOWN_GROUP),
        grid=(t // tm,),
        in_specs=[rows, _const_spec((1, d)), _const_spec((d, D_FF)), _const_spec((d, D_FF)),
                  _const_spec((SUBLANES, D_FF)), _const_spec((1, D_FF)), _const_spec((D_FF, d))]
        + [_const_spec((1, d))] * len(extra),
        out_specs=rows,
        out_shape=jax.ShapeDtypeStruct((t, d), F32),
        scratch_shapes=[pltpu.VMEM((SUBLANES, D_FF), F32), pltpu.VMEM((tm, D_FF), BF16)],
        compiler_params=_params(("arbitrary",), VMEM_LIMIT),
        name="conv_ffn",
    )(x, g, wg, wv, cw, cb, wd, *extra)


def _pad_heads(w, heads, width):
    lead = w.shape[:-1]
    w = w.reshape(lead + (heads, width))
    w = jnp.pad(w, [(0, 0)] * len(lead) + [(0, 0), (0, LANES - width)])
    return w.reshape(lead + (heads * LANES,))


def _inproj_weight(w):
    kv = NSA_KV_HEADS * HEAD_DIM
    sizes = (NSA_HEADS * HEAD_DIM,) + (kv,) * 6 + (NSA_HEADS * N_BRANCH, Q_LORA, KV_LORA, MLA_ROPE,
                                                   SSM_WIDTH)
    offs = [0]
    for n in sizes:
        offs.append(offs[-1] + n)
    (w_q, w_kc, w_vc, w_ks, w_vs, w_kw, w_vw, w_g, w_cq, w_ckv, w_kr, w_u) = [
        w[:, a:b] for a, b in zip(offs[:-1], offs[1:])]
    d = w.shape[0]
    half = MLA_ROPE // 2
    z64 = jnp.zeros((d, HEAD_DIM), w.dtype)
    z32 = jnp.zeros((d, LANES - HEAD_DIM - MLA_ROPE), w.dtype)
    r1, r2 = w_kr[:, :half], w_kr[:, half:]
    cols = [
        _pad_heads(w_q * (HEAD_DIM ** -0.5 * LOG2E), NSA_HEADS, HEAD_DIM),
        w_kc, w_vc,
        _pad_heads(w_ks, NSA_KV_HEADS, HEAD_DIM),
        _pad_heads(w_kw, NSA_KV_HEADS, HEAD_DIM),
        w_cq,
        jnp.pad(w_g, ((0, 0), (0, LANES - w_g.shape[1]))),
        w_ckv,
        jnp.concatenate([z64, r1, r2, z32], axis=1),
        jnp.concatenate([z64, r2, r1, z32], axis=1),
        w_u,
    ]
    w_vt = jnp.concatenate([_pad_heads(w_vs, NSA_KV_HEADS, HEAD_DIM),
                            _pad_heads(w_vw, NSA_KV_HEADS, HEAD_DIM)], axis=1).T
    return jnp.concatenate(cols, axis=1).astype(BF16), w_vt.astype(BF16)


def _mla_weights(w_uq, w_uk, w_uv):
    half = MLA_ROPE // 2
    w = w_uq.reshape(Q_LORA, MLA_HEADS, MLA_NOPE + MLA_ROPE)
    nope, r1, r2 = w[..., :MLA_NOPE], w[..., MLA_NOPE:MLA_NOPE + half], w[..., MLA_NOPE + half:]
    z32 = jnp.zeros((Q_LORA, MLA_HEADS, LANES - MLA_NOPE - MLA_ROPE), w.dtype)
    wa = jnp.concatenate([nope, r1, r2, z32], axis=-1).reshape(Q_LORA, MLA_HEADS * LANES)
    wb = jnp.concatenate([jnp.zeros_like(nope), r2, r1, z32], axis=-1).reshape(Q_LORA, MLA_HEADS * LANES)
    wk = _pad_heads(w_uk, MLA_HEADS, MLA_NOPE)
    wvt = _pad_heads(w_uv, MLA_HEADS, HEAD_DIM).T
    return wa.astype(BF16), wb.astype(BF16), wk.astype(BF16), wvt.astype(BF16)


def _ssm_weights(log_dt, a_re, a_im, b_re, b_im, c_re, c_im, d):
    dt = jnp.exp(log_dt.astype(F32))[:, None]
    ar, ai = a_re.astype(F32), a_im.astype(F32)
    mag = jnp.exp(ar * dt)
    abr, abi = mag * jnp.cos(ai * dt), mag * jnp.sin(ai * dt)
    den = ar * ar + ai * ai
    fr = ((abr - 1.0) * ar + abi * ai) / den
    fi = (abi * ar - (abr - 1.0) * ai) / den
    br, bi = b_re.astype(F32), b_im.astype(F32)
    bbr = fr[..., None] * br - fi[..., None] * bi
    bbi = fr[..., None] * bi + fi[..., None] * br
    eye = jnp.eye(SSM_GROUPS, dtype=F32)
    blk_in = lambda m: jnp.einsum("gpc,gh->gchp", m, eye).reshape(SSM_WIDTH, SSM_LANES)
    blk_out = lambda m: jnp.einsum("gcp,gh->gphc", m, eye).reshape(SSM_LANES, SSM_WIDTH)
    bblk = jnp.concatenate([blk_in(bbr), blk_in(bbi)], axis=1)
    ccat = jnp.concatenate([blk_out(c_re.astype(F32)), -blk_out(c_im.astype(F32))], axis=0)
    a_rows = jnp.stack([abr.reshape(SSM_LANES), abi.reshape(SSM_LANES)])
    return bblk.astype(BF16), a_rows, ccat.astype(BF16), d.astype(F32).reshape(1, SSM_WIDTH)


def _overlap_t(nc_pad):
    start = jnp.arange(nc_pad) * CMP_STRIDE
    lo = jnp.arange(MAX_SEL_BLOCKS) * SEL_BLOCK
    hit = (start[None, :] < lo[:, None] + SEL_BLOCK) & (start[None, :] + CMP_BLOCK > lo[:, None])
    return hit.astype(BF16)


def _rope_multipliers(positions):
    half = MLA_ROPE // 2
    per_row = LANES // half
    t = positions.size
    inv_freq = ROPE_THETA ** (-jnp.arange(half, dtype=F32) / half)
    pos = jnp.broadcast_to(positions.astype(F32).reshape(t // per_row, per_row, 1),
                           (t // per_row, per_row, half)).reshape(t // per_row, LANES)
    cos, sin, nsin = [a.reshape(t, half)
                      for a in _trig(pos, jnp.tile(inv_freq, per_row).reshape(1, LANES))]
    one = jnp.ones((t, HEAD_DIM), F32)
    zero = jnp.zeros((t, HEAD_DIM), F32)
    pad = jnp.zeros((t, LANES - HEAD_DIM - MLA_ROPE), F32)
    return (jnp.concatenate([one, cos, cos, pad], axis=1),
            jnp.concatenate([zero, nsin, sin, pad], axis=1))


def kernel(x, positions, attn_norm, w_in, nsa_pe, nsa_ck_w1, nsa_ck_b1, nsa_ck_w2, nsa_ck_b2, nsa_cv_w1, nsa_cv_b1, nsa_cv_w2, nsa_cv_b2, nsa_gate_b, mla_q_norm, mla_kv_norm, mla_w_uq, mla_w_uk, mla_w_uv, ssm_log_dt, ssm_a_re, ssm_a_im, ssm_b_re, ssm_b_im, ssm_c_re, ssm_c_im, ssm_d, ssm_w_glu, ssm_b_glu, out_norm_nsa, out_norm_mla, out_norm_ssm, w_out, ffn_norm, ffn_w_up, ffn_conv_w, ffn_conv_b, ffn_w_down, final_norm):
    batch, seq, d_model = x.shape
    depth = w_in.shape[0]
    t = batch * seq
    n_half = seq // CMP_STRIDE
    n_sel = seq // SEL_BLOCK
    assert depth >= 1 and seq % ROW_TILE == 0 and n_sel <= MAX_SEL_BLOCKS and n_half % LANES == 0
    n_top = min(SEL_TOP, n_sel)
    row = lambda v: v.astype(F32).reshape(1, -1)

    cmul, smul = _rope_multipliers(positions)
    ovl_t = _overlap_t(n_half)

    xf = x.reshape(t, d_model)
    for l in range(depth):
        (q, kc, vc, ks, kw, gate, u, vs_t, vw_t, q_m, k_m, vm_t) = _inproj(
            xf, row(attn_norm[l]), *_inproj_weight(w_in[l]), cmul, smul,
            row(mla_q_norm[l]), row(mla_kv_norm[l]),
            *_mla_weights(mla_w_uq[l], mla_w_uk[l], mla_w_uv[l]), batch, seq)

        kvcmp = _compress(
            kc, vc, batch,
            *_compress_weights(nsa_pe[l], nsa_ck_w1[l], nsa_ck_b1[l], nsa_ck_w2[l], nsa_ck_b2[l],
                               nsa_cv_w1[l], nsa_cv_b1[l], nsa_cv_w2[l], nsa_cv_b2[l]))
        vcmp_t = kvcmp[:, NSA_KV_HEADS:].transpose(0, 1, 3, 2).astype(BF16)
        o_cmp, q_aug, o_win = _nsa_local(q, kvcmp, vcmp_t, ovl_t, kw, vw_t, batch, seq, n_top)
        o_sel = _flash(q_aug, ks, vs_t, batch, seq, NSA_GROUP, "nsa_selected")

        o_mla = _flash(q_m, k_m, vm_t, batch, seq, 1, "mla_attention")

        bblk, a_rows, ccat, d_row = _ssm_weights(
            ssm_log_dt[l], ssm_a_re[l], ssm_a_im[l], ssm_b_re[l], ssm_b_im[l],
            ssm_c_re[l], ssm_c_im[l], ssm_d[l])
        o_ssm = _ssm(u.reshape(seq, batch, SSM_WIDTH), bblk, a_rows, ccat, d_row,
                     ssm_w_glu[l].astype(BF16), row(ssm_b_glu[l]))
        o_ssm = o_ssm.reshape(seq, batch * SSM_WIDTH)

        w_o = w_out[l].astype(BF16)
        n_a = NSA_HEADS * HEAD_DIM
        n_b = n_a + MLA_HEADS * HEAD_DIM
        xf = _outproj(
            o_cmp, o_sel, o_win, gate,
            jnp.pad(row(nsa_gate_b[l]), ((0, 0), (0, LANES - NSA_HEADS * N_BRANCH))),
            o_mla, o_ssm, xf,
            row(out_norm_nsa[l]), row(out_norm_mla[l]), row(out_norm_ssm[l]),
            w_o[:n_a], w_o[n_a:n_b], w_o[n_b:])

        w_up = ffn_w_up[l].astype(BF16)
        xf = _ffn(xf, row(ffn_norm[l]), w_up[:, :D_FF], w_up[:, D_FF:],
                  jnp.pad(ffn_conv_w[l].astype(F32), ((0, SUBLANES - ffn_conv_w.shape[1]), (0, 0))),
                  row(ffn_conv_b[l]), ffn_w_down[l].astype(BF16), seq,
                  final_gain=row(final_norm) if l == depth - 1 else None)

    return xf.reshape(batch, seq, d_model)
```

```python
import functools
import math

import jax
import jax.numpy as jnp
from jax import lax
from jax.experimental import pallas as pl
from jax.experimental.pallas import tpu as pltpu

F32 = jnp.float32
BF16 = jnp.bfloat16

LANES = 128
SUBLANES = 8
HEAD_DIM = 64
NSA_HEADS = 6
NSA_KV_HEADS = 2
NSA_GROUP = NSA_HEADS // NSA_KV_HEADS
N_BRANCH = 3
CMP_BLOCK = 32
CMP_STRIDE = 16
CMP_HIDDEN = 128
SEL_BLOCK = 64
SEL_TOP = 16
MAX_SEL_BLOCKS = 64
WINDOW = 512
MLA_HEADS = 6
MLA_NOPE = 64
MLA_ROPE = 32
Q_LORA = 384
KV_LORA = 128
ROPE_THETA = 10000.0
SSM_WIDTH = 256
SSM_GROUPS = 16
SSM_GROUP_CH = 16
SSM_STATE = 64
SSM_LANES = SSM_GROUPS * SSM_STATE
D_FF = 2816
EPS = 1e-6
NEG = -1e30
LOG2E = math.log2(math.e)

ROW_TILE = 512
ATTN_TILE = 256
KV_CHUNK = 512
FLASH_ROWS = 1536
FLASH_CHAINS = 6
V_ROWS = 80
SSM_CHUNK = 64
SSM_SUB = 16
FF_CHUNK = 256
FF_DOWN_GROUP = 6
VMEM_LIMIT = 56 * 1024 * 1024


def _params(sem, vmem=None):
    return pltpu.CompilerParams(dimension_semantics=sem, vmem_limit_bytes=vmem)


def _rms(x, g):
    return x * lax.rsqrt(jnp.mean(x * x, axis=-1, keepdims=True) + EPS) * g


def _gelu(x):
    c = math.sqrt(2.0 / math.pi)
    return 0.5 * x * (1.0 + jnp.tanh(c * (x + 0.044715 * (x * x * x))))


def _sigmoid(x):
    return 1.0 / (1.0 + jnp.exp(-x))


def _mod_pow2(x, n):
    assert n & (n - 1) == 0
    return jnp.bitwise_and(x, n - 1)


def _div_pow2(x, n):
    assert n & (n - 1) == 0
    return jnp.right_shift(x, n.bit_length() - 1)


def _const_spec(shape):
    nd = len(shape)
    return pl.BlockSpec(shape, lambda *_: (0,) * nd)


def _trig_kernel(pos_ref, invf_ref, c_ref, s_ref, ns_ref):
    ang = pos_ref[...] * invf_ref[...]
    sin = jnp.sin(ang)
    c_ref[...] = jnp.cos(ang)
    s_ref[...] = sin
    ns_ref[...] = -sin


def _trig(pos_rows, invf_row):
    rows = pos_rows.shape[0]
    tile = min(ROW_TILE, rows)
    spec = pl.BlockSpec((tile, LANES), lambda i: (i, 0))
    return pl.pallas_call(
        _trig_kernel,
        grid=(rows // tile,),
        in_specs=[spec, _const_spec((1, LANES))],
        out_specs=[spec] * 3,
        out_shape=[jax.ShapeDtypeStruct((rows, LANES), F32)] * 3,
        compiler_params=_params(("parallel",)),
        name="rope_trig",
    )(pos_rows, invf_row)


_IN_SEGS = (
    ("q", NSA_HEADS * LANES, BF16),
    ("kc", NSA_KV_HEADS * HEAD_DIM, F32),
    ("vc", NSA_KV_HEADS * HEAD_DIM, F32),
    ("ks", NSA_KV_HEADS * LANES, BF16),
    ("kw", NSA_KV_HEADS * LANES, BF16),
    ("cq", Q_LORA, None),
    ("gate", LANES, F32),
    ("ckv", KV_LORA, None),
    ("kra", LANES, None),
    ("krb", LANES, None),
    ("u", SSM_WIDTH, F32),
)
_IN_GROUPS = ((0,), (1, 2), (3,), (4,), (5, 6), (7, 8), (9,), (10,))
_IN_COLS = sum(n for _, n, _ in _IN_SEGS)
_IN_OUTS = tuple((name, n, dt) for name, n, dt in _IN_SEGS if dt is not None)
_VT_ROWS = 2 * NSA_KV_HEADS * LANES


def _ones_row(shape):
    row = _mod_pow2(lax.broadcasted_iota(jnp.int32, shape, 0), LANES)
    return jnp.where(row == HEAD_DIM, 1.0, 0.0)


def _inproj_kernel(x_ref, g_ref, w_ref, wvt_ref, cm_ref, sm_ref, qg_ref, kg_ref,
                   wa_ref, wb_ref, wk_ref, wmvt_ref, *o_refs, seq, tm, mla_scale):
    h = _rms(x_ref[...], g_ref[...]).astype(BF16)
    outs = dict(zip([name for name, _, _ in _IN_OUTS], o_refs))
    latent = {}
    off = 0
    for group in _IN_GROUPS:
        width = sum(_IN_SEGS[s][1] for s in group)
        y_all = jnp.dot(h, w_ref[:, off:off + width], preferred_element_type=F32)
        off += width
        lo = 0
        for s in group:
            name, n, dt = _IN_SEGS[s]
            y = y_all[:, lo:lo + n]
            lo += n
            if name == "ks":
                s0 = lax.rem(pl.program_id(0) * tm, seq)
                blk = _div_pow2(s0 + lax.broadcasted_iota(jnp.int32, (tm, n), 0), SEL_BLOCK)
                lane = _mod_pow2(lax.broadcasted_iota(jnp.int32, (tm, n), 1), LANES)
                y = y + jnp.where(lane - HEAD_DIM == blk, 1.0, 0.0)
            if dt is None:
                latent[name] = y
            else:
                outs[name][...] = y.astype(dt)
    vst_ref, vwt_ref, qm_ref, km_ref, vmt_ref = o_refs[len(_IN_OUTS):]
    _mla_stage(latent["cq"], latent["ckv"], latent["kra"], latent["krb"], cm_ref[...], sm_ref[...],
               qg_ref, kg_ref, wa_ref, wb_ref, wk_ref, wmvt_ref, qm_ref, km_ref, vmt_ref, mla_scale)
    v_t = lax.dot_general(wvt_ref[...], h, (((1,), (1,)), ((), ())), preferred_element_type=F32)
    v_t = (v_t + _ones_row(v_t.shape)).astype(BF16)
    wc = vwt_ref.shape[-1]
    for hh in range(NSA_KV_HEADS):
        vst_ref[0, hh, 0] = v_t[hh * LANES:(hh + 1) * LANES, :]
        r0 = (NSA_KV_HEADS + hh) * LANES
        for c in range(tm // wc):
            vwt_ref[0, hh, c] = v_t[r0:r0 + LANES, c * wc:(c + 1) * wc]


def _inproj(x, g, w, wvt, cmul, smul, qg, kg, wa, wb, wk, wmvt, batch, seq):
    t, d = x.shape
    tm = ROW_TILE
    assert tm == min(KV_CHUNK, seq) and tm % ATTN_TILE == 0
    nt = seq // tm
    rows = lambda n: pl.BlockSpec((tm, n), lambda i: (i, 0))
    chunked = lambda heads, per_tile, ck: pl.BlockSpec((1, heads, per_tile, LANES, ck),
                                                       lambda i: (i // nt, 0, i % nt, 0, 0))
    out_specs = [rows(n) for _, n, _ in _IN_OUTS]
    out_shape = [jax.ShapeDtypeStruct((t, n), dt) for _, n, dt in _IN_OUTS]
    out_specs[-1] = pl.BlockSpec((tm, SSM_WIDTH), lambda i: (i % nt, i // nt))
    out_shape[-1] = jax.ShapeDtypeStruct((seq, batch * SSM_WIDTH), F32)
    mla_wide = MLA_HEADS * LANES
    out_specs += [chunked(NSA_KV_HEADS, 1, tm), chunked(NSA_KV_HEADS, tm // ATTN_TILE, ATTN_TILE),
                  rows(mla_wide), rows(mla_wide), chunked(MLA_HEADS, 1, tm)]
    out_shape += [jax.ShapeDtypeStruct((batch, NSA_KV_HEADS, nt, LANES, tm), BF16),
                  jax.ShapeDtypeStruct((batch, NSA_KV_HEADS, seq // ATTN_TILE, LANES, ATTN_TILE), BF16),
                  jax.ShapeDtypeStruct((t, mla_wide), BF16), jax.ShapeDtypeStruct((t, mla_wide), BF16),
                  jax.ShapeDtypeStruct((batch, MLA_HEADS, nt, LANES, tm), BF16)]
    mla_scale = (MLA_NOPE + MLA_ROPE) ** -0.5 * LOG2E
    return pl.pallas_call(
        functools.partial(_inproj_kernel, seq=seq, tm=tm, mla_scale=mla_scale),
        grid=(t // tm,),
        in_specs=[pl.BlockSpec((tm, d), lambda i: (i, 0)), _const_spec((1, d)),
                  _const_spec((d, _IN_COLS)), _const_spec((_VT_ROWS, d)),
                  rows(LANES), rows(LANES), _const_spec((1, Q_LORA)), _const_spec((1, KV_LORA)),
                  _const_spec((Q_LORA, mla_wide)), _const_spec((Q_LORA, mla_wide)),
                  _const_spec((KV_LORA, mla_wide)), _const_spec((mla_wide, KV_LORA))],
        out_specs=out_specs,
        out_shape=out_shape,
        compiler_params=_params(("parallel",), VMEM_LIMIT),
        name="in_proj",
    )(x, g, w, wvt, cmul, smul, qg, kg, wa, wb, wk, wmvt)


_CMP_STREAMS = 2 * NSA_KV_HEADS


def _compress_kernel(kc_ref, vc_ref, pe_ref, w1_ref, b1_ref, w2_ref, b2_ref, o_ref):
    n = kc_ref.shape[0] // CMP_STRIDE
    x = jnp.concatenate([r[pl.ds(l, n, stride=CMP_STRIDE), :]
                         for l in range(CMP_STRIDE) for r in (kc_ref, vc_ref)], axis=1)
    top = jnp.dot((x + pe_ref[0:1, :]).astype(BF16), w1_ref[0], preferred_element_type=F32)
    bot = jnp.dot((x + pe_ref[1:2, :]).astype(BF16), w1_ref[1], preferred_element_type=F32)
    hid = _gelu(top + pltpu.roll(bot, n - 1, 0) + b1_ref[...]).astype(BF16)
    for j in range(_CMP_STREAMS):
        kind = j // NSA_KV_HEADS
        o_ref[0, j] = (jnp.dot(hid[:, j * CMP_HIDDEN:(j + 1) * CMP_HIDDEN], w2_ref[kind],
                               preferred_element_type=F32) + b2_ref[kind])


def _compress(kc, vc, batch, pe, w1, b1, w2, b2):
    b = batch
    seq = kc.shape[0] // b
    n = seq // CMP_STRIDE
    width = CMP_STRIDE * (kc.shape[1] + vc.shape[1])
    hid = _CMP_STREAMS * CMP_HIDDEN
    xspec = pl.BlockSpec((seq, kc.shape[1]), lambda i: (i, 0))
    return pl.pallas_call(
        _compress_kernel,
        grid=(b,),
        in_specs=[xspec, xspec,
                  _const_spec((2, width)), _const_spec((2, width, hid)), _const_spec((1, hid)),
                  _const_spec((2, CMP_HIDDEN, LANES)), _const_spec((2, 1, LANES))],
        out_specs=pl.BlockSpec((1, _CMP_STREAMS, n, LANES), lambda i: (i, 0, 0, 0)),
        out_shape=jax.ShapeDtypeStruct((b, _CMP_STREAMS, n, LANES), F32),
        compiler_params=_params(("parallel",), VMEM_LIMIT),
        name="nsa_compress",
    )(kc, vc, pe, w1, b1, w2, b2)


def _compress_weights(pe, ck_w1, ck_b1, ck_w2, ck_b2, cv_w1, cv_b1, cv_w2, cv_b2):
    eye = jnp.eye(_CMP_STREAMS, dtype=F32)
    per_stream = jnp.stack([ck_w1, ck_w1, cv_w1, cv_w1]).astype(F32)

    def place(w):
        return jnp.einsum("jldf,jk->ljdkf", w, eye).reshape(
            CMP_STRIDE * _CMP_STREAMS * HEAD_DIM, _CMP_STREAMS * CMP_HIDDEN)

    w1 = jnp.stack([place(per_stream[:, :CMP_STRIDE]), place(per_stream[:, CMP_STRIDE:])])
    spread = lambda p: jnp.broadcast_to(p[:, None, :], (CMP_STRIDE, _CMP_STREAMS, HEAD_DIM)).reshape(-1)
    pe2 = jnp.stack([spread(pe[:CMP_STRIDE]), spread(pe[CMP_STRIDE:])]).astype(F32)
    b1 = jnp.concatenate([ck_b1, ck_b1, cv_b1, cv_b1]).astype(F32).reshape(1, -1)
    pad2 = lambda w: jnp.pad(w.astype(F32), ((0, 0), (0, LANES - HEAD_DIM)))
    w2 = jnp.stack([pad2(ck_w2), pad2(cv_w2)]).astype(BF16)
    b2 = jnp.stack([pad2(ck_b2.reshape(1, -1)), pad2(cv_b2.reshape(1, -1))])
    return pe2, w1.astype(BF16), b1, w2, b2


def _stack_group(q_ref, h, groups):
    return jnp.concatenate([q_ref[:, (h * groups + g) * LANES:(h * groups + g + 1) * LANES]
                            for g in range(groups)], axis=0)


def _store_heads_t(o_ref, chains_t, tq):
    heads = [c[:, g * tq:(g + 1) * tq] for c in chains_t for g in range(c.shape[1] // tq)]
    for p in range(len(heads) // 2):
        pair = jnp.concatenate([heads[2 * p], heads[2 * p + 1]], axis=0)
        o_ref[:, p * LANES:(p + 1) * LANES] = pair.T.astype(o_ref.dtype)


def _stable_rank(val):
    sub = SUBLANES
    n_blk = val.shape[0] // sub
    blocks = [val[r * sub:(r + 1) * sub] for r in range(n_blk)]
    ranks = [jnp.zeros(blocks[0].shape, F32) for _ in range(n_blk)]
    row_in = lax.broadcasted_iota(jnp.int32, blocks[0].shape, 0)
    for i in range(val.shape[0]):
        row = val[i:i + 1, :]
        for r in range(n_blk):
            if r * sub > i:
                ahead = jnp.where(row >= blocks[r], 1.0, 0.0)
            elif (r + 1) * sub <= i:
                ahead = jnp.where(row > blocks[r], 1.0, 0.0)
            else:
                ahead = jnp.where(row_in + r * sub > i, jnp.where(row >= blocks[r], 1.0, 0.0),
                                  jnp.where(row > blocks[r], 1.0, 0.0))
            ranks[r] = ranks[r] + ahead
    return jnp.concatenate(ranks, axis=0)


def _cmp_scores(kc_ref, qs):
    nt = (((1,), (1,)), ((), ()))
    return [lax.dot_general(kc_ref[0, h].astype(BF16), qs[h], nt, preferred_element_type=F32)
            for h in range(NSA_KV_HEADS)]


def _cmp_finish(ss, q0, q_ref, vct_ref, ovl_ref, o_ref, qa_ref, *, tq, n_top):
    m_cols = NSA_GROUP * tq
    nc = ss[0].shape[0]
    t_col = q0 + _mod_pow2(lax.broadcasted_iota(jnp.int32, (nc, m_cols), 1), tq)
    n_row = lax.broadcasted_iota(jnp.int32, (nc, m_cols), 0)
    mask = n_row * CMP_STRIDE + (CMP_BLOCK - 1) <= t_col
    has_valid = t_col[0:1, :] >= CMP_BLOCK - 1
    j_idx =lax.broadcasted_iota(jnp.int32, (MAX_SEL_BLOCKS, tq), 0)
    cur = _div_pow2(q0 + lax.broadcasted_iota(jnp.int32, (MAX_SEL_BLOCKS, tq), 1), SEL_BLOCK)
    forced = (j_idx == 0) | (j_idx == cur) | (j_idx == cur - 1)
    outs = []
    for h in range(NSA_KV_HEADS):
        sm = jnp.where(mask, ss[h], NEG)
        e = jnp.exp2(sm - jnp.max(sm, axis=0, keepdims=True))
        scale = jnp.where(has_valid, 1.0 / jnp.sum(e, axis=0, keepdims=True), 0.0)
        p = e * scale
        outs.append(jnp.dot(vct_ref[0, h, 0:HEAD_DIM, :], p.astype(BF16),
                            preferred_element_type=F32))

        psum = p[:, 0:tq] + p[:, tq:2 * tq] + p[:, 2 * tq:3 * tq]
        hi = psum.astype(BF16)
        lo = (psum - hi.astype(F32)).astype(BF16)
        imp = (jnp.dot(ovl_ref[...], hi, preferred_element_type=F32)
               + jnp.dot(ovl_ref[...], lo, preferred_element_type=F32))
        val = jnp.where(forced, jnp.inf, jnp.where(j_idx > cur, -jnp.inf, imp))
        bias_t = jnp.where(_stable_rank(val) < n_top, 0.0, NEG)
        bias = jnp.concatenate([jnp.zeros_like(bias_t), bias_t], axis=0).T
        for g in range(NSA_GROUP):
            lane0 = (h * NSA_GROUP + g) * LANES
            qa_ref[:, lane0:lane0 + LANES] = (q_ref[:, lane0:lane0 + LANES].astype(F32)
                                              + bias).astype(qa_ref.dtype)
    _store_heads_t(o_ref, outs, tq)


def _flash_kernel(q_ref, k_ref, vt_ref, tri_ref, o_ref, s_a, s_b, *, chains, groups, tq, ck):
    q0 = pl.program_id(2) * tq
    m_cols = groups * tq
    qs = [_stack_group(q_ref, h, groups) for h in range(chains)]

    def issue_scores(c, s_ref):
        k0 = pl.multiple_of(c * ck, ck)
        for h in range(chains):
            s_ref[h] = lax.dot_general(k_ref[pl.ds(k0, ck), h * LANES:(h + 1) * LANES], qs[h],
                                       (((1,), (1,)), ((), ())), preferred_element_type=F32)

    def absorb(c, s_ref, stats, diagonal=False):
        out = []
        for h in range(chains):
            s = s_ref[h]
            if diagonal:
                s = s + tri_ref[...]
            m_old, acc = stats[h]
            m_new = jnp.maximum(m_old, jnp.max(s, axis=0, keepdims=True))
            alpha = jnp.exp2(m_old - m_new)
            p = jnp.exp2(s - m_new).astype(BF16)
            vt = vt_ref[0, h, c, 0:V_ROWS, :]
            out.append((m_new, alpha * acc + jnp.dot(vt, p, preferred_element_type=F32)))
        return tuple(out)

    def finish(stats):
        _store_heads_t(o_ref, [acc[0:HEAD_DIM] / acc[HEAD_DIM:HEAD_DIM + 1] for _, acc in stats], tq)

    last = (q0 + tq - 1) // ck

    def pair(j, stats):
        c = 2 * j
        issue_scores(c + 1, s_b)
        stats = absorb(c, s_a, stats)
        issue_scores(c + 2, s_a)
        return absorb(c + 1, s_b, stats)

    issue_scores(0, s_a)
    stats = tuple((jnp.full((1, m_cols), NEG, F32), jnp.zeros((V_ROWS, m_cols), F32))
                  for _ in range(chains))
    stats = lax.fori_loop(0, last // 2, pair, stats)

    @pl.when(lax.rem(last, 2) == 0)
    def _():
        finish(absorb(last, s_a, stats, diagonal=True))

    @pl.when(lax.rem(last, 2) == 1)
    def _():
        issue_scores(last, s_b)
        finish(absorb(last, s_b, absorb(last - 1, s_a, stats), diagonal=True))


def _flash(q, k, v_t, batch, seq, groups, name):
    t, qcols = q.shape
    kv_heads = k.shape[1] // LANES
    ck = min(KV_CHUNK, seq)
    tq = min(FLASH_ROWS // groups, ck)
    assert tq == ck and seq % ck == 0
    nq = seq // tq
    n_chunks = seq // ck
    assert v_t.shape == (batch, kv_heads, n_chunks, LANES, ck)
    key = jnp.arange(ck)[:, None]
    qry = jnp.arange(groups * tq)[None, :] % tq
    tri = jnp.where(key <= qry, 0.0, NEG).astype(F32)
    chains = max(c for c in range(1, FLASH_CHAINS + 1)
                 if kv_heads % c == 0 and (c * groups) % 2 == 0)
    heads = kv_heads * groups
    return pl.pallas_call(
        functools.partial(_flash_kernel, chains=chains, groups=groups, tq=tq, ck=ck),
        grid=(batch, kv_heads // chains, nq),
        in_specs=[pl.BlockSpec((tq, chains * groups * LANES), lambda b, h, i: (b * nq + i, h)),
                  pl.BlockSpec((seq, chains * LANES), lambda b, h, i: (b, h)),
                  pl.BlockSpec((1, chains, n_chunks, LANES, ck), lambda b, h, i: (b, h, 0, 0, 0)),
                  _const_spec((ck, groups * tq))],
        out_specs=pl.BlockSpec((tq, chains * groups * HEAD_DIM), lambda b, h, i: (b * nq + i, h)),
        out_shape=jax.ShapeDtypeStruct((t, heads * HEAD_DIM), BF16),
        scratch_shapes=[pltpu.VMEM((chains, ck, groups * tq), F32)] * 2,
        compiler_params=_params(("parallel", "parallel", "parallel"), VMEM_LIMIT),
        name=name,
    )(q, k, v_t, tri)


def _nsa_local_kernel(q_ref, kc_ref, vct_ref, ovl_ref, k_ref, vt_ref, oc_ref, qa_ref, ow_ref,
                      *, tq, n_top):
    i = pl.program_id(1)
    q0 = i * tq
    n_back = WINDOW // tq
    n_span = n_back + 1
    m_cols = NSA_GROUP * tq
    qs = [_stack_group(q_ref, h, NSA_GROUP) for h in range(NSA_KV_HEADS)]
    q_pos = q0 + _mod_pow2(lax.broadcasted_iota(jnp.int32, (tq, m_cols), 1), tq)
    key_in = lax.broadcasted_iota(jnp.int32, (tq, m_cols), 0)
    nt = (((1,), (1,)), ((), ()))

    def run(c0, clamped):
        start = c0 * tq if clamped else pl.multiple_of(c0 * tq, tq)
        ss = [lax.dot_general(k_ref[pl.ds(start, n_span * tq), h * LANES:(h + 1) * LANES], qs[h],
                              nt, preferred_element_type=F32) for h in range(NSA_KV_HEADS)]
        ss_cmp = _cmp_scores(kc_ref, qs)
        outs = []
        for h in range(NSA_KV_HEADS):
            blocks = []
            for j in range(n_span):
                blk = ss[h][j * tq:(j + 1) * tq]
                key_pos = start + j * tq + key_in
                if clamped:
                    blk = jnp.where((key_pos <= q_pos) & (key_pos > q_pos - WINDOW), blk, NEG)
                elif j == 0:
                    blk = jnp.where(key_pos > q_pos - WINDOW, blk, NEG)
                elif j == n_span - 1:
                    blk = jnp.where(key_pos <= q_pos, blk, NEG)
                blocks.append(blk)
            m = functools.reduce(jnp.maximum, [jnp.max(b, axis=0, keepdims=True) for b in blocks])
            acc = 0.0
            for j in range(n_span):
                p = jnp.exp2(blocks[j] - m).astype(BF16)
                acc = acc + jnp.dot(vt_ref[0, h, c0 + j, 0:V_ROWS, :], p, preferred_element_type=F32)
            outs.append(acc[0:HEAD_DIM] / acc[HEAD_DIM:HEAD_DIM + 1])
        _store_heads_t(ow_ref, outs, tq)
        _cmp_finish(ss_cmp, q0, q_ref, vct_ref, ovl_ref, oc_ref, qa_ref, tq=tq, n_top=n_top)

    @pl.when(i < n_back)
    def _():
        run(0, True)

    @pl.when(i >= n_back)
    def _():
        run(i - n_back, False)


def _nsa_local(q, kcmp, vcmp_t, ovl, k, v_t, batch, seq, n_top):
    t = q.shape[0]
    tq = ATTN_TILE
    nq = seq // tq
    nc = kcmp.shape[2]
    assert WINDOW % tq == 0 and seq >= WINDOW + tq
    wide = NSA_HEADS * LANES
    qspec = pl.BlockSpec((tq, wide), lambda b, i: (b * nq + i, 0))
    ospec = pl.BlockSpec((tq, NSA_HEADS * HEAD_DIM), lambda b, i: (b * nq + i, 0))
    o_sds = jax.ShapeDtypeStruct((t, NSA_HEADS * HEAD_DIM), BF16)
    return pl.pallas_call(
        functools.partial(_nsa_local_kernel, tq=tq, n_top=n_top),
        grid=(batch, nq),
        in_specs=[qspec,
                  pl.BlockSpec((1, NSA_KV_HEADS, nc, LANES), lambda b, i: (b, 0, 0, 0)),
                  pl.BlockSpec((1, NSA_KV_HEADS, LANES, nc), lambda b, i: (b, 0, 0, 0)),
                  _const_spec((MAX_SEL_BLOCKS, nc)),
                  pl.BlockSpec((seq, NSA_KV_HEADS * LANES), lambda b, i: (b, 0)),
                  pl.BlockSpec((1, NSA_KV_HEADS, nq, LANES, tq), lambda b, i: (b, 0, 0, 0, 0))],
        out_specs=[ospec, qspec, ospec],
        out_shape=[o_sds, jax.ShapeDtypeStruct((t, wide), BF16), o_sds],
        compiler_params=_params(("parallel", "parallel"), VMEM_LIMIT),
        name="nsa_local",
    )(q, kcmp, vcmp_t, ovl, k, v_t)


def _mla_stage(cq, ckv, kra, krb, cm, sm, qg_ref, kg_ref, wa_ref, wb_ref, wk_ref, wvt_ref,
               q_ref, k_ref, vt_ref, scale):
    qn = _rms(cq, qg_ref[...]).astype(BF16)
    cn = _rms(ckv, kg_ref[...]).astype(BF16)
    k_rot = kra * cm + krb * sm
    cm2 = jnp.concatenate([cm, cm], axis=1)
    sm2 = jnp.concatenate([sm, sm], axis=1)
    k_rot2 = jnp.concatenate([k_rot, k_rot], axis=1)
    for h in range(0, MLA_HEADS, 2):
        sl = slice(h * LANES, (h + 2) * LANES)
        qa = jnp.dot(qn, wa_ref[:, sl], preferred_element_type=F32)
        qb = jnp.dot(qn, wb_ref[:, sl], preferred_element_type=F32)
        q_ref[:, sl] = ((qa * cm2 + qb * sm2) * scale).astype(q_ref.dtype)
        k_ref[:, sl] = (jnp.dot(cn, wk_ref[:, sl], preferred_element_type=F32) + k_rot2).astype(k_ref.dtype)
    v_t = lax.dot_general(wvt_ref[...], cn, (((1,), (1,)), ((), ())), preferred_element_type=F32)
    v_t = (v_t + _ones_row(v_t.shape)).astype(vt_ref.dtype)
    for h in range(MLA_HEADS):
        vt_ref[0, h, 0] = v_t[h * LANES:(h + 1) * LANES, :]


def _ssm_kernel(u_ref, bblk_ref, a_ref, ccat_ref, d_ref, wglu_ref, bglu_ref, o_ref,
                h_sc, st_sc, *, tc, sub, nb):
    @pl.when(pl.program_id(0) == 0)
    def _():
        st_sc[...] = jnp.zeros_like(st_sc)

    n = SSM_LANES
    ar = jnp.broadcast_to(a_ref[0:1, :], (nb, n))
    ai = jnp.broadcast_to(a_ref[1:2, :], (nb, n))
    n_sub = tc // sub
    rows = sub * nb
    us = []
    for j in range(n_sub):
        u = u_ref[j * sub:(j + 1) * sub].reshape(rows, SSM_WIDTH)
        us.append(u)
        h_sc[j * rows:(j + 1) * rows, :] = jnp.dot(u.astype(BF16), bblk_ref[...],
                                                   preferred_element_type=F32)
    hr, hi = st_sc[0], st_sc[1]
    ys = []
    for j in range(n_sub):
        for t in range(j * sub, (j + 1) * sub):
            r0 = t * nb
            nr = ar * hr - ai * hi + h_sc[r0:r0 + nb, 0:n]
            ni = ar * hi + ai * hr + h_sc[r0:r0 + nb, n:2 * n]
            h_sc[r0:r0 + nb, 0:n] = nr
            h_sc[r0:r0 + nb, n:2 * n] = ni
            hr, hi = nr, ni
        ys.append(jnp.dot(h_sc[j * rows:(j + 1) * rows, :].astype(BF16), ccat_ref[...],
                          preferred_element_type=F32) + d_ref[...] * us[j])
    st_sc[0] = hr
    st_sc[1] = hi
    z = _gelu(jnp.concatenate(ys, axis=0))
    gate = jnp.dot(z.astype(BF16), wglu_ref[...], preferred_element_type=F32) + bglu_ref[...]
    o_ref[...] = (z * _sigmoid(gate)).reshape(tc, nb, SSM_WIDTH)


def _ssm(u_t, bblk, a_rows, ccat, d_row, wglu, bglu):
    seq, nb, _ = u_t.shape
    tc = SSM_CHUNK
    n = SSM_LANES
    uspec = pl.BlockSpec((tc, nb, SSM_WIDTH), lambda i: (i, 0, 0))
    return pl.pallas_call(
        functools.partial(_ssm_kernel, tc=tc, sub=SSM_SUB, nb=nb),
        grid=(seq // tc,),
        in_specs=[uspec, _const_spec((SSM_WIDTH, 2 * n)), _const_spec((2, n)),
                  _const_spec((2 * n, SSM_WIDTH)), _const_spec((1, SSM_WIDTH)),
                  _const_spec((SSM_WIDTH, SSM_WIDTH)), _const_spec((1, SSM_WIDTH))],
        out_specs=uspec,
        out_shape=jax.ShapeDtypeStruct(u_t.shape, F32),
        scratch_shapes=[pltpu.VMEM((tc * nb, 2 * n), F32), pltpu.VMEM((2, nb, n), F32)],
        compiler_params=_params(("arbitrary",), VMEM_LIMIT),
        name="s5_scan",
    )(u_t, bblk, a_rows, ccat, d_row, wglu, bglu)


def _outproj_kernel(oc_ref, os_ref, ow_ref, gate_ref, gb_ref, ex_ref, om_ref, oz_ref, x_ref,
                    gn_ref, gm_ref, gz_ref, wn_ref, wm_ref, wz_ref, o_ref):
    g = _sigmoid(gate_ref[...] + gb_ref[...])
    g_hi = g.astype(BF16)
    g_lo = (g - g_hi.astype(F32)).astype(BF16)
    spread = jnp.dot(jnp.concatenate([g_hi, g_lo], axis=1), ex_ref[...],
                     preferred_element_type=F32)
    o_a = 0.0
    wide = oc_ref.shape[1]
    for r, br_ref in enumerate((oc_ref, os_ref, ow_ref)):
        o_a = o_a + spread[:, r * wide:(r + 1) * wide] * br_ref[...].astype(F32)
    inv = lax.rsqrt(jnp.sum(o_a * o_a, axis=-1, keepdims=True) * (1.0 / (NSA_HEADS * HEAD_DIM)) + EPS)
    acc = x_ref[...] + jnp.dot((o_a * inv * gn_ref[...]).astype(BF16), wn_ref[...],
                               preferred_element_type=F32)
    om = om_ref[...].astype(F32)
    inv = lax.rsqrt(jnp.sum(om * om, axis=-1, keepdims=True) * (1.0 / (MLA_HEADS * HEAD_DIM)) + EPS)
    acc = acc + jnp.dot((om * inv * gm_ref[...]).astype(BF16), wm_ref[...], preferred_element_type=F32)
    acc = acc + jnp.dot(_rms(oz_ref[...], gz_ref[...]).astype(BF16), wz_ref[...],
                        preferred_element_type=F32)
    o_ref[...] = acc


def _gate_spread():
    col = jnp.arange(LANES)[:, None]
    out = jnp.arange(N_BRANCH * NSA_HEADS * HEAD_DIM)[None, :]
    lane, r = out % (NSA_HEADS * HEAD_DIM), out // (NSA_HEADS * HEAD_DIM)
    hit = (col == N_BRANCH * (lane // HEAD_DIM) + r).astype(BF16)
    return jnp.concatenate([hit, hit], axis=0)


def _outproj(oc, osel, ow, gate, gate_b, om, oz, x, gn, gm, gz, wn, wm, wz):
    t, d = x.shape
    tm = ROW_TILE
    nt = oz.shape[0] // tm
    wide = NSA_HEADS * HEAD_DIM
    assert wide == MLA_HEADS * HEAD_DIM
    rows = lambda n: pl.BlockSpec((tm, n), lambda i: (i, 0))
    return pl.pallas_call(
        _outproj_kernel,
        grid=(t // tm,),
        in_specs=[rows(wide), rows(wide), rows(wide), rows(LANES), _const_spec((1, LANES)),
                  _const_spec((2 * LANES, N_BRANCH * wide)), rows(wide),
                  pl.BlockSpec((tm, SSM_WIDTH), lambda i: (i % nt, i // nt)), rows(d),
                  _const_spec((1, wide)), _const_spec((1, wide)), _const_spec((1, SSM_WIDTH)),
                  _const_spec((wide, d)), _const_spec((wide, d)), _const_spec((SSM_WIDTH, d))],
        out_specs=rows(d),
        out_shape=jax.ShapeDtypeStruct((t, d), F32),
        compiler_params=_params(("parallel",), VMEM_LIMIT),
        name="out_proj",
    )(oc, osel, ow, gate, gate_b, _gate_spread(), om, oz, x, gn, gm, gz, wn, wm, wz)


def _ffn_kernel(x_ref, g_ref, wg_ref, wv_ref, cw_ref, cb_ref, wd_ref, *rest,
                tm, tiles_per_seq, cf, down_group):
    fg_ref = rest[0] if len(rest) == 4 else None
    o_ref, carry_sc, act_sc = rest[-3:]

    @pl.when(lax.rem(pl.program_id(0), tiles_per_seq) == 0)
    def _():
        carry_sc[...] = jnp.zeros_like(carry_sc)

    x = x_ref[...]
    h = _rms(x, g_ref[...]).astype(BF16)
    row = lax.broadcasted_iota(jnp.int32, (tm, cf), 0)
    n_chunks = D_FF // cf
    acc = x

    def up(c):
        sl = slice(c * cf, (c + 1) * cf)
        return (jnp.dot(h, wg_ref[:, sl], preferred_element_type=F32),
                jnp.dot(h, wv_ref[:, sl], preferred_element_type=F32))

    nxt = up(0)
    for c in range(n_chunks):
        sl = slice(c * cf, (c + 1) * cf)
        gate, val = nxt
        if c + 1 < n_chunks:
            nxt = up(c + 1)
        tail = carry_sc[:, sl]
        p1 = tail[SUBLANES - 1:SUBLANES, :]
        p2 = tail[SUBLANES - 2:SUBLANES - 1, :]
        g1 = jnp.where(row == 0, p1, pltpu.roll(gate, 1, 0))
        g2 = jnp.where(row == 0, p2, jnp.where(row == 1, p1, pltpu.roll(gate, 2, 0)))
        carry_sc[:, sl] = gate[tm - SUBLANES:tm, :]
        gc = cw_ref[0:1, sl] * g2 + cw_ref[1:2, sl] * g1 + cw_ref[2:3, sl] * gate + cb_ref[:, sl]
        act_sc[:, sl] = (gc * _sigmoid(gc) * val).astype(BF16)
        if (c + 1) % down_group == 0 or c + 1 == n_chunks:
            lo = (c // down_group) * down_group * cf
            acc = acc + jnp.dot(act_sc[:, lo:(c + 1) * cf], wd_ref[lo:(c + 1) * cf, :],
                                preferred_element_type=F32)
    o_ref[...] = acc if fg_ref is None else _rms(acc, fg_ref[...])


def _ffn(x, g, wg, wv, cw, cb, wd, seq, final_gain=None):
    t, d = x.shape
    tm = ROW_TILE
    rows = pl.BlockSpec((tm, d), lambda i: (i, 0))
    extra = [] if final_gain is None else [final_gain]
    return pl.pallas_call(
        functools.partial(_ffn_kernel, tm=tm, tiles_per_seq=seq // tm, cf=FF_CHUNK,
                          down_group=FF_DOWN_GROUP),
        grid=(t // tm,),
        in_specs=[rows, _const_spec((1, d)), _const_spec((d, D_FF)), _const_spec((d, D_FF)),
                  _const_spec((SUBLANES, D_FF)), _const_spec((1, D_FF)), _const_spec((D_FF, d))]
        + [_const_spec((1, d))] * len(extra),
        out_specs=rows,
        out_shape=jax.ShapeDtypeStruct((t, d), F32),
        scratch_shapes=[pltpu.VMEM((SUBLANES, D_FF), F32), pltpu.VMEM((tm, D_FF), BF16)],
        compiler_params=_params(("arbitrary",), VMEM_LIMIT),
        name="conv_ffn",
    )(x, g, wg, wv, cw, cb, wd, *extra)


def _pad_heads(w, heads, width):
    lead = w.shape[:-1]
    w = w.reshape(lead + (heads, width))
    w = jnp.pad(w, [(0, 0)] * len(lead) + [(0, 0), (0, LANES - width)])
    return w.reshape(lead + (heads * LANES,))


def _inproj_weight(w):
    kv = NSA_KV_HEADS * HEAD_DIM
    sizes = (NSA_HEADS * HEAD_DIM,) + (kv,) * 6 + (NSA_HEADS * N_BRANCH, Q_LORA, KV_LORA, MLA_ROPE,
                                                   SSM_WIDTH)
    offs = [0]
    for n in sizes:
        offs.append(offs[-1] + n)
    (w_q, w_kc, w_vc, w_ks, w_vs, w_kw, w_vw, w_g, w_cq, w_ckv, w_kr, w_u) = [
        w[:, a:b] for a, b in zip(offs[:-1], offs[1:])]
    d = w.shape[0]
    half = MLA_ROPE // 2
    z64 = jnp.zeros((d, HEAD_DIM), w.dtype)
    z32 = jnp.zeros((d, LANES - HEAD_DIM - MLA_ROPE), w.dtype)
    r1, r2 = w_kr[:, :half], w_kr[:, half:]
    cols = [
        _pad_heads(w_q * (HEAD_DIM ** -0.5 * LOG2E), NSA_HEADS, HEAD_DIM),
        w_kc, w_vc,
        _pad_heads(w_ks, NSA_KV_HEADS, HEAD_DIM),
        _pad_heads(w_kw, NSA_KV_HEADS, HEAD_DIM),
        w_cq,
        jnp.pad(w_g, ((0, 0), (0, LANES - w_g.shape[1]))),
        w_ckv,
        jnp.concatenate([z64, r1, r2, z32], axis=1),
        jnp.concatenate([z64, r2, r1, z32], axis=1),
        w_u,
    ]
    w_vt = jnp.concatenate([_pad_heads(w_vs, NSA_KV_HEADS, HEAD_DIM),
                            _pad_heads(w_vw, NSA_KV_HEADS, HEAD_DIM)], axis=1).T
    return jnp.concatenate(cols, axis=1).astype(BF16), w_vt.astype(BF16)


def _mla_weights(w_uq, w_uk, w_uv):
    half = MLA_ROPE // 2
    w = w_uq.reshape(Q_LORA, MLA_HEADS, MLA_NOPE + MLA_ROPE)
    nope, r1, r2 = w[..., :MLA_NOPE], w[..., MLA_NOPE:MLA_NOPE + half], w[..., MLA_NOPE + half:]
    z32 = jnp.zeros((Q_LORA, MLA_HEADS, LANES - MLA_NOPE - MLA_ROPE), w.dtype)
    wa = jnp.concatenate([nope, r1, r2, z32], axis=-1).reshape(Q_LORA, MLA_HEADS * LANES)
    wb = jnp.concatenate([jnp.zeros_like(nope), r2, r1, z32], axis=-1).reshape(Q_LORA, MLA_HEADS * LANES)
    wk = _pad_heads(w_uk, MLA_HEADS, MLA_NOPE)
    wvt = _pad_heads(w_uv, MLA_HEADS, HEAD_DIM).T
    return wa.astype(BF16), wb.astype(BF16), wk.astype(BF16), wvt.astype(BF16)


def _ssm_weights(log_dt, a_re, a_im, b_re, b_im, c_re, c_im, d):
    dt = jnp.exp(log_dt.astype(F32))[:, None]
    ar, ai = a_re.astype(F32), a_im.astype(F32)
    mag = jnp.exp(ar * dt)
    abr, abi = mag * jnp.cos(ai * dt), mag * jnp.sin(ai * dt)
    den = ar * ar + ai * ai
    fr = ((abr - 1.0) * ar + abi * ai) / den
    fi = (abi * ar - (abr - 1.0) * ai) / den
    br, bi = b_re.astype(F32), b_im.astype(F32)
    bbr = fr[..., None] * br - fi[..., None] * bi
    bbi = fr[..., None] * bi + fi[..., None] * br
    eye = jnp.eye(SSM_GROUPS, dtype=F32)
    blk_in = lambda m: jnp.einsum("gpc,gh->gchp", m, eye).reshape(SSM_WIDTH, SSM_LANES)
    blk_out = lambda m: jnp.einsum("gcp,gh->gphc", m, eye).reshape(SSM_LANES, SSM_WIDTH)
    bblk = jnp.concatenate([blk_in(bbr), blk_in(bbi)], axis=1)
    ccat = jnp.concatenate([blk_out(c_re.astype(F32)), -blk_out(c_im.astype(F32))], axis=0)
    a_rows = jnp.stack([abr.reshape(SSM_LANES), abi.reshape(SSM_LANES)])
    return bblk.astype(BF16), a_rows, ccat.astype(BF16), d.astype(F32).reshape(1, SSM_WIDTH)


def _overlap_t(nc_pad):
    start = jnp.arange(nc_pad) * CMP_STRIDE
    lo = jnp.arange(MAX_SEL_BLOCKS) * SEL_BLOCK
    hit = (start[None, :] < lo[:, None] + SEL_BLOCK) & (start[None, :] + CMP_BLOCK > lo[:, None])
    return hit.astype(BF16)


def _rope_multipliers(positions):
    half = MLA_ROPE // 2
    per_row = LANES // half
    t = positions.size
    inv_freq = ROPE_THETA ** (-jnp.arange(half, dtype=F32) / half)
    pos = jnp.broadcast_to(positions.astype(F32).reshape(t // per_row, per_row, 1),
                           (t // per_row, per_row, half)).reshape(t // per_row, LANES)
    cos, sin, nsin = [a.reshape(t, half)
                      for a in _trig(pos, jnp.tile(inv_freq, per_row).reshape(1, LANES))]
    one = jnp.ones((t, HEAD_DIM), F32)
    zero = jnp.zeros((t, HEAD_DIM), F32)
    pad = jnp.zeros((t, LANES - HEAD_DIM - MLA_ROPE), F32)
    return (jnp.concatenate([one, cos, cos, pad], axis=1),
            jnp.concatenate([zero, nsin, sin, pad], axis=1))


def kernel(x, positions, attn_norm, w_in, nsa_pe, nsa_ck_w1, nsa_ck_b1, nsa_ck_w2, nsa_ck_b2, nsa_cv_w1, nsa_cv_b1, nsa_cv_w2, nsa_cv_b2, nsa_gate_b, mla_q_norm, mla_kv_norm, mla_w_uq, mla_w_uk, mla_w_uv, ssm_log_dt, ssm_a_re, ssm_a_im, ssm_b_re, ssm_b_im, ssm_c_re, ssm_c_im, ssm_d, ssm_w_glu, ssm_b_glu, out_norm_nsa, out_norm_mla, out_norm_ssm, w_out, ffn_norm, ffn_w_up, ffn_conv_w, ffn_conv_b, ffn_w_down, final_norm):
    batch, seq, d_model = x.shape
    depth = w_in.shape[0]
    t = batch * seq
    n_half = seq // CMP_STRIDE
    n_sel = seq // SEL_BLOCK
    assert depth >= 1 and seq % ROW_TILE == 0 and n_sel <= MAX_SEL_BLOCKS and n_half % LANES == 0
    n_top = min(SEL_TOP, n_sel)
    row = lambda v: v.astype(F32).reshape(1, -1)

    cmul, smul = _rope_multipliers(positions)
    ovl_t = _overlap_t(n_half)

    xf = x.reshape(t, d_model)
    for l in range(depth):
        (q, kc, vc, ks, kw, gate, u, vs_t, vw_t, q_m, k_m, vm_t) = _inproj(
            xf, row(attn_norm[l]), *_inproj_weight(w_in[l]), cmul, smul,
            row(mla_q_norm[l]), row(mla_kv_norm[l]),
            *_mla_weights(mla_w_uq[l], mla_w_uk[l], mla_w_uv[l]), batch, seq)

        kvcmp = _compress(
            kc, vc, batch,
            *_compress_weights(nsa_pe[l], nsa_ck_w1[l], nsa_ck_b1[l], nsa_ck_w2[l], nsa_ck_b2[l],
                               nsa_cv_w1[l], nsa_cv_b1[l], nsa_cv_w2[l], nsa_cv_b2[l]))
        vcmp_t = kvcmp[:, NSA_KV_HEADS:].transpose(0, 1, 3, 2).astype(BF16)
        o_cmp, q_aug, o_win = _nsa_local(q, kvcmp, vcmp_t, ovl_t, kw, vw_t, batch, seq, n_top)
        o_sel = _flash(q_aug, ks, vs_t, batch, seq, NSA_GROUP, "nsa_selected")

        o_mla = _flash(q_m, k_m, vm_t, batch, seq, 1, "mla_attention")

        bblk, a_rows, ccat, d_row = _ssm_weights(
            ssm_log_dt[l], ssm_a_re[l], ssm_a_im[l], ssm_b_re[l], ssm_b_im[l],
            ssm_c_re[l], ssm_c_im[l], ssm_d[l])
        o_ssm = _ssm(u.reshape(seq, batch, SSM_WIDTH), bblk, a_rows, ccat, d_row,
                     ssm_w_glu[l].astype(BF16), row(ssm_b_glu[l]))
        o_ssm = o_ssm.reshape(seq, batch * SSM_WIDTH)

        w_o = w_out[l].astype(BF16)
        n_a = NSA_HEADS * HEAD_DIM
        n_b = n_a + MLA_HEADS * HEAD_DIM
        xf = _outproj(
            o_cmp, o_sel, o_win, gate,
            jnp.pad(row(nsa_gate_b[l]), ((0, 0), (0, LANES - NSA_HEADS * N_BRANCH))),
            o_mla, o_ssm, xf,
            row(out_norm_nsa[l]), row(out_norm_mla[l]), row(out_norm_ssm[l]),
            w_o[:n_a], w_o[n_a:n_b], w_o[n_b:])

        w_up = ffn_w_up[l].astype(BF16)
        xf = _ffn(xf, row(ffn_norm[l]), w_up[:, :D_FF], w_up[:, D_FF:],
                  jnp.pad(ffn_conv_w[l].astype(F32), ((0, SUBLANES - ffn_conv_w.shape[1]), (0, 0))),
                  row(ffn_conv_b[l]), ffn_w_down[l].astype(BF16), seq,
                  final_gain=row(final_norm) if l == depth - 1 else None)

    return xf.reshape(batch, seq, d_model)
```

```python
import functools
import math

import jax
import jax.numpy as jnp
from jax import lax
from jax.experimental import pallas as pl
from jax.experimental.pallas import tpu as pltpu

F32 = jnp.float32
BF16 = jnp.bfloat16

LANES = 128
SUBLANES = 8
HEAD_DIM = 64
NSA_HEADS = 6
NSA_KV_HEADS = 2
NSA_GROUP = NSA_HEADS // NSA_KV_HEADS
N_BRANCH = 3
CMP_BLOCK = 32
CMP_STRIDE = 16
CMP_HIDDEN = 128
SEL_BLOCK = 64
SEL_TOP = 16
MAX_SEL_BLOCKS = 64
WINDOW = 512
MLA_HEADS = 6
MLA_NOPE = 64
MLA_ROPE = 32
Q_LORA = 384
KV_LORA = 128
ROPE_THETA = 10000.0
SSM_WIDTH = 256
SSM_GROUPS = 16
SSM_GROUP_CH = 16
SSM_STATE = 64
SSM_LANES = SSM_GROUPS * SSM_STATE
D_FF = 2816
EPS = 1e-6
NEG = -1e30
LOG2E = math.log2(math.e)

ROW_TILE = 512
ATTN_TILE = 256
KV_CHUNK = 512
FLASH_ROWS = 1536
FLASH_CHAINS = 6
V_ROWS = 80
SSM_CHUNK = 64
SSM_SUB = 16
FF_CHUNK = 256
FF_DOWN_GROUP = 6
VMEM_LIMIT = 56 * 1024 * 1024


def _params(sem, vmem=None):
    return pltpu.CompilerParams(dimension_semantics=sem, vmem_limit_bytes=vmem)


def _rms(x, g):
    return x * lax.rsqrt(jnp.mean(x * x, axis=-1, keepdims=True) + EPS) * g


def _gelu(x):
    c = math.sqrt(2.0 / math.pi)
    return 0.5 * x * (1.0 + jnp.tanh(c * (x + 0.044715 * (x * x * x))))


def _sigmoid(x):
    return 1.0 / (1.0 + jnp.exp(-x))


def _mod_pow2(x, n):
    assert n & (n - 1) == 0
    return jnp.bitwise_and(x, n - 1)


def _div_pow2(x, n):
    assert n & (n - 1) == 0
    return jnp.right_shift(x, n.bit_length() - 1)


def _const_spec(shape):
    nd = len(shape)
    return pl.BlockSpec(shape, lambda *_: (0,) * nd)


def _trig_kernel(pos_ref, invf_ref, c_ref, s_ref, ns_ref):
    ang = pos_ref[...] * invf_ref[...]
    sin = jnp.sin(ang)
    c_ref[...] = jnp.cos(ang)
    s_ref[...] = sin
    ns_ref[...] = -sin


def _trig(pos_rows, invf_row):
    rows = pos_rows.shape[0]
    tile = min(ROW_TILE, rows)
    spec = pl.BlockSpec((tile, LANES), lambda i: (i, 0))
    return pl.pallas_call(
        _trig_kernel,
        grid=(rows // tile,),
        in_specs=[spec, _const_spec((1, LANES))],
        out_specs=[spec] * 3,
        out_shape=[jax.ShapeDtypeStruct((rows, LANES), F32)] * 3,
        compiler_params=_params(("parallel",)),
        name="rope_trig",
    )(pos_rows, invf_row)


_IN_SEGS = (
    ("q", NSA_HEADS * LANES, BF16),
    ("kc", NSA_KV_HEADS * HEAD_DIM, F32),
    ("vc", NSA_KV_HEADS * HEAD_DIM, F32),
    ("ks", NSA_KV_HEADS * LANES, BF16),
    ("kw", NSA_KV_HEADS * LANES, BF16),
    ("cq", Q_LORA, None),
    ("gate", LANES, F32),
    ("ckv", KV_LORA, None),
    ("kra", LANES, None),
    ("krb", LANES, None),
    ("u", SSM_WIDTH, F32),
)
_IN_GROUPS = ((0,), (1, 2), (3,), (4,), (5, 6), (7, 8), (9,), (10,))
_IN_COLS = sum(n for _, n, _ in _IN_SEGS)
_IN_OUTS = tuple((name, n, dt) for name, n, dt in _IN_SEGS if dt is not None)
_VT_ROWS = 2 * NSA_KV_HEADS * LANES


def _ones_row(shape):
    row = _mod_pow2(lax.broadcasted_iota(jnp.int32, shape, 0), LANES)
    return jnp.where(row == HEAD_DIM, 1.0, 0.0)


def _inproj_kernel(x_ref, g_ref, w_ref, wvt_ref, cm_ref, sm_ref, qg_ref, kg_ref,
                   wa_ref, wb_ref, wk_ref, wmvt_ref, *o_refs, seq, tm, mla_scale):
    h = _rms(x_ref[...], g_ref[...]).astype(BF16)
    outs = dict(zip([name for name, _, _ in _IN_OUTS], o_refs))
    latent = {}
    off = 0
    for group in _IN_GROUPS:
        width = sum(_IN_SEGS[s][1] for s in group)
        y_all = jnp.dot(h, w_ref[:, off:off + width], preferred_element_type=F32)
        off += width
        lo = 0
        for s in group:
            name, n, dt = _IN_SEGS[s]
            y = y_all[:, lo:lo + n]
            lo += n
            if name == "ks":
                s0 = lax.rem(pl.program_id(0) * tm, seq)
                blk = _div_pow2(s0 + lax.broadcasted_iota(jnp.int32, (tm, n), 0), SEL_BLOCK)
                lane = _mod_pow2(lax.broadcasted_iota(jnp.int32, (tm, n), 1), LANES)
                y = y + jnp.where(lane - HEAD_DIM == blk, 1.0, 0.0)
            if dt is None:
                latent[name] = y
            else:
                outs[name][...] = y.astype(dt)
    vst_ref, vwt_ref, qm_ref, km_ref, vmt_ref = o_refs[len(_IN_OUTS):]
    _mla_stage(latent["cq"], latent["ckv"], latent["kra"], latent["krb"], cm_ref[...], sm_ref[...],
               qg_ref, kg_ref, wa_ref, wb_ref, wk_ref, wmvt_ref, qm_ref, km_ref, vmt_ref, mla_scale)
    v_t = lax.dot_general(wvt_ref[...], h, (((1,), (1,)), ((), ())), preferred_element_type=F32)
    v_t = (v_t + _ones_row(v_t.shape)).astype(BF16)
    wc = vwt_ref.shape[-1]
    for hh in range(NSA_KV_HEADS):
        vst_ref[0, hh, 0] = v_t[hh * LANES:(hh + 1) * LANES, :]
        r0 = (NSA_KV_HEADS + hh) * LANES
        for c in range(tm // wc):
            vwt_ref[0, hh, c] = v_t[r0:r0 + LANES, c * wc:(c + 1) * wc]


def _inproj(x, g, w, wvt, cmul, smul, qg, kg, wa, wb, wk, wmvt, batch, seq):
    t, d = x.shape
    tm = ROW_TILE
    assert tm == min(KV_CHUNK, seq) and tm % ATTN_TILE == 0
    nt = seq // tm
    rows = lambda n: pl.BlockSpec((tm, n), lambda i: (i, 0))
    chunked = lambda heads, per_tile, ck: pl.BlockSpec((1, heads, per_tile, LANES, ck),
                                                       lambda i: (i // nt, 0, i % nt, 0, 0))
    out_specs = [rows(n) for _, n, _ in _IN_OUTS]
    out_shape = [jax.ShapeDtypeStruct((t, n), dt) for _, n, dt in _IN_OUTS]
    out_specs[-1] = pl.BlockSpec((tm, SSM_WIDTH), lambda i: (i % nt, i // nt))
    out_shape[-1] = jax.ShapeDtypeStruct((seq, batch * SSM_WIDTH), F32)
    mla_wide = MLA_HEADS * LANES
    out_specs += [chunked(NSA_KV_HEADS, 1, tm), chunked(NSA_KV_HEADS, tm // ATTN_TILE, ATTN_TILE),
                  rows(mla_wide), rows(mla_wide), chunked(MLA_HEADS, 1, tm)]
    out_shape += [jax.ShapeDtypeStruct((batch, NSA_KV_HEADS, nt, LANES, tm), BF16),
                  jax.ShapeDtypeStruct((batch, NSA_KV_HEADS, seq // ATTN_TILE, LANES, ATTN_TILE), BF16),
                  jax.ShapeDtypeStruct((t, mla_wide), BF16), jax.ShapeDtypeStruct((t, mla_wide), BF16),
                  jax.ShapeDtypeStruct((batch, MLA_HEADS, nt, LANES, tm), BF16)]
    mla_scale = (MLA_NOPE + MLA_ROPE) ** -0.5 * LOG2E
    return pl.pallas_call(
        functools.partial(_inproj_kernel, seq=seq, tm=tm, mla_scale=mla_scale),
        grid=(t // tm,),
        in_specs=[pl.BlockSpec((tm, d), lambda i: (i, 0)), _const_spec((1, d)),
                  _const_spec((d, _IN_COLS)), _const_spec((_VT_ROWS, d)),
                  rows(LANES), rows(LANES), _const_spec((1, Q_LORA)), _const_spec((1, KV_LORA)),
                  _const_spec((Q_LORA, mla_wide)), _const_spec((Q_LORA, mla_wide)),
                  _const_spec((KV_LORA, mla_wide)), _const_spec((mla_wide, KV_LORA))],
        out_specs=out_specs,
        out_shape=out_shape,
        compiler_params=_params(("parallel",), VMEM_LIMIT),
        name="in_proj",
    )(x, g, w, wvt, cmul, smul, qg, kg, wa, wb, wk, wmvt)


_CMP_STREAMS = 2 * NSA_KV_HEADS


def _compress_kernel(kc_ref, vc_ref, pe_ref, w1_ref, b1_ref, w2_ref, b2_ref, o_ref):
    n = kc_ref.shape[0] // CMP_STRIDE
    x = jnp.concatenate([r[pl.ds(l, n, stride=CMP_STRIDE), :]
                         for l in range(CMP_STRIDE) for r in (kc_ref, vc_ref)], axis=1)
    top = jnp.dot((x + pe_ref[0:1, :]).astype(BF16), w1_ref[0], preferred_element_type=F32)
    bot = jnp.dot((x + pe_ref[1:2, :]).astype(BF16), w1_ref[1], preferred_element_type=F32)
    hid = _gelu(top + pltpu.roll(bot, n - 1, 0) + b1_ref[...]).astype(BF16)
    for j in range(_CMP_STREAMS):
        kind = j // NSA_KV_HEADS
        o_ref[0, j] = (jnp.dot(hid[:, j * CMP_HIDDEN:(j + 1) * CMP_HIDDEN], w2_ref[kind],
                               preferred_element_type=F32) + b2_ref[kind])


def _compress(kc, vc, batch, pe, w1, b1, w2, b2):
    b = batch
    seq = kc.shape[0] // b
    n = seq // CMP_STRIDE
    width = CMP_STRIDE * (kc.shape[1] + vc.shape[1])
    hid = _CMP_STREAMS * CMP_HIDDEN
    xspec = pl.BlockSpec((seq, kc.shape[1]), lambda i: (i, 0))
    return pl.pallas_call(
        _compress_kernel,
        grid=(b,),
        in_specs=[xspec, xspec,
                  _const_spec((2, width)), _const_spec((2, width, hid)), _const_spec((1, hid)),
                  _const_spec((2, CMP_HIDDEN, LANES)), _const_spec((2, 1, LANES))],
        out_specs=pl.BlockSpec((1, _CMP_STREAMS, n, LANES), lambda i: (i, 0, 0, 0)),
        out_shape=jax.ShapeDtypeStruct((b, _CMP_STREAMS, n, LANES), F32),
        compiler_params=_params(("parallel",), VMEM_LIMIT),
        name="nsa_compress",
    )(kc, vc, pe, w1, b1, w2, b2)


def _compress_weights(pe, ck_w1, ck_b1, ck_w2, ck_b2, cv_w1, cv_b1, cv_w2, cv_b2):
    eye = jnp.eye(_CMP_STREAMS, dtype=F32)
    per_stream = jnp.stack([ck_w1, ck_w1, cv_w1, cv_w1]).astype(F32)

    def place(w):
        return jnp.einsum("jldf,jk->ljdkf", w, eye).reshape(
            CMP_STRIDE * _CMP_STREAMS * HEAD_DIM, _CMP_STREAMS * CMP_HIDDEN)

    w1 = jnp.stack([place(per_stream[:, :CMP_STRIDE]), place(per_stream[:, CMP_STRIDE:])])
    spread = lambda p: jnp.broadcast_to(p[:, None, :], (CMP_STRIDE, _CMP_STREAMS, HEAD_DIM)).reshape(-1)
    pe2 = jnp.stack([spread(pe[:CMP_STRIDE]), spread(pe[CMP_STRIDE:])]).astype(F32)
    b1 = jnp.concatenate([ck_b1, ck_b1, cv_b1, cv_b1]).astype(F32).reshape(1, -1)
    pad2 = lambda w: jnp.pad(w.astype(F32), ((0, 0), (0, LANES - HEAD_DIM)))
    w2 = jnp.stack([pad2(ck_w2), pad2(cv_w2)]).astype(BF16)
    b2 = jnp.stack([pad2(ck_b2.reshape(1, -1)), pad2(cv_b2.reshape(1, -1))])
    return pe2, w1.astype(BF16), b1, w2, b2


def _stack_group(q_ref, h, groups):
    return jnp.concatenate([q_ref[:, (h * groups + g) * LANES:(h * groups + g + 1) * LANES]
                            for g in range(groups)], axis=0)


def _store_heads_t(o_ref, chains_t, tq):
    heads = [c[:, g * tq:(g + 1) * tq] for c in chains_t for g in range(c.shape[1] // tq)]
    for p in range(len(heads) // 2):
        pair = jnp.concatenate([heads[2 * p], heads[2 * p + 1]], axis=0)
        o_ref[:, p * LANES:(p + 1) * LANES] = pair.T.astype(o_ref.dtype)


def _stable_rank(val):
    sub = SUBLANES
    n_blk = val.shape[0] // sub
    blocks = [val[r * sub:(r + 1) * sub] for r in range(n_blk)]
    ranks = [jnp.zeros(blocks[0].shape, F32) for _ in range(n_blk)]
    row_in = lax.broadcasted_iota(jnp.int32, blocks[0].shape, 0)
    for i in range(val.shape[0]):
        row = val[i:i + 1, :]
        for r in range(n_blk):
            if r * sub > i:
                ahead = jnp.where(row >= blocks[r], 1.0, 0.0)
            elif (r + 1) * sub <= i:
                ahead = jnp.where(row > blocks[r], 1.0, 0.0)
            else:
                ahead = jnp.where(row_in + r * sub > i, jnp.where(row >= blocks[r], 1.0, 0.0),
                                  jnp.where(row > blocks[r], 1.0, 0.0))
            ranks[r] = ranks[r] + ahead
    return jnp.concatenate(ranks, axis=0)


def _cmp_scores(kc_ref, qs):
    nt = (((1,), (1,)), ((), ()))
    return [lax.dot_general(kc_ref[0, h].astype(BF16), qs[h], nt, preferred_element_type=F32)
            for h in range(NSA_KV_HEADS)]


def _cmp_finish(ss, q0, q_ref, vct_ref, ovl_ref, o_ref, qa_ref, *, tq, n_top):
    m_cols = NSA_GROUP * tq
    nc = ss[0].shape[0]
    t_col = q0 + _mod_pow2(lax.broadcasted_iota(jnp.int32, (nc, m_cols), 1), tq)
    n_row = lax.broadcasted_iota(jnp.int32, (nc, m_cols), 0)
    mask = n_row * CMP_STRIDE + (CMP_BLOCK - 1) <= t_col
    has_valid = t_col[0:1, :] >= CMP_BLOCK - 1
    j_idx =lax.broadcasted_iota(jnp.int32, (MAX_SEL_BLOCKS, tq), 0)
    cur = _div_pow2(q0 + lax.broadcasted_iota(jnp.int32, (MAX_SEL_BLOCKS, tq), 1), SEL_BLOCK)
    forced = (j_idx == 0) | (j_idx == cur) | (j_idx == cur - 1)
    outs = []
    for h in range(NSA_KV_HEADS):
        sm = jnp.where(mask, ss[h], NEG)
        e = jnp.exp2(sm - jnp.max(sm, axis=0, keepdims=True))
        scale = jnp.where(has_valid, 1.0 / jnp.sum(e, axis=0, keepdims=True), 0.0)
        p = e * scale
        outs.append(jnp.dot(vct_ref[0, h, 0:HEAD_DIM, :], p.astype(BF16),
                            preferred_element_type=F32))

        psum = p[:, 0:tq] + p[:, tq:2 * tq] + p[:, 2 * tq:3 * tq]
        hi = psum.astype(BF16)
        lo = (psum - hi.astype(F32)).astype(BF16)
        imp = (jnp.dot(ovl_ref[...], hi, preferred_element_type=F32)
               + jnp.dot(ovl_ref[...], lo, preferred_element_type=F32))
        val = jnp.where(forced, jnp.inf, jnp.where(j_idx > cur, -jnp.inf, imp))
        bias_t = jnp.where(_stable_rank(val) < n_top, 0.0, NEG)
        bias = jnp.concatenate([jnp.zeros_like(bias_t), bias_t], axis=0).T
        for g in range(NSA_GROUP):
            lane0 = (h * NSA_GROUP + g) * LANES
            qa_ref[:, lane0:lane0 + LANES] = (q_ref[:, lane0:lane0 + LANES].astype(F32)
                                              + bias).astype(qa_ref.dtype)
    _store_heads_t(o_ref, outs, tq)


def _flash_kernel(q_ref, k_ref, vt_ref, o_ref, s_a, s_b, *, chains, groups, tq, ck):
    q0 = pl.program_id(2) * tq
    m_cols = groups * tq
    qs = [_stack_group(q_ref, h, groups) for h in range(chains)]

    def issue_scores(c, s_ref):
        k0 = pl.multiple_of(c * ck, ck)
        for h in range(chains):
            s_ref[h] = lax.dot_general(k_ref[pl.ds(k0, ck), h * LANES:(h + 1) * LANES], qs[h],
                                       (((1,), (1,)), ((), ())), preferred_element_type=F32)

    def absorb(c, s_ref, stats, visible=None):
        out = []
        for h in range(chains):
            s = s_ref[h]
            if visible is not None:
                s = jnp.where(visible, s, NEG)
            m_old, acc = stats[h]
            m_new = jnp.maximum(m_old, jnp.max(s, axis=0, keepdims=True))
            alpha = jnp.exp2(m_old - m_new)
            p = jnp.exp2(s - m_new).astype(BF16)
            vt = vt_ref[0, h, c, 0:V_ROWS, :]
            out.append((m_new, alpha * acc + jnp.dot(vt, p, preferred_element_type=F32)))
        return tuple(out)

    def finish(stats):
        _store_heads_t(o_ref, [acc[0:HEAD_DIM] / acc[HEAD_DIM:HEAD_DIM + 1] for _, acc in stats], tq)

    last = (q0 + tq - 1) // ck
    shift = 0 if tq == ck else last * ck - q0
    visible = (shift + lax.broadcasted_iota(jnp.int32, (ck, m_cols), 0)
               <= _mod_pow2(lax.broadcasted_iota(jnp.int32, (ck, m_cols), 1), tq))

    def pair(j, stats):
        c = 2 * j
        issue_scores(c + 1, s_b)
        stats = absorb(c, s_a, stats)
        issue_scores(c + 2, s_a)
        return absorb(c + 1, s_b, stats)

    issue_scores(0, s_a)
    stats = tuple((jnp.full((1, m_cols), NEG, F32), jnp.zeros((V_ROWS, m_cols), F32))
                  for _ in range(chains))
    stats = lax.fori_loop(0, last // 2, pair, stats)

    @pl.when(lax.rem(last, 2) == 0)
    def _():
        finish(absorb(last, s_a, stats, visible))

    @pl.when(lax.rem(last, 2) == 1)
    def _():
        issue_scores(last, s_b)
        finish(absorb(last, s_b, absorb(last - 1, s_a, stats), visible))


def _flash(q, k, v_t, batch, seq, groups, name):
    t, qcols = q.shape
    kv_heads = k.shape[1] // LANES
    ck = min(KV_CHUNK, seq)
    tq = min(FLASH_ROWS // groups, ck)
    assert tq & (tq - 1) == 0 and ck % tq == 0 and seq % ck == 0
    nq = seq // tq
    n_chunks = seq // ck
    assert v_t.shape == (batch, kv_heads, n_chunks, LANES, ck)
    chains = max(c for c in range(1, FLASH_CHAINS + 1)
                 if kv_heads % c == 0 and (c * groups) % 2 == 0)
    heads = kv_heads * groups
    return pl.pallas_call(
        functools.partial(_flash_kernel, chains=chains, groups=groups, tq=tq, ck=ck),
        grid=(batch, kv_heads // chains, nq),
        in_specs=[pl.BlockSpec((tq, chains * groups * LANES), lambda b, h, i: (b * nq + i, h)),
                  pl.BlockSpec((seq, chains * LANES), lambda b, h, i: (b, h)),
                  pl.BlockSpec((1, chains, n_chunks, LANES, ck), lambda b, h, i: (b, h, 0, 0, 0))],
        out_specs=pl.BlockSpec((tq, chains * groups * HEAD_DIM), lambda b, h, i: (b * nq + i, h)),
        out_shape=jax.ShapeDtypeStruct((t, heads * HEAD_DIM), BF16),
        scratch_shapes=[pltpu.VMEM((chains, ck, groups * tq), F32)] * 2,
        compiler_params=_params(("parallel", "parallel", "parallel"), VMEM_LIMIT),
        name=name,
    )(q, k, v_t)


def _nsa_local_kernel(q_ref, kc_ref, vct_ref, ovl_ref, k_ref, vt_ref, oc_ref, qa_ref, ow_ref,
                      *, tq, n_top):
    i = pl.program_id(1)
    q0 = i * tq
    n_back = WINDOW // tq
    n_span = n_back + 1
    m_cols = NSA_GROUP * tq
    qs = [_stack_group(q_ref, h, NSA_GROUP) for h in range(NSA_KV_HEADS)]
    q_in = _mod_pow2(lax.broadcasted_iota(jnp.int32, (tq, m_cols), 1), tq)
    q_pos = q0 + q_in
    key_in = lax.broadcasted_iota(jnp.int32, (tq, m_cols), 0)
    nt = (((1,), (1,)), ((), ()))

    def run(c0, clamped):
        start = c0 * tq if clamped else pl.multiple_of(c0 * tq, tq)
        ss = [lax.dot_general(k_ref[pl.ds(start, n_span * tq), h * LANES:(h + 1) * LANES], qs[h],
                              nt, preferred_element_type=F32) for h in range(NSA_KV_HEADS)]
        ss_cmp = _cmp_scores(kc_ref, qs)
        outs = []
        for h in range(NSA_KV_HEADS):
            blocks = []
            for j in range(n_span):
                blk = ss[h][j * tq:(j + 1) * tq]
                if clamped:
                    key_pos = start + j * tq + key_in
                    blk = jnp.where((key_pos <= q_pos) & (key_pos > q_pos - WINDOW), blk, NEG)
                elif j == 0:
                    blk = jnp.where(key_in > q_in, blk, NEG)
                elif j == n_span - 1:
                    blk = jnp.where(key_in <= q_in, blk, NEG)
                blocks.append(blk)
            m = functools.reduce(jnp.maximum, [jnp.max(b, axis=0, keepdims=True) for b in blocks])
            acc = 0.0
            for j in range(n_span):
                p = jnp.exp2(blocks[j] - m).astype(BF16)
                acc = acc + jnp.dot(vt_ref[0, h, c0 + j, 0:V_ROWS, :], p, preferred_element_type=F32)
            outs.append(acc[0:HEAD_DIM] / acc[HEAD_DIM:HEAD_DIM + 1])
        _store_heads_t(ow_ref, outs, tq)
        _cmp_finish(ss_cmp, q0, q_ref, vct_ref, ovl_ref, oc_ref, qa_ref, tq=tq, n_top=n_top)

    @pl.when(i < n_back)
    def _():
        run(0, True)

    @pl.when(i >= n_back)
    def _():
        run(i - n_back, False)


def _nsa_local(q, kcmp, vcmp_t, ovl, k, v_t, batch, seq, n_top):
    t = q.shape[0]
    tq = ATTN_TILE
    nq = seq // tq
    nc = kcmp.shape[2]
    assert WINDOW % tq == 0 and seq >= WINDOW + tq
    wide = NSA_HEADS * LANES
    qspec = pl.BlockSpec((tq, wide), lambda b, i: (b * nq + i, 0))
    ospec = pl.BlockSpec((tq, NSA_HEADS * HEAD_DIM), lambda b, i: (b * nq + i, 0))
    o_sds = jax.ShapeDtypeStruct((t, NSA_HEADS * HEAD_DIM), BF16)
    return pl.pallas_call(
        functools.partial(_nsa_local_kernel, tq=tq, n_top=n_top),
        grid=(batch, nq),
        in_specs=[qspec,
                  pl.BlockSpec((1, NSA_KV_HEADS, nc, LANES), lambda b, i: (b, 0, 0, 0)),
                  pl.BlockSpec((1, NSA_KV_HEADS, LANES, nc), lambda b, i: (b, 0, 0, 0)),
                  _const_spec((MAX_SEL_BLOCKS, nc)),
                  pl.BlockSpec((seq, NSA_KV_HEADS * LANES), lambda b, i: (b, 0)),
                  pl.BlockSpec((1, NSA_KV_HEADS, nq, LANES, tq), lambda b, i: (b, 0, 0, 0, 0))],
        out_specs=[ospec, qspec, ospec],
        out_shape=[o_sds, jax.ShapeDtypeStruct((t, wide), BF16), o_sds],
        compiler_params=_params(("parallel", "parallel"), VMEM_LIMIT),
        name="nsa_local",
    )(q, kcmp, vcmp_t, ovl, k, v_t)


def _mla_stage(cq, ckv, kra, krb, cm, sm, qg_ref, kg_ref, wa_ref, wb_ref, wk_ref, wvt_ref,
               q_ref, k_ref, vt_ref, scale):
    qn = _rms(cq, qg_ref[...]).astype(BF16)
    cn = _rms(ckv, kg_ref[...]).astype(BF16)
    k_rot = kra * cm + krb * sm
    cm2 = jnp.concatenate([cm, cm], axis=1)
    sm2 = jnp.concatenate([sm, sm], axis=1)
    k_rot2 = jnp.concatenate([k_rot, k_rot], axis=1)
    for h in range(0, MLA_HEADS, 2):
        sl = slice(h * LANES, (h + 2) * LANES)
        qa = jnp.dot(qn, wa_ref[:, sl], preferred_element_type=F32)
        qb = jnp.dot(qn, wb_ref[:, sl], preferred_element_type=F32)
        q_ref[:, sl] = ((qa * cm2 + qb * sm2) * scale).astype(q_ref.dtype)
        k_ref[:, sl] = (jnp.dot(cn, wk_ref[:, sl], preferred_element_type=F32) + k_rot2).astype(k_ref.dtype)
    v_t = lax.dot_general(wvt_ref[...], cn, (((1,), (1,)), ((), ())), preferred_element_type=F32)
    v_t = (v_t + _ones_row(v_t.shape)).astype(vt_ref.dtype)
    for h in range(MLA_HEADS):
        vt_ref[0, h, 0] = v_t[h * LANES:(h + 1) * LANES, :]


def _ssm_kernel(u_ref, bblk_ref, a_ref, ccat_ref, d_ref, wglu_ref, bglu_ref, o_ref,
                h_sc, st_sc, *, tc, sub, nb):
    @pl.when(pl.program_id(0) == 0)
    def _():
        st_sc[...] = jnp.zeros_like(st_sc)

    n = SSM_LANES
    ar = jnp.broadcast_to(a_ref[0:1, :], (nb, n))
    ai = jnp.broadcast_to(a_ref[1:2, :], (nb, n))
    n_sub = tc // sub
    rows = sub * nb
    us = []
    for j in range(n_sub):
        u = u_ref[j * sub:(j + 1) * sub].reshape(rows, SSM_WIDTH)
        us.append(u)
        h_sc[j * rows:(j + 1) * rows, :] = jnp.dot(u.astype(BF16), bblk_ref[...],
                                                   preferred_element_type=F32)
    hr, hi = st_sc[0], st_sc[1]
    ys = []
    for j in range(n_sub):
        for t in range(j * sub, (j + 1) * sub):
            r0 = t * nb
            nr = ar * hr - ai * hi + h_sc[r0:r0 + nb, 0:n]
            ni = ar * hi + ai * hr + h_sc[r0:r0 + nb, n:2 * n]
            h_sc[r0:r0 + nb, 0:n] = nr
            h_sc[r0:r0 + nb, n:2 * n] = ni
            hr, hi = nr, ni
        ys.append(jnp.dot(h_sc[j * rows:(j + 1) * rows, :].astype(BF16), ccat_ref[...],
                          preferred_element_type=F32) + d_ref[...] * us[j])
    st_sc[0] = hr
    st_sc[1] = hi
    z = _gelu(jnp.concatenate(ys, axis=0))
    gate = jnp.dot(z.astype(BF16), wglu_ref[...], preferred_element_type=F32) + bglu_ref[...]
    o_ref[...] = (z * _sigmoid(gate)).reshape(tc, nb, SSM_WIDTH)


def _ssm(u_t, bblk, a_rows, ccat, d_row, wglu, bglu):
    seq, nb, _ = u_t.shape
    tc = SSM_CHUNK
    n = SSM_LANES
    uspec = pl.BlockSpec((tc, nb, SSM_WIDTH), lambda i: (i, 0, 0))
    return pl.pallas_call(
        functools.partial(_ssm_kernel, tc=tc, sub=SSM_SUB, nb=nb),
        grid=(seq // tc,),
        in_specs=[uspec, _const_spec((SSM_WIDTH, 2 * n)), _const_spec((2, n)),
                  _const_spec((2 * n, SSM_WIDTH)), _const_spec((1, SSM_WIDTH)),
                  _const_spec((SSM_WIDTH, SSM_WIDTH)), _const_spec((1, SSM_WIDTH))],
        out_specs=uspec,
        out_shape=jax.ShapeDtypeStruct(u_t.shape, F32),
        scratch_shapes=[pltpu.VMEM((tc * nb, 2 * n), F32), pltpu.VMEM((2, nb, n), F32)],
        compiler_params=_params(("arbitrary",), VMEM_LIMIT),
        name="s5_scan",
    )(u_t, bblk, a_rows, ccat, d_row, wglu, bglu)


def _outproj_kernel(oc_ref, os_ref, ow_ref, gate_ref, gb_ref, ex_ref, om_ref, oz_ref, x_ref,
                    gn_ref, gm_ref, gz_ref, wn_ref, wm_ref, wz_ref, o_ref):
    g = _sigmoid(gate_ref[...] + gb_ref[...])
    g_hi = g.astype(BF16)
    g_lo = (g - g_hi.astype(F32)).astype(BF16)
    spread = jnp.dot(jnp.concatenate([g_hi, g_lo], axis=1), ex_ref[...],
                     preferred_element_type=F32)
    o_a = 0.0
    wide = oc_ref.shape[1]
    for r, br_ref in enumerate((oc_ref, os_ref, ow_ref)):
        o_a = o_a + spread[:, r * wide:(r + 1) * wide] * br_ref[...].astype(F32)
    inv = lax.rsqrt(jnp.sum(o_a * o_a, axis=-1, keepdims=True) * (1.0 / (NSA_HEADS * HEAD_DIM)) + EPS)
    acc = x_ref[...] + jnp.dot((o_a * inv * gn_ref[...]).astype(BF16), wn_ref[...],
                               preferred_element_type=F32)
    om = om_ref[...].astype(F32)
    inv = lax.rsqrt(jnp.sum(om * om, axis=-1, keepdims=True) * (1.0 / (MLA_HEADS * HEAD_DIM)) + EPS)
    acc = acc + jnp.dot((om * inv * gm_ref[...]).astype(BF16), wm_ref[...], preferred_element_type=F32)
    acc = acc + jnp.dot(_rms(oz_ref[...], gz_ref[...]).astype(BF16), wz_ref[...],
                        preferred_element_type=F32)
    o_ref[...] = acc


def _gate_spread():
    col = jnp.arange(LANES)[:, None]
    out = jnp.arange(N_BRANCH * NSA_HEADS * HEAD_DIM)[None, :]
    lane, r = out % (NSA_HEADS * HEAD_DIM), out // (NSA_HEADS * HEAD_DIM)
    hit = (col == N_BRANCH * (lane // HEAD_DIM) + r).astype(BF16)
    return jnp.concatenate([hit, hit], axis=0)


def _outproj(oc, osel, ow, gate, gate_b, om, oz, x, gn, gm, gz, wn, wm, wz):
    t, d = x.shape
    tm = ROW_TILE
    nt = oz.shape[0] // tm
    wide = NSA_HEADS * HEAD_DIM
    assert wide == MLA_HEADS * HEAD_DIM
    rows = lambda n: pl.BlockSpec((tm, n), lambda i: (i, 0))
    return pl.pallas_call(
        _outproj_kernel,
        grid=(t // tm,),
        in_specs=[rows(wide), rows(wide), rows(wide), rows(LANES), _const_spec((1, LANES)),
                  _const_spec((2 * LANES, N_BRANCH * wide)), rows(wide),
                  pl.BlockSpec((tm, SSM_WIDTH), lambda i: (i % nt, i // nt)), rows(d),
                  _const_spec((1, wide)), _const_spec((1, wide)), _const_spec((1, SSM_WIDTH)),
                  _const_spec((wide, d)), _const_spec((wide, d)), _const_spec((SSM_WIDTH, d))],
        out_specs=rows(d),
        out_shape=jax.ShapeDtypeStruct((t, d), F32),
        compiler_params=_params(("parallel",), VMEM_LIMIT),
        name="out_proj",
    )(oc, osel, ow, gate, gate_b, _gate_spread(), om, oz, x, gn, gm, gz, wn, wm, wz)


def _ffn_kernel(x_ref, g_ref, wg_ref, wv_ref, cw_ref, cb_ref, wd_ref, *rest,
                tm, tiles_per_seq, cf, down_group):
    fg_ref = rest[0] if len(rest) == 4 else None
    o_ref, carry_sc, act_sc = rest[-3:]

    @pl.when(lax.rem(pl.program_id(0), tiles_per_seq) == 0)
    def _():
        carry_sc[...] = jnp.zeros_like(carry_sc)

    x = x_ref[...]
    h = _rms(x, g_ref[...]).astype(BF16)
    row = lax.broadcasted_iota(jnp.int32, (tm, cf), 0)
    n_chunks = D_FF // cf
    acc = x

    def up(c):
        sl = slice(c * cf, (c + 1) * cf)
        return (jnp.dot(h, wg_ref[:, sl], preferred_element_type=F32),
                jnp.dot(h, wv_ref[:, sl], preferred_element_type=F32))

    nxt = up(0)
    for c in range(n_chunks):
        sl = slice(c * cf, (c + 1) * cf)
        gate, val = nxt
        if c + 1 < n_chunks:
            nxt = up(c + 1)
        tail = carry_sc[:, sl]
        p1 = tail[SUBLANES - 1:SUBLANES, :]
        p2 = tail[SUBLANES - 2:SUBLANES - 1, :]
        g1 = jnp.where(row == 0, p1, pltpu.roll(gate, 1, 0))
        g2 = jnp.where(row == 0, p2, jnp.where(row == 1, p1, pltpu.roll(gate, 2, 0)))
        carry_sc[:, sl] = gate[tm - SUBLANES:tm, :]
        gc = cw_ref[0:1, sl] * g2 + cw_ref[1:2, sl] * g1 + cw_ref[2:3, sl] * gate + cb_ref[:, sl]
        act_sc[:, sl] = (gc * _sigmoid(gc) * val).astype(BF16)
        if (c + 1) % down_group == 0 or c + 1 == n_chunks:
            lo = (c // down_group) * down_group * cf
            acc = acc + jnp.dot(act_sc[:, lo:(c + 1) * cf], wd_ref[lo:(c + 1) * cf, :],
                                preferred_element_type=F32)
    o_ref[...] = acc if fg_ref is None else _rms(acc, fg_ref[...])


def _ffn(x, g, wg, wv, cw, cb, wd, seq, final_gain=None):
    t, d = x.shape
    tm = ROW_TILE
    rows = pl.BlockSpec((tm, d), lambda i: (i, 0))
    extra = [] if final_gain is None else [final_gain]
    return pl.pallas_call(
        functools.partial(_ffn_kernel, tm=tm, tiles_per_seq=seq // tm, cf=FF_CHUNK,
                          down_group=FF_DOWN_GROUP),
        grid=(t // tm,),
        in_specs=[rows, _const_spec((1, d)), _const_spec((d, D_FF)), _const_spec((d, D_FF)),
                  _const_spec((SUBLANES, D_FF)), _const_spec((1, D_FF)), _const_spec((D_FF, d))]
        + [_const_spec((1, d))] * len(extra),
        out_specs=rows,
        out_shape=jax.ShapeDtypeStruct((t, d), F32),
        scratch_shapes=[pltpu.VMEM((SUBLANES, D_FF), F32), pltpu.VMEM((tm, D_FF), BF16)],
        compiler_params=_params(("arbitrary",), VMEM_LIMIT),
        name="conv_ffn",
    )(x, g, wg, wv, cw, cb, wd, *extra)


def _pad_heads(w, heads, width):
    lead = w.shape[:-1]
    w = w.reshape(lead + (heads, width))
    w = jnp.pad(w, [(0, 0)] * len(lead) + [(0, 0), (0, LANES - width)])
    return w.reshape(lead + (heads * LANES,))


def _inproj_weight(w):
    kv = NSA_KV_HEADS * HEAD_DIM
    sizes = (NSA_HEADS * HEAD_DIM,) + (kv,) * 6 + (NSA_HEADS * N_BRANCH, Q_LORA, KV_LORA, MLA_ROPE,
                                                   SSM_WIDTH)
    offs = [0]
    for n in sizes:
        offs.append(offs[-1] + n)
    (w_q, w_kc, w_vc, w_ks, w_vs, w_kw, w_vw, w_g, w_cq, w_ckv, w_kr, w_u) = [
        w[:, a:b] for a, b in zip(offs[:-1], offs[1:])]
    d = w.shape[0]
    half = MLA_ROPE // 2
    z64 = jnp.zeros((d, HEAD_DIM), w.dtype)
    z32 = jnp.zeros((d, LANES - HEAD_DIM - MLA_ROPE), w.dtype)
    r1, r2 = w_kr[:, :half], w_kr[:, half:]
    cols = [
        _pad_heads(w_q * (HEAD_DIM ** -0.5 * LOG2E), NSA_HEADS, HEAD_DIM),
        w_kc, w_vc,
        _pad_heads(w_ks, NSA_KV_HEADS, HEAD_DIM),
        _pad_heads(w_kw, NSA_KV_HEADS, HEAD_DIM),
        w_cq,
        jnp.pad(w_g, ((0, 0), (0, LANES - w_g.shape[1]))),
        w_ckv,
        jnp.concatenate([z64, r1, r2, z32], axis=1),
        jnp.concatenate([z64, r2, r1, z32], axis=1),
        w_u,
    ]
    w_vt = jnp.concatenate([_pad_heads(w_vs, NSA_KV_HEADS, HEAD_DIM),
                            _pad_heads(w_vw, NSA_KV_HEADS, HEAD_DIM)], axis=1).T
    return jnp.concatenate(cols, axis=1).astype(BF16), w_vt.astype(BF16)


def _mla_weights(w_uq, w_uk, w_uv):
    half = MLA_ROPE // 2
    w = w_uq.reshape(Q_LORA, MLA_HEADS, MLA_NOPE + MLA_ROPE)
    nope, r1, r2 = w[..., :MLA_NOPE], w[..., MLA_NOPE:MLA_NOPE + half], w[..., MLA_NOPE + half:]
    z32 = jnp.zeros((Q_LORA, MLA_HEADS, LANES - MLA_NOPE - MLA_ROPE), w.dtype)
    wa = jnp.concatenate([nope, r1, r2, z32], axis=-1).reshape(Q_LORA, MLA_HEADS * LANES)
    wb = jnp.concatenate([jnp.zeros_like(nope), r2, r1, z32], axis=-1).reshape(Q_LORA, MLA_HEADS * LANES)
    wk = _pad_heads(w_uk, MLA_HEADS, MLA_NOPE)
    wvt = _pad_heads(w_uv, MLA_HEADS, HEAD_DIM).T
    return wa.astype(BF16), wb.astype(BF16), wk.astype(BF16), wvt.astype(BF16)


def _ssm_weights(log_dt, a_re, a_im, b_re, b_im, c_re, c_im, d):
    dt = jnp.exp(log_dt.astype(F32))[:, None]
    ar, ai = a_re.astype(F32), a_im.astype(F32)
    mag = jnp.exp(ar * dt)
    abr, abi = mag * jnp.cos(ai * dt), mag * jnp.sin(ai * dt)
    den = ar * ar + ai * ai
    fr = ((abr - 1.0) * ar + abi * ai) / den
    fi = (abi * ar - (abr - 1.0) * ai) / den
    br, bi = b_re.astype(F32), b_im.astype(F32)
    bbr = fr[..., None] * br - fi[..., None] * bi
    bbi = fr[..., None] * bi + fi[..., None] * br
    eye = jnp.eye(SSM_GROUPS, dtype=F32)
    blk_in = lambda m: jnp.einsum("gpc,gh->gchp", m, eye).reshape(SSM_WIDTH, SSM_LANES)
    blk_out = lambda m: jnp.einsum("gcp,gh->gphc", m, eye).reshape(SSM_LANES, SSM_WIDTH)
    bblk = jnp.concatenate([blk_in(bbr), blk_in(bbi)], axis=1)
    ccat = jnp.concatenate([blk_out(c_re.astype(F32)), -blk_out(c_im.astype(F32))], axis=0)
    a_rows = jnp.stack([abr.reshape(SSM_LANES), abi.reshape(SSM_LANES)])
    return bblk.astype(BF16), a_rows, ccat.astype(BF16), d.astype(F32).reshape(1, SSM_WIDTH)


def _overlap_t(nc_pad):
    start = jnp.arange(nc_pad) * CMP_STRIDE
    lo = jnp.arange(MAX_SEL_BLOCKS) * SEL_BLOCK
    hit = (start[None, :] < lo[:, None] + SEL_BLOCK) & (start[None, :] + CMP_BLOCK > lo[:, None])
    return hit.astype(BF16)


def _rope_multipliers(positions):
    half = MLA_ROPE // 2
    per_row = LANES // half
    t = positions.size
    inv_freq = ROPE_THETA ** (-jnp.arange(half, dtype=F32) / half)
    pos = jnp.broadcast_to(positions.astype(F32).reshape(t // per_row, per_row, 1),
                           (t // per_row, per_row, half)).reshape(t // per_row, LANES)
    cos, sin, nsin = [a.reshape(t, half)
                      for a in _trig(pos, jnp.tile(inv_freq, per_row).reshape(1, LANES))]
    one = jnp.ones((t, HEAD_DIM), F32)
    zero = jnp.zeros((t, HEAD_DIM), F32)
    pad = jnp.zeros((t, LANES - HEAD_DIM - MLA_ROPE), F32)
    return (jnp.concatenate([one, cos, cos, pad], axis=1),
            jnp.concatenate([zero, nsin, sin, pad], axis=1))


def kernel(x, positions, attn_norm, w_in, nsa_pe, nsa_ck_w1, nsa_ck_b1, nsa_ck_w2, nsa_ck_b2, nsa_cv_w1, nsa_cv_b1, nsa_cv_w2, nsa_cv_b2, nsa_gate_b, mla_q_norm, mla_kv_norm, mla_w_uq, mla_w_uk, mla_w_uv, ssm_log_dt, ssm_a_re, ssm_a_im, ssm_b_re, ssm_b_im, ssm_c_re, ssm_c_im, ssm_d, ssm_w_glu, ssm_b_glu, out_norm_nsa, out_norm_mla, out_norm_ssm, w_out, ffn_norm, ffn_w_up, ffn_conv_w, ffn_conv_b, ffn_w_down, final_norm):
    batch, seq, d_model = x.shape
    depth = w_in.shape[0]
    t = batch * seq
    n_half = seq // CMP_STRIDE
    n_sel = seq // SEL_BLOCK
    assert depth >= 1 and seq % ROW_TILE == 0 and n_sel <= MAX_SEL_BLOCKS and n_half % LANES == 0
    n_top = min(SEL_TOP, n_sel)
    row = lambda v: v.astype(F32).reshape(1, -1)

    cmul, smul = _rope_multipliers(positions)
    ovl_t = _overlap_t(n_half)

    xf = x.reshape(t, d_model)
    for l in range(depth):
        (q, kc, vc, ks, kw, gate, u, vs_t, vw_t, q_m, k_m, vm_t) = _inproj(
            xf, row(attn_norm[l]), *_inproj_weight(w_in[l]), cmul, smul,
            row(mla_q_norm[l]), row(mla_kv_norm[l]),
            *_mla_weights(mla_w_uq[l], mla_w_uk[l], mla_w_uv[l]), batch, seq)

        kvcmp = _compress(
            kc, vc, batch,
            *_compress_weights(nsa_pe[l], nsa_ck_w1[l], nsa_ck_b1[l], nsa_ck_w2[l], nsa_ck_b2[l],
                               nsa_cv_w1[l], nsa_cv_b1[l], nsa_cv_w2[l], nsa_cv_b2[l]))
        vcmp_t = kvcmp[:, NSA_KV_HEADS:].transpose(0, 1, 3, 2).astype(BF16)
        o_cmp, q_aug, o_win = _nsa_local(q, kvcmp, vcmp_t, ovl_t, kw, vw_t, batch, seq, n_top)
        o_sel = _flash(q_aug, ks, vs_t, batch, seq, NSA_GROUP, "nsa_selected")

        o_mla = _flash(q_m, k_m, vm_t, batch, seq, 1, "mla_attention")

        bblk, a_rows, ccat, d_row = _ssm_weights(
            ssm_log_dt[l], ssm_a_re[l], ssm_a_im[l], ssm_b_re[l], ssm_b_im[l],
            ssm_c_re[l], ssm_c_im[l], ssm_d[l])
        o_ssm = _ssm(u.reshape(seq, batch, SSM_WIDTH), bblk, a_rows, ccat, d_row,
                     ssm_w_glu[l].astype(BF16), row(ssm_b_glu[l]))
        o_ssm = o_ssm.reshape(seq, batch * SSM_WIDTH)

        w_o = w_out[l].astype(BF16)
        n_a = NSA_HEADS * HEAD_DIM
        n_b = n_a + MLA_HEADS * HEAD_DIM
        xf = _outproj(
            o_cmp, o_sel, o_win, gate,
            jnp.pad(row(nsa_gate_b[l]), ((0, 0), (0, LANES - NSA_HEADS * N_BRANCH))),
            o_mla, o_ssm, xf,
            row(out_norm_nsa[l]), row(out_norm_mla[l]), row(out_norm_ssm[l]),
            w_o[:n_a], w_o[n_a:n_b], w_o[n_b:])

        w_up = ffn_w_up[l].astype(BF16)
        xf = _ffn(xf, row(ffn_norm[l]), w_up[:, :D_FF], w_up[:, D_FF:],
                  jnp.pad(ffn_conv_w[l].astype(F32), ((0, SUBLANES - ffn_conv_w.shape[1]), (0, 0))),
                  row(ffn_conv_b[l]), ffn_w_down[l].astype(BF16), seq,
                  final_gain=row(final_norm) if l == depth - 1 else None)

    return xf.reshape(batch, seq, d_model)
```
